```python
import jax, jax.numpy as jnp
from jax import lax
import numpy as np

D_MODEL = 1024
BATCH = 2
SEQ = 8192
DEPTH = 2
DEC_BATCH = 16
DEC_SEQ = 32
PAST_LEN = 4096

CHUNK = 64
HEAD_DIM = 64
MIX_WIDTH = D_MODEL
A_WIDTH = MIX_WIDTH // 2
N_HEADS_A = A_WIDTH // HEAD_DIM
N_KV_A = 2
GROUP_A = N_HEADS_A // N_KV_A
KV_WIDTH_A = N_KV_A * HEAD_DIM
WINDOW = 128
WINDOW_CHUNKS = WINDOW // CHUNK
SWA_CACHE = min(WINDOW, PAST_LEN)
ROPE_THETA = 10000.0
R_WIDTH = MIX_WIDTH - A_WIDTH
N_HEADS_R = R_WIDTH // HEAD_DIM
DECAY_LORA = max(32, int(round(1.8 * D_MODEL ** 0.5 / 32)) * 32)
AAA_LORA = max(32, int(round(1.8 * D_MODEL ** 0.5 / 32)) * 32)
GATE_LORA = max(32, int(round(0.6 * D_MODEL ** 0.8 / 32)) * 32)
RWKV_PROJ = 3 * R_WIDTH + DECAY_LORA + AAA_LORA + GATE_LORA
GN_EPS = 6.4e-4
IN_COLS = A_WIDTH + 2 * KV_WIDTH_A + RWKV_PROJ
MEM_LEN = 256
MEM_HEADS = 4
MEM_HEAD_DIM = 128
MEM_WIDTH = MEM_HEADS * MEM_HEAD_DIM
D_FF = -(-8 * D_MODEL // (3 * 256)) * 256
RMS_EPS = 1e-6
NEG_INF = -1e30
NORM_MIX_PRE, NORM_MIX_POST, NORM_X_PRE, NORM_X_POST, NORM_MEM, NORM_FFN_PRE, NORM_FFN_POST = range(7)
N_NORMS = 7

kernel_name = "hybrid_swa_rwkv7_streaming_step"


def rms_norm(x, g):
    xf = x.astype(jnp.float32)
    y = xf * lax.rsqrt(jnp.mean(xf * xf, axis=-1, keepdims=True) + RMS_EPS)
    return (y * g.astype(jnp.float32)).astype(x.dtype)


def rope(x, pos):
    half = HEAD_DIM // 2
    inv = ROPE_THETA ** (-jnp.arange(half, dtype=jnp.float32) / half)
    ang = pos.astype(jnp.float32)[:, None] * inv[None, :]
    cos = jnp.cos(ang)[:, None, :]
    sin = jnp.sin(ang)[:, None, :]
    xf = x.astype(jnp.float32)
    x1, x2 = xf[..., :half], xf[..., half:]
    return jnp.concatenate([x1 * cos - x2 * sin, x2 * cos + x1 * sin], axis=-1).astype(x.dtype)


def sink_attention(q, k, v, sink, mask):
    s = jnp.einsum('...qhgd,...khd->...hgqk', q.astype(jnp.float32), k.astype(jnp.float32)) * (HEAD_DIM ** -0.5)
    if mask is not None:
        s = jnp.where(mask, s, NEG_INF)
    sk = sink.astype(jnp.float32).reshape(N_KV_A, GROUP_A)[:, :, None, None]
    m = jnp.maximum(jnp.max(s, axis=-1, keepdims=True), sk)
    e = jnp.exp(s - m)
    p = e / (jnp.sum(e, axis=-1, keepdims=True) + jnp.exp(sk - m))
    o = jnp.einsum('...hgqk,...khd->...qhgd', p, v.astype(jnp.float32))
    return o.astype(q.dtype)


def swa_banded(q, k, v, sink):
    B, T = q.shape[:2]
    n_c = T // CHUNK
    qc = q.reshape(B, n_c, CHUNK, N_KV_A, GROUP_A, HEAD_DIM)
    pad = ((0, 0), (WINDOW_CHUNKS * CHUNK, 0), (0, 0), (0, 0))
    kp = jnp.pad(k, pad).reshape(B, n_c + WINDOW_CHUNKS, CHUNK, N_KV_A, HEAD_DIM)
    vp = jnp.pad(v, pad).reshape(B, n_c + WINDOW_CHUNKS, CHUNK, N_KV_A, HEAD_DIM)
    kb = jnp.concatenate([kp[:, j:j + n_c] for j in range(WINDOW_CHUNKS + 1)], axis=2)
    vb = jnp.concatenate([vp[:, j:j + n_c] for j in range(WINDOW_CHUNKS + 1)], axis=2)
    slot_chunk = jnp.arange((WINDOW_CHUNKS + 1) * CHUNK)[None, :] // CHUNK
    valid = (jnp.arange(n_c)[:, None] - WINDOW_CHUNKS + slot_chunk) >= 0
    mask = valid[:, None, None, None, :]
    o = sink_attention(qc, kb, vb, sink, mask)
    return o.reshape(B, T, A_WIDTH)


def rwkv_step(S, inp):
    r_t, w_t, k_t, v_t, kk_t, b_t = inp
    sa = jnp.einsum('bhvk,bhk->bhv', S, -kk_t)
    S = S * w_t[:, :, None, :] + sa[..., None] * b_t[:, :, None, :] + v_t[..., None] * k_t[:, :, None, :]
    return S, jnp.einsum('bhvk,bhk->bhv', S, r_t)


def rwkv_mix(P, shift0, S0, mu, rvecs, rk, w2, a2, g2):
    B, T, _ = P.shape
    f32 = jnp.float32
    p_prev = jnp.concatenate([shift0.astype(P.dtype), P[:, :-1]], axis=1)
    ps = P + (p_prev - P) * mu
    o = R_WIDTH
    r, k, v = ps[..., :o], ps[..., o:2 * o], ps[..., 2 * o:3 * o]
    xw = ps[..., 3 * o:3 * o + DECAY_LORA]
    xa = ps[..., 3 * o + DECAY_LORA:3 * o + DECAY_LORA + AAA_LORA]
    xg = ps[..., 3 * o + DECAY_LORA + AAA_LORA:]
    w0, a0, k_k, k_a, gn_g, gn_b = rvecs[0], rvecs[1], rvecs[2], rvecs[3], rvecs[4], rvecs[5]
    w_raw = -jax.nn.softplus(-(w0 + jnp.tanh(xw) @ w2).astype(f32)) - 0.5
    decay = jnp.exp(-jnp.exp(w_raw))
    a = jax.nn.sigmoid((a0 + xa @ a2).astype(f32))
    g = (jax.nn.sigmoid(xg) @ g2).astype(f32)

    def heads(t):
        return t.reshape(B, T, N_HEADS_R, HEAD_DIM)

    kk = heads((k * k_k).astype(f32))
    kk = kk * lax.rsqrt(jnp.maximum(jnp.sum(kk * kk, axis=-1, keepdims=True), 1e-24))
    k_f = heads(k.astype(f32) * (1.0 + (a - 1.0) * k_a.astype(f32)))
    r_f = heads(r.astype(f32))
    v_f = heads(v.astype(f32))
    a_h = heads(a)
    w_h = heads(decay)

    def tmaj(t):
        return jnp.moveaxis(t, 1, 0)

    S_fin, y = lax.scan(rwkv_step, S0.astype(f32),
                        (tmaj(r_f), tmaj(w_h), tmaj(k_f), tmaj(v_f), tmaj(kk), tmaj(kk * a_h)))
    y = jnp.moveaxis(y, 0, 1)
    mean = jnp.mean(y, axis=-1, keepdims=True)
    var = jnp.mean(jnp.square(y - mean), axis=-1, keepdims=True)
    yn = ((y - mean) * lax.rsqrt(var + GN_EPS)).reshape(B, T, R_WIDTH) * gn_g.astype(f32) + gn_b.astype(f32)
    bonus = (jnp.sum(r_f * k_f * rk.astype(f32), axis=-1, keepdims=True) * v_f).reshape(B, T, R_WIDTH)
    out = (yn + bonus) * g
    return out.astype(P.dtype), S_fin, P[:, -1:]


def token_mix(h, pos, w_in, sink, mu, rvecs, rk, w2, a2, g2, w_out, swa_k_past, swa_v_past, S0, shift0):
    B, T, _ = h.shape
    P = h @ w_in
    q = rope(P[..., :A_WIDTH].reshape(B, T, N_HEADS_A, HEAD_DIM), pos)
    ka = rope(P[..., A_WIDTH:A_WIDTH + KV_WIDTH_A].reshape(B, T, N_KV_A, HEAD_DIM), pos)
    va = P[..., A_WIDTH + KV_WIDTH_A:A_WIDTH + 2 * KV_WIDTH_A].reshape(B, T, N_KV_A, HEAD_DIM)
    pr = P[..., A_WIDTH + 2 * KV_WIDTH_A:]
    if swa_k_past is None:
        attn = swa_banded(q, ka, va, sink)
        keep = T - min(WINDOW, T)
        k_keep, v_keep = ka[:, keep:], va[:, keep:]
    else:
        keys = jnp.concatenate([swa_k_past.astype(ka.dtype), ka], axis=1)
        vals = jnp.concatenate([swa_v_past.astype(va.dtype), va], axis=1)
        qg = q.reshape(B, T, N_KV_A, GROUP_A, HEAD_DIM)
        attn = sink_attention(qg, keys, vals, sink, None).reshape(B, T, A_WIDTH)
        k_keep, v_keep = ka, va
    rw, S_fin, shift_new = rwkv_mix(pr, shift0, S0, mu, rvecs, rk, w2, a2, g2)
    out = jnp.concatenate([attn, rw], axis=-1) @ w_out
    return out, k_keep, v_keep, S_fin, shift_new


def memory_kv(mem, g_mem, w_mem_kv):
    B, M, _ = mem.shape
    kv = rms_norm(mem, g_mem) @ w_mem_kv
    mk = kv[..., :MEM_WIDTH].reshape(B, M, MEM_HEADS, MEM_HEAD_DIM)
    mv = kv[..., MEM_WIDTH:].reshape(B, M, MEM_HEADS, MEM_HEAD_DIM)
    return mk, mv


def cross_attend(h, mem_k, mem_v, w_q, w_o):
    B, T, _ = h.shape
    q = (h @ w_q).reshape(B, T, MEM_HEADS, MEM_HEAD_DIM)
    s = jnp.einsum('bqhd,bkhd->bhqk', q.astype(jnp.float32), mem_k.astype(jnp.float32)) * (MEM_HEAD_DIM ** -0.5)
    p = jax.nn.softmax(s, axis=-1)
    o = jnp.einsum('bhqk,bkhd->bqhd', p, mem_v.astype(jnp.float32)).astype(h.dtype)
    return o.reshape(B, T, MEM_WIDTH) @ w_o


def swiglu(h, w_gu, w_dn):
    gu = h @ w_gu
    return (jax.nn.silu(gu[..., :D_FF]) * gu[..., D_FF:]) @ w_dn


def layer(x, pos, mem_k, mem_v, swa_k_past, swa_v_past, S0, shift0, lw):
    ng, w_in, sink, mu, rvecs, rk, w2, a2, g2, w_out, w_mq, w_mo, w_gu, w_dn = lw
    m, k_keep, v_keep, S_fin, shift_new = token_mix(
        rms_norm(x, ng[NORM_MIX_PRE]), pos, w_in, sink, mu, rvecs, rk, w2, a2, g2, w_out,
        swa_k_past, swa_v_past, S0, shift0)
    x = x + rms_norm(m, ng[NORM_MIX_POST])
    c = cross_attend(rms_norm(x, ng[NORM_X_PRE]), mem_k, mem_v, w_mq, w_mo)
    x = x + rms_norm(c, ng[NORM_X_POST])
    f = swiglu(rms_norm(x, ng[NORM_FFN_PRE]), w_gu, w_dn)
    x = x + rms_norm(f, ng[NORM_FFN_POST])
    return x, k_keep, v_keep, S_fin, shift_new


def setup_inputs(seed: int = 0) -> dict:
    key = jax.random.key(seed)
    ks = jax.random.split(key, 24)
    nrm = jax.random.normal
    f32 = jnp.float32
    vec_off = jnp.array([0.0, 0.0, 0.85, 1.0, 1.0, 0.0], f32)[None, :, None]
    vec_scl = jnp.array([0.5, 0.5, 0.05, 0.05, 0.05, 0.02], f32)[None, :, None]
    return {
        "x_prompt": nrm(ks[0], (BATCH, SEQ, D_MODEL), f32),
        "mem_prompt": nrm(ks[1], (BATCH, MEM_LEN, D_MODEL), f32),
        "x_sample": nrm(ks[2], (DEC_BATCH, DEC_SEQ, D_MODEL), f32),
        "cache_swa_k": nrm(ks[3], (DEPTH, DEC_BATCH, SWA_CACHE, N_KV_A, HEAD_DIM), f32),
        "cache_swa_v": nrm(ks[4], (DEPTH, DEC_BATCH, SWA_CACHE, N_KV_A, HEAD_DIM), f32),
        "cache_mem_k": nrm(ks[5], (DEPTH, DEC_BATCH, MEM_LEN, MEM_HEADS, MEM_HEAD_DIM), f32),
        "cache_mem_v": nrm(ks[6], (DEPTH, DEC_BATCH, MEM_LEN, MEM_HEADS, MEM_HEAD_DIM), f32),
        "state_rwkv": 0.5 * nrm(ks[7], (DEPTH, DEC_BATCH, N_HEADS_R, HEAD_DIM, HEAD_DIM), f32),
        "state_shift": nrm(ks[8], (DEPTH, DEC_BATCH, 1, RWKV_PROJ), f32),
        "norm_gains": 1.0 + 0.05 * nrm(ks[9], (DEPTH, N_NORMS, D_MODEL), f32),
        "w_in": nrm(ks[10], (DEPTH, D_MODEL, IN_COLS), f32) * D_MODEL ** -0.5,
        "attn_sink": 0.5 * nrm(ks[11], (DEPTH, N_HEADS_A), f32),
        "shift_mu": jax.random.uniform(ks[12], (DEPTH, RWKV_PROJ), f32),
        "rwkv_vecs": vec_off + vec_scl * nrm(ks[13], (DEPTH, 6, R_WIDTH), f32),
        "rwkv_rk": 0.1 * nrm(ks[14], (DEPTH, N_HEADS_R, HEAD_DIM), f32),
        "rwkv_w2": nrm(ks[15], (DEPTH, DECAY_LORA, R_WIDTH), f32) * DECAY_LORA ** -0.5,
        "rwkv_a2": nrm(ks[16], (DEPTH, AAA_LORA, R_WIDTH), f32) * AAA_LORA ** -0.5,
        "rwkv_g2": nrm(ks[17], (DEPTH, GATE_LORA, R_WIDTH), f32) * GATE_LORA ** -0.5,
        "w_out": nrm(ks[18], (DEPTH, MIX_WIDTH, D_MODEL), f32) * MIX_WIDTH ** -0.5,
        "w_mem_q": nrm(ks[19], (DEPTH, D_MODEL, MEM_WIDTH), f32) * D_MODEL ** -0.5,
        "w_mem_kv": nrm(ks[20], (DEPTH, D_MODEL, 2 * MEM_WIDTH), f32) * D_MODEL ** -0.5,
        "w_mem_o": nrm(ks[21], (DEPTH, MEM_WIDTH, D_MODEL), f32) * MEM_WIDTH ** -0.5,
        "w_gate_up": nrm(ks[22], (DEPTH, D_MODEL, 2 * D_FF), f32) * D_MODEL ** -0.5,
        "w_down": nrm(ks[23], (DEPTH, D_FF, D_MODEL), f32) * D_FF ** -0.5,
    }


def reference(x_prompt, mem_prompt, x_sample, cache_swa_k, cache_swa_v, cache_mem_k, cache_mem_v,
              state_rwkv, state_shift, norm_gains, w_in, attn_sink, shift_mu, rwkv_vecs, rwkv_rk,
              rwkv_w2, rwkv_a2, rwkv_g2, w_out, w_mem_q, w_mem_kv, w_mem_o, w_gate_up, w_down):
    B, T, _ = x_prompt.shape
    Bd, Tn, _ = x_sample.shape
    pos_p = jnp.arange(T, dtype=jnp.int32)
    pos_s = PAST_LEN + jnp.arange(Tn, dtype=jnp.int32)
    S0_p = jnp.zeros((B, N_HEADS_R, HEAD_DIM, HEAD_DIM), jnp.float32)
    sh0_p = jnp.zeros((B, 1, RWKV_PROJ), x_prompt.dtype)
    xp, xs = x_prompt, x_sample
    pk, pv, pmk, pmv, pS, psh = [], [], [], [], [], []
    sk, sv, sS, ssh = [], [], [], []
    for l in range(DEPTH):
        lw = (norm_gains[l], w_in[l], attn_sink[l], shift_mu[l], rwkv_vecs[l], rwkv_rk[l], rwkv_w2[l],
              rwkv_a2[l], rwkv_g2[l], w_out[l], w_mem_q[l], w_mem_o[l], w_gate_up[l], w_down[l])
        mk_l, mv_l = memory_kv(mem_prompt, norm_gains[l, NORM_MEM], w_mem_kv[l])
        xp, k_l, v_l, S_l, sh_l = layer(xp, pos_p, mk_l, mv_l, None, None, S0_p, sh0_p, lw)
        pk.append(k_l); pv.append(v_l); pmk.append(mk_l); pmv.append(mv_l); pS.append(S_l); psh.append(sh_l)
        xs, k2, v2, S2, sh2 = layer(xs, pos_s, cache_mem_k[l], cache_mem_v[l], cache_swa_k[l], cache_swa_v[l],
                                    state_rwkv[l], state_shift[l], lw)
        sk.append(k2); sv.append(v2); sS.append(S2.astype(state_rwkv.dtype)); ssh.append(sh2)
    return (xp, xs, jnp.stack(pk), jnp.stack(pv), jnp.stack(pmk), jnp.stack(pmv), jnp.stack(pS), jnp.stack(psh),
            jnp.stack(sk), jnp.stack(sv), jnp.stack(sS), jnp.stack(ssh))
```

```python
import functools
import math

import jax
import jax.numpy as jnp
from jax import lax
from jax.experimental import pallas as pl
from jax.experimental.pallas import tpu as pltpu

F32 = jnp.float32
BF16 = jnp.bfloat16

D_MODEL = 1024
HEAD_DIM = 64
CHUNK = 64
A_WIDTH = 512
KV_WIDTH_A = 128
N_KV_A = 2
WINDOW = 128
PAST_LEN = 4096
ROPE_THETA = 10000.0
R_WIDTH = 512
N_HEADS_R = 8
DECAY_LORA = 64
AAA_LORA = 64
GATE_LORA = 160
RWKV_PROJ = 3 * R_WIDTH + DECAY_LORA + AAA_LORA + GATE_LORA
RWKV_PAD = 1920
IN_COLS_PAD = A_WIDTH + 2 * KV_WIDTH_A + RWKV_PAD
GN_EPS = 6.4e-4
MEM_LEN = 256
MEM_HEADS = 4
MEM_HEAD_DIM = 128
MEM_WIDTH = 512
D_FF = 2816
RMS_EPS = 1e-6
NEG_INF = -1e30
NORM_MIX_PRE, NORM_MIX_POST, NORM_X_PRE, NORM_X_POST, NORM_MEM, NORM_FFN_PRE, NORM_FFN_POST = range(7)
EXP_M05 = math.exp(-0.5)

LANES = 128
VMEM_LIMIT = 56 * 1024 * 1024


def _params(sem):
    return pltpu.CompilerParams(dimension_semantics=sem, vmem_limit_bytes=VMEM_LIMIT)


def _rms(x, g):
    ms = jnp.mean(x * x, axis=-1, keepdims=True)
    return x * lax.rsqrt(ms + RMS_EPS) * g


def _mm(a, b):
    return jnp.dot(a.astype(BF16), b.astype(BF16), preferred_element_type=F32)


def _mm_nt(a, b):
    return lax.dot_general(a.astype(BF16), b.astype(BF16), (((1,), (1,)), ((), ())),
                           preferred_element_type=F32)


def _mm_hi(a, b):
    return jnp.dot(a, b, preferred_element_type=F32, precision=lax.Precision.HIGHEST)


def _split(x):
    hi = x.astype(BF16)
    lo = (x - hi.astype(F32)).astype(BF16)
    return hi, lo


def _mm_lsplit(a_exact_bf16, x):
    hi, lo = _split(x)
    return (jnp.dot(a_exact_bf16, hi, preferred_element_type=F32)
            + jnp.dot(a_exact_bf16, lo, preferred_element_type=F32))


def _mm_rsplit(x, b_exact_bf16):
    hi, lo = _split(x)
    return (jnp.dot(hi, b_exact_bf16, preferred_element_type=F32)
            + jnp.dot(lo, b_exact_bf16, preferred_element_type=F32))


def _in_kernel(x_ref, g_ref, w_ref, cos_ref, sin_ref, q_ref, k_ref, v_ref, pr_ref):
    h = _rms(x_ref[...], g_ref[...]).astype(BF16)
    p = jnp.dot(h, w_ref[...], preferred_element_type=F32)
    cos = cos_ref[...]
    sin = sin_ref[...]
    lane = lax.broadcasted_iota(jnp.int32, cos.shape, 1)
    first_half = (lane & (HEAD_DIM // 2)) == 0

    def rope(xc):
        sw = jnp.where(first_half, pltpu.roll(xc, LANES - HEAD_DIM // 2, 1), pltpu.roll(xc, HEAD_DIM // 2, 1))
        return xc * cos + sw * sin

    for j in range(A_WIDTH // LANES):
        q_ref[:, j * LANES:(j + 1) * LANES] = rope(p[:, j * LANES:(j + 1) * LANES])
    k_ref[...] = rope(p[:, A_WIDTH:A_WIDTH + KV_WIDTH_A])
    v_ref[...] = p[:, A_WIDTH + KV_WIDTH_A:A_WIDTH + 2 * KV_WIDTH_A]
    pr_ref[...] = p[:, A_WIDTH + 2 * KV_WIDTH_A:]


def _in_proj(x2d, g, w_in_p, cos_t, sin_t, tm):
    n = x2d.shape[0]
    tab_blocks = cos_t.shape[0] // tm
    row = lambda i: (i, 0)
    const = lambda i: (0, 0)
    tab = lambda i: (i % tab_blocks, 0)
    return pl.pallas_call(
        _in_kernel,
        grid=(n // tm,),
        in_specs=[pl.BlockSpec((tm, D_MODEL), row), pl.BlockSpec((1, D_MODEL), const),
                  pl.BlockSpec((D_MODEL, IN_COLS_PAD), const),
                  pl.BlockSpec((tm, LANES), tab), pl.BlockSpec((tm, LANES), tab)],
        out_specs=[pl.BlockSpec((tm, A_WIDTH), row), pl.BlockSpec((tm, KV_WIDTH_A), row),
                   pl.BlockSpec((tm, KV_WIDTH_A), row), pl.BlockSpec((tm, RWKV_PAD), row)],
        out_shape=[jax.ShapeDtypeStruct((n, A_WIDTH), F32), jax.ShapeDtypeStruct((n, KV_WIDTH_A), F32),
                   jax.ShapeDtypeStruct((n, KV_WIDTH_A), F32), jax.ShapeDtypeStruct((n, RWKV_PAD), F32)],
        compiler_params=_params(("parallel",)),
        name="in_proj",
    )(x2d, g, w_in_p, cos_t, sin_t)


def _sink_attend(q_rows, kdup, vdup, sink_ref, kv, valid, t):
    lane = lax.broadcasted_iota(jnp.int32, (t, LANES), 1)
    m0 = lane < HEAD_DIM
    parts = []
    for p in (2 * kv, 2 * kv + 1):
        qp = q_rows[:, p * LANES:(p + 1) * LANES]
        parts.append(jnp.where(m0, qp, 0.0))
        parts.append(jnp.where(m0, 0.0, qp))
    lhs = jnp.concatenate(parts, axis=0)
    s = _mm_nt(lhs, kdup) * (HEAD_DIM ** -0.5)
    if valid is not None:
        s = jnp.where(valid, s, NEG_INF)
    rowi = lax.broadcasted_iota(jnp.int32, (4 * t, 1), 0)
    sk = jnp.zeros((4 * t, 1), F32)
    for g in range(4):
        sg = sink_ref[4 * kv + g:4 * kv + g + 1, 0:1]
        sk = jnp.where((rowi >= g * t) & (rowi < (g + 1) * t), sg, sk)
    m = jnp.maximum(jnp.max(s, axis=-1, keepdims=True), sk)
    e = jnp.exp(s - m)
    den = jnp.sum(e, axis=-1, keepdims=True) + jnp.exp(sk - m)
    o = _mm(e, vdup) / den
    return [jnp.where(m0, o[(2 * pi) * t:(2 * pi + 1) * t], o[(2 * pi + 1) * t:(2 * pi + 2) * t])
            for pi in range(2)]


def _dup_heads(x):
    lane = lax.broadcasted_iota(jnp.int32, x.shape, 1)
    m0 = lane < HEAD_DIM
    xs = pltpu.roll(x, HEAD_DIM, 1)
    return [jnp.where(m0, x, xs), jnp.where(m0, xs, x)]


SWA_TQ = 256


def _swa_prompt_kernel(q_ref, kp_ref, kc_ref, vp_ref, vc_ref, sink_ref, o_ref):
    i = pl.program_id(1)
    k = jnp.concatenate([kp_ref[...], kc_ref[...]], axis=0)
    v = jnp.concatenate([vp_ref[...], vc_ref[...]], axis=0)
    kd = _dup_heads(k)
    vd = _dup_heads(v)
    nk = 3 * CHUNK
    slot = lax.broadcasted_iota(jnp.int32, (1, nk), 1) // CHUNK
    for j in range(SWA_TQ // CHUNK):
        qj = q_ref[j * CHUNK:(j + 1) * CHUNK, :]
        valid = (slot + (i * (SWA_TQ // CHUNK) + j - 2)) >= 0
        for kv in range(N_KV_A):
            outs = _sink_attend(qj, kd[kv][j * CHUNK:j * CHUNK + nk], vd[kv][j * CHUNK:j * CHUNK + nk],
                                sink_ref, kv, valid, CHUNK)
            for pi in range(2):
                p = 2 * kv + pi
                o_ref[j * CHUNK:(j + 1) * CHUNK, p * LANES:(p + 1) * LANES] = outs[pi]


def _swa_prompt(q, k, v, sink_b, b, t):
    n = b * t
    nq = t // SWA_TQ
    qmap = lambda bi, i: (bi * nq + i, 0)
    pmap = lambda bi, i: (jnp.maximum(bi * (t // WINDOW) + 2 * i - 1, 0), 0)
    const = lambda bi, i: (0, 0)
    return pl.pallas_call(
        _swa_prompt_kernel,
        grid=(b, nq),
        in_specs=[pl.BlockSpec((SWA_TQ, A_WIDTH), qmap),
                  pl.BlockSpec((WINDOW, KV_WIDTH_A), pmap), pl.BlockSpec((SWA_TQ, KV_WIDTH_A), qmap),
                  pl.BlockSpec((WINDOW, KV_WIDTH_A), pmap), pl.BlockSpec((SWA_TQ, KV_WIDTH_A), qmap),
                  pl.BlockSpec((8, LANES), const)],
        out_specs=pl.BlockSpec((SWA_TQ, A_WIDTH), qmap),
        out_shape=jax.ShapeDtypeStruct((n, A_WIDTH), F32),
        compiler_params=_params(("parallel", "parallel")),
        name="swa_prompt",
    )(q, k, k, v, v, sink_b)


def _swa_sample_kernel(q_ref, kc_ref, kn_ref, vc_ref, vn_ref, sink_ref, o_ref, *, t):
    k = jnp.concatenate([kc_ref[0], kn_ref[...]], axis=0)
    v = jnp.concatenate([vc_ref[0], vn_ref[...]], axis=0)
    kd = _dup_heads(k)
    vd = _dup_heads(v)
    q = q_ref[...]
    for kv in range(N_KV_A):
        outs = _sink_attend(q, kd[kv], vd[kv], sink_ref, kv, None, t)
        for pi in range(2):
            p = 2 * kv + pi
            o_ref[:, p * LANES:(p + 1) * LANES] = outs[pi]


def _swa_sample(q, k, v, kc, vc, sink_b, b, t):
    n = b * t
    cache = kc.shape[1]
    row = lambda bi: (bi, 0)
    cmap = lambda bi: (bi, 0, 0)
    const = lambda bi: (0, 0)
    return pl.pallas_call(
        functools.partial(_swa_sample_kernel, t=t),
        grid=(b,),
        in_specs=[pl.BlockSpec((t, A_WIDTH), row),
                  pl.BlockSpec((1, cache, KV_WIDTH_A), cmap), pl.BlockSpec((t, KV_WIDTH_A), row),
                  pl.BlockSpec((1, cache, KV_WIDTH_A), cmap), pl.BlockSpec((t, KV_WIDTH_A), row),
                  pl.BlockSpec((8, LANES), const)],
        out_specs=pl.BlockSpec((t, A_WIDTH), row),
        out_shape=jax.ShapeDtypeStruct((n, A_WIDTH), F32),
        compiler_params=_params(("parallel",)),
        name="swa_sample",
    )(q, kc, k, vc, v, sink_b)


def _tri_inverse(nmat, masks, eye):
    p = eye + jnp.where(masks[0], nmat, 0.0)
    for m in masks[1:]:
        e = jnp.where(m, nmat, 0.0)
        p = p + _mm_hi(_mm_hi(p, e), p)
    return p


def _rwkv_kernel(pr_ref, sh0_ref, s0_ref, mu_ref, vec_ref, rk_ref, w2_ref, a2_ref, g2_ref, hsum_ref,
                 ltri_ref, ball_ref, out_ref, sfin_ref,
                 s_scr, carry_scr, rt_s, at_s, bt_s, kt_s, bb_s, kb_s, v_s, et_s, y_s, *, c_len, tt):
    i = pl.program_id(1)
    n_i = pl.num_programs(1)

    @pl.when(i == 0)
    def _():
        s_scr[...] = s0_ref[0]
        carry_scr[...] = sh0_ref[0]

    row0 = lax.broadcasted_iota(jnp.int32, (tt, 1), 0) == 0

    def shifted(a, b):
        cur = pr_ref[0, :, a:b]
        prev = jnp.where(row0, carry_scr[:, a:b], pltpu.roll(cur, 1, 0))
        return cur + (prev - cur) * mu_ref[:, a:b]

    r = shifted(0, 512)
    k = shifted(512, 1024)
    v = shifted(1024, 1536)
    xwa = shifted(1536, 1664)
    xg = shifted(1664, RWKV_PAD)
    carry_scr[...] = pr_ref[0, tt - 1:tt, :]

    w0 = vec_ref[0:1, :]
    a0 = vec_ref[1:2, :]
    k_k = vec_ref[2:3, :]
    k_a = vec_ref[3:4, :]
    gn_g = vec_ref[4:5, :]
    gn_b = vec_ref[5:6, :]
    hsum = hsum_ref[...]

    z = w0 + _mm(jnp.tanh(xwa), w2_ref[...])
    wlog = -EXP_M05 * jax.nn.sigmoid(z)
    a = jax.nn.sigmoid(a0 + _mm(xwa, a2_ref[...]))
    g = _mm(jax.nn.sigmoid(xg), g2_ref[...])
    kk = k * k_k
    kk = kk * lax.rsqrt(jnp.maximum(_mm_rsplit(kk * kk, hsum), 1e-24))
    k_f = k * (1.0 + (a - 1.0) * k_a)
    bvec = kk * a
    bonus = _mm_rsplit(r * k_f * rk_ref[...], hsum) * v

    cum = _mm_lsplit(ltri_ref[...], wlog)
    tot = _mm_lsplit(ball_ref[...], wlog)
    e_in = jnp.exp(cum)
    e_inv = jnp.exp(-cum)
    e_end = jnp.exp(tot - cum)
    rt_s[...] = r * e_in
    at_s[...] = -kk * jnp.exp(cum - wlog)
    bt_s[...] = bvec * e_inv
    kt_s[...] = k_f * e_inv
    bb_s[...] = bvec * e_end
    kb_s[...] = k_f * e_end
    v_s[...] = v
    et_s[...] = jnp.exp(tot)

    n2 = 2 * c_len
    ri = lax.broadcasted_iota(jnp.int32, (n2, n2), 0)
    ci = lax.broadcasted_iota(jnp.int32, (n2, n2), 1)
    same_head = (ri >= c_len) == (ci >= c_len)
    strict = same_head & (ri > ci)
    incl = same_head & (ri >= ci)
    eye = jnp.where(ri == ci, 1.0, 0.0).astype(F32)
    masks = []
    half = 1
    while half < c_len:
        blk = 2 * half
        masks.append(((ri & ~(blk - 1)) == (ci & ~(blk - 1))) & ((ri & half) != 0) & ((ci & half) == 0))
        half = blk
    lane_c = lax.broadcasted_iota(jnp.int32, (c_len, LANES), 1)
    m0 = lane_c < HEAD_DIM
    rl = lax.broadcasted_iota(jnp.int32, (LANES, LANES), 0)
    cl = lax.broadcasted_iota(jnp.int32, (LANES, LANES), 1)
    blockdiag = (rl >= HEAD_DIM) == (cl >= HEAD_DIM)

    def stack(x):
        return jnp.concatenate([jnp.where(m0, x, 0.0), jnp.where(m0, 0.0, x)], axis=0)

    def chunk_body(c, carry):
        r0 = pl.multiple_of(c * c_len, c_len)
        rows = pl.ds(r0, c_len)
        for p in range(4):
            ls = slice(p * LANES, (p + 1) * LANES)
            la = stack(at_s[rows, ls])
            lr = stack(rt_s[rows, ls])
            rb = stack(bt_s[rows, ls])
            rkt = stack(kt_s[rows, ls])
            rbb = stack(bb_s[rows, ls])
            rkb = stack(kb_s[rows, ls])
            vst = stack(v_s[rows, ls])
            e_last = et_s[pl.ds(r0, 1), ls]
            a_ab = jnp.where(strict, _mm_nt(la, rb), 0.0)
            a_ak = jnp.where(strict, _mm_nt(la, rkt), 0.0)
            a_rb = jnp.where(incl, _mm_nt(lr, rb), 0.0)
            a_rk = jnp.where(incl, _mm_nt(lr, rkt), 0.0)
            tinv = _tri_inverse(a_ab, masks, eye)
            s_old = s_scr[p]
            x_a = _mm_nt(la, s_old)
            x_r = _mm_nt(lr, s_old)
            u = _mm_hi(tinv, x_a + _mm(a_ak, vst))
            y = x_r + _mm(a_rb, u) + _mm(a_rk, vst)
            y_s[rows, ls] = y[:c_len] + y[c_len:]
            upd = _mm(u.T, rbb) + _mm(vst.T, rkb)
            s_scr[p] = jnp.where(blockdiag, s_old * e_last + upd, 0.0)
        return carry

    lax.fori_loop(0, tt // c_len, chunk_body, 0)

    y = y_s[...]
    mean = _mm_rsplit(y, hsum) * (1.0 / HEAD_DIM)
    yc = y - mean
    var = _mm_rsplit(yc * yc, hsum) * (1.0 / HEAD_DIM)
    yn = yc * lax.rsqrt(var + GN_EPS) * gn_g + gn_b
    out_ref[0] = (yn + bonus) * g

    @pl.when(i == n_i - 1)
    def _():
        sfin_ref[0] = s_scr[...]


def _rwkv(pr3, shift0, s0bd, mu_p, vecs, rk_flat, w2p, a2p, g2p, hsum, tt, c_len):
    b, t, _ = pr3.shape
    ri = jnp.arange(tt)[:, None]
    ci = jnp.arange(tt)[None, :]
    same = (ri // c_len) == (ci // c_len)
    ltri = (same & (ri >= ci)).astype(BF16)
    ball = same.astype(BF16)
    blk = lambda bi, i: (bi, i, 0)
    perb = lambda bi, i: (bi, 0, 0)
    perb4 = lambda bi, i: (bi, 0, 0, 0)
    const = lambda bi, i: (0, 0)
    big = lambda: pltpu.VMEM((tt, R_WIDTH), F32)
    return pl.pallas_call(
        functools.partial(_rwkv_kernel, c_len=c_len, tt=tt),
        grid=(b, t // tt),
        in_specs=[pl.BlockSpec((1, tt, RWKV_PAD), blk), pl.BlockSpec((1, 1, RWKV_PAD), perb),
                  pl.BlockSpec((1, 4, LANES, LANES), perb4),
                  pl.BlockSpec((1, RWKV_PAD), const), pl.BlockSpec((6, R_WIDTH), const),
                  pl.BlockSpec((1, R_WIDTH), const),
                  pl.BlockSpec((LANES, R_WIDTH), const), pl.BlockSpec((LANES, R_WIDTH), const),
                  pl.BlockSpec((2 * LANES, R_WIDTH), const), pl.BlockSpec((R_WIDTH, R_WIDTH), const),
                  pl.BlockSpec((tt, tt), const), pl.BlockSpec((tt, tt), const)],
        out_specs=[pl.BlockSpec((1, tt, R_WIDTH), blk), pl.BlockSpec((1, 4, LANES, LANES), perb4)],
        out_shape=[jax.ShapeDtypeStruct((b, t, R_WIDTH), F32),
                   jax.ShapeDtypeStruct((b, 4, LANES, LANES), F32)],
        scratch_shapes=[pltpu.VMEM((4, LANES, LANES), F32), pltpu.VMEM((1, RWKV_PAD), F32),
                        big(), big(), big(), big(), big(), big(), big(), big(), big()],
        compiler_params=_params(("parallel", "arbitrary")),
        name="rwkv_mix",
    )(pr3, shift0, s0bd, mu_p, vecs, rk_flat, w2p, a2p, g2p, hsum, ltri, ball)


def _post_kernel(attn_ref, rw_ref, x_ref, g_ref, wout_ref, wq_ref, wo_ref, mk_ref, mv_ref, o_ref, *, nb, t):
    m = _mm(attn_ref[...], wout_ref[0:A_WIDTH, :]) + _mm(rw_ref[...], wout_ref[A_WIDTH:, :])
    x1 = x_ref[...] + _rms(m, g_ref[NORM_MIX_POST:NORM_MIX_POST + 1, :])
    q = _mm(_rms(x1, g_ref[NORM_X_PRE:NORM_X_PRE + 1, :]), wq_ref[...])
    scale = MEM_HEAD_DIM ** -0.5
    row_blocks = []
    for bi in range(nb):
        heads = []
        for hd in range(MEM_HEADS):
            ls = slice(hd * MEM_HEAD_DIM, (hd + 1) * MEM_HEAD_DIM)
            s = _mm_nt(q[bi * t:(bi + 1) * t, ls], mk_ref[bi, :, ls]) * scale
            mx = jnp.max(s, axis=-1, keepdims=True)
            e = jnp.exp(s - mx)
            heads.append(_mm(e, mv_ref[bi, :, ls]) / jnp.sum(e, axis=-1, keepdims=True))
        row_blocks.append(jnp.concatenate(heads, axis=1))
    o = row_blocks[0] if nb == 1 else jnp.concatenate(row_blocks, axis=0)
    c = _mm(o, wo_ref[...])
    o_ref[...] = x1 + _rms(c, g_ref[NORM_X_POST:NORM_X_POST + 1, :])


def _post(attn, rw, x2d, gains, w_out, w_mq, w_mo, mk, mv, kcol, vcol, nb, t, tiles_per_batch):
    n = x2d.shape[0]
    tm = nb * t
    row = lambda i: (i, 0)
    const = lambda i: (0, 0)
    if nb == 1:
        kmap = lambda i: (i // tiles_per_batch, 0, kcol)
        vmap = lambda i: (i // tiles_per_batch, 0, vcol)
    else:
        kmap = lambda i: (i, 0, kcol)
        vmap = lambda i: (i, 0, vcol)
    return pl.pallas_call(
        functools.partial(_post_kernel, nb=nb, t=t),
        grid=(n // tm,),
        in_specs=[pl.BlockSpec((tm, A_WIDTH), row), pl.BlockSpec((tm, R_WIDTH), row),
                  pl.BlockSpec((tm, D_MODEL), row), pl.BlockSpec((8, D_MODEL), const),
                  pl.BlockSpec((D_MODEL, D_MODEL), const), pl.BlockSpec((D_MODEL, MEM_WIDTH), const),
                  pl.BlockSpec((MEM_WIDTH, D_MODEL), const),
                  pl.BlockSpec((nb, MEM_LEN, MEM_WIDTH), kmap), pl.BlockSpec((nb, MEM_LEN, MEM_WIDTH), vmap)],
        out_specs=pl.BlockSpec((tm, D_MODEL), row),
        out_shape=jax.ShapeDtypeStruct((n, D_MODEL), F32),
        compiler_params=_params(("parallel",)),
        name="post_mix",
    )(attn, rw, x2d, gains, w_out, w_mq, w_mo, mk, mv)


FFN_TILE = 1408


def _ffn_kernel(x_ref, g_ref, wg_ref, wu_ref, wd_ref, o_ref, h_scr, acc_scr):
    j = pl.program_id(1)

    @pl.when(j == 0)
    def _():
        h_scr[...] = _rms(x_ref[...], g_ref[NORM_FFN_PRE:NORM_FFN_PRE + 1, :]).astype(BF16)
        acc_scr[...] = jnp.zeros_like(acc_scr)

    h = h_scr[...]
    gate = jnp.dot(h, wg_ref[...], preferred_element_type=F32)
    up = jnp.dot(h, wu_ref[...], preferred_element_type=F32)
    act = (gate * jax.nn.sigmoid(gate)) * up
    acc_scr[...] += _mm(act, wd_ref[...])

    @pl.when(j == pl.num_programs(1) - 1)
    def _():
        o_ref[...] = x_ref[...] + _rms(acc_scr[...], g_ref[NORM_FFN_POST:NORM_FFN_POST + 1, :])


def _ffn(x2d, gains, w_gu, w_dn, tm):
    n = x2d.shape[0]
    nf = D_FF // FFN_TILE
    row = lambda i, j: (i, 0)
    const = lambda i, j: (0, 0)
    return pl.pallas_call(
        _ffn_kernel,
        grid=(n // tm, nf),
        in_specs=[pl.BlockSpec((tm, D_MODEL), row), pl.BlockSpec((8, D_MODEL), const),
                  pl.BlockSpec((D_MODEL, FFN_TILE), lambda i, j: (0, j)),
                  pl.BlockSpec((D_MODEL, FFN_TILE), lambda i, j: (0, nf + j)),
                  pl.BlockSpec((FFN_TILE, D_MODEL), lambda i, j: (j, 0))],
        out_specs=pl.BlockSpec((tm, D_MODEL), row),
        out_shape=jax.ShapeDtypeStruct((n, D_MODEL), F32),
        scratch_shapes=[pltpu.VMEM((tm, D_MODEL), BF16), pltpu.VMEM((tm, D_MODEL), F32)],
        compiler_params=_params(("parallel", "arbitrary")),
        name="ffn",
    )(x2d, gains, w_gu, w_gu, w_dn)


def _memkv_kernel(x_ref, g_ref, w_ref, o_ref):
    o_ref[...] = _mm(_rms(x_ref[...], g_ref[NORM_MEM:NORM_MEM + 1, :]), w_ref[...])


def _memkv(mem2d, gains, w_kv):
    n = mem2d.shape[0]
    tm = MEM_LEN
    row = lambda i: (i, 0)
    const = lambda i: (0, 0)
    return pl.pallas_call(
        _memkv_kernel,
        grid=(n // tm,),
        in_specs=[pl.BlockSpec((tm, D_MODEL), row), pl.BlockSpec((8, D_MODEL), const),
                  pl.BlockSpec((D_MODEL, 2 * MEM_WIDTH), const)],
        out_specs=pl.BlockSpec((tm, 2 * MEM_WIDTH), row),
        out_shape=jax.ShapeDtypeStruct((n, 2 * MEM_WIDTH), F32),
        compiler_params=_params(("parallel",)),
        name="mem_kv",
    )(mem2d, gains, w_kv)


def _rope_tables(pos):
    half = HEAD_DIM // 2
    inv = ROPE_THETA ** (-jnp.arange(half, dtype=F32) / half)
    ang = pos.astype(F32)[:, None] * inv[None, :]
    cos = jnp.cos(ang)
    sin = jnp.sin(ang)
    cos_t = jnp.tile(cos, (1, LANES // half))
    sin_t = jnp.tile(jnp.concatenate([-sin, sin], axis=1), (1, LANES // HEAD_DIM))
    return cos_t, sin_t


def _state_to_blockdiag(s):
    b = s.shape[0]
    s5 = s.reshape(b, 4, 2, HEAD_DIM, HEAD_DIM)
    eye2 = jnp.eye(2, dtype=s.dtype)
    bd = s5[:, :, :, :, None, :] * eye2[None, None, :, None, :, None]
    return bd.reshape(b, 4, LANES, LANES)


def _blockdiag_to_state(bd):
    b = bd.shape[0]
    x = bd.reshape(b, 4, 2, HEAD_DIM, 2, HEAD_DIM)
    s = jnp.stack([x[:, :, 0, :, 0, :], x[:, :, 1, :, 1, :]], axis=2)
    return s.reshape(b, N_HEADS_R, HEAD_DIM, HEAD_DIM)


def _layer_weights(l, norm_gains, w_in, attn_sink, shift_mu, rwkv_vecs, rwkv_rk, rwkv_w2, rwkv_a2, rwkv_g2,
                   w_out, w_mem_q, w_mem_kv, w_mem_o, w_gate_up, w_down):
    gains = jnp.concatenate([norm_gains[l], jnp.zeros((1, D_MODEL), F32)], axis=0)
    w_in_p = jnp.pad(w_in[l], ((0, 0), (0, IN_COLS_PAD - w_in.shape[2]))).astype(BF16)
    sink_b = jnp.broadcast_to(attn_sink[l][:, None], (8, LANES)).astype(F32)
    mu_p = jnp.pad(shift_mu[l], (0, RWKV_PAD - RWKV_PROJ))[None, :]
    w2p = jnp.pad(rwkv_w2[l], ((0, LANES - DECAY_LORA), (0, 0))).astype(BF16)
    a2p = jnp.pad(rwkv_a2[l], ((DECAY_LORA, LANES - DECAY_LORA - AAA_LORA), (0, 0))).astype(BF16)
    g2p = jnp.pad(rwkv_g2[l], ((0, 2 * LANES - GATE_LORA), (0, 0))).astype(BF16)
    return dict(gains=gains, g_pre=norm_gains[l, NORM_MIX_PRE][None, :], w_in=w_in_p, sink=sink_b, mu=mu_p,
                vecs=rwkv_vecs[l], rk=rwkv_rk[l].reshape(1, R_WIDTH), w2=w2p, a2=a2p, g2=g2p,
                w_out=w_out[l].astype(BF16), w_mq=w_mem_q[l].astype(BF16), w_mkv=w_mem_kv[l].astype(BF16),
                w_mo=w_mem_o[l].astype(BF16), w_gu=w_gate_up[l].astype(BF16), w_dn=w_down[l].astype(BF16))


def _head_sum_matrix():
    i = jnp.arange(R_WIDTH)
    return ((i[:, None] // HEAD_DIM) == (i[None, :] // HEAD_DIM)).astype(BF16)


def _layer(x2d, b, t, tabs, mk, mv, kcol, vcol, swa_cache, s0bd, shift0, lw, hsum, in_tm, rw_tt, post_nb, post_t,
           ffn_tm):
    q, k, v, pr = _in_proj(x2d, lw["g_pre"], lw["w_in"], tabs[0], tabs[1], in_tm)
    if swa_cache is None:
        attn = _swa_prompt(q, k, v, lw["sink"], b, t)
    else:
        attn = _swa_sample(q, k, v, swa_cache[0], swa_cache[1], lw["sink"], b, t)
    pr3 = pr.reshape(b, t, RWKV_PAD)
    rw, s_fin = _rwkv(pr3, shift0, s0bd, lw["mu"], lw["vecs"], lw["rk"], lw["w2"], lw["a2"], lw["g2"], hsum,
                      rw_tt, min(CHUNK, t))
    x2 = _post(attn, rw.reshape(b * t, R_WIDTH), x2d, lw["gains"], lw["w_out"], lw["w_mq"], lw["w_mo"],
               mk, mv, kcol, vcol, post_nb, post_t, t // post_t)
    x3 = _ffn(x2, lw["gains"], lw["w_gu"], lw["w_dn"], ffn_tm)
    return x3, k, v, s_fin, pr3[:, t - 1:t, :RWKV_PROJ]


def kernel(x_prompt, mem_prompt, x_sample, cache_swa_k, cache_swa_v, cache_mem_k, cache_mem_v, state_rwkv,
           state_shift, norm_gains, w_in, attn_sink, shift_mu, rwkv_vecs, rwkv_rk, rwkv_w2, rwkv_a2, rwkv_g2,
           w_out, w_mem_q, w_mem_kv, w_mem_o, w_gate_up, w_down):
    b, t, _ = x_prompt.shape
    bd, tn, _ = x_sample.shape
    depth = w_in.shape[0]
    m_len = mem_prompt.shape[1]
    cache_len = cache_swa_k.shape[2]
    tabs_p = _rope_tables(jnp.arange(t, dtype=jnp.int32))
    cs, sn = _rope_tables(PAST_LEN + jnp.arange(tn, dtype=jnp.int32))
    tabs_s = (jnp.tile(cs, (bd, 1)), jnp.tile(sn, (bd, 1)))
    hsum = _head_sum_matrix()
    s0_p = jnp.zeros((b, 4, LANES, LANES), F32)
    sh0_p = jnp.zeros((b, 1, RWKV_PAD), F32)
    xp = x_prompt.reshape(b * t, D_MODEL)
    xs = x_sample.reshape(bd * tn, D_MODEL)
    mem2d = mem_prompt.reshape(b * m_len, D_MODEL)
    in_tm_p = min(512, t)
    rw_tt_p = min(256, t)
    post_t_p = min(256, t)
    ffn_tm_p = min(512, b * t)
    pk, pv, pmk, pmv, pS, psh = [], [], [], [], [], []
    sk, sv, sS, ssh = [], [], [], []
    for l in range(depth):
        lw = _layer_weights(l, norm_gains, w_in, attn_sink, shift_mu, rwkv_vecs, rwkv_rk, rwkv_w2, rwkv_a2,
                            rwkv_g2, w_out, w_mem_q, w_mem_kv, w_mem_o, w_gate_up, w_down)
        mkv = _memkv(mem2d, lw["gains"], lw["w_mkv"])
        mkv3 = mkv.reshape(b, m_len, 2 * MEM_WIDTH)
        xp, k_l, v_l, s_l, sh_l = _layer(xp, b, t, tabs_p, mkv3, mkv3, 0, 1, None, s0_p, sh0_p, lw, hsum,
                                         in_tm_p, rw_tt_p, 1, post_t_p, ffn_tm_p)
        keep = t - min(WINDOW, t)
        pk.append(k_l.reshape(b, t, N_KV_A, HEAD_DIM)[:, keep:])
        pv.append(v_l.reshape(b, t, N_KV_A, HEAD_DIM)[:, keep:])
        pmk.append(mkv3[:, :, :MEM_WIDTH].reshape(b, m_len, MEM_HEADS, MEM_HEAD_DIM))
        pmv.append(mkv3[:, :, MEM_WIDTH:].reshape(b, m_len, MEM_HEADS, MEM_HEAD_DIM))
        pS.append(_blockdiag_to_state(s_l))
        psh.append(sh_l)

        cache = (cache_swa_k[l].reshape(bd, cache_len, KV_WIDTH_A), cache_swa_v[l].reshape(bd, cache_len, KV_WIDTH_A))
        s0_s = _state_to_blockdiag(state_rwkv[l].astype(F32))
        sh0_s = jnp.pad(state_shift[l], ((0, 0), (0, 0), (0, RWKV_PAD - RWKV_PROJ)))
        cmk = cache_mem_k[l].reshape(bd, m_len, MEM_WIDTH)
        cmv = cache_mem_v[l].reshape(bd, m_len, MEM_WIDTH)
        xs, k2, v2, s2, sh2 = _layer(xs, bd, tn, tabs_s, cmk, cmv, 0, 0, cache, s0_s, sh0_s, lw, hsum,
                                     bd * tn, tn, bd, tn, bd * tn)
        sk.append(k2.reshape(bd, tn, N_KV_A, HEAD_DIM))
        sv.append(v2.reshape(bd, tn, N_KV_A, HEAD_DIM))
        sS.append(_blockdiag_to_state(s2).astype(state_rwkv.dtype))
        ssh.append(sh2)
    return (xp.reshape(b, t, D_MODEL), xs.reshape(bd, tn, D_MODEL), jnp.stack(pk), jnp.stack(pv),
            jnp.stack(pmk), jnp.stack(pmv), jnp.stack(pS), jnp.stack(psh),
            jnp.stack(sk), jnp.stack(sv), jnp.stack(sS), jnp.stack(ssh))
```

```python
import functools
import math

import jax
import jax.numpy as jnp
from jax import lax
from jax.experimental import pallas as pl
from jax.experimental.pallas import tpu as pltpu

F32 = jnp.float32
BF16 = jnp.bfloat16

D_MODEL = 1024
HEAD_DIM = 64
CHUNK = 64
A_WIDTH = 512
KV_WIDTH_A = 128
N_KV_A = 2
WINDOW = 128
PAST_LEN = 4096
ROPE_THETA = 10000.0
R_WIDTH = 512
N_HEADS_R = 8
DECAY_LORA = 64
AAA_LORA = 64
GATE_LORA = 160
RWKV_PROJ = 3 * R_WIDTH + DECAY_LORA + AAA_LORA + GATE_LORA
RWKV_PAD = 1920
IN_COLS_PAD = A_WIDTH + 2 * KV_WIDTH_A + RWKV_PAD
GN_EPS = 6.4e-4
MEM_LEN = 256
MEM_HEADS = 4
MEM_HEAD_DIM = 128
MEM_WIDTH = 512
D_FF = 2816
RMS_EPS = 1e-6
NEG_INF = -1e30
NORM_MIX_PRE, NORM_MIX_POST, NORM_X_PRE, NORM_X_POST, NORM_MEM, NORM_FFN_PRE, NORM_FFN_POST = range(7)
EXP_M05 = math.exp(-0.5)

LANES = 128
VMEM_LIMIT = 56 * 1024 * 1024


def _params(sem):
    return pltpu.CompilerParams(dimension_semantics=sem, vmem_limit_bytes=VMEM_LIMIT)


def _rms(x, g):
    ms = jnp.mean(x * x, axis=-1, keepdims=True)
    return x * lax.rsqrt(ms + RMS_EPS) * g


def _mm(a, b):
    return jnp.dot(a.astype(BF16), b.astype(BF16), preferred_element_type=F32)


def _mm_nt(a, b):
    return lax.dot_general(a.astype(BF16), b.astype(BF16), (((1,), (1,)), ((), ())),
                           preferred_element_type=F32)


def _split(x):
    hi = x.astype(BF16)
    lo = (x - hi.astype(F32)).astype(BF16)
    return hi, lo


def _mm_lsplit(a_exact_bf16, x):
    hi, lo = _split(x)
    return (jnp.dot(a_exact_bf16, hi, preferred_element_type=F32)
            + jnp.dot(a_exact_bf16, lo, preferred_element_type=F32))


def _mm_rsplit(x, b_exact_bf16):
    hi, lo = _split(x)
    return (jnp.dot(hi, b_exact_bf16, preferred_element_type=F32)
            + jnp.dot(lo, b_exact_bf16, preferred_element_type=F32))


def _in_kernel(x_ref, g_ref, w_ref, cos_ref, sin_ref, q_ref, k_ref, v_ref, pr_ref):
    h = _rms(x_ref[...], g_ref[...]).astype(BF16)
    p = jnp.dot(h, w_ref[...], preferred_element_type=F32)
    cos = cos_ref[...]
    sin = sin_ref[...]
    lane = lax.broadcasted_iota(jnp.int32, cos.shape, 1)
    first_half = (lane & (HEAD_DIM // 2)) == 0

    def rope(xc):
        sw = jnp.where(first_half, pltpu.roll(xc, LANES - HEAD_DIM // 2, 1), pltpu.roll(xc, HEAD_DIM // 2, 1))
        return xc * cos + sw * sin

    for j in range(A_WIDTH // LANES):
        q_ref[:, j * LANES:(j + 1) * LANES] = rope(p[:, j * LANES:(j + 1) * LANES])
    k_ref[...] = rope(p[:, A_WIDTH:A_WIDTH + KV_WIDTH_A])
    v_ref[...] = p[:, A_WIDTH + KV_WIDTH_A:A_WIDTH + 2 * KV_WIDTH_A]
    pr_ref[...] = p[:, A_WIDTH + 2 * KV_WIDTH_A:]


def _in_proj(x2d, g, w_in_p, cos_t, sin_t, tm):
    n = x2d.shape[0]
    tab_blocks = cos_t.shape[0] // tm
    row = lambda i: (i, 0)
    const = lambda i: (0, 0)
    tab = lambda i: (i % tab_blocks, 0)
    return pl.pallas_call(
        _in_kernel,
        grid=(n // tm,),
        in_specs=[pl.BlockSpec((tm, D_MODEL), row), pl.BlockSpec((1, D_MODEL), const),
                  pl.BlockSpec((D_MODEL, IN_COLS_PAD), const),
                  pl.BlockSpec((tm, LANES), tab), pl.BlockSpec((tm, LANES), tab)],
        out_specs=[pl.BlockSpec((tm, A_WIDTH), row), pl.BlockSpec((tm, KV_WIDTH_A), row),
                   pl.BlockSpec((tm, KV_WIDTH_A), row), pl.BlockSpec((tm, RWKV_PAD), row)],
        out_shape=[jax.ShapeDtypeStruct((n, A_WIDTH), F32), jax.ShapeDtypeStruct((n, KV_WIDTH_A), F32),
                   jax.ShapeDtypeStruct((n, KV_WIDTH_A), F32), jax.ShapeDtypeStruct((n, RWKV_PAD), F32)],
        compiler_params=_params(("parallel",)),
        name="in_proj",
    )(x2d, g, w_in_p, cos_t, sin_t)


def _sink_attend(q_rows, kdup, vdup, sink_ref, kv, valid, t):
    lane = lax.broadcasted_iota(jnp.int32, (t, LANES), 1)
    m0 = lane < HEAD_DIM
    parts = []
    for p in (2 * kv, 2 * kv + 1):
        qp = q_rows[:, p * LANES:(p + 1) * LANES]
        parts.append(jnp.where(m0, qp, 0.0))
        parts.append(jnp.where(m0, 0.0, qp))
    lhs = jnp.concatenate(parts, axis=0)
    s = _mm_nt(lhs, kdup) * (HEAD_DIM ** -0.5)
    if valid is not None:
        s = jnp.where(valid, s, NEG_INF)
    rowi = lax.broadcasted_iota(jnp.int32, (4 * t, 1), 0)
    sk = jnp.zeros((4 * t, 1), F32)
    for g in range(4):
        sg = sink_ref[4 * kv + g:4 * kv + g + 1, 0:1]
        sk = jnp.where((rowi >= g * t) & (rowi < (g + 1) * t), sg, sk)
    m = jnp.maximum(jnp.max(s, axis=-1, keepdims=True), sk)
    e = jnp.exp(s - m)
    den = jnp.sum(e, axis=-1, keepdims=True) + jnp.exp(sk - m)
    o = _mm(e, vdup) / den
    return [jnp.where(m0, o[(2 * pi) * t:(2 * pi + 1) * t], o[(2 * pi + 1) * t:(2 * pi + 2) * t])
            for pi in range(2)]


def _dup_heads(x):
    lane = lax.broadcasted_iota(jnp.int32, x.shape, 1)
    m0 = lane < HEAD_DIM
    xs = pltpu.roll(x, HEAD_DIM, 1)
    return [jnp.where(m0, x, xs), jnp.where(m0, xs, x)]


SWA_TQ = 256


def _swa_prompt_kernel(q_ref, kp_ref, kc_ref, vp_ref, vc_ref, sink_ref, o_ref):
    i = pl.program_id(1)
    k = jnp.concatenate([kp_ref[...], kc_ref[...]], axis=0)
    v = jnp.concatenate([vp_ref[...], vc_ref[...]], axis=0)
    kd = _dup_heads(k)
    vd = _dup_heads(v)
    nk = 3 * CHUNK
    slot = lax.broadcasted_iota(jnp.int32, (1, nk), 1) // CHUNK
    for j in range(SWA_TQ // CHUNK):
        qj = q_ref[j * CHUNK:(j + 1) * CHUNK, :]
        valid = (slot + (i * (SWA_TQ // CHUNK) + j - 2)) >= 0
        for kv in range(N_KV_A):
            outs = _sink_attend(qj, kd[kv][j * CHUNK:j * CHUNK + nk], vd[kv][j * CHUNK:j * CHUNK + nk],
                                sink_ref, kv, valid, CHUNK)
            for pi in range(2):
                p = 2 * kv + pi
                o_ref[j * CHUNK:(j + 1) * CHUNK, p * LANES:(p + 1) * LANES] = outs[pi]


def _swa_prompt(q, k, v, sink_b, b, t):
    n = b * t
    nq = t // SWA_TQ
    qmap = lambda bi, i: (bi * nq + i, 0)
    pmap = lambda bi, i: (jnp.maximum(bi * (t // WINDOW) + 2 * i - 1, 0), 0)
    const = lambda bi, i: (0, 0)
    return pl.pallas_call(
        _swa_prompt_kernel,
        grid=(b, nq),
        in_specs=[pl.BlockSpec((SWA_TQ, A_WIDTH), qmap),
                  pl.BlockSpec((WINDOW, KV_WIDTH_A), pmap), pl.BlockSpec((SWA_TQ, KV_WIDTH_A), qmap),
                  pl.BlockSpec((WINDOW, KV_WIDTH_A), pmap), pl.BlockSpec((SWA_TQ, KV_WIDTH_A), qmap),
                  pl.BlockSpec((8, LANES), const)],
        out_specs=pl.BlockSpec((SWA_TQ, A_WIDTH), qmap),
        out_shape=jax.ShapeDtypeStruct((n, A_WIDTH), F32),
        compiler_params=_params(("parallel", "parallel")),
        name="swa_prompt",
    )(q, k, k, v, v, sink_b)


def _swa_sample_kernel(q_ref, kc_ref, kn_ref, vc_ref, vn_ref, sink_ref, o_ref, *, t):
    k = jnp.concatenate([kc_ref[0], kn_ref[...]], axis=0)
    v = jnp.concatenate([vc_ref[0], vn_ref[...]], axis=0)
    kd = _dup_heads(k)
    vd = _dup_heads(v)
    q = q_ref[...]
    for kv in range(N_KV_A):
        outs = _sink_attend(q, kd[kv], vd[kv], sink_ref, kv, None, t)
        for pi in range(2):
            p = 2 * kv + pi
            o_ref[:, p * LANES:(p + 1) * LANES] = outs[pi]


def _swa_sample(q, k, v, kc, vc, sink_b, b, t):
    n = b * t
    cache = kc.shape[1]
    row = lambda bi: (bi, 0)
    cmap = lambda bi: (bi, 0, 0)
    const = lambda bi: (0, 0)
    return pl.pallas_call(
        functools.partial(_swa_sample_kernel, t=t),
        grid=(b,),
        in_specs=[pl.BlockSpec((t, A_WIDTH), row),
                  pl.BlockSpec((1, cache, KV_WIDTH_A), cmap), pl.BlockSpec((t, KV_WIDTH_A), row),
                  pl.BlockSpec((1, cache, KV_WIDTH_A), cmap), pl.BlockSpec((t, KV_WIDTH_A), row),
                  pl.BlockSpec((8, LANES), const)],
        out_specs=pl.BlockSpec((t, A_WIDTH), row),
        out_shape=jax.ShapeDtypeStruct((n, A_WIDTH), F32),
        compiler_params=_params(("parallel",)),
        name="swa_sample",
    )(q, kc, k, vc, v, sink_b)


def _tri_inverse(nmat, masks, eye):
    p = eye + jnp.where(masks[0], nmat, 0.0)
    for m in masks[1:]:
        e = jnp.where(m, nmat, 0.0)
        p = p + _mm(_mm(p, e), p)
    return p


def _rwkv_kernel(pr_ref, sh0_ref, s0_ref, mu_ref, vec_ref, rk_ref, w2_ref, a2_ref, g2_ref, hsum_ref,
                 ltri_ref, ball_ref, out_ref, sfin_ref,
                 s_scr, carry_scr, rt_s, at_s, bt_s, kt_s, bb_s, kb_s, v_s, et_s, y_s, *, c_len, tt, nb):
    i = pl.program_id(1)
    n_i = pl.num_programs(1)
    n_rows = nb * tt

    @pl.when(i == 0)
    def _():
        s_scr[...] = s0_ref[...]
        carry_scr[...] = sh0_ref[...]

    rowi = lax.broadcasted_iota(jnp.int32, (n_rows, 1), 0)

    def shifted(a, b):
        cur = pr_ref[:, :, a:b].reshape(n_rows, b - a)
        prev = pltpu.roll(cur, 1, 0)
        for bi in range(nb):
            prev = jnp.where(rowi == bi * tt, carry_scr[bi, :, a:b], prev)
        return cur + (prev - cur) * mu_ref[:, a:b]

    r = shifted(0, 512)
    k = shifted(512, 1024)
    v = shifted(1024, 1536)
    xwa = shifted(1536, 1664)
    xg = shifted(1664, RWKV_PAD)
    for bi in range(nb):
        carry_scr[bi] = pr_ref[bi, tt - 1:tt, :]

    w0 = vec_ref[0:1, :]
    a0 = vec_ref[1:2, :]
    k_k = vec_ref[2:3, :]
    k_a = vec_ref[3:4, :]
    gn_g = vec_ref[4:5, :]
    gn_b = vec_ref[5:6, :]
    hsum = hsum_ref[...]

    z = w0 + _mm(jnp.tanh(xwa), w2_ref[...])
    wlog = -EXP_M05 * jax.nn.sigmoid(z)
    a = jax.nn.sigmoid(a0 + _mm(xwa, a2_ref[...]))
    g = _mm(jax.nn.sigmoid(xg), g2_ref[...])
    kk = k * k_k
    kk = kk * lax.rsqrt(jnp.maximum(_mm_rsplit(kk * kk, hsum), 1e-24))
    k_f = k * (1.0 + (a - 1.0) * k_a)
    bvec = kk * a
    bonus = _mm_rsplit(r * k_f * rk_ref[...], hsum) * v

    cum = _mm_lsplit(ltri_ref[...], wlog)
    tot = _mm_lsplit(ball_ref[...], wlog)
    e_in = jnp.exp(cum)
    e_inv = jnp.exp(-cum)
    e_end = jnp.exp(tot - cum)
    rt_s[...] = r * e_in
    at_s[...] = -kk * jnp.exp(cum - wlog)
    bt_s[...] = bvec * e_inv
    kt_s[...] = k_f * e_inv
    bb_s[...] = bvec * e_end
    kb_s[...] = k_f * e_end
    v_s[...] = v
    et_s[...] = jnp.exp(tot)

    n2 = 2 * c_len
    ri = lax.broadcasted_iota(jnp.int32, (n2, n2), 0)
    ci = lax.broadcasted_iota(jnp.int32, (n2, n2), 1)
    same_head = (ri >= c_len) == (ci >= c_len)
    strict = same_head & (ri > ci)
    incl = same_head & (ri >= ci)
    eye = jnp.where(ri == ci, 1.0, 0.0).astype(F32)
    masks = []
    half = 1
    while half < c_len:
        blk = 2 * half
        masks.append(((ri & ~(blk - 1)) == (ci & ~(blk - 1))) & ((ri & half) != 0) & ((ci & half) == 0))
        half = blk
    lane_c = lax.broadcasted_iota(jnp.int32, (c_len, LANES), 1)
    m0 = lane_c < HEAD_DIM
    fused = n2 == LANES
    zeros_sq = jnp.zeros((n2, LANES), F32)

    def stack(x):
        return jnp.concatenate([jnp.where(m0, x, 0.0), jnp.where(m0, 0.0, x)], axis=0)

    chains = [(bi, c, p) for bi in range(nb) for c in range(tt // c_len) for p in range(4)]

    def rows_lanes(ch):
        bi, c, p = ch
        r0 = bi * tt + c * c_len
        return slice(r0, r0 + c_len), slice(p * LANES, (p + 1) * LANES)

    def operand(ref, ch):
        rows, ls = rows_lanes(ch)
        return stack(ref[rows, ls])

    a_ab, a_ak, a_rb, a_rk = {}, {}, {}, {}
    for ch in chains:
        la, lr, rb, rkt = operand(at_s, ch), operand(rt_s, ch), operand(bt_s, ch), operand(kt_s, ch)
        if fused:
            amat = _mm_nt(jnp.concatenate([la, lr], axis=0), jnp.concatenate([rb, rkt], axis=0))
            q_ab, q_ak, q_rb, q_rk = amat[:n2, :n2], amat[:n2, n2:], amat[n2:, :n2], amat[n2:, n2:]
        else:
            q_ab, q_ak, q_rb, q_rk = _mm_nt(la, rb), _mm_nt(la, rkt), _mm_nt(lr, rb), _mm_nt(lr, rkt)
        a_ab[ch] = jnp.where(strict, q_ab, 0.0)
        a_ak[ch] = jnp.where(strict, q_ak, 0.0)
        a_rb[ch] = jnp.where(incl, q_rb, 0.0)
        a_rk[ch] = jnp.where(incl, q_rk, 0.0)

    tinv = {ch: eye + jnp.where(masks[0], a_ab[ch], 0.0) for ch in chains}
    for m in masks[1:]:
        pe = {ch: _mm(tinv[ch], jnp.where(m, a_ab[ch], 0.0)) for ch in chains}
        tinv = {ch: tinv[ch] + _mm(pe[ch], tinv[ch]) for ch in chains}

    zv = {ch: _mm(a_ak[ch], operand(v_s, ch)) for ch in chains}
    r_hat, y_hat, m_mat, g_mat = {}, {}, {}, {}
    if fused:
        w = {ch: _mm(tinv[ch], jnp.concatenate([operand(at_s, ch), zv[ch]], axis=1)) for ch in chains}
        rhs2 = {ch: jnp.concatenate([w[ch], jnp.concatenate([zeros_sq, operand(v_s, ch)], axis=1)], axis=0)
                for ch in chains}
        for ch in chains:
            ry = _mm(jnp.concatenate([a_rb[ch], a_rk[ch]], axis=1), rhs2[ch])
            r_hat[ch] = operand(rt_s, ch) + ry[:, :LANES]
            y_hat[ch] = ry[:, LANES:]
        for ch in chains:
            mg = _mm(rhs2[ch].T, jnp.concatenate([operand(bb_s, ch), operand(kb_s, ch)], axis=0))
            m_mat[ch] = mg[:LANES]
            g_mat[ch] = mg[LANES:]
    else:
        a_hat = {ch: _mm(tinv[ch], operand(at_s, ch)) for ch in chains}
        u_hat = {ch: _mm(tinv[ch], zv[ch]) for ch in chains}
        for ch in chains:
            r_hat[ch] = operand(rt_s, ch) + _mm(a_rb[ch], a_hat[ch])
            y_hat[ch] = _mm(a_rb[ch], u_hat[ch]) + _mm(a_rk[ch], operand(v_s, ch))
        for ch in chains:
            m_mat[ch] = _mm(a_hat[ch].T, operand(bb_s, ch))
            g_mat[ch] = _mm(u_hat[ch].T, operand(bb_s, ch)) + _mm(operand(v_s, ch).T, operand(kb_s, ch))

    for bi in range(nb):
        s_cur = [s_scr[bi, p] for p in range(4)]
        for c in range(tt // c_len):
            for p in range(4):
                ch = (bi, c, p)
                rows, ls = rows_lanes(ch)
                e_last = et_s[rows.start:rows.start + 1, ls]
                yst = y_hat[ch] + _mm_nt(r_hat[ch], s_cur[p])
                y_s[rows, ls] = yst[:c_len] + yst[c_len:]
                s_cur[p] = s_cur[p] * e_last + _mm(s_cur[p], m_mat[ch]) + g_mat[ch]
        for p in range(4):
            s_scr[bi, p] = s_cur[p]

    y = y_s[...]
    mean = _mm_rsplit(y, hsum) * (1.0 / HEAD_DIM)
    yc = y - mean
    var = _mm_rsplit(yc * yc, hsum) * (1.0 / HEAD_DIM)
    yn = yc * lax.rsqrt(var + GN_EPS) * gn_g + gn_b
    res = (yn + bonus) * g
    for bi in range(nb):
        out_ref[bi] = res[bi * tt:(bi + 1) * tt]

    @pl.when(i == n_i - 1)
    def _():
        sfin_ref[...] = s_scr[...]


def _rwkv(pr3, shift0, s0bd, mu_p, vecs, rk_flat, w2p, a2p, g2p, hsum, tt, c_len, nb):
    b, t, _ = pr3.shape
    n_rows = nb * tt
    ri = jnp.arange(n_rows)[:, None]
    ci = jnp.arange(n_rows)[None, :]
    same = (ri // c_len) == (ci // c_len)
    ltri = (same & (ri >= ci)).astype(BF16)
    ball = same.astype(BF16)
    blk = lambda bi, i: (bi, i, 0)
    perb = lambda bi, i: (bi, 0, 0)
    perb4 = lambda bi, i: (bi, 0, 0, 0)
    const = lambda bi, i: (0, 0)
    big = lambda: pltpu.VMEM((n_rows, R_WIDTH), F32)
    return pl.pallas_call(
        functools.partial(_rwkv_kernel, c_len=c_len, tt=tt, nb=nb),
        grid=(b // nb, t // tt),
        in_specs=[pl.BlockSpec((nb, tt, RWKV_PAD), blk), pl.BlockSpec((nb, 1, RWKV_PAD), perb),
                  pl.BlockSpec((nb, 4, LANES, LANES), perb4),
                  pl.BlockSpec((1, RWKV_PAD), const), pl.BlockSpec((6, R_WIDTH), const),
                  pl.BlockSpec((1, R_WIDTH), const),
                  pl.BlockSpec((LANES, R_WIDTH), const), pl.BlockSpec((LANES, R_WIDTH), const),
                  pl.BlockSpec((2 * LANES, R_WIDTH), const), pl.BlockSpec((R_WIDTH, R_WIDTH), const),
                  pl.BlockSpec((n_rows, n_rows), const), pl.BlockSpec((n_rows, n_rows), const)],
        out_specs=[pl.BlockSpec((nb, tt, R_WIDTH), blk), pl.BlockSpec((nb, 4, LANES, LANES), perb4)],
        out_shape=[jax.ShapeDtypeStruct((b, t, R_WIDTH), F32),
                   jax.ShapeDtypeStruct((b, 4, LANES, LANES), F32)],
        scratch_shapes=[pltpu.VMEM((nb, 4, LANES, LANES), F32), pltpu.VMEM((nb, 1, RWKV_PAD), F32),
                        big(), big(), big(), big(), big(), big(), big(), big(), big()],
        compiler_params=_params(("parallel", "arbitrary")),
        name="rwkv_mix",
    )(pr3, shift0, s0bd, mu_p, vecs, rk_flat, w2p, a2p, g2p, hsum, ltri, ball)


def _post_kernel(attn_ref, rw_ref, x_ref, g_ref, wout_ref, wq_ref, wo_ref, mk_ref, mv_ref, o_ref, *, nb, t):
    m = _mm(attn_ref[...], wout_ref[0:A_WIDTH, :]) + _mm(rw_ref[...], wout_ref[A_WIDTH:, :])
    x1 = x_ref[...] + _rms(m, g_ref[NORM_MIX_POST:NORM_MIX_POST + 1, :])
    q = _mm(_rms(x1, g_ref[NORM_X_PRE:NORM_X_PRE + 1, :]), wq_ref[...])
    scale = MEM_HEAD_DIM ** -0.5
    row_blocks = []
    for bi in range(nb):
        heads = []
        for hd in range(MEM_HEADS):
            ls = slice(hd * MEM_HEAD_DIM, (hd + 1) * MEM_HEAD_DIM)
            s = _mm_nt(q[bi * t:(bi + 1) * t, ls], mk_ref[bi, :, ls]) * scale
            mx = jnp.max(s, axis=-1, keepdims=True)
            e = jnp.exp(s - mx)
            heads.append(_mm(e, mv_ref[bi, :, ls]) / jnp.sum(e, axis=-1, keepdims=True))
        row_blocks.append(jnp.concatenate(heads, axis=1))
    o = row_blocks[0] if nb == 1 else jnp.concatenate(row_blocks, axis=0)
    c = _mm(o, wo_ref[...])
    o_ref[...] = x1 + _rms(c, g_ref[NORM_X_POST:NORM_X_POST + 1, :])


def _post(attn, rw, x2d, gains, w_out, w_mq, w_mo, mk, mv, kcol, vcol, nb, t, tiles_per_batch):
    n = x2d.shape[0]
    tm = nb * t
    row = lambda i: (i, 0)
    const = lambda i: (0, 0)
    if nb == 1:
        kmap = lambda i: (i // tiles_per_batch, 0, kcol)
        vmap = lambda i: (i // tiles_per_batch, 0, vcol)
    else:
        kmap = lambda i: (i, 0, kcol)
        vmap = lambda i: (i, 0, vcol)
    return pl.pallas_call(
        functools.partial(_post_kernel, nb=nb, t=t),
        grid=(n // tm,),
        in_specs=[pl.BlockSpec((tm, A_WIDTH), row), pl.BlockSpec((tm, R_WIDTH), row),
                  pl.BlockSpec((tm, D_MODEL), row), pl.BlockSpec((8, D_MODEL), const),
                  pl.BlockSpec((D_MODEL, D_MODEL), const), pl.BlockSpec((D_MODEL, MEM_WIDTH), const),
                  pl.BlockSpec((MEM_WIDTH, D_MODEL), const),
                  pl.BlockSpec((nb, MEM_LEN, MEM_WIDTH), kmap), pl.BlockSpec((nb, MEM_LEN, MEM_WIDTH), vmap)],
        out_specs=pl.BlockSpec((tm, D_MODEL), row),
        out_shape=jax.ShapeDtypeStruct((n, D_MODEL), F32),
        compiler_params=_params(("parallel",)),
        name="post_mix",
    )(attn, rw, x2d, gains, w_out, w_mq, w_mo, mk, mv)


FFN_TILE = 1408


def _ffn_kernel(x_ref, g_ref, wg_ref, wu_ref, wd_ref, o_ref, h_scr, acc_scr):
    j = pl.program_id(1)

    @pl.when(j == 0)
    def _():
        h_scr[...] = _rms(x_ref[...], g_ref[NORM_FFN_PRE:NORM_FFN_PRE + 1, :]).astype(BF16)
        acc_scr[...] = jnp.zeros_like(acc_scr)

    h = h_scr[...]
    gate = jnp.dot(h, wg_ref[...], preferred_element_type=F32)
    up = jnp.dot(h, wu_ref[...], preferred_element_type=F32)
    act = (gate * jax.nn.sigmoid(gate)) * up
    acc_scr[...] += _mm(act, wd_ref[...])

    @pl.when(j == pl.num_programs(1) - 1)
    def _():
        o_ref[...] = x_ref[...] + _rms(acc_scr[...], g_ref[NORM_FFN_POST:NORM_FFN_POST + 1, :])


def _ffn(x2d, gains, w_gu, w_dn, tm):
    n = x2d.shape[0]
    nf = D_FF // FFN_TILE
    row = lambda i, j: (i, 0)
    const = lambda i, j: (0, 0)
    return pl.pallas_call(
        _ffn_kernel,
        grid=(n // tm, nf),
        in_specs=[pl.BlockSpec((tm, D_MODEL), row), pl.BlockSpec((8, D_MODEL), const),
                  pl.BlockSpec((D_MODEL, FFN_TILE), lambda i, j: (0, j)),
                  pl.BlockSpec((D_MODEL, FFN_TILE), lambda i, j: (0, nf + j)),
                  pl.BlockSpec((FFN_TILE, D_MODEL), lambda i, j: (j, 0))],
        out_specs=pl.BlockSpec((tm, D_MODEL), row),
        out_shape=jax.ShapeDtypeStruct((n, D_MODEL), F32),
        scratch_shapes=[pltpu.VMEM((tm, D_MODEL), BF16), pltpu.VMEM((tm, D_MODEL), F32)],
        compiler_params=_params(("parallel", "arbitrary")),
        name="ffn",
    )(x2d, gains, w_gu, w_gu, w_dn)


def _memkv_kernel(x_ref, g_ref, w_ref, o_ref):
    o_ref[...] = _mm(_rms(x_ref[...], g_ref[NORM_MEM:NORM_MEM + 1, :]), w_ref[...])


def _memkv(mem2d, gains, w_kv):
    n = mem2d.shape[0]
    tm = MEM_LEN
    row = lambda i: (i, 0)
    const = lambda i: (0, 0)
    return pl.pallas_call(
        _memkv_kernel,
        grid=(n // tm,),
        in_specs=[pl.BlockSpec((tm, D_MODEL), row), pl.BlockSpec((8, D_MODEL), const),
                  pl.BlockSpec((D_MODEL, 2 * MEM_WIDTH), const)],
        out_specs=pl.BlockSpec((tm, 2 * MEM_WIDTH), row),
        out_shape=jax.ShapeDtypeStruct((n, 2 * MEM_WIDTH), F32),
        compiler_params=_params(("parallel",)),
        name="mem_kv",
    )(mem2d, gains, w_kv)


def _rope_tables(pos):
    half = HEAD_DIM // 2
    inv = ROPE_THETA ** (-jnp.arange(half, dtype=F32) / half)
    ang = pos.astype(F32)[:, None] * inv[None, :]
    cos = jnp.cos(ang)
    sin = jnp.sin(ang)
    cos_t = jnp.tile(cos, (1, LANES // half))
    sin_t = jnp.tile(jnp.concatenate([-sin, sin], axis=1), (1, LANES // HEAD_DIM))
    return cos_t, sin_t


def _state_to_blockdiag(s):
    b = s.shape[0]
    s5 = s.reshape(b, 4, 2, HEAD_DIM, HEAD_DIM)
    eye2 = jnp.eye(2, dtype=s.dtype)
    bd = s5[:, :, :, :, None, :] * eye2[None, None, :, None, :, None]
    return bd.reshape(b, 4, LANES, LANES)


def _blockdiag_to_state(bd):
    b = bd.shape[0]
    x = bd.reshape(b, 4, 2, HEAD_DIM, 2, HEAD_DIM)
    s = jnp.stack([x[:, :, 0, :, 0, :], x[:, :, 1, :, 1, :]], axis=2)
    return s.reshape(b, N_HEADS_R, HEAD_DIM, HEAD_DIM)


def _layer_weights(l, norm_gains, w_in, attn_sink, shift_mu, rwkv_vecs, rwkv_rk, rwkv_w2, rwkv_a2, rwkv_g2,
                   w_out, w_mem_q, w_mem_kv, w_mem_o, w_gate_up, w_down):
    gains = jnp.concatenate([norm_gains[l], jnp.zeros((1, D_MODEL), F32)], axis=0)
    w_in_p = jnp.pad(w_in[l], ((0, 0), (0, IN_COLS_PAD - w_in.shape[2]))).astype(BF16)
    sink_b = jnp.broadcast_to(attn_sink[l][:, None], (8, LANES)).astype(F32)
    mu_p = jnp.pad(shift_mu[l], (0, RWKV_PAD - RWKV_PROJ))[None, :]
    w2p = jnp.pad(rwkv_w2[l], ((0, LANES - DECAY_LORA), (0, 0))).astype(BF16)
    a2p = jnp.pad(rwkv_a2[l], ((DECAY_LORA, LANES - DECAY_LORA - AAA_LORA), (0, 0))).astype(BF16)
    g2p = jnp.pad(rwkv_g2[l], ((0, 2 * LANES - GATE_LORA), (0, 0))).astype(BF16)
    return dict(gains=gains, g_pre=norm_gains[l, NORM_MIX_PRE][None, :], w_in=w_in_p, sink=sink_b, mu=mu_p,
                vecs=rwkv_vecs[l], rk=rwkv_rk[l].reshape(1, R_WIDTH), w2=w2p, a2=a2p, g2=g2p,
                w_out=w_out[l].astype(BF16), w_mq=w_mem_q[l].astype(BF16), w_mkv=w_mem_kv[l].astype(BF16),
                w_mo=w_mem_o[l].astype(BF16), w_gu=w_gate_up[l].astype(BF16), w_dn=w_down[l].astype(BF16))


def _head_sum_matrix():
    i = jnp.arange(R_WIDTH)
    return ((i[:, None] // HEAD_DIM) == (i[None, :] // HEAD_DIM)).astype(BF16)


def _layer(x2d, b, t, tabs, mk, mv, kcol, vcol, swa_cache, s0bd, shift0, lw, hsum, in_tm, rw_tt, post_nb, post_t,
           ffn_tm):
    q, k, v, pr = _in_proj(x2d, lw["g_pre"], lw["w_in"], tabs[0], tabs[1], in_tm)
    if swa_cache is None:
        attn = _swa_prompt(q, k, v, lw["sink"], b, t)
    else:
        attn = _swa_sample(q, k, v, swa_cache[0], swa_cache[1], lw["sink"], b, t)
    pr3 = pr.reshape(b, t, RWKV_PAD)
    rw, s_fin = _rwkv(pr3, shift0, s0bd, lw["mu"], lw["vecs"], lw["rk"], lw["w2"], lw["a2"], lw["g2"], hsum,
                      rw_tt[0], min(CHUNK, t), rw_tt[1])
    x2 = _post(attn, rw.reshape(b * t, R_WIDTH), x2d, lw["gains"], lw["w_out"], lw["w_mq"], lw["w_mo"],
               mk, mv, kcol, vcol, post_nb, post_t, t // post_t)
    x3 = _ffn(x2, lw["gains"], lw["w_gu"], lw["w_dn"], ffn_tm)
    return x3, k, v, s_fin, pr3[:, t - 1:t, :RWKV_PROJ]


def kernel(x_prompt, mem_prompt, x_sample, cache_swa_k, cache_swa_v, cache_mem_k, cache_mem_v, state_rwkv,
           state_shift, norm_gains, w_in, attn_sink, shift_mu, rwkv_vecs, rwkv_rk, rwkv_w2, rwkv_a2, rwkv_g2,
           w_out, w_mem_q, w_mem_kv, w_mem_o, w_gate_up, w_down):
    b, t, _ = x_prompt.shape
    bd, tn, _ = x_sample.shape
    depth = w_in.shape[0]
    m_len = mem_prompt.shape[1]
    cache_len = cache_swa_k.shape[2]
    tabs_p = _rope_tables(jnp.arange(t, dtype=jnp.int32))
    cs, sn = _rope_tables(PAST_LEN + jnp.arange(tn, dtype=jnp.int32))
    tabs_s = (jnp.tile(cs, (bd, 1)), jnp.tile(sn, (bd, 1)))
    hsum = _head_sum_matrix()
    s0_p = jnp.zeros((b, 4, LANES, LANES), F32)
    sh0_p = jnp.zeros((b, 1, RWKV_PAD), F32)
    xp = x_prompt.reshape(b * t, D_MODEL)
    xs = x_sample.reshape(bd * tn, D_MODEL)
    mem2d = mem_prompt.reshape(b * m_len, D_MODEL)
    in_tm_p = min(512, t)
    rw_tt_p = (min(256, t), 1)
    rw_tt_s = (tn, 4 if bd % 4 == 0 else 1)
    post_t_p = min(256, t)
    ffn_tm_p = min(512, b * t)
    pk, pv, pmk, pmv, pS, psh = [], [], [], [], [], []
    sk, sv, sS, ssh = [], [], [], []
    for l in range(depth):
        lw = _layer_weights(l, norm_gains, w_in, attn_sink, shift_mu, rwkv_vecs, rwkv_rk, rwkv_w2, rwkv_a2,
                            rwkv_g2, w_out, w_mem_q, w_mem_kv, w_mem_o, w_gate_up, w_down)
        mkv = _memkv(mem2d, lw["gains"], lw["w_mkv"])
        mkv3 = mkv.reshape(b, m_len, 2 * MEM_WIDTH)
        xp, k_l, v_l, s_l, sh_l = _layer(xp, b, t, tabs_p, mkv3, mkv3, 0, 1, None, s0_p, sh0_p, lw, hsum,
                                         in_tm_p, rw_tt_p, 1, post_t_p, ffn_tm_p)
        keep = t - min(WINDOW, t)
        pk.append(k_l.reshape(b, t, N_KV_A, HEAD_DIM)[:, keep:])
        pv.append(v_l.reshape(b, t, N_KV_A, HEAD_DIM)[:, keep:])
        pmk.append(mkv3[:, :, :MEM_WIDTH].reshape(b, m_len, MEM_HEADS, MEM_HEAD_DIM))
        pmv.append(mkv3[:, :, MEM_WIDTH:].reshape(b, m_len, MEM_HEADS, MEM_HEAD_DIM))
        pS.append(_blockdiag_to_state(s_l))
        psh.append(sh_l)

        cache = (cache_swa_k[l].reshape(bd, cache_len, KV_WIDTH_A), cache_swa_v[l].reshape(bd, cache_len, KV_WIDTH_A))
        s0_s = _state_to_blockdiag(state_rwkv[l].astype(F32))
        sh0_s = jnp.pad(state_shift[l], ((0, 0), (0, 0), (0, RWKV_PAD - RWKV_PROJ)))
        cmk = cache_mem_k[l].reshape(bd, m_len, MEM_WIDTH)
        cmv = cache_mem_v[l].reshape(bd, m_len, MEM_WIDTH)
        xs, k2, v2, s2, sh2 = _layer(xs, bd, tn, tabs_s, cmk, cmv, 0, 0, cache, s0_s, sh0_s, lw, hsum,
                                     bd * tn, rw_tt_s, bd, tn, bd * tn)
        sk.append(k2.reshape(bd, tn, N_KV_A, HEAD_DIM))
        sv.append(v2.reshape(bd, tn, N_KV_A, HEAD_DIM))
        sS.append(_blockdiag_to_state(s2).astype(state_rwkv.dtype))
        ssh.append(sh2)
    return (xp.reshape(b, t, D_MODEL), xs.reshape(bd, tn, D_MODEL), jnp.stack(pk), jnp.stack(pv),
            jnp.stack(pmk), jnp.stack(pmv), jnp.stack(pS), jnp.stack(psh),
            jnp.stack(sk), jnp.stack(sv), jnp.stack(sS), jnp.stack(ssh))
```

```python
import functools
import math

import jax
import jax.numpy as jnp
from jax import lax
from jax.experimental import pallas as pl
from jax.experimental.pallas import tpu as pltpu

F32 = jnp.float32
BF16 = jnp.bfloat16

D_MODEL = 1024
HEAD_DIM = 64
CHUNK = 64
A_WIDTH = 512
KV_WIDTH_A = 128
N_KV_A = 2
WINDOW = 128
PAST_LEN = 4096
ROPE_THETA = 10000.0
R_WIDTH = 512
N_HEADS_R = 8
DECAY_LORA = 64
AAA_LORA = 64
GATE_LORA = 160
RWKV_PROJ = 3 * R_WIDTH + DECAY_LORA + AAA_LORA + GATE_LORA
RWKV_PAD = 1920
IN_COLS_PAD = A_WIDTH + 2 * KV_WIDTH_A + RWKV_PAD
GN_EPS = 6.4e-4
MEM_LEN = 256
MEM_HEADS = 4
MEM_HEAD_DIM = 128
MEM_WIDTH = 512
D_FF = 2816
RMS_EPS = 1e-6
NEG_INF = -1e30
NORM_MIX_PRE, NORM_MIX_POST, NORM_X_PRE, NORM_X_POST, NORM_MEM, NORM_FFN_PRE, NORM_FFN_POST = range(7)
EXP_M05 = math.exp(-0.5)

LANES = 128
VMEM_LIMIT = 56 * 1024 * 1024


def _params(sem):
    return pltpu.CompilerParams(dimension_semantics=sem, vmem_limit_bytes=VMEM_LIMIT)


def _rms(x, g):
    ms = jnp.mean(x * x, axis=-1, keepdims=True)
    return x * lax.rsqrt(ms + RMS_EPS) * g


def _mm(a, b):
    return jnp.dot(a.astype(BF16), b.astype(BF16), preferred_element_type=F32)


def _mm_nt(a, b):
    return lax.dot_general(a.astype(BF16), b.astype(BF16), (((1,), (1,)), ((), ())),
                           preferred_element_type=F32)


def _split(x):
    hi = x.astype(BF16)
    lo = (x - hi.astype(F32)).astype(BF16)
    return hi, lo


def _mm_lsplit(a_exact_bf16, x):
    hi, lo = _split(x)
    return (jnp.dot(a_exact_bf16, hi, preferred_element_type=F32)
            + jnp.dot(a_exact_bf16, lo, preferred_element_type=F32))


def _mm_rsplit(x, b_exact_bf16):
    hi, lo = _split(x)
    return (jnp.dot(hi, b_exact_bf16, preferred_element_type=F32)
            + jnp.dot(lo, b_exact_bf16, preferred_element_type=F32))


def _in_kernel(x_ref, g_ref, w_ref, cos_ref, sin_ref, q_ref, k_ref, v_ref, pr_ref):
    h = _rms(x_ref[...], g_ref[...]).astype(BF16)
    p = jnp.dot(h, w_ref[...], preferred_element_type=F32)
    cos = cos_ref[...]
    sin = sin_ref[...]
    lane = lax.broadcasted_iota(jnp.int32, cos.shape, 1)
    first_half = (lane & (HEAD_DIM // 2)) == 0

    def rope(xc):
        sw = jnp.where(first_half, pltpu.roll(xc, LANES - HEAD_DIM // 2, 1), pltpu.roll(xc, HEAD_DIM // 2, 1))
        return xc * cos + sw * sin

    for j in range(A_WIDTH // LANES):
        q_ref[:, j * LANES:(j + 1) * LANES] = rope(p[:, j * LANES:(j + 1) * LANES])
    k_ref[...] = rope(p[:, A_WIDTH:A_WIDTH + KV_WIDTH_A])
    v_ref[...] = p[:, A_WIDTH + KV_WIDTH_A:A_WIDTH + 2 * KV_WIDTH_A]
    pr_ref[...] = p[:, A_WIDTH + 2 * KV_WIDTH_A:]


def _in_proj(x2d, g, w_in_p, cos_t, sin_t, tm):
    n = x2d.shape[0]
    tab_blocks = cos_t.shape[0] // tm
    row = lambda i: (i, 0)
    const = lambda i: (0, 0)
    tab = lambda i: (i % tab_blocks, 0)
    return pl.pallas_call(
        _in_kernel,
        grid=(n // tm,),
        in_specs=[pl.BlockSpec((tm, D_MODEL), row), pl.BlockSpec((1, D_MODEL), const),
                  pl.BlockSpec((D_MODEL, IN_COLS_PAD), const),
                  pl.BlockSpec((tm, LANES), tab), pl.BlockSpec((tm, LANES), tab)],
        out_specs=[pl.BlockSpec((tm, A_WIDTH), row), pl.BlockSpec((tm, KV_WIDTH_A), row),
                   pl.BlockSpec((tm, KV_WIDTH_A), row), pl.BlockSpec((tm, RWKV_PAD), row)],
        out_shape=[jax.ShapeDtypeStruct((n, A_WIDTH), F32), jax.ShapeDtypeStruct((n, KV_WIDTH_A), F32),
                   jax.ShapeDtypeStruct((n, KV_WIDTH_A), F32), jax.ShapeDtypeStruct((n, RWKV_PAD), F32)],
        compiler_params=_params(("parallel",)),
        name="in_proj",
    )(x2d, g, w_in_p, cos_t, sin_t)


def _sink_attend(jobs, sink_ref, t):
    lane = lax.broadcasted_iota(jnp.int32, (t, LANES), 1)
    m0 = lane < HEAD_DIM
    rowi = lax.broadcasted_iota(jnp.int32, (4 * t, 1), 0)
    sinks = []
    for kv in range(N_KV_A):
        sk = jnp.zeros((4 * t, 1), F32)
        for g in range(4):
            sg = sink_ref[4 * kv + g:4 * kv + g + 1, 0:1]
            sk = jnp.where((rowi >= g * t) & (rowi < (g + 1) * t), sg, sk)
        sinks.append(sk)
    scores = []
    for q_rows, kdup, _, kv, valid in jobs:
        parts = []
        for p in (2 * kv, 2 * kv + 1):
            qp = q_rows[:, p * LANES:(p + 1) * LANES]
            parts.append(jnp.where(m0, qp, 0.0))
            parts.append(jnp.where(m0, 0.0, qp))
        lhs = jnp.concatenate(parts, axis=0)
        s = _mm_nt(lhs, kdup)
        scores.append(s if valid is None else jnp.where(valid, s, NEG_INF))
    exps, dens = [], []
    for (_, _, _, kv, _), s in zip(jobs, scores):
        m = jnp.maximum(jnp.max(s, axis=-1, keepdims=True), sinks[kv])
        e = jnp.exp(s - m)
        exps.append(e)
        dens.append(jnp.sum(e, axis=-1, keepdims=True) + jnp.exp(sinks[kv] - m))
    outs = []
    for (_, _, vdup, _, _), e, den in zip(jobs, exps, dens):
        o = _mm(e, vdup) / den
        outs.append([jnp.where(m0, o[(2 * pi) * t:(2 * pi + 1) * t], o[(2 * pi + 1) * t:(2 * pi + 2) * t])
                     for pi in range(2)])
    return outs


SWA_SCALE = HEAD_DIM ** -0.5


def _dup_heads(x):
    lane = lax.broadcasted_iota(jnp.int32, x.shape, 1)
    m0 = lane < HEAD_DIM
    xs = pltpu.roll(x, HEAD_DIM, 1)
    return [jnp.where(m0, x, xs), jnp.where(m0, xs, x)]


SWA_TQ = 512


def _swa_prompt_kernel(q_ref, kp_ref, kc_ref, vp_ref, vc_ref, sink_ref, o_ref):
    i = pl.program_id(1)
    k = jnp.concatenate([kp_ref[...], kc_ref[...]], axis=0)
    v = jnp.concatenate([vp_ref[...], vc_ref[...]], axis=0)
    kd = _dup_heads(k * SWA_SCALE)
    vd = _dup_heads(v)
    nk = 3 * CHUNK
    slot = lax.broadcasted_iota(jnp.int32, (1, nk), 1) // CHUNK
    jobs = []
    for j in range(SWA_TQ // CHUNK):
        qj = q_ref[j * CHUNK:(j + 1) * CHUNK, :]
        valid = (slot + (i * (SWA_TQ // CHUNK) + j - 2)) >= 0
        for kv in range(N_KV_A):
            jobs.append((qj, kd[kv][j * CHUNK:j * CHUNK + nk], vd[kv][j * CHUNK:j * CHUNK + nk], kv, valid))
    outs = _sink_attend(jobs, sink_ref, CHUNK)
    for n, out in enumerate(outs):
        j, kv = divmod(n, N_KV_A)
        for pi in range(2):
            p = 2 * kv + pi
            o_ref[j * CHUNK:(j + 1) * CHUNK, p * LANES:(p + 1) * LANES] = out[pi]


def _swa_prompt(q, k, v, sink_b, b, t):
    n = b * t
    nq = t // SWA_TQ
    qmap = lambda bi, i: (bi * nq + i, 0)
    pmap = lambda bi, i: (jnp.maximum(bi * (t // WINDOW) + (SWA_TQ // WINDOW) * i - 1, 0), 0)
    const = lambda bi, i: (0, 0)
    return pl.pallas_call(
        _swa_prompt_kernel,
        grid=(b, nq),
        in_specs=[pl.BlockSpec((SWA_TQ, A_WIDTH), qmap),
                  pl.BlockSpec((WINDOW, KV_WIDTH_A), pmap), pl.BlockSpec((SWA_TQ, KV_WIDTH_A), qmap),
                  pl.BlockSpec((WINDOW, KV_WIDTH_A), pmap), pl.BlockSpec((SWA_TQ, KV_WIDTH_A), qmap),
                  pl.BlockSpec((8, LANES), const)],
        out_specs=pl.BlockSpec((SWA_TQ, A_WIDTH), qmap),
        out_shape=jax.ShapeDtypeStruct((n, A_WIDTH), F32),
        compiler_params=_params(("parallel", "parallel")),
        name="swa_prompt",
    )(q, k, k, v, v, sink_b)


def _swa_sample_kernel(q_ref, kc_ref, kn_ref, vc_ref, vn_ref, sink_ref, o_ref, *, t, nb):
    jobs = []
    for bi in range(nb):
        rows = slice(bi * t, (bi + 1) * t)
        kd = _dup_heads(jnp.concatenate([kc_ref[bi], kn_ref[rows, :]], axis=0) * SWA_SCALE)
        vd = _dup_heads(jnp.concatenate([vc_ref[bi], vn_ref[rows, :]], axis=0))
        for kv in range(N_KV_A):
            jobs.append((q_ref[rows, :], kd[kv], vd[kv], kv, None))
    outs = _sink_attend(jobs, sink_ref, t)
    for n, out in enumerate(outs):
        bi, kv = divmod(n, N_KV_A)
        for pi in range(2):
            p = 2 * kv + pi
            o_ref[bi * t:(bi + 1) * t, p * LANES:(p + 1) * LANES] = out[pi]


def _swa_sample(q, k, v, kc, vc, sink_b, b, t):
    n = b * t
    cache = kc.shape[1]
    nb = 4 if b % 4 == 0 else 1
    row = lambda bi: (bi, 0)
    cmap = lambda bi: (bi, 0, 0)
    const = lambda bi: (0, 0)
    return pl.pallas_call(
        functools.partial(_swa_sample_kernel, t=t, nb=nb),
        grid=(b // nb,),
        in_specs=[pl.BlockSpec((nb * t, A_WIDTH), row),
                  pl.BlockSpec((nb, cache, KV_WIDTH_A), cmap), pl.BlockSpec((nb * t, KV_WIDTH_A), row),
                  pl.BlockSpec((nb, cache, KV_WIDTH_A), cmap), pl.BlockSpec((nb * t, KV_WIDTH_A), row),
                  pl.BlockSpec((8, LANES), const)],
        out_specs=pl.BlockSpec((nb * t, A_WIDTH), row),
        out_shape=jax.ShapeDtypeStruct((n, A_WIDTH), F32),
        compiler_params=_params(("parallel",)),
        name="swa_sample",
    )(q, kc, k, vc, v, sink_b)


def _tri_inverse(nmat, masks, eye):
    p = eye + jnp.where(masks[0], nmat, 0.0)
    for m in masks[1:]:
        e = jnp.where(m, nmat, 0.0)
        p = p + _mm(_mm(p, e), p)
    return p


def _rwkv_kernel(pr_ref, sh0_ref, s0_ref, mu_ref, vec_ref, rk_ref, w2_ref, a2_ref, g2_ref, hsum_ref,
                 ltri_ref, out_ref, sfin_ref,
                 s_scr, carry_scr, rt_s, at_s, bt_s, kt_s, bb_s, kb_s, v_s, et_s, y_s, *, c_len, tt, nb):
    i = pl.program_id(1)
    n_i = pl.num_programs(1)
    n_rows = nb * tt

    @pl.when(i == 0)
    def _():
        s_scr[...] = s0_ref[...]
        carry_scr[...] = sh0_ref[...]

    rowi = lax.broadcasted_iota(jnp.int32, (n_rows, 1), 0)

    def shifted(a, b):
        cur = pr_ref[:, :, a:b].reshape(n_rows, b - a)
        prev = pltpu.roll(cur, 1, 0)
        for bi in range(nb):
            prev = jnp.where(rowi == bi * tt, carry_scr[bi, :, a:b], prev)
        return cur + (prev - cur) * mu_ref[:, a:b]

    r = shifted(0, 512)
    k = shifted(512, 1024)
    v = shifted(1024, 1536)
    xwa = shifted(1536, 1664)
    xg = shifted(1664, RWKV_PAD)
    for bi in range(nb):
        carry_scr[bi] = pr_ref[bi, tt - 1:tt, :]

    w0 = vec_ref[0:1, :]
    a0 = vec_ref[1:2, :]
    k_k = vec_ref[2:3, :]
    k_a = vec_ref[3:4, :]
    gn_g = vec_ref[4:5, :]
    gn_b = vec_ref[5:6, :]
    hsum = hsum_ref[...]
    hw = hsum.shape[0]

    def head_sum(x):
        return jnp.concatenate([_mm_rsplit(x[:, j * hw:(j + 1) * hw], hsum) for j in range(R_WIDTH // hw)], axis=1)

    z = w0 + _mm(jnp.tanh(xwa), w2_ref[...])
    wlog = -EXP_M05 * jax.nn.sigmoid(z)
    a = jax.nn.sigmoid(a0 + _mm(xwa, a2_ref[...]))
    g = _mm(jax.nn.sigmoid(xg), g2_ref[...])
    kk = k * k_k
    kk = kk * lax.rsqrt(jnp.maximum(head_sum(kk * kk), 1e-24))
    k_f = k * (1.0 + (a - 1.0) * k_a)
    bvec = kk * a
    bonus = head_sum(r * k_f * rk_ref[...]) * v

    cum = _mm_lsplit(ltri_ref[...], wlog)
    tot = jnp.concatenate([jnp.broadcast_to(cum[r1 - 1:r1, :], (c_len, R_WIDTH))
                           for r1 in range(c_len, n_rows + 1, c_len)], axis=0)
    e_in = jnp.exp(cum)
    e_inv = jnp.exp(-cum)
    e_end = jnp.exp(tot - cum)
    rt_s[...] = r * e_in
    at_s[...] = -kk * jnp.exp(cum - wlog)
    bt_s[...] = bvec * e_inv
    kt_s[...] = k_f * e_inv
    bb_s[...] = bvec * e_end
    kb_s[...] = k_f * e_end
    v_s[...] = v
    et_s[...] = jnp.exp(tot)

    n2 = 2 * c_len
    ti = lax.broadcasted_iota(jnp.int32, (c_len, n2), 0)
    lane2 = lax.broadcasted_iota(jnp.int32, (c_len, n2), 1)
    si = lane2 & (c_len - 1)
    first_c = lane2 < c_len
    strict = ti > si
    incl = ti >= si
    eye = jnp.where(ti == si, 1.0, 0.0).astype(F32)
    masks = []
    half = 1
    while half < c_len:
        blk = 2 * half
        masks.append(((ti & ~(blk - 1)) == (si & ~(blk - 1))) & ((ti & half) != 0) & ((si & half) == 0))
        half = blk
    m0 = lax.broadcasted_iota(jnp.int32, (c_len, LANES), 1) < HEAD_DIM
    rl = lax.broadcasted_iota(jnp.int32, (LANES, LANES), 0)
    cl = lax.broadcasted_iota(jnp.int32, (LANES, LANES), 1)
    same_head = (rl >= HEAD_DIM) == (cl >= HEAD_DIM)
    fused = n2 == LANES
    zeros_c = jnp.zeros((c_len, LANES), F32)

    def bd(x):
        return jnp.concatenate([jnp.where(m0, x, 0.0), jnp.where(m0, 0.0, x)], axis=0)

    def bd_t(x):
        return jnp.concatenate([jnp.where(first_c, x, 0.0), jnp.where(first_c, 0.0, x)], axis=0)

    chains = [(bi, c, p) for bi in range(nb) for c in range(tt // c_len) for p in range(4)]

    def rows_lanes(ch):
        bi, c, p = ch
        r0 = bi * tt + c * c_len
        return slice(r0, r0 + c_len), slice(p * LANES, (p + 1) * LANES)

    def cat(ref, ch):
        rows, ls = rows_lanes(ch)
        return ref[rows, ls]

    a_ab, a_ak, a_rb, a_rk = {}, {}, {}, {}
    for ch in chains:
        at, rt, rb, rkt = cat(at_s, ch), cat(rt_s, ch), bd(cat(bt_s, ch)), bd(cat(kt_s, ch))
        if fused:
            amat = _mm_nt(jnp.concatenate([at, rt], axis=0), jnp.concatenate([rb, rkt], axis=0))
            q_ab, q_ak = amat[:c_len, :n2], amat[:c_len, n2:]
            q_rb, q_rk = amat[c_len:, :n2], amat[c_len:, n2:]
        else:
            q_ab, q_ak, q_rb, q_rk = _mm_nt(at, rb), _mm_nt(at, rkt), _mm_nt(rt, rb), _mm_nt(rt, rkt)
        a_ab[ch] = jnp.where(strict, q_ab, 0.0)
        a_ak[ch] = jnp.where(strict, q_ak, 0.0)
        a_rb[ch] = jnp.where(incl, q_rb, 0.0)
        a_rk[ch] = jnp.where(incl, q_rk, 0.0)

    tinv = {ch: eye + jnp.where(masks[0], a_ab[ch], 0.0) for ch in chains}
    for m in masks[1:]:
        pe = {ch: _mm(tinv[ch], bd_t(jnp.where(m, a_ab[ch], 0.0))) for ch in chains}
        tinv = {ch: tinv[ch] + _mm(pe[ch], bd_t(tinv[ch])) for ch in chains}

    zv = {ch: _mm(a_ak[ch], bd(cat(v_s, ch))) for ch in chains}
    r_hat, y_hat, m_mat, g_mat = {}, {}, {}, {}
    if fused:
        w = {ch: _mm(tinv[ch], jnp.concatenate([bd(cat(at_s, ch)), bd(zv[ch])], axis=1)) for ch in chains}
        for ch in chains:
            rhs2 = jnp.concatenate([jnp.concatenate([bd(w[ch][:, :LANES]), bd(w[ch][:, LANES:])], axis=1),
                                    jnp.concatenate([jnp.zeros((n2, LANES), F32), bd(cat(v_s, ch))], axis=1)], axis=0)
            ry = _mm(jnp.concatenate([a_rb[ch], a_rk[ch]], axis=1), rhs2)
            r_hat[ch] = cat(rt_s, ch) + ry[:, :LANES]
            y_hat[ch] = ry[:, LANES:]
        for ch in chains:
            lhs_t = jnp.concatenate([w[ch], jnp.concatenate([zeros_c, cat(v_s, ch)], axis=1)], axis=0)
            mg = _mm(lhs_t.T, jnp.concatenate([cat(bb_s, ch), cat(kb_s, ch)], axis=0))
            m_mat[ch] = jnp.where(same_head, mg[:LANES], 0.0)
            g_mat[ch] = jnp.where(same_head, mg[LANES:], 0.0)
    else:
        a_hat = {ch: _mm(tinv[ch], bd(cat(at_s, ch))) for ch in chains}
        u_hat = {ch: _mm(tinv[ch], bd(zv[ch])) for ch in chains}
        for ch in chains:
            r_hat[ch] = cat(rt_s, ch) + _mm(a_rb[ch], bd(a_hat[ch]))
            y_hat[ch] = _mm(a_rb[ch], bd(u_hat[ch])) + _mm(a_rk[ch], bd(cat(v_s, ch)))
        for ch in chains:
            m_mat[ch] = jnp.where(same_head, _mm(a_hat[ch].T, cat(bb_s, ch)), 0.0)
            g_mat[ch] = jnp.where(same_head, _mm(u_hat[ch].T, cat(bb_s, ch)) + _mm(cat(v_s, ch).T, cat(kb_s, ch)), 0.0)

    for bi in range(nb):
        s_cur = [s_scr[bi, p] for p in range(4)]
        for c in range(tt // c_len):
            for p in range(4):
                ch = (bi, c, p)
                rows, ls = rows_lanes(ch)
                e_last = et_s[rows.start:rows.start + 1, ls]
                y_s[rows, ls] = y_hat[ch] + _mm_nt(r_hat[ch], s_cur[p])
                s_cur[p] = s_cur[p] * e_last + _mm(s_cur[p], m_mat[ch]) + g_mat[ch]
        for p in range(4):
            s_scr[bi, p] = s_cur[p]

    y = y_s[...]
    mean = head_sum(y) * (1.0 / HEAD_DIM)
    yc = y - mean
    var = head_sum(yc * yc) * (1.0 / HEAD_DIM)
    yn = yc * lax.rsqrt(var + GN_EPS) * gn_g + gn_b
    res = (yn + bonus) * g
    for bi in range(nb):
        out_ref[bi] = res[bi * tt:(bi + 1) * tt]

    @pl.when(i == n_i - 1)
    def _():
        sfin_ref[...] = s_scr[...]


def _rwkv(pr3, shift0, s0bd, mu_p, vecs, rk_flat, w2p, a2p, g2p, hsum, tt, c_len, nb):
    b, t, _ = pr3.shape
    n_rows = nb * tt
    ri = jnp.arange(n_rows)[:, None]
    ci = jnp.arange(n_rows)[None, :]
    same = (ri // c_len) == (ci // c_len)
    ltri = (same & (ri >= ci)).astype(BF16)
    hw = hsum.shape[0]
    blk = lambda bi, i: (bi, i, 0)
    perb = lambda bi, i: (bi, 0, 0)
    perb4 = lambda bi, i: (bi, 0, 0, 0)
    const = lambda bi, i: (0, 0)
    big = lambda: pltpu.VMEM((n_rows, R_WIDTH), F32)
    return pl.pallas_call(
        functools.partial(_rwkv_kernel, c_len=c_len, tt=tt, nb=nb),
        grid=(b // nb, t // tt),
        in_specs=[pl.BlockSpec((nb, tt, RWKV_PAD), blk), pl.BlockSpec((nb, 1, RWKV_PAD), perb),
                  pl.BlockSpec((nb, 4, LANES, LANES), perb4),
                  pl.BlockSpec((1, RWKV_PAD), const), pl.BlockSpec((6, R_WIDTH), const),
                  pl.BlockSpec((1, R_WIDTH), const),
                  pl.BlockSpec((LANES, R_WIDTH), const), pl.BlockSpec((LANES, R_WIDTH), const),
                  pl.BlockSpec((2 * LANES, R_WIDTH), const), pl.BlockSpec((hw, hw), const),
                  pl.BlockSpec((n_rows, n_rows), const)],
        out_specs=[pl.BlockSpec((nb, tt, R_WIDTH), blk), pl.BlockSpec((nb, 4, LANES, LANES), perb4)],
        out_shape=[jax.ShapeDtypeStruct((b, t, R_WIDTH), F32),
                   jax.ShapeDtypeStruct((b, 4, LANES, LANES), F32)],
        scratch_shapes=[pltpu.VMEM((nb, 4, LANES, LANES), F32), pltpu.VMEM((nb, 1, RWKV_PAD), F32),
                        big(), big(), big(), big(), big(), big(), big(), big(), big()],
        compiler_params=_params(("parallel", "arbitrary")),
        name="rwkv_mix",
    )(pr3, shift0, s0bd, mu_p, vecs, rk_flat, w2p, a2p, g2p, hsum, ltri)


def _post_kernel(attn_ref, rw_ref, x_ref, g_ref, wout_ref, wq_ref, wo_ref, mk_ref, mv_ref, o_ref, *, nb, t):
    m = _mm(attn_ref[...], wout_ref[0:A_WIDTH, :]) + _mm(rw_ref[...], wout_ref[A_WIDTH:, :])
    x1 = x_ref[...] + _rms(m, g_ref[NORM_MIX_POST:NORM_MIX_POST + 1, :])
    q = _mm(_rms(x1, g_ref[NORM_X_PRE:NORM_X_PRE + 1, :]), wq_ref[...])
    scale = MEM_HEAD_DIM ** -0.5
    jobs = [(bi, hd) for bi in range(nb) for hd in range(MEM_HEADS)]
    cols = lambda hd: slice(hd * MEM_HEAD_DIM, (hd + 1) * MEM_HEAD_DIM)
    scores = [_mm_nt(q[bi * t:(bi + 1) * t, cols(hd)], mk_ref[bi, :, cols(hd)]) * scale for bi, hd in jobs]
    exps = [jnp.exp(s - jnp.max(s, axis=-1, keepdims=True)) for s in scores]
    outs = [_mm(e, mv_ref[bi, :, cols(hd)]) / jnp.sum(e, axis=-1, keepdims=True) for (bi, hd), e in zip(jobs, exps)]
    row_blocks = [jnp.concatenate(outs[bi * MEM_HEADS:(bi + 1) * MEM_HEADS], axis=1) for bi in range(nb)]
    o = row_blocks[0] if nb == 1 else jnp.concatenate(row_blocks, axis=0)
    c = _mm(o, wo_ref[...])
    o_ref[...] = x1 + _rms(c, g_ref[NORM_X_POST:NORM_X_POST + 1, :])


def _post(attn, rw, x2d, gains, w_out, w_mq, w_mo, mk, mv, kcol, vcol, nb, t, tiles_per_batch):
    n = x2d.shape[0]
    tm = nb * t
    row = lambda i: (i, 0)
    const = lambda i: (0, 0)
    if nb == 1:
        kmap = lambda i: (i // tiles_per_batch, 0, kcol)
        vmap = lambda i: (i // tiles_per_batch, 0, vcol)
    else:
        kmap = lambda i: (i, 0, kcol)
        vmap = lambda i: (i, 0, vcol)
    return pl.pallas_call(
        functools.partial(_post_kernel, nb=nb, t=t),
        grid=(n // tm,),
        in_specs=[pl.BlockSpec((tm, A_WIDTH), row), pl.BlockSpec((tm, R_WIDTH), row),
                  pl.BlockSpec((tm, D_MODEL), row), pl.BlockSpec((8, D_MODEL), const),
                  pl.BlockSpec((D_MODEL, D_MODEL), const), pl.BlockSpec((D_MODEL, MEM_WIDTH), const),
                  pl.BlockSpec((MEM_WIDTH, D_MODEL), const),
                  pl.BlockSpec((nb, MEM_LEN, MEM_WIDTH), kmap), pl.BlockSpec((nb, MEM_LEN, MEM_WIDTH), vmap)],
        out_specs=pl.BlockSpec((tm, D_MODEL), row),
        out_shape=jax.ShapeDtypeStruct((n, D_MODEL), F32),
        compiler_params=_params(("parallel",)),
        name="post_mix",
    )(attn, rw, x2d, gains, w_out, w_mq, w_mo, mk, mv)


FFN_TILE = 1408


def _ffn_kernel(x_ref, g_ref, wg_ref, wu_ref, wd_ref, o_ref, h_scr, acc_scr):
    j = pl.program_id(1)

    @pl.when(j == 0)
    def _():
        h_scr[...] = _rms(x_ref[...], g_ref[NORM_FFN_PRE:NORM_FFN_PRE + 1, :]).astype(BF16)
        acc_scr[...] = jnp.zeros_like(acc_scr)

    h = h_scr[...]
    gate = jnp.dot(h, wg_ref[...], preferred_element_type=F32)
    up = jnp.dot(h, wu_ref[...], preferred_element_type=F32)
    act = (gate * jax.nn.sigmoid(gate)) * up
    acc_scr[...] += _mm(act, wd_ref[...])

    @pl.when(j == pl.num_programs(1) - 1)
    def _():
        o_ref[...] = x_ref[...] + _rms(acc_scr[...], g_ref[NORM_FFN_POST:NORM_FFN_POST + 1, :])


def _ffn(x2d, gains, w_gu, w_dn, tm):
    n = x2d.shape[0]
    nf = D_FF // FFN_TILE
    row = lambda i, j: (i, 0)
    const = lambda i, j: (0, 0)
    return pl.pallas_call(
        _ffn_kernel,
        grid=(n // tm, nf),
        in_specs=[pl.BlockSpec((tm, D_MODEL), row), pl.BlockSpec((8, D_MODEL), const),
                  pl.BlockSpec((D_MODEL, FFN_TILE), lambda i, j: (0, j)),
                  pl.BlockSpec((D_MODEL, FFN_TILE), lambda i, j: (0, nf + j)),
                  pl.BlockSpec((FFN_TILE, D_MODEL), lambda i, j: (j, 0))],
        out_specs=pl.BlockSpec((tm, D_MODEL), row),
        out_shape=jax.ShapeDtypeStruct((n, D_MODEL), F32),
        scratch_shapes=[pltpu.VMEM((tm, D_MODEL), BF16), pltpu.VMEM((tm, D_MODEL), F32)],
        compiler_params=_params(("parallel", "arbitrary")),
        name="ffn",
    )(x2d, gains, w_gu, w_gu, w_dn)


def _memkv_kernel(x_ref, g_ref, w_ref, o_ref):
    o_ref[...] = _mm(_rms(x_ref[...], g_ref[NORM_MEM:NORM_MEM + 1, :]), w_ref[...])


def _memkv(mem2d, gains, w_kv):
    n = mem2d.shape[0]
    tm = MEM_LEN
    row = lambda i: (i, 0)
    const = lambda i: (0, 0)
    return pl.pallas_call(
        _memkv_kernel,
        grid=(n // tm,),
        in_specs=[pl.BlockSpec((tm, D_MODEL), row), pl.BlockSpec((8, D_MODEL), const),
                  pl.BlockSpec((D_MODEL, 2 * MEM_WIDTH), const)],
        out_specs=pl.BlockSpec((tm, 2 * MEM_WIDTH), row),
        out_shape=jax.ShapeDtypeStruct((n, 2 * MEM_WIDTH), F32),
        compiler_params=_params(("parallel",)),
        name="mem_kv",
    )(mem2d, gains, w_kv)


def _rope_tables(pos):
    half = HEAD_DIM // 2
    inv = ROPE_THETA ** (-jnp.arange(half, dtype=F32) / half)
    ang = pos.astype(F32)[:, None] * inv[None, :]
    cos = jnp.cos(ang)
    sin = jnp.sin(ang)
    cos_t = jnp.tile(cos, (1, LANES // half))
    sin_t = jnp.tile(jnp.concatenate([-sin, sin], axis=1), (1, LANES // HEAD_DIM))
    return cos_t, sin_t


def _state_to_blockdiag(s):
    b = s.shape[0]
    s5 = s.reshape(b, 4, 2, HEAD_DIM, HEAD_DIM)
    eye2 = jnp.eye(2, dtype=s.dtype)
    bd = s5[:, :, :, :, None, :] * eye2[None, None, :, None, :, None]
    return bd.reshape(b, 4, LANES, LANES)


def _blockdiag_to_state(bd):
    b = bd.shape[0]
    x = bd.reshape(b, 4, 2, HEAD_DIM, 2, HEAD_DIM)
    s = jnp.stack([x[:, :, 0, :, 0, :], x[:, :, 1, :, 1, :]], axis=2)
    return s.reshape(b, N_HEADS_R, HEAD_DIM, HEAD_DIM)


def _layer_weights(l, norm_gains, w_in, attn_sink, shift_mu, rwkv_vecs, rwkv_rk, rwkv_w2, rwkv_a2, rwkv_g2,
                   w_out, w_mem_q, w_mem_kv, w_mem_o, w_gate_up, w_down):
    gains = jnp.concatenate([norm_gains[l], jnp.zeros((1, D_MODEL), F32)], axis=0)
    w_in_p = jnp.pad(w_in[l], ((0, 0), (0, IN_COLS_PAD - w_in.shape[2]))).astype(BF16)
    sink_b = jnp.broadcast_to(attn_sink[l][:, None], (8, LANES)).astype(F32)
    mu_p = jnp.pad(shift_mu[l], (0, RWKV_PAD - RWKV_PROJ))[None, :]
    w2p = jnp.pad(rwkv_w2[l], ((0, LANES - DECAY_LORA), (0, 0))).astype(BF16)
    a2p = jnp.pad(rwkv_a2[l], ((DECAY_LORA, LANES - DECAY_LORA - AAA_LORA), (0, 0))).astype(BF16)
    g2p = jnp.pad(rwkv_g2[l], ((0, 2 * LANES - GATE_LORA), (0, 0))).astype(BF16)
    return dict(gains=gains, g_pre=norm_gains[l, NORM_MIX_PRE][None, :], w_in=w_in_p, sink=sink_b, mu=mu_p,
                vecs=rwkv_vecs[l], rk=rwkv_rk[l].reshape(1, R_WIDTH), w2=w2p, a2=a2p, g2=g2p,
                w_out=w_out[l].astype(BF16), w_mq=w_mem_q[l].astype(BF16), w_mkv=w_mem_kv[l].astype(BF16),
                w_mo=w_mem_o[l].astype(BF16), w_gu=w_gate_up[l].astype(BF16), w_dn=w_down[l].astype(BF16))


def _head_sum_matrix():
    i = jnp.arange(2 * LANES)
    return ((i[:, None] // HEAD_DIM) == (i[None, :] // HEAD_DIM)).astype(BF16)


def _layer(x2d, b, t, tabs, mk, mv, kcol, vcol, swa_cache, s0bd, shift0, lw, hsum, in_tm, rw_tt, post_nb, post_t,
           ffn_tm):
    q, k, v, pr = _in_proj(x2d, lw["g_pre"], lw["w_in"], tabs[0], tabs[1], in_tm)
    if swa_cache is None:
        attn = _swa_prompt(q, k, v, lw["sink"], b, t)
    else:
        attn = _swa_sample(q, k, v, swa_cache[0], swa_cache[1], lw["sink"], b, t)
    pr3 = pr.reshape(b, t, RWKV_PAD)
    rw, s_fin = _rwkv(pr3, shift0, s0bd, lw["mu"], lw["vecs"], lw["rk"], lw["w2"], lw["a2"], lw["g2"], hsum,
                      rw_tt[0], min(CHUNK, t), rw_tt[1])
    x2 = _post(attn, rw.reshape(b * t, R_WIDTH), x2d, lw["gains"], lw["w_out"], lw["w_mq"], lw["w_mo"],
               mk, mv, kcol, vcol, post_nb, post_t, t // post_t)
    x3 = _ffn(x2, lw["gains"], lw["w_gu"], lw["w_dn"], ffn_tm)
    return x3, k, v, s_fin, pr3[:, t - 1:t, :RWKV_PROJ]


def kernel(x_prompt, mem_prompt, x_sample, cache_swa_k, cache_swa_v, cache_mem_k, cache_mem_v, state_rwkv,
           state_shift, norm_gains, w_in, attn_sink, shift_mu, rwkv_vecs, rwkv_rk, rwkv_w2, rwkv_a2, rwkv_g2,
           w_out, w_mem_q, w_mem_kv, w_mem_o, w_gate_up, w_down):
    b, t, _ = x_prompt.shape
    bd, tn, _ = x_sample.shape
    depth = w_in.shape[0]
    m_len = mem_prompt.shape[1]
    cache_len = cache_swa_k.shape[2]
    tabs_p = _rope_tables(jnp.arange(t, dtype=jnp.int32))
    cs, sn = _rope_tables(PAST_LEN + jnp.arange(tn, dtype=jnp.int32))
    tabs_s = (jnp.tile(cs, (bd, 1)), jnp.tile(sn, (bd, 1)))
    hsum = _head_sum_matrix()
    s0_p = jnp.zeros((b, 4, LANES, LANES), F32)
    sh0_p = jnp.zeros((b, 1, RWKV_PAD), F32)
    xp = x_prompt.reshape(b * t, D_MODEL)
    xs = x_sample.reshape(bd * tn, D_MODEL)
    mem2d = mem_prompt.reshape(b * m_len, D_MODEL)
    in_tm_p = min(512, t)
    rw_tt_p = (min(256, t), 1)
    rw_tt_s = (tn, 4 if bd % 4 == 0 else 1)
    post_t_p = min(512, t)
    ffn_tm_p = min(512, b * t)
    pk, pv, pmk, pmv, pS, psh = [], [], [], [], [], []
    sk, sv, sS, ssh = [], [], [], []
    for l in range(depth):
        lw = _layer_weights(l, norm_gains, w_in, attn_sink, shift_mu, rwkv_vecs, rwkv_rk, rwkv_w2, rwkv_a2,
                            rwkv_g2, w_out, w_mem_q, w_mem_kv, w_mem_o, w_gate_up, w_down)
        mkv = _memkv(mem2d, lw["gains"], lw["w_mkv"])
        mkv3 = mkv.reshape(b, m_len, 2 * MEM_WIDTH)
        xp, k_l, v_l, s_l, sh_l = _layer(xp, b, t, tabs_p, mkv3, mkv3, 0, 1, None, s0_p, sh0_p, lw, hsum,
                                         in_tm_p, rw_tt_p, 1, post_t_p, ffn_tm_p)
        keep = t - min(WINDOW, t)
        pk.append(k_l.reshape(b, t, KV_WIDTH_A)[:, keep:].reshape(b, t - keep, N_KV_A, HEAD_DIM))
        pv.append(v_l.reshape(b, t, KV_WIDTH_A)[:, keep:].reshape(b, t - keep, N_KV_A, HEAD_DIM))
        pmk.append(mkv3[:, :, :MEM_WIDTH].reshape(b, m_len, MEM_HEADS, MEM_HEAD_DIM))
        pmv.append(mkv3[:, :, MEM_WIDTH:].reshape(b, m_len, MEM_HEADS, MEM_HEAD_DIM))
        pS.append(_blockdiag_to_state(s_l))
        psh.append(sh_l)

        cache = (cache_swa_k[l].reshape(bd, cache_len, KV_WIDTH_A), cache_swa_v[l].reshape(bd, cache_len, KV_WIDTH_A))
        s0_s = _state_to_blockdiag(state_rwkv[l].astype(F32))
        sh0_s = jnp.pad(state_shift[l], ((0, 0), (0, 0), (0, RWKV_PAD - RWKV_PROJ)))
        cmk = cache_mem_k[l].reshape(bd, m_len, MEM_WIDTH)
        cmv = cache_mem_v[l].reshape(bd, m_len, MEM_WIDTH)
        xs, k2, v2, s2, sh2 = _layer(xs, bd, tn, tabs_s, cmk, cmv, 0, 0, cache, s0_s, sh0_s, lw, hsum,
                                     bd * tn, rw_tt_s, bd, tn, bd * tn)
        sk.append(k2.reshape(bd, tn, N_KV_A, HEAD_DIM))
        sv.append(v2.reshape(bd, tn, N_KV_A, HEAD_DIM))
        sS.append(_blockdiag_to_state(s2).astype(state_rwkv.dtype))
        ssh.append(sh2)
    return (xp.reshape(b, t, D_MODEL), xs.reshape(bd, tn, D_MODEL), jnp.stack(pk), jnp.stack(pv),
            jnp.stack(pmk), jnp.stack(pmv), jnp.stack(pS), jnp.stack(psh),
            jnp.stack(sk), jnp.stack(sv), jnp.stack(sS), jnp.stack(ssh))
```

```python
import functools
import math

import jax
import jax.numpy as jnp
from jax import lax
from jax.experimental import pallas as pl
from jax.experimental.pallas import tpu as pltpu

F32 = jnp.float32
BF16 = jnp.bfloat16

D_MODEL = 1024
HEAD_DIM = 64
CHUNK = 64
A_WIDTH = 512
KV_WIDTH_A = 128
N_KV_A = 2
WINDOW = 128
PAST_LEN = 4096
ROPE_THETA = 10000.0
R_WIDTH = 512
N_HEADS_R = 8
DECAY_LORA = 64
AAA_LORA = 64
GATE_LORA = 160
RWKV_PROJ = 3 * R_WIDTH + DECAY_LORA + AAA_LORA + GATE_LORA
RWKV_PAD = 1920
IN_COLS_PAD = A_WIDTH + 2 * KV_WIDTH_A + RWKV_PAD
GN_EPS = 6.4e-4
MEM_LEN = 256
MEM_HEADS = 4
MEM_HEAD_DIM = 128
MEM_WIDTH = 512
D_FF = 2816
RMS_EPS = 1e-6
NEG_INF = -1e30
NORM_MIX_PRE, NORM_MIX_POST, NORM_X_PRE, NORM_X_POST, NORM_MEM, NORM_FFN_PRE, NORM_FFN_POST = range(7)
EXP_M05 = math.exp(-0.5)

LANES = 128
VMEM_LIMIT = 56 * 1024 * 1024


def _params(sem):
    return pltpu.CompilerParams(dimension_semantics=sem, vmem_limit_bytes=VMEM_LIMIT)


def _rms(x, g):
    ms = jnp.mean(x * x, axis=-1, keepdims=True)
    return x * lax.rsqrt(ms + RMS_EPS) * g


def _mm(a, b):
    return jnp.dot(a.astype(BF16), b.astype(BF16), preferred_element_type=F32)


def _mm_nt(a, b):
    return lax.dot_general(a.astype(BF16), b.astype(BF16), (((1,), (1,)), ((), ())),
                           preferred_element_type=F32)


def _split(x):
    hi = x.astype(BF16)
    lo = (x - hi.astype(F32)).astype(BF16)
    return hi, lo


def _mm_lsplit(a_exact_bf16, x):
    hi, lo = _split(x)
    return (jnp.dot(a_exact_bf16, hi, preferred_element_type=F32)
            + jnp.dot(a_exact_bf16, lo, preferred_element_type=F32))


def _mm_rsplit(x, b_exact_bf16):
    hi, lo = _split(x)
    return (jnp.dot(hi, b_exact_bf16, preferred_element_type=F32)
            + jnp.dot(lo, b_exact_bf16, preferred_element_type=F32))


def _in_kernel(x_ref, g_ref, w_ref, cos_ref, sin_ref, q_ref, k_ref, v_ref, pr_ref):
    h = _rms(x_ref[...], g_ref[NORM_MIX_PRE:NORM_MIX_PRE + 1, :]).astype(BF16)
    p = jnp.dot(h, w_ref[...], preferred_element_type=F32)
    cos = cos_ref[...]
    sin = sin_ref[...]
    lane = lax.broadcasted_iota(jnp.int32, cos.shape, 1)
    first_half = (lane & (HEAD_DIM // 2)) == 0

    def rope(xc):
        sw = jnp.where(first_half, pltpu.roll(xc, LANES - HEAD_DIM // 2, 1), pltpu.roll(xc, HEAD_DIM // 2, 1))
        return xc * cos + sw * sin

    for j in range(A_WIDTH // LANES):
        q_ref[:, j * LANES:(j + 1) * LANES] = rope(p[:, j * LANES:(j + 1) * LANES])
    k_ref[...] = rope(p[:, A_WIDTH:A_WIDTH + KV_WIDTH_A])
    v_ref[...] = p[:, A_WIDTH + KV_WIDTH_A:A_WIDTH + 2 * KV_WIDTH_A]
    pr_ref[...] = p[:, A_WIDTH + 2 * KV_WIDTH_A:]


def _layer_block(shape, l):
    nd = len(shape)
    return pl.BlockSpec((None,) + tuple(shape), lambda *_: (l,) + (0,) * nd)


def _in_proj(x2d, l, gains, w_in_p, cos_t, sin_t, tm):
    n = x2d.shape[0]
    tab_blocks = cos_t.shape[0] // tm
    row = lambda i: (i, 0)
    tab = lambda i: (i % tab_blocks, 0)
    return pl.pallas_call(
        _in_kernel,
        grid=(n // tm,),
        in_specs=[pl.BlockSpec((tm, D_MODEL), row), _layer_block((8, D_MODEL), l),
                  _layer_block((D_MODEL, IN_COLS_PAD), l),
                  pl.BlockSpec((tm, LANES), tab), pl.BlockSpec((tm, LANES), tab)],
        out_specs=[pl.BlockSpec((tm, A_WIDTH), row), pl.BlockSpec((tm, KV_WIDTH_A), row),
                   pl.BlockSpec((tm, KV_WIDTH_A), row), pl.BlockSpec((tm, RWKV_PAD), row)],
        out_shape=[jax.ShapeDtypeStruct((n, A_WIDTH), F32), jax.ShapeDtypeStruct((n, KV_WIDTH_A), F32),
                   jax.ShapeDtypeStruct((n, KV_WIDTH_A), F32), jax.ShapeDtypeStruct((n, RWKV_PAD), F32)],
        compiler_params=_params(("parallel",)),
        name="in_proj",
    )(x2d, gains, w_in_p, cos_t, sin_t)


def _sink_attend(jobs, sink_ref, t):
    lane = lax.broadcasted_iota(jnp.int32, (t, LANES), 1)
    m0 = lane < HEAD_DIM
    rowi = lax.broadcasted_iota(jnp.int32, (4 * t, 1), 0)
    sinks = []
    for kv in range(N_KV_A):
        sk = jnp.zeros((4 * t, 1), F32)
        for g in range(4):
            sg = sink_ref[4 * kv + g:4 * kv + g + 1, 0:1]
            sk = jnp.where((rowi >= g * t) & (rowi < (g + 1) * t), sg, sk)
        sinks.append(sk)
    scores = []
    for q_rows, kdup, _, kv, valid in jobs:
        parts = []
        for p in (2 * kv, 2 * kv + 1):
            qp = q_rows[:, p * LANES:(p + 1) * LANES]
            parts.append(jnp.where(m0, qp, 0.0))
            parts.append(jnp.where(m0, 0.0, qp))
        lhs = jnp.concatenate(parts, axis=0)
        s = _mm_nt(lhs, kdup)
        scores.append(s if valid is None else jnp.where(valid, s, NEG_INF))
    exps, dens = [], []
    for (_, _, _, kv, _), s in zip(jobs, scores):
        m = jnp.maximum(jnp.max(s, axis=-1, keepdims=True), sinks[kv])
        e = jnp.exp(s - m)
        exps.append(e)
        dens.append(jnp.sum(e, axis=-1, keepdims=True) + jnp.exp(sinks[kv] - m))
    outs = []
    for (_, _, vdup, _, _), e, den in zip(jobs, exps, dens):
        o = _mm(e, vdup) / den
        outs.append([jnp.where(m0, o[(2 * pi) * t:(2 * pi + 1) * t], o[(2 * pi + 1) * t:(2 * pi + 2) * t])
                     for pi in range(2)])
    return outs


SWA_SCALE = HEAD_DIM ** -0.5


def _dup_heads(x):
    lane = lax.broadcasted_iota(jnp.int32, x.shape, 1)
    m0 = lane < HEAD_DIM
    xs = pltpu.roll(x, HEAD_DIM, 1)
    return [jnp.where(m0, x, xs), jnp.where(m0, xs, x)]


SWA_TQ = 512


def _swa_prompt_kernel(q_ref, kp_ref, kc_ref, vp_ref, vc_ref, sink_ref, o_ref):
    i = pl.program_id(1)
    k = jnp.concatenate([kp_ref[...], kc_ref[...]], axis=0)
    v = jnp.concatenate([vp_ref[...], vc_ref[...]], axis=0)
    kd = _dup_heads(k * SWA_SCALE)
    vd = _dup_heads(v)
    nk = 3 * CHUNK
    slot = lax.broadcasted_iota(jnp.int32, (1, nk), 1) // CHUNK
    jobs = []
    for j in range(SWA_TQ // CHUNK):
        qj = q_ref[j * CHUNK:(j + 1) * CHUNK, :]
        valid = (slot + (i * (SWA_TQ // CHUNK) + j - 2)) >= 0
        for kv in range(N_KV_A):
            jobs.append((qj, kd[kv][j * CHUNK:j * CHUNK + nk], vd[kv][j * CHUNK:j * CHUNK + nk], kv, valid))
    outs = _sink_attend(jobs, sink_ref, CHUNK)
    for n, out in enumerate(outs):
        j, kv = divmod(n, N_KV_A)
        for pi in range(2):
            p = 2 * kv + pi
            o_ref[j * CHUNK:(j + 1) * CHUNK, p * LANES:(p + 1) * LANES] = out[pi]


def _swa_prompt(q, k, v, l, sink_b, b, t):
    n = b * t
    nq = t // SWA_TQ
    qmap = lambda bi, i: (bi * nq + i, 0)
    pmap = lambda bi, i: (jnp.maximum(bi * (t // WINDOW) + (SWA_TQ // WINDOW) * i - 1, 0), 0)
    const = lambda bi, i: (0, 0)
    return pl.pallas_call(
        _swa_prompt_kernel,
        grid=(b, nq),
        in_specs=[pl.BlockSpec((SWA_TQ, A_WIDTH), qmap),
                  pl.BlockSpec((WINDOW, KV_WIDTH_A), pmap), pl.BlockSpec((SWA_TQ, KV_WIDTH_A), qmap),
                  pl.BlockSpec((WINDOW, KV_WIDTH_A), pmap), pl.BlockSpec((SWA_TQ, KV_WIDTH_A), qmap),
                  _layer_block((8, LANES), l)],
        out_specs=pl.BlockSpec((SWA_TQ, A_WIDTH), qmap),
        out_shape=jax.ShapeDtypeStruct((n, A_WIDTH), F32),
        compiler_params=_params(("parallel", "parallel")),
        name="swa_prompt",
    )(q, k, k, v, v, sink_b)


def _swa_sample_kernel(q_ref, kc_ref, kn_ref, vc_ref, vn_ref, sink_ref, o_ref, *, t, nb):
    jobs = []
    for bi in range(nb):
        rows = slice(bi * t, (bi + 1) * t)
        kd = _dup_heads(jnp.concatenate([kc_ref[bi], kn_ref[rows, :]], axis=0) * SWA_SCALE)
        vd = _dup_heads(jnp.concatenate([vc_ref[bi], vn_ref[rows, :]], axis=0))
        for kv in range(N_KV_A):
            jobs.append((q_ref[rows, :], kd[kv], vd[kv], kv, None))
    outs = _sink_attend(jobs, sink_ref, t)
    for n, out in enumerate(outs):
        bi, kv = divmod(n, N_KV_A)
        for pi in range(2):
            p = 2 * kv + pi
            o_ref[bi * t:(bi + 1) * t, p * LANES:(p + 1) * LANES] = out[pi]


def _swa_sample(q, k, v, l, kc, vc, sink_b, b, t):
    n = b * t
    cache = kc.shape[2]
    nb = 4 if b % 4 == 0 else 1
    row = lambda bi: (bi, 0)
    cmap = lambda bi: (l, bi, 0, 0)
    return pl.pallas_call(
        functools.partial(_swa_sample_kernel, t=t, nb=nb),
        grid=(b // nb,),
        in_specs=[pl.BlockSpec((nb * t, A_WIDTH), row),
                  pl.BlockSpec((None, nb, cache, KV_WIDTH_A), cmap), pl.BlockSpec((nb * t, KV_WIDTH_A), row),
                  pl.BlockSpec((None, nb, cache, KV_WIDTH_A), cmap), pl.BlockSpec((nb * t, KV_WIDTH_A), row),
                  _layer_block((8, LANES), l)],
        out_specs=pl.BlockSpec((nb * t, A_WIDTH), row),
        out_shape=jax.ShapeDtypeStruct((n, A_WIDTH), F32),
        compiler_params=_params(("parallel",)),
        name="swa_sample",
    )(q, kc, k, vc, v, sink_b)


def _tri_inverse(nmat, masks, eye):
    p = eye + jnp.where(masks[0], nmat, 0.0)
    for m in masks[1:]:
        e = jnp.where(m, nmat, 0.0)
        p = p + _mm(_mm(p, e), p)
    return p


def _rwkv_kernel(pr_ref, sh0_ref, s0_ref, mu_ref, vec_ref, rk_ref, w2_ref, a2_ref, g2_ref, hsum_ref,
                 ltri_ref, out_ref, sfin_ref,
                 s_scr, carry_scr, rt_s, at_s, bt_s, kt_s, bb_s, kb_s, v_s, et_s, y_s, *, c_len, tt, nb):
    i = pl.program_id(1)
    n_i = pl.num_programs(1)
    n_rows = nb * tt

    @pl.when(i == 0)
    def _():
        zero = jnp.zeros((HEAD_DIM, HEAD_DIM), F32)
        for bi in range(nb):
            for p in range(4):
                s_scr[bi, p] = jnp.concatenate(
                    [jnp.concatenate([s0_ref[bi, 2 * p].astype(F32), zero], axis=1),
                     jnp.concatenate([zero, s0_ref[bi, 2 * p + 1].astype(F32)], axis=1)], axis=0)
        carry_scr[...] = sh0_ref[...]

    rowi = lax.broadcasted_iota(jnp.int32, (n_rows, 1), 0)

    def shifted(a, b):
        cur = pr_ref[:, :, a:b].reshape(n_rows, b - a)
        prev = pltpu.roll(cur, 1, 0)
        for bi in range(nb):
            prev = jnp.where(rowi == bi * tt, carry_scr[bi, :, a:b], prev)
        return cur + (prev - cur) * mu_ref[:, a:b]

    r = shifted(0, 512)
    k = shifted(512, 1024)
    v = shifted(1024, 1536)
    xwa = shifted(1536, 1664)
    xg = shifted(1664, RWKV_PAD)
    for bi in range(nb):
        carry_scr[bi] = pr_ref[bi, tt - 1:tt, :]

    w0 = vec_ref[0:1, :]
    a0 = vec_ref[1:2, :]
    k_k = vec_ref[2:3, :]
    k_a = vec_ref[3:4, :]
    gn_g = vec_ref[4:5, :]
    gn_b = vec_ref[5:6, :]
    hsum = hsum_ref[...]
    hw = hsum.shape[0]

    def head_sum(x):
        return jnp.concatenate([_mm_rsplit(x[:, j * hw:(j + 1) * hw], hsum) for j in range(R_WIDTH // hw)], axis=1)

    z = w0 + _mm(jnp.tanh(xwa), w2_ref[...])
    wlog = -EXP_M05 * jax.nn.sigmoid(z)
    a = jax.nn.sigmoid(a0 + _mm(xwa, a2_ref[...]))
    g = _mm(jax.nn.sigmoid(xg), g2_ref[...])
    kk = k * k_k
    kk = kk * lax.rsqrt(jnp.maximum(head_sum(kk * kk), 1e-24))
    k_f = k * (1.0 + (a - 1.0) * k_a)
    bvec = kk * a
    bonus = head_sum(r * k_f * rk_ref[...]) * v

    cum = _mm_lsplit(ltri_ref[...], wlog)
    tot = jnp.concatenate([jnp.broadcast_to(cum[r1 - 1:r1, :], (c_len, R_WIDTH))
                           for r1 in range(c_len, n_rows + 1, c_len)], axis=0)
    e_in = jnp.exp(cum)
    e_inv = jnp.exp(-cum)
    e_end = jnp.exp(tot - cum)
    rt_s[...] = r * e_in
    at_s[...] = -kk * jnp.exp(cum - wlog)
    bt_s[...] = bvec * e_inv
    kt_s[...] = k_f * e_inv
    bb_s[...] = bvec * e_end
    kb_s[...] = k_f * e_end
    v_s[...] = v
    et_s[...] = jnp.exp(tot)

    n2 = 2 * c_len
    ti = lax.broadcasted_iota(jnp.int32, (c_len, n2), 0)
    lane2 = lax.broadcasted_iota(jnp.int32, (c_len, n2), 1)
    si = lane2 & (c_len - 1)
    first_c = lane2 < c_len
    strict = ti > si
    incl = ti >= si
    eye = jnp.where(ti == si, 1.0, 0.0).astype(F32)
    masks = []
    half = 1
    while half < c_len:
        blk = 2 * half
        masks.append(((ti & ~(blk - 1)) == (si & ~(blk - 1))) & ((ti & half) != 0) & ((si & half) == 0))
        half = blk
    m0 = lax.broadcasted_iota(jnp.int32, (c_len, LANES), 1) < HEAD_DIM
    rl = lax.broadcasted_iota(jnp.int32, (LANES, LANES), 0)
    cl = lax.broadcasted_iota(jnp.int32, (LANES, LANES), 1)
    same_head = (rl >= HEAD_DIM) == (cl >= HEAD_DIM)
    fused = n2 == LANES
    zeros_c = jnp.zeros((c_len, LANES), F32)

    def bd(x):
        return jnp.concatenate([jnp.where(m0, x, 0.0), jnp.where(m0, 0.0, x)], axis=0)

    def bd_t(x):
        return jnp.concatenate([jnp.where(first_c, x, 0.0), jnp.where(first_c, 0.0, x)], axis=0)

    chains = [(bi, c, p) for bi in range(nb) for c in range(tt // c_len) for p in range(4)]

    def rows_lanes(ch):
        bi, c, p = ch
        r0 = bi * tt + c * c_len
        return slice(r0, r0 + c_len), slice(p * LANES, (p + 1) * LANES)

    def cat(ref, ch):
        rows, ls = rows_lanes(ch)
        return ref[rows, ls]

    a_ab, a_ak, a_rb, a_rk = {}, {}, {}, {}
    for ch in chains:
        at, rt, rb, rkt = cat(at_s, ch), cat(rt_s, ch), bd(cat(bt_s, ch)), bd(cat(kt_s, ch))
        if fused:
            amat = _mm_nt(jnp.concatenate([at, rt], axis=0), jnp.concatenate([rb, rkt], axis=0))
            q_ab, q_ak = amat[:c_len, :n2], amat[:c_len, n2:]
            q_rb, q_rk = amat[c_len:, :n2], amat[c_len:, n2:]
        else:
            q_ab, q_ak, q_rb, q_rk = _mm_nt(at, rb), _mm_nt(at, rkt), _mm_nt(rt, rb), _mm_nt(rt, rkt)
        a_ab[ch] = jnp.where(strict, q_ab, 0.0)
        a_ak[ch] = jnp.where(strict, q_ak, 0.0)
        a_rb[ch] = jnp.where(incl, q_rb, 0.0)
        a_rk[ch] = jnp.where(incl, q_rk, 0.0)

    tinv = {ch: eye + jnp.where(masks[0], a_ab[ch], 0.0) for ch in chains}
    for m in masks[1:]:
        pe = {ch: _mm(tinv[ch], bd_t(jnp.where(m, a_ab[ch], 0.0))) for ch in chains}
        tinv = {ch: tinv[ch] + _mm(pe[ch], bd_t(tinv[ch])) for ch in chains}

    zv = {ch: _mm(a_ak[ch], bd(cat(v_s, ch))) for ch in chains}
    r_hat, y_hat, m_mat, g_mat = {}, {}, {}, {}
    if fused:
        w = {ch: _mm(tinv[ch], jnp.concatenate([bd(cat(at_s, ch)), bd(zv[ch])], axis=1)) for ch in chains}
        for ch in chains:
            rhs2 = jnp.concatenate([jnp.concatenate([bd(w[ch][:, :LANES]), bd(w[ch][:, LANES:])], axis=1),
                                    jnp.concatenate([jnp.zeros((n2, LANES), F32), bd(cat(v_s, ch))], axis=1)], axis=0)
            ry = _mm(jnp.concatenate([a_rb[ch], a_rk[ch]], axis=1), rhs2)
            r_hat[ch] = cat(rt_s, ch) + ry[:, :LANES]
            y_hat[ch] = ry[:, LANES:]
        for ch in chains:
            lhs_t = jnp.concatenate([w[ch], jnp.concatenate([zeros_c, cat(v_s, ch)], axis=1)], axis=0)
            mg = _mm(lhs_t.T, jnp.concatenate([cat(bb_s, ch), cat(kb_s, ch)], axis=0))
            m_mat[ch] = jnp.where(same_head, mg[:LANES], 0.0)
            g_mat[ch] = jnp.where(same_head, mg[LANES:], 0.0)
    else:
        a_hat = {ch: _mm(tinv[ch], bd(cat(at_s, ch))) for ch in chains}
        u_hat = {ch: _mm(tinv[ch], bd(zv[ch])) for ch in chains}
        for ch in chains:
            r_hat[ch] = cat(rt_s, ch) + _mm(a_rb[ch], bd(a_hat[ch]))
            y_hat[ch] = _mm(a_rb[ch], bd(u_hat[ch])) + _mm(a_rk[ch], bd(cat(v_s, ch)))
        for ch in chains:
            m_mat[ch] = jnp.where(same_head, _mm(a_hat[ch].T, cat(bb_s, ch)), 0.0)
            g_mat[ch] = jnp.where(same_head, _mm(u_hat[ch].T, cat(bb_s, ch)) + _mm(cat(v_s, ch).T, cat(kb_s, ch)), 0.0)

    for bi in range(nb):
        s_cur = [s_scr[bi, p] for p in range(4)]
        for c in range(tt // c_len):
            for p in range(4):
                ch = (bi, c, p)
                rows, ls = rows_lanes(ch)
                e_last = et_s[rows.start:rows.start + 1, ls]
                y_s[rows, ls] = y_hat[ch] + _mm_nt(r_hat[ch], s_cur[p])
                s_cur[p] = s_cur[p] * e_last + _mm(s_cur[p], m_mat[ch]) + g_mat[ch]
        for p in range(4):
            s_scr[bi, p] = s_cur[p]

    y = y_s[...]
    mean = head_sum(y) * (1.0 / HEAD_DIM)
    yc = y - mean
    var = head_sum(yc * yc) * (1.0 / HEAD_DIM)
    yn = yc * lax.rsqrt(var + GN_EPS) * gn_g + gn_b
    res = (yn + bonus) * g
    for bi in range(nb):
        out_ref[bi] = res[bi * tt:(bi + 1) * tt]

    @pl.when(i == n_i - 1)
    def _():
        for bi in range(nb):
            for p in range(4):
                s_pair = s_scr[bi, p]
                sfin_ref[bi, 2 * p] = s_pair[:HEAD_DIM, :HEAD_DIM].astype(sfin_ref.dtype)
                sfin_ref[bi, 2 * p + 1] = s_pair[HEAD_DIM:, HEAD_DIM:].astype(sfin_ref.dtype)


def _rwkv(pr3, l, ls, shift0, s0, mu_p, vecs, rk_flat, w2p, a2p, g2p, hsum, tt, c_len, nb):
    b, t, _ = pr3.shape
    n_rows = nb * tt
    ri = jnp.arange(n_rows)[:, None]
    ci = jnp.arange(n_rows)[None, :]
    same = (ri // c_len) == (ci // c_len)
    ltri = (same & (ri >= ci)).astype(BF16)
    hw = hsum.shape[0]
    blk = lambda bi, i: (bi, i, 0)
    const = lambda bi, i: (0, 0)
    big = lambda: pltpu.VMEM((n_rows, R_WIDTH), F32)
    return pl.pallas_call(
        functools.partial(_rwkv_kernel, c_len=c_len, tt=tt, nb=nb),
        grid=(b // nb, t // tt),
        in_specs=[pl.BlockSpec((nb, tt, RWKV_PAD), blk),
                  pl.BlockSpec((None, nb, 1, RWKV_PAD), lambda bi, i: (ls, bi, 0, 0)),
                  pl.BlockSpec((None, nb, N_HEADS_R, HEAD_DIM, HEAD_DIM), lambda bi, i: (ls, bi, 0, 0, 0)),
                  _layer_block((1, RWKV_PAD), l), _layer_block((6, R_WIDTH), l), _layer_block((1, R_WIDTH), l),
                  _layer_block((LANES, R_WIDTH), l), _layer_block((LANES, R_WIDTH), l),
                  _layer_block((2 * LANES, R_WIDTH), l), pl.BlockSpec((hw, hw), const),
                  pl.BlockSpec((n_rows, n_rows), const)],
        out_specs=[pl.BlockSpec((nb, tt, R_WIDTH), blk),
                   pl.BlockSpec((nb, N_HEADS_R, HEAD_DIM, HEAD_DIM), lambda bi, i: (bi, 0, 0, 0))],
        out_shape=[jax.ShapeDtypeStruct((b, t, R_WIDTH), F32),
                   jax.ShapeDtypeStruct((b, N_HEADS_R, HEAD_DIM, HEAD_DIM), s0.dtype)],
        scratch_shapes=[pltpu.VMEM((nb, 4, LANES, LANES), F32), pltpu.VMEM((nb, 1, RWKV_PAD), F32),
                        big(), big(), big(), big(), big(), big(), big(), big(), big()],
        compiler_params=_params(("parallel", "arbitrary")),
        name="rwkv_mix",
    )(pr3, shift0, s0, mu_p, vecs, rk_flat, w2p, a2p, g2p, hsum, ltri)


def _post_kernel(attn_ref, rw_ref, x_ref, g_ref, wout_ref, wq_ref, wo_ref, mk_ref, mv_ref, o_ref, *, nb, t):
    m = _mm(attn_ref[...], wout_ref[0:A_WIDTH, :]) + _mm(rw_ref[...], wout_ref[A_WIDTH:, :])
    x1 = x_ref[...] + _rms(m, g_ref[NORM_MIX_POST:NORM_MIX_POST + 1, :])
    q = _mm(_rms(x1, g_ref[NORM_X_PRE:NORM_X_PRE + 1, :]), wq_ref[...])
    scale = MEM_HEAD_DIM ** -0.5
    jobs = [(bi, hd) for bi in range(nb) for hd in range(MEM_HEADS)]
    cols = lambda hd: slice(hd * MEM_HEAD_DIM, (hd + 1) * MEM_HEAD_DIM)
    head_rows = lambda hd: pl.ds(hd, MEM_LEN, stride=MEM_HEADS)
    scores = [_mm_nt(q[bi * t:(bi + 1) * t, cols(hd)], mk_ref[bi, head_rows(hd), :]) * scale for bi, hd in jobs]
    exps = [jnp.exp(s - jnp.max(s, axis=-1, keepdims=True)) for s in scores]
    outs = [_mm(e, mv_ref[bi, head_rows(hd), :]) / jnp.sum(e, axis=-1, keepdims=True)
            for (bi, hd), e in zip(jobs, exps)]
    row_blocks = [jnp.concatenate(outs[bi * MEM_HEADS:(bi + 1) * MEM_HEADS], axis=1) for bi in range(nb)]
    o = row_blocks[0] if nb == 1 else jnp.concatenate(row_blocks, axis=0)
    c = _mm(o, wo_ref[...])
    o_ref[...] = x1 + _rms(c, g_ref[NORM_X_POST:NORM_X_POST + 1, :])


def _post(attn, rw, x2d, l, gains, w_out, w_mq, w_mo, lm, mk, mv, nb, t, tiles_per_batch):
    n = x2d.shape[0]
    tm = nb * t
    row = lambda i: (i, 0)
    if nb == 1:
        mmap = lambda i: (lm, i // tiles_per_batch, 0, 0)
    else:
        mmap = lambda i: (lm, i, 0, 0)
    mem_rows = MEM_LEN * MEM_HEADS
    return pl.pallas_call(
        functools.partial(_post_kernel, nb=nb, t=t),
        grid=(n // tm,),
        in_specs=[pl.BlockSpec((tm, A_WIDTH), row), pl.BlockSpec((tm, R_WIDTH), row),
                  pl.BlockSpec((tm, D_MODEL), row), _layer_block((8, D_MODEL), l),
                  _layer_block((D_MODEL, D_MODEL), l), _layer_block((D_MODEL, MEM_WIDTH), l),
                  _layer_block((MEM_WIDTH, D_MODEL), l),
                  pl.BlockSpec((None, nb, mem_rows, MEM_HEAD_DIM), mmap),
                  pl.BlockSpec((None, nb, mem_rows, MEM_HEAD_DIM), mmap)],
        out_specs=pl.BlockSpec((tm, D_MODEL), row),
        out_shape=jax.ShapeDtypeStruct((n, D_MODEL), F32),
        compiler_params=_params(("parallel",)),
        name="post_mix",
    )(attn, rw, x2d, gains, w_out, w_mq, w_mo, mk, mv)


FFN_TILE = 1408


def _ffn_kernel(x_ref, g_ref, wg_ref, wu_ref, wd_ref, o_ref, h_scr, acc_scr):
    j = pl.program_id(1)

    @pl.when(j == 0)
    def _():
        h_scr[...] = _rms(x_ref[...], g_ref[NORM_FFN_PRE:NORM_FFN_PRE + 1, :]).astype(BF16)
        acc_scr[...] = jnp.zeros_like(acc_scr)

    h = h_scr[...]
    gate = jnp.dot(h, wg_ref[...], preferred_element_type=F32)
    up = jnp.dot(h, wu_ref[...], preferred_element_type=F32)
    act = (gate * jax.nn.sigmoid(gate)) * up
    acc_scr[...] += _mm(act, wd_ref[...])

    @pl.when(j == pl.num_programs(1) - 1)
    def _():
        o_ref[...] = x_ref[...] + _rms(acc_scr[...], g_ref[NORM_FFN_POST:NORM_FFN_POST + 1, :])


def _ffn(x2d, l, gains, w_gu, w_dn, tm):
    n = x2d.shape[0]
    nf = D_FF // FFN_TILE
    row = lambda i, j: (i, 0)
    return pl.pallas_call(
        _ffn_kernel,
        grid=(n // tm, nf),
        in_specs=[pl.BlockSpec((tm, D_MODEL), row), _layer_block((8, D_MODEL), l),
                  pl.BlockSpec((None, D_MODEL, FFN_TILE), lambda i, j: (l, 0, j)),
                  pl.BlockSpec((None, D_MODEL, FFN_TILE), lambda i, j: (l, 0, nf + j)),
                  pl.BlockSpec((None, FFN_TILE, D_MODEL), lambda i, j: (l, j, 0))],
        out_specs=pl.BlockSpec((tm, D_MODEL), row),
        out_shape=jax.ShapeDtypeStruct((n, D_MODEL), F32),
        scratch_shapes=[pltpu.VMEM((tm, D_MODEL), BF16), pltpu.VMEM((tm, D_MODEL), F32)],
        compiler_params=_params(("parallel", "arbitrary")),
        name="ffn",
    )(x2d, gains, w_gu, w_gu, w_dn)


def _memkv_kernel(x_ref, g_ref, w_ref, k_ref, v_ref):
    kv = _mm(_rms(x_ref[...], g_ref[NORM_MEM:NORM_MEM + 1, :]), w_ref[...])
    for hd in range(MEM_HEADS):
        rows = pl.ds(hd, MEM_LEN, stride=MEM_HEADS)
        k_ref[0, rows, :] = kv[:, hd * MEM_HEAD_DIM:(hd + 1) * MEM_HEAD_DIM]
        v_ref[0, rows, :] = kv[:, MEM_WIDTH + hd * MEM_HEAD_DIM:MEM_WIDTH + (hd + 1) * MEM_HEAD_DIM]


def _memkv(mem2d, l, gains, w_kv):
    n = mem2d.shape[0]
    nbatch = n // MEM_LEN
    mem_rows = MEM_LEN * MEM_HEADS
    out = jax.ShapeDtypeStruct((nbatch, mem_rows, MEM_HEAD_DIM), F32)
    ospec = pl.BlockSpec((1, mem_rows, MEM_HEAD_DIM), lambda i: (i, 0, 0))
    return pl.pallas_call(
        _memkv_kernel,
        grid=(nbatch,),
        in_specs=[pl.BlockSpec((MEM_LEN, D_MODEL), lambda i: (i, 0)), _layer_block((8, D_MODEL), l),
                  _layer_block((D_MODEL, 2 * MEM_WIDTH), l)],
        out_specs=[ospec, ospec],
        out_shape=[out, out],
        compiler_params=_params(("parallel",)),
        name="mem_kv",
    )(mem2d, gains, w_kv)


def _rope_tables(pos):
    half = HEAD_DIM // 2
    inv = ROPE_THETA ** (-jnp.arange(half, dtype=F32) / half)
    ang = pos.astype(F32)[:, None] * inv[None, :]
    cos = jnp.cos(ang)
    sin = jnp.sin(ang)
    cos_t = jnp.tile(cos, (1, LANES // half))
    sin_t = jnp.tile(jnp.concatenate([-sin, sin], axis=1), (1, LANES // HEAD_DIM))
    return cos_t, sin_t


def _stacked_params(norm_gains, w_in, attn_sink, shift_mu, rwkv_vecs, rwkv_rk, rwkv_w2, rwkv_a2, rwkv_g2,
                    w_out, w_mem_q, w_mem_kv, w_mem_o, w_gate_up, w_down):
    depth = w_in.shape[0]
    return dict(
        gains=jnp.pad(norm_gains, ((0, 0), (0, 8 - norm_gains.shape[1]), (0, 0))),
        w_in=jnp.pad(w_in, ((0, 0), (0, 0), (0, IN_COLS_PAD - w_in.shape[2]))).astype(BF16),
        sink=jnp.broadcast_to(attn_sink[:, :, None], attn_sink.shape + (LANES,)).astype(F32),
        mu=jnp.pad(shift_mu, ((0, 0), (0, RWKV_PAD - RWKV_PROJ)))[:, None, :],
        vecs=rwkv_vecs, rk=rwkv_rk.reshape(depth, 1, R_WIDTH),
        w2=jnp.pad(rwkv_w2, ((0, 0), (0, LANES - DECAY_LORA), (0, 0))).astype(BF16),
        a2=jnp.pad(rwkv_a2, ((0, 0), (DECAY_LORA, LANES - DECAY_LORA - AAA_LORA), (0, 0))).astype(BF16),
        g2=jnp.pad(rwkv_g2, ((0, 0), (0, 2 * LANES - GATE_LORA), (0, 0))).astype(BF16),
        w_out=w_out.astype(BF16), w_mq=w_mem_q.astype(BF16), w_mkv=w_mem_kv.astype(BF16),
        w_mo=w_mem_o.astype(BF16), w_gu=w_gate_up.astype(BF16), w_dn=w_down.astype(BF16))


def _head_sum_matrix():
    i = jnp.arange(2 * LANES)
    return ((i[:, None] // HEAD_DIM) == (i[None, :] // HEAD_DIM)).astype(BF16)


def _layer(x2d, l, b, t, tabs, lm, mk, mv, swa_cache, ls, s0, shift0, sp, hsum, in_tm, rw_tt, post_nb, post_t,
           ffn_tm):
    q, k, v, pr = _in_proj(x2d, l, sp["gains"], sp["w_in"], tabs[0], tabs[1], in_tm)
    if swa_cache is None:
        attn = _swa_prompt(q, k, v, l, sp["sink"], b, t)
    else:
        attn = _swa_sample(q, k, v, l, swa_cache[0], swa_cache[1], sp["sink"], b, t)
    pr3 = pr.reshape(b, t, RWKV_PAD)
    rw, s_fin = _rwkv(pr3, l, ls, shift0, s0, sp["mu"], sp["vecs"], sp["rk"], sp["w2"], sp["a2"], sp["g2"], hsum,
                      rw_tt[0], min(CHUNK, t), rw_tt[1])
    x2 = _post(attn, rw.reshape(b * t, R_WIDTH), x2d, l, sp["gains"], sp["w_out"], sp["w_mq"], sp["w_mo"],
               lm, mk, mv, post_nb, post_t, t // post_t)
    x3 = _ffn(x2, l, sp["gains"], sp["w_gu"], sp["w_dn"], ffn_tm)
    return x3, k, v, s_fin, pr3[:, t - 1:t, :RWKV_PROJ]


def kernel(x_prompt, mem_prompt, x_sample, cache_swa_k, cache_swa_v, cache_mem_k, cache_mem_v, state_rwkv,
           state_shift, norm_gains, w_in, attn_sink, shift_mu, rwkv_vecs, rwkv_rk, rwkv_w2, rwkv_a2, rwkv_g2,
           w_out, w_mem_q, w_mem_kv, w_mem_o, w_gate_up, w_down):
    b, t, _ = x_prompt.shape
    bd, tn, _ = x_sample.shape
    depth = w_in.shape[0]
    m_len = mem_prompt.shape[1]
    cache_len = cache_swa_k.shape[2]
    tabs_p = _rope_tables(jnp.arange(t, dtype=jnp.int32))
    cs, sn = _rope_tables(PAST_LEN + jnp.arange(tn, dtype=jnp.int32))
    tabs_s = (jnp.tile(cs, (bd, 1)), jnp.tile(sn, (bd, 1)))
    hsum = _head_sum_matrix()
    sp = _stacked_params(norm_gains, w_in, attn_sink, shift_mu, rwkv_vecs, rwkv_rk, rwkv_w2, rwkv_a2, rwkv_g2,
                         w_out, w_mem_q, w_mem_kv, w_mem_o, w_gate_up, w_down)
    s0_p = jnp.zeros((1, b, N_HEADS_R, HEAD_DIM, HEAD_DIM), F32)
    sh0_p = jnp.zeros((1, b, 1, RWKV_PAD), F32)
    sh0_s = jnp.pad(state_shift, ((0, 0), (0, 0), (0, 0), (0, RWKV_PAD - RWKV_PROJ)))
    swa_cache = (cache_swa_k.reshape(depth, bd, cache_len, KV_WIDTH_A),
                 cache_swa_v.reshape(depth, bd, cache_len, KV_WIDTH_A))
    mem_rows = m_len * MEM_HEADS
    cmk = cache_mem_k.reshape(depth, bd, mem_rows, MEM_HEAD_DIM)
    cmv = cache_mem_v.reshape(depth, bd, mem_rows, MEM_HEAD_DIM)
    xp = x_prompt.reshape(b * t, D_MODEL)
    xs = x_sample.reshape(bd * tn, D_MODEL)
    mem2d = mem_prompt.reshape(b * m_len, D_MODEL)
    in_tm_p = min(512, t)
    rw_tt_p = (min(256, t), 1)
    rw_tt_s = (tn, 4 if bd % 4 == 0 else 1)
    post_t_p = min(512, t)
    ffn_tm_p = min(512, b * t)
    keep = t - min(WINDOW, t)
    pk, pv, pmk, pmv, pS, psh = [], [], [], [], [], []
    sk, sv, sS, ssh = [], [], [], []
    for l in range(depth):
        mk_l, mv_l = _memkv(mem2d, l, sp["gains"], sp["w_mkv"])
        xp, k_l, v_l, s_l, sh_l = _layer(xp, l, b, t, tabs_p, 0, mk_l[None], mv_l[None], None, 0, s0_p, sh0_p, sp,
                                         hsum, in_tm_p, rw_tt_p, 1, post_t_p, ffn_tm_p)
        pk.append(k_l.reshape(b, t, KV_WIDTH_A)[:, keep:].reshape(b, t - keep, N_KV_A, HEAD_DIM))
        pv.append(v_l.reshape(b, t, KV_WIDTH_A)[:, keep:].reshape(b, t - keep, N_KV_A, HEAD_DIM))
        pmk.append(mk_l.reshape(b, m_len, MEM_HEADS, MEM_HEAD_DIM))
        pmv.append(mv_l.reshape(b, m_len, MEM_HEADS, MEM_HEAD_DIM))
        pS.append(s_l)
        psh.append(sh_l)

        xs, k2, v2, s2, sh2 = _layer(xs, l, bd, tn, tabs_s, l, cmk, cmv, swa_cache, l, state_rwkv, sh0_s, sp,
                                     hsum, bd * tn, rw_tt_s, bd, tn, bd * tn)
        sk.append(k2.reshape(bd, tn, N_KV_A, HEAD_DIM))
        sv.append(v2.reshape(bd, tn, N_KV_A, HEAD_DIM))
        sS.append(s2)
        ssh.append(sh2)
    return (xp.reshape(b, t, D_MODEL), xs.reshape(bd, tn, D_MODEL), jnp.stack(pk), jnp.stack(pv),
            jnp.stack(pmk), jnp.stack(pmv), jnp.stack(pS), jnp.stack(psh),
            jnp.stack(sk), jnp.stack(sv), jnp.stack(sS), jnp.stack(ssh))
```

```python
import functools
import math

import jax
import jax.numpy as jnp
from jax import lax
from jax.experimental import pallas as pl
from jax.experimental.pallas import tpu as pltpu

F32 = jnp.float32
BF16 = jnp.bfloat16

D_MODEL = 1024
HEAD_DIM = 64
CHUNK = 64
A_WIDTH = 512
KV_WIDTH_A = 128
N_KV_A = 2
WINDOW = 128
PAST_LEN = 4096
ROPE_THETA = 10000.0
R_WIDTH = 512
N_HEADS_R = 8
DECAY_LORA = 64
AAA_LORA = 64
GATE_LORA = 160
RWKV_PROJ = 3 * R_WIDTH + DECAY_LORA + AAA_LORA + GATE_LORA
RWKV_PAD = 1920
IN_COLS_PAD = A_WIDTH + 2 * KV_WIDTH_A + RWKV_PAD
GN_EPS = 6.4e-4
MEM_LEN = 256
MEM_HEADS = 4
MEM_HEAD_DIM = 128
MEM_WIDTH = 512
D_FF = 2816
RMS_EPS = 1e-6
NEG_INF = -1e30
NORM_MIX_PRE, NORM_MIX_POST, NORM_X_PRE, NORM_X_POST, NORM_MEM, NORM_FFN_PRE, NORM_FFN_POST = range(7)
EXP_M05 = math.exp(-0.5)

LANES = 128
VMEM_LIMIT = 56 * 1024 * 1024


def _params(sem):
    return pltpu.CompilerParams(dimension_semantics=sem, vmem_limit_bytes=VMEM_LIMIT)


def _rms(x, g):
    ms = jnp.mean(x * x, axis=-1, keepdims=True)
    return x * lax.rsqrt(ms + RMS_EPS) * g


def _mm(a, b):
    return jnp.dot(a.astype(BF16), b.astype(BF16), preferred_element_type=F32)


def _mm_nt(a, b):
    return lax.dot_general(a.astype(BF16), b.astype(BF16), (((1,), (1,)), ((), ())),
                           preferred_element_type=F32)


def _split(x):
    hi = x.astype(BF16)
    lo = (x - hi.astype(F32)).astype(BF16)
    return hi, lo


def _mm_lsplit(a_exact_bf16, x):
    hi, lo = _split(x)
    return (jnp.dot(a_exact_bf16, hi, preferred_element_type=F32)
            + jnp.dot(a_exact_bf16, lo, preferred_element_type=F32))


def _mm_rsplit(x, b_exact_bf16):
    hi, lo = _split(x)
    return (jnp.dot(hi, b_exact_bf16, preferred_element_type=F32)
            + jnp.dot(lo, b_exact_bf16, preferred_element_type=F32))


def _in_kernel(x_ref, g_ref, w_ref, cos_ref, sin_ref, q_ref, k_ref, v_ref, pr_ref):
    h = _rms(x_ref[...], g_ref[NORM_MIX_PRE:NORM_MIX_PRE + 1, :]).astype(BF16)
    p = jnp.dot(h, w_ref[...], preferred_element_type=F32)
    cos = cos_ref[...]
    sin = sin_ref[...]
    lane = lax.broadcasted_iota(jnp.int32, cos.shape, 1)
    first_half = (lane & (HEAD_DIM // 2)) == 0

    def rope(xc):
        sw = jnp.where(first_half, pltpu.roll(xc, LANES - HEAD_DIM // 2, 1), pltpu.roll(xc, HEAD_DIM // 2, 1))
        return xc * cos + sw * sin

    for j in range(A_WIDTH // LANES):
        q_ref[:, j * LANES:(j + 1) * LANES] = rope(p[:, j * LANES:(j + 1) * LANES])
    k_ref[...] = rope(p[:, A_WIDTH:A_WIDTH + KV_WIDTH_A])
    v_ref[...] = p[:, A_WIDTH + KV_WIDTH_A:A_WIDTH + 2 * KV_WIDTH_A]
    pr_ref[...] = p[:, A_WIDTH + 2 * KV_WIDTH_A:]


def _layer_block(shape, l):
    nd = len(shape)
    return pl.BlockSpec((None,) + tuple(shape), lambda *_: (l,) + (0,) * nd)


def _in_proj(x2d, l, gains, w_in_p, cos_t, sin_t, tm):
    n = x2d.shape[0]
    tab_blocks = cos_t.shape[0] // tm
    row = lambda i: (i, 0)
    tab = lambda i: (i % tab_blocks, 0)
    return pl.pallas_call(
        _in_kernel,
        grid=(n // tm,),
        in_specs=[pl.BlockSpec((tm, D_MODEL), row), _layer_block((8, D_MODEL), l),
                  _layer_block((D_MODEL, IN_COLS_PAD), l),
                  pl.BlockSpec((tm, LANES), tab), pl.BlockSpec((tm, LANES), tab)],
        out_specs=[pl.BlockSpec((tm, A_WIDTH), row), pl.BlockSpec((tm, KV_WIDTH_A), row),
                   pl.BlockSpec((tm, KV_WIDTH_A), row), pl.BlockSpec((tm, RWKV_PAD), row)],
        out_shape=[jax.ShapeDtypeStruct((n, A_WIDTH), F32), jax.ShapeDtypeStruct((n, KV_WIDTH_A), F32),
                   jax.ShapeDtypeStruct((n, KV_WIDTH_A), F32), jax.ShapeDtypeStruct((n, RWKV_PAD), F32)],
        compiler_params=_params(("parallel",)),
        name="in_proj",
    )(x2d, gains, w_in_p, cos_t, sin_t)


def _sink_attend(jobs, sink_ref, t):
    lane = lax.broadcasted_iota(jnp.int32, (t, LANES), 1)
    m0 = lane < HEAD_DIM
    rowi = lax.broadcasted_iota(jnp.int32, (4 * t, 1), 0)
    sinks = []
    for kv in range(N_KV_A):
        sk = jnp.zeros((4 * t, 1), F32)
        for g in range(4):
            sg = sink_ref[4 * kv + g:4 * kv + g + 1, 0:1]
            sk = jnp.where((rowi >= g * t) & (rowi < (g + 1) * t), sg, sk)
        sinks.append(sk)
    scores = []
    for q_rows, kdup, _, kv, valid in jobs:
        parts = []
        for p in (2 * kv, 2 * kv + 1):
            qp = q_rows[:, p * LANES:(p + 1) * LANES]
            parts.append(jnp.where(m0, qp, 0.0))
            parts.append(jnp.where(m0, 0.0, qp))
        lhs = jnp.concatenate(parts, axis=0)
        s = _mm_nt(lhs, kdup)
        scores.append(s if valid is None else jnp.where(valid, s, NEG_INF))
    exps, dens = [], []
    for (_, _, _, kv, _), s in zip(jobs, scores):
        m = jnp.maximum(jnp.max(s, axis=-1, keepdims=True), sinks[kv])
        e = jnp.exp(s - m)
        exps.append(e)
        dens.append(jnp.sum(e, axis=-1, keepdims=True) + jnp.exp(sinks[kv] - m))
    outs = []
    for (_, _, vdup, _, _), e, den in zip(jobs, exps, dens):
        o = _mm(e, vdup) / den
        outs.append([jnp.where(m0, o[(2 * pi) * t:(2 * pi + 1) * t], o[(2 * pi + 1) * t:(2 * pi + 2) * t])
                     for pi in range(2)])
    return outs


SWA_SCALE = HEAD_DIM ** -0.5


def _dup_heads(x):
    lane = lax.broadcasted_iota(jnp.int32, x.shape, 1)
    m0 = lane < HEAD_DIM
    xs = pltpu.roll(x, HEAD_DIM, 1)
    return [jnp.where(m0, x, xs), jnp.where(m0, xs, x)]


SWA_TQ = 512


def _swa_prompt_kernel(q_ref, kp_ref, kc_ref, vp_ref, vc_ref, sink_ref, o_ref):
    i = pl.program_id(1)
    k = jnp.concatenate([kp_ref[...], kc_ref[...]], axis=0)
    v = jnp.concatenate([vp_ref[...], vc_ref[...]], axis=0)
    kd = _dup_heads(k * SWA_SCALE)
    vd = _dup_heads(v)
    nk = 3 * CHUNK
    slot = lax.broadcasted_iota(jnp.int32, (1, nk), 1) // CHUNK
    jobs = []
    for j in range(SWA_TQ // CHUNK):
        qj = q_ref[j * CHUNK:(j + 1) * CHUNK, :]
        valid = (slot + (i * (SWA_TQ // CHUNK) + j - 2)) >= 0
        for kv in range(N_KV_A):
            jobs.append((qj, kd[kv][j * CHUNK:j * CHUNK + nk], vd[kv][j * CHUNK:j * CHUNK + nk], kv, valid))
    outs = _sink_attend(jobs, sink_ref, CHUNK)
    for n, out in enumerate(outs):
        j, kv = divmod(n, N_KV_A)
        for pi in range(2):
            p = 2 * kv + pi
            o_ref[j * CHUNK:(j + 1) * CHUNK, p * LANES:(p + 1) * LANES] = out[pi]


def _swa_prompt(q, k, v, l, sink_b, b, t):
    n = b * t
    nq = t // SWA_TQ
    qmap = lambda bi, i: (bi * nq + i, 0)
    pmap = lambda bi, i: (jnp.maximum(bi * (t // WINDOW) + (SWA_TQ // WINDOW) * i - 1, 0), 0)
    const = lambda bi, i: (0, 0)
    return pl.pallas_call(
        _swa_prompt_kernel,
        grid=(b, nq),
        in_specs=[pl.BlockSpec((SWA_TQ, A_WIDTH), qmap),
                  pl.BlockSpec((WINDOW, KV_WIDTH_A), pmap), pl.BlockSpec((SWA_TQ, KV_WIDTH_A), qmap),
                  pl.BlockSpec((WINDOW, KV_WIDTH_A), pmap), pl.BlockSpec((SWA_TQ, KV_WIDTH_A), qmap),
                  _layer_block((8, LANES), l)],
        out_specs=pl.BlockSpec((SWA_TQ, A_WIDTH), qmap),
        out_shape=jax.ShapeDtypeStruct((n, A_WIDTH), F32),
        compiler_params=_params(("parallel", "parallel")),
        name="swa_prompt",
    )(q, k, k, v, v, sink_b)


def _swa_sample_kernel(q_ref, kc_ref, kn_ref, vc_ref, vn_ref, sink_ref, o_ref, *, t, nb):
    jobs = []
    for bi in range(nb):
        rows = slice(bi * t, (bi + 1) * t)
        kd = _dup_heads(jnp.concatenate([kc_ref[bi], kn_ref[rows, :]], axis=0) * SWA_SCALE)
        vd = _dup_heads(jnp.concatenate([vc_ref[bi], vn_ref[rows, :]], axis=0))
        for kv in range(N_KV_A):
            jobs.append((q_ref[rows, :], kd[kv], vd[kv], kv, None))
    outs = _sink_attend(jobs, sink_ref, t)
    for n, out in enumerate(outs):
        bi, kv = divmod(n, N_KV_A)
        for pi in range(2):
            p = 2 * kv + pi
            o_ref[bi * t:(bi + 1) * t, p * LANES:(p + 1) * LANES] = out[pi]


def _swa_sample(q, k, v, l, kc, vc, sink_b, b, t):
    n = b * t
    cache = kc.shape[2]
    nb = 4 if b % 4 == 0 else 1
    row = lambda bi: (bi, 0)
    cmap = lambda bi: (l, bi, 0, 0)
    return pl.pallas_call(
        functools.partial(_swa_sample_kernel, t=t, nb=nb),
        grid=(b // nb,),
        in_specs=[pl.BlockSpec((nb * t, A_WIDTH), row),
                  pl.BlockSpec((None, nb, cache, KV_WIDTH_A), cmap), pl.BlockSpec((nb * t, KV_WIDTH_A), row),
                  pl.BlockSpec((None, nb, cache, KV_WIDTH_A), cmap), pl.BlockSpec((nb * t, KV_WIDTH_A), row),
                  _layer_block((8, LANES), l)],
        out_specs=pl.BlockSpec((nb * t, A_WIDTH), row),
        out_shape=jax.ShapeDtypeStruct((n, A_WIDTH), F32),
        compiler_params=_params(("parallel",)),
        name="swa_sample",
    )(q, kc, k, vc, v, sink_b)


def _tri_inverse(nmat, masks, eye):
    p = eye + jnp.where(masks[0], nmat, 0.0)
    for m in masks[1:]:
        e = jnp.where(m, nmat, 0.0)
        p = p + _mm(_mm(p, e), p)
    return p


def _rwkv_kernel_old(pr_ref, sh0_ref, s0_ref, mu_ref, vec_ref, rk_ref, w2_ref, a2_ref, g2_ref, hsum_ref,
                     ltri_ref, out_ref, sfin_ref,
                     s_scr, carry_scr, rt_s, at_s, bt_s, kt_s, bb_s, kb_s, v_s, et_s, y_s, *, c_len, tt, nb):
    i = pl.program_id(1)
    n_i = pl.num_programs(1)
    n_rows = nb * tt

    @pl.when(i == 0)
    def _():
        zero = jnp.zeros((HEAD_DIM, HEAD_DIM), F32)
        for bi in range(nb):
            for p in range(4):
                s_scr[bi, p] = jnp.concatenate(
                    [jnp.concatenate([s0_ref[bi, 2 * p].astype(F32), zero], axis=1),
                     jnp.concatenate([zero, s0_ref[bi, 2 * p + 1].astype(F32)], axis=1)], axis=0)
        carry_scr[...] = sh0_ref[...]

    rowi = lax.broadcasted_iota(jnp.int32, (n_rows, 1), 0)

    def shifted(a, b):
        cur = pr_ref[:, :, a:b].reshape(n_rows, b - a)
        prev = pltpu.roll(cur, 1, 0)
        for bi in range(nb):
            prev = jnp.where(rowi == bi * tt, carry_scr[bi, :, a:b], prev)
        return cur + (prev - cur) * mu_ref[:, a:b]

    r = shifted(0, 512)
    k = shifted(512, 1024)
    v = shifted(1024, 1536)
    xwa = shifted(1536, 1664)
    xg = shifted(1664, RWKV_PAD)
    for bi in range(nb):
        carry_scr[bi] = pr_ref[bi, tt - 1:tt, :]

    w0 = vec_ref[0:1, :]
    a0 = vec_ref[1:2, :]
    k_k = vec_ref[2:3, :]
    k_a = vec_ref[3:4, :]
    gn_g = vec_ref[4:5, :]
    gn_b = vec_ref[5:6, :]
    hsum = hsum_ref[...]
    hw = hsum.shape[0]

    def head_sum(x):
        return jnp.concatenate([_mm_rsplit(x[:, j * hw:(j + 1) * hw], hsum) for j in range(R_WIDTH // hw)], axis=1)

    z = w0 + _mm(jnp.tanh(xwa), w2_ref[...])
    wlog = -EXP_M05 * jax.nn.sigmoid(z)
    a = jax.nn.sigmoid(a0 + _mm(xwa, a2_ref[...]))
    g = _mm(jax.nn.sigmoid(xg), g2_ref[...])
    kk = k * k_k
    kk = kk * lax.rsqrt(jnp.maximum(head_sum(kk * kk), 1e-24))
    k_f = k * (1.0 + (a - 1.0) * k_a)
    bvec = kk * a
    bonus = head_sum(r * k_f * rk_ref[...]) * v

    cum = _mm_lsplit(ltri_ref[...], wlog)
    tot = jnp.concatenate([jnp.broadcast_to(cum[r1 - 1:r1, :], (c_len, R_WIDTH))
                           for r1 in range(c_len, n_rows + 1, c_len)], axis=0)
    e_in = jnp.exp(cum)
    e_inv = jnp.exp(-cum)
    e_end = jnp.exp(tot - cum)
    rt_s[...] = r * e_in
    at_s[...] = -kk * jnp.exp(cum - wlog)
    bt_s[...] = bvec * e_inv
    kt_s[...] = k_f * e_inv
    bb_s[...] = bvec * e_end
    kb_s[...] = k_f * e_end
    v_s[...] = v
    et_s[...] = jnp.exp(tot)

    n2 = 2 * c_len
    ti = lax.broadcasted_iota(jnp.int32, (c_len, n2), 0)
    lane2 = lax.broadcasted_iota(jnp.int32, (c_len, n2), 1)
    si = lane2 & (c_len - 1)
    first_c = lane2 < c_len
    strict = ti > si
    incl = ti >= si
    eye = jnp.where(ti == si, 1.0, 0.0).astype(F32)
    masks = []
    half = 1
    while half < c_len:
        blk = 2 * half
        masks.append(((ti & ~(blk - 1)) == (si & ~(blk - 1))) & ((ti & half) != 0) & ((si & half) == 0))
        half = blk
    m0 = lax.broadcasted_iota(jnp.int32, (c_len, LANES), 1) < HEAD_DIM
    rl = lax.broadcasted_iota(jnp.int32, (LANES, LANES), 0)
    cl = lax.broadcasted_iota(jnp.int32, (LANES, LANES), 1)
    same_head = (rl >= HEAD_DIM) == (cl >= HEAD_DIM)
    fused = n2 == LANES
    zeros_c = jnp.zeros((c_len, LANES), F32)

    def bd(x):
        return jnp.concatenate([jnp.where(m0, x, 0.0), jnp.where(m0, 0.0, x)], axis=0)

    def bd_t(x):
        return jnp.concatenate([jnp.where(first_c, x, 0.0), jnp.where(first_c, 0.0, x)], axis=0)

    chains = [(bi, c, p) for bi in range(nb) for c in range(tt // c_len) for p in range(4)]

    def rows_lanes(ch):
        bi, c, p = ch
        r0 = bi * tt + c * c_len
        return slice(r0, r0 + c_len), slice(p * LANES, (p + 1) * LANES)

    def cat(ref, ch):
        rows, ls = rows_lanes(ch)
        return ref[rows, ls]

    a_ab, a_ak, a_rb, a_rk = {}, {}, {}, {}
    for ch in chains:
        at, rt, rb, rkt = cat(at_s, ch), cat(rt_s, ch), bd(cat(bt_s, ch)), bd(cat(kt_s, ch))
        if fused:
            amat = _mm_nt(jnp.concatenate([at, rt], axis=0), jnp.concatenate([rb, rkt], axis=0))
            q_ab, q_ak = amat[:c_len, :n2], amat[:c_len, n2:]
            q_rb, q_rk = amat[c_len:, :n2], amat[c_len:, n2:]
        else:
            q_ab, q_ak, q_rb, q_rk = _mm_nt(at, rb), _mm_nt(at, rkt), _mm_nt(rt, rb), _mm_nt(rt, rkt)
        a_ab[ch] = jnp.where(strict, q_ab, 0.0)
        a_ak[ch] = jnp.where(strict, q_ak, 0.0)
        a_rb[ch] = jnp.where(incl, q_rb, 0.0)
        a_rk[ch] = jnp.where(incl, q_rk, 0.0)

    tinv = {ch: eye + jnp.where(masks[0], a_ab[ch], 0.0) for ch in chains}
    for m in masks[1:]:
        pe = {ch: _mm(tinv[ch], bd_t(jnp.where(m, a_ab[ch], 0.0))) for ch in chains}
        tinv = {ch: tinv[ch] + _mm(pe[ch], bd_t(tinv[ch])) for ch in chains}

    zv = {ch: _mm(a_ak[ch], bd(cat(v_s, ch))) for ch in chains}
    r_hat, y_hat, m_mat, g_mat = {}, {}, {}, {}
    if fused:
        w = {ch: _mm(tinv[ch], jnp.concatenate([bd(cat(at_s, ch)), bd(zv[ch])], axis=1)) for ch in chains}
        for ch in chains:
            rhs2 = jnp.concatenate([jnp.concatenate([bd(w[ch][:, :LANES]), bd(w[ch][:, LANES:])], axis=1),
                                    jnp.concatenate([jnp.zeros((n2, LANES), F32), bd(cat(v_s, ch))], axis=1)], axis=0)
            ry = _mm(jnp.concatenate([a_rb[ch], a_rk[ch]], axis=1), rhs2)
            r_hat[ch] = cat(rt_s, ch) + ry[:, :LANES]
            y_hat[ch] = ry[:, LANES:]
        for ch in chains:
            lhs_t = jnp.concatenate([w[ch], jnp.concatenate([zeros_c, cat(v_s, ch)], axis=1)], axis=0)
            mg = _mm(lhs_t.T, jnp.concatenate([cat(bb_s, ch), cat(kb_s, ch)], axis=0))
            m_mat[ch] = jnp.where(same_head, mg[:LANES], 0.0)
            g_mat[ch] = jnp.where(same_head, mg[LANES:], 0.0)
    else:
        a_hat = {ch: _mm(tinv[ch], bd(cat(at_s, ch))) for ch in chains}
        u_hat = {ch: _mm(tinv[ch], bd(zv[ch])) for ch in chains}
        for ch in chains:
            r_hat[ch] = cat(rt_s, ch) + _mm(a_rb[ch], bd(a_hat[ch]))
            y_hat[ch] = _mm(a_rb[ch], bd(u_hat[ch])) + _mm(a_rk[ch], bd(cat(v_s, ch)))
        for ch in chains:
            m_mat[ch] = jnp.where(same_head, _mm(a_hat[ch].T, cat(bb_s, ch)), 0.0)
            g_mat[ch] = jnp.where(same_head, _mm(u_hat[ch].T, cat(bb_s, ch)) + _mm(cat(v_s, ch).T, cat(kb_s, ch)), 0.0)

    for bi in range(nb):
        s_cur = [s_scr[bi, p] for p in range(4)]
        for c in range(tt // c_len):
            for p in range(4):
                ch = (bi, c, p)
                rows, ls = rows_lanes(ch)
                e_last = et_s[rows.start:rows.start + 1, ls]
                y_s[rows, ls] = y_hat[ch] + _mm_nt(r_hat[ch], s_cur[p])
                s_cur[p] = s_cur[p] * e_last + _mm(s_cur[p], m_mat[ch]) + g_mat[ch]
        for p in range(4):
            s_scr[bi, p] = s_cur[p]

    y = y_s[...]
    mean = head_sum(y) * (1.0 / HEAD_DIM)
    yc = y - mean
    var = head_sum(yc * yc) * (1.0 / HEAD_DIM)
    yn = yc * lax.rsqrt(var + GN_EPS) * gn_g + gn_b
    res = (yn + bonus) * g
    for bi in range(nb):
        out_ref[bi] = res[bi * tt:(bi + 1) * tt]

    @pl.when(i == n_i - 1)
    def _():
        for bi in range(nb):
            for p in range(4):
                s_pair = s_scr[bi, p]
                sfin_ref[bi, 2 * p] = s_pair[:HEAD_DIM, :HEAD_DIM].astype(sfin_ref.dtype)
                sfin_ref[bi, 2 * p + 1] = s_pair[HEAD_DIM:, HEAD_DIM:].astype(sfin_ref.dtype)


RWKV_SUB = 256


def _interleave(gens):
    gens = list(gens)
    while gens:
        for g in list(gens):
            try:
                next(g)
            except StopIteration:
                gens.remove(g)


def _rwkv_kernel(pr_ref, sh0_ref, s0_ref, mu_ref, vec_ref, rk_ref, w2_ref, a2_ref, g2_ref, hsum_ref,
                 ltri_ref, out_ref, sfin_ref,
                 s_scr, carry_scr, rt_s, at_s, bt_s, kt_s, bb_s, kb_s, v_s, et_s, y_s, bonus_s, gate_s,
                 *, c_len, tt, nb, sub):
    i = pl.program_id(1)
    n_i = pl.num_programs(1)
    n_rows = nb * tt

    @pl.when(i == 0)
    def _():
        zero = jnp.zeros((HEAD_DIM, HEAD_DIM), F32)
        for bi in range(nb):
            for p in range(4):
                s_scr[bi, p] = jnp.concatenate(
                    [jnp.concatenate([s0_ref[bi, 2 * p].astype(F32), zero], axis=1),
                     jnp.concatenate([zero, s0_ref[bi, 2 * p + 1].astype(F32)], axis=1)], axis=0)
        carry_scr[...] = sh0_ref[...]

    w0 = vec_ref[0:1, :]
    a0 = vec_ref[1:2, :]
    k_k = vec_ref[2:3, :]
    k_a = vec_ref[3:4, :]
    gn_g = vec_ref[4:5, :]
    gn_b = vec_ref[5:6, :]
    hsum = hsum_ref[...]
    hw = hsum.shape[0]
    halves = [slice(j * hw, (j + 1) * hw) for j in range(R_WIDTH // hw)]

    def head_sum(x):
        return _mm(x, hsum)

    n2 = 2 * c_len
    ti = lax.broadcasted_iota(jnp.int32, (c_len, n2), 0)
    lane2 = lax.broadcasted_iota(jnp.int32, (c_len, n2), 1)
    si = lane2 & (c_len - 1)
    first_c = lane2 < c_len
    strict = ti > si
    incl = ti >= si
    eye = jnp.where(ti == si, 1.0, 0.0).astype(F32)
    masks = []
    half = 1
    while half < c_len:
        blk = 2 * half
        masks.append(((ti & ~(blk - 1)) == (si & ~(blk - 1))) & ((ti & half) != 0) & ((si & half) == 0))
        half = blk
    m0 = lax.broadcasted_iota(jnp.int32, (c_len, LANES), 1) < HEAD_DIM
    rl = lax.broadcasted_iota(jnp.int32, (LANES, LANES), 0)
    cl = lax.broadcasted_iota(jnp.int32, (LANES, LANES), 1)
    same_head = (rl >= HEAD_DIM) == (cl >= HEAD_DIM)
    fused = n2 == LANES
    zeros_c = jnp.zeros((c_len, LANES), F32)

    def bd(x):
        return jnp.concatenate([jnp.where(m0, x, 0.0), jnp.where(m0, 0.0, x)], axis=0)

    def bd_t(x):
        return jnp.concatenate([jnp.where(first_c, x, 0.0), jnp.where(first_c, 0.0, x)], axis=0)

    def prologue(lo, hi):
        n = hi - lo
        bi0 = lo // tt
        rowi = lax.broadcasted_iota(jnp.int32, (n, 1), 0)

        def shifted(a, b):
            if hi - lo <= tt:
                cur = pr_ref[bi0, lo - bi0 * tt:hi - bi0 * tt, a:b]
            else:
                cur = pr_ref[bi0:hi // tt, :, a:b].reshape(n, b - a)
            prev = pltpu.roll(cur, 1, 0)
            for r in range(lo, hi):
                if r % tt == 0:
                    prev = jnp.where(rowi == r - lo, carry_scr[r // tt, :, a:b], prev)
            if lo % tt != 0:
                prev = jnp.where(rowi == 0, pr_ref[bi0, lo - bi0 * tt - 1:lo - bi0 * tt, a:b], prev)
            return cur + (prev - cur) * mu_ref[:, a:b]

        xwa = shifted(3 * R_WIDTH, 3 * R_WIDTH + LANES)
        xg = shifted(3 * R_WIDTH + LANES, RWKV_PAD)
        th = jnp.tanh(xwa)
        sg = jax.nn.sigmoid(xg)
        ltri = ltri_ref[...]
        yield
        for cs in halves:
            r = shifted(cs.start, cs.stop)
            k = shifted(R_WIDTH + cs.start, R_WIDTH + cs.stop)
            v = shifted(2 * R_WIDTH + cs.start, 2 * R_WIDTH + cs.stop)
            yield
            z = w0[:, cs] + _mm(th, w2_ref[:, cs])
            wlog = -EXP_M05 * jax.nn.sigmoid(z)
            a = jax.nn.sigmoid(a0[:, cs] + _mm(xwa, a2_ref[:, cs]))
            gate_s[lo:hi, cs] = _mm(sg, g2_ref[:, cs])
            yield
            kk = k * k_k[:, cs]
            kk = kk * lax.rsqrt(jnp.maximum(head_sum(kk * kk), 1e-24))
            k_f = k * (1.0 + (a - 1.0) * k_a[:, cs])
            bvec = kk * a
            bonus_s[lo:hi, cs] = head_sum(r * k_f * rk_ref[:, cs]) * v
            yield
            cum = _mm_lsplit(ltri, wlog)
            tot = jnp.concatenate([jnp.broadcast_to(cum[r1 - 1:r1, :], (c_len, hw))
                                   for r1 in range(c_len, n + 1, c_len)], axis=0)
            e_in = jnp.exp(cum)
            e_inv = jnp.exp(-cum)
            e_end = jnp.exp(tot - cum)
            rt_s[lo:hi, cs] = r * e_in
            at_s[lo:hi, cs] = -kk * jnp.exp(cum - wlog)
            yield
            bt_s[lo:hi, cs] = bvec * e_inv
            kt_s[lo:hi, cs] = k_f * e_inv
            bb_s[lo:hi, cs] = bvec * e_end
            kb_s[lo:hi, cs] = k_f * e_end
            v_s[lo:hi, cs] = v
            et_s[lo:hi, cs] = jnp.exp(tot)
            yield

    s_state = [[s_scr[bi, p] for p in range(4)] for bi in range(nb)]

    def chains(lo, hi):
        chs = [(r0, p) for r0 in range(lo, hi, c_len) for p in range(4)]

        def cat(ref, ch):
            return ref[ch[0]:ch[0] + c_len, ch[1] * LANES:(ch[1] + 1) * LANES]

        a_ab, a_ak, a_rb, a_rk = {}, {}, {}, {}
        for ch in chs:
            at, rt, rb, rkt = cat(at_s, ch), cat(rt_s, ch), bd(cat(bt_s, ch)), bd(cat(kt_s, ch))
            if fused:
                amat = _mm_nt(jnp.concatenate([at, rt], axis=0), jnp.concatenate([rb, rkt], axis=0))
                q_ab, q_ak = amat[:c_len, :n2], amat[:c_len, n2:]
                q_rb, q_rk = amat[c_len:, :n2], amat[c_len:, n2:]
            else:
                q_ab, q_ak, q_rb, q_rk = _mm_nt(at, rb), _mm_nt(at, rkt), _mm_nt(rt, rb), _mm_nt(rt, rkt)
            a_ab[ch] = jnp.where(strict, q_ab, 0.0)
            a_ak[ch] = jnp.where(strict, q_ak, 0.0)
            a_rb[ch] = jnp.where(incl, q_rb, 0.0)
            a_rk[ch] = jnp.where(incl, q_rk, 0.0)
        yield
        tinv = {ch: eye + jnp.where(masks[0], a_ab[ch], 0.0) for ch in chs}
        for m in masks[1:]:
            pe = {ch: _mm(tinv[ch], bd_t(jnp.where(m, a_ab[ch], 0.0))) for ch in chs}
            yield
            tinv = {ch: tinv[ch] + _mm(pe[ch], bd_t(tinv[ch])) for ch in chs}
            yield
        zv = {ch: _mm(a_ak[ch], bd(cat(v_s, ch))) for ch in chs}
        yield
        r_hat, y_hat, m_mat, g_mat = {}, {}, {}, {}
        if fused:
            w = {ch: _mm(tinv[ch], jnp.concatenate([bd(cat(at_s, ch)), bd(zv[ch])], axis=1)) for ch in chs}
            yield
            for ch in chs:
                rhs2 = jnp.concatenate(
                    [jnp.concatenate([bd(w[ch][:, :LANES]), bd(w[ch][:, LANES:])], axis=1),
                     jnp.concatenate([jnp.zeros((n2, LANES), F32), bd(cat(v_s, ch))], axis=1)], axis=0)
                ry = _mm(jnp.concatenate([a_rb[ch], a_rk[ch]], axis=1), rhs2)
                r_hat[ch] = cat(rt_s, ch) + ry[:, :LANES]
                y_hat[ch] = ry[:, LANES:]
            yield
            for ch in chs:
                lhs_t = jnp.concatenate([w[ch], jnp.concatenate([zeros_c, cat(v_s, ch)], axis=1)], axis=0)
                mg = _mm(lhs_t.T, jnp.concatenate([cat(bb_s, ch), cat(kb_s, ch)], axis=0))
                m_mat[ch] = jnp.where(same_head, mg[:LANES], 0.0)
                g_mat[ch] = jnp.where(same_head, mg[LANES:], 0.0)
            yield
        else:
            a_hat = {ch: _mm(tinv[ch], bd(cat(at_s, ch))) for ch in chs}
            u_hat = {ch: _mm(tinv[ch], bd(zv[ch])) for ch in chs}
            yield
            for ch in chs:
                r_hat[ch] = cat(rt_s, ch) + _mm(a_rb[ch], bd(a_hat[ch]))
                y_hat[ch] = _mm(a_rb[ch], bd(u_hat[ch])) + _mm(a_rk[ch], bd(cat(v_s, ch)))
            yield
            for ch in chs:
                m_mat[ch] = jnp.where(same_head, _mm(a_hat[ch].T, cat(bb_s, ch)), 0.0)
                g_mat[ch] = jnp.where(same_head,
                                      _mm(u_hat[ch].T, cat(bb_s, ch)) + _mm(cat(v_s, ch).T, cat(kb_s, ch)), 0.0)
            yield
        for r0 in range(lo, hi, c_len):
            s_cur = s_state[r0 // tt]
            for p in range(4):
                ch = (r0, p)
                ls = slice(p * LANES, (p + 1) * LANES)
                e_last = et_s[r0:r0 + 1, ls]
                y_s[r0:r0 + c_len, ls] = y_hat[ch] + _mm_nt(r_hat[ch], s_cur[p])
                s_cur[p] = s_cur[p] * e_last + _mm(s_cur[p], m_mat[ch]) + g_mat[ch]
            yield

    def epilogue(lo, hi):
        for cs in halves:
            y = y_s[lo:hi, cs]
            mean = head_sum(y) * (1.0 / HEAD_DIM)
            yc = y - mean
            yield
            var = head_sum(yc * yc) * (1.0 / HEAD_DIM)
            yn = yc * lax.rsqrt(var + GN_EPS) * gn_g[:, cs] + gn_b[:, cs]
            res = (yn + bonus_s[lo:hi, cs]) * gate_s[lo:hi, cs]
            for r0 in range(lo, hi, tt) if hi - lo > tt else [lo]:
                r1 = min(r0 + tt, hi)
                out_ref[r0 // tt, r0 % tt:r0 % tt + (r1 - r0), cs] = res[r0 - lo:r1 - lo]
            yield

    subs = [(lo, lo + sub) for lo in range(0, n_rows, sub)]
    _interleave([prologue(*subs[0])])
    for n, sb in enumerate(subs):
        phases = [chains(*sb)]
        if n + 1 < len(subs):
            phases.append(prologue(*subs[n + 1]))
        if n > 0:
            phases.append(epilogue(*subs[n - 1]))
        _interleave(phases)
    _interleave([epilogue(*subs[-1])])

    for bi in range(nb):
        carry_scr[bi] = pr_ref[bi, tt - 1:tt, :]
        for p in range(4):
            s_scr[bi, p] = s_state[bi][p]

    @pl.when(i == n_i - 1)
    def _():
        for bi in range(nb):
            for p in range(4):
                s_pair = s_state[bi][p]
                sfin_ref[bi, 2 * p] = s_pair[:HEAD_DIM, :HEAD_DIM].astype(sfin_ref.dtype)
                sfin_ref[bi, 2 * p + 1] = s_pair[HEAD_DIM:, HEAD_DIM:].astype(sfin_ref.dtype)


def _rwkv(pr3, l, ls, shift0, s0, mu_p, vecs, rk_flat, w2p, a2p, g2p, hsum, tt, c_len, nb):
    b, t, _ = pr3.shape
    n_rows = nb * tt
    sub = min(RWKV_SUB, n_rows)
    ri = jnp.arange(sub)[:, None]
    ci = jnp.arange(sub)[None, :]
    same = (ri // c_len) == (ci // c_len)
    ltri = (same & (ri >= ci)).astype(BF16)
    hw = hsum.shape[0]
    blk = lambda bi, i: (bi, i, 0)
    const = lambda bi, i: (0, 0)
    big = lambda: pltpu.VMEM((n_rows, R_WIDTH), F32)
    return pl.pallas_call(
        functools.partial(_rwkv_kernel, c_len=c_len, tt=tt, nb=nb, sub=sub),
        grid=(b // nb, t // tt),
        in_specs=[pl.BlockSpec((nb, tt, RWKV_PAD), blk),
                  pl.BlockSpec((None, nb, 1, RWKV_PAD), lambda bi, i: (ls, bi, 0, 0)),
                  pl.BlockSpec((None, nb, N_HEADS_R, HEAD_DIM, HEAD_DIM), lambda bi, i: (ls, bi, 0, 0, 0)),
                  _layer_block((1, RWKV_PAD), l), _layer_block((6, R_WIDTH), l), _layer_block((1, R_WIDTH), l),
                  _layer_block((LANES, R_WIDTH), l), _layer_block((LANES, R_WIDTH), l),
                  _layer_block((2 * LANES, R_WIDTH), l), pl.BlockSpec((hw, hw), const),
                  pl.BlockSpec((sub, sub), const)],
        out_specs=[pl.BlockSpec((nb, tt, R_WIDTH), blk),
                   pl.BlockSpec((nb, N_HEADS_R, HEAD_DIM, HEAD_DIM), lambda bi, i: (bi, 0, 0, 0))],
        out_shape=[jax.ShapeDtypeStruct((b, t, R_WIDTH), F32),
                   jax.ShapeDtypeStruct((b, N_HEADS_R, HEAD_DIM, HEAD_DIM), s0.dtype)],
        scratch_shapes=[pltpu.VMEM((nb, 4, LANES, LANES), F32), pltpu.VMEM((nb, 1, RWKV_PAD), F32),
                        big(), big(), big(), big(), big(), big(), big(), big(), big(), big(), big()],
        compiler_params=_params(("parallel", "arbitrary")),
        name="rwkv_mix",
    )(pr3, shift0, s0, mu_p, vecs, rk_flat, w2p, a2p, g2p, hsum, ltri)


def _post_kernel(attn_ref, rw_ref, x_ref, g_ref, wout_ref, wq_ref, wo_ref, mk_ref, mv_ref, o_ref, *, nb, t):
    m = _mm(attn_ref[...], wout_ref[0:A_WIDTH, :]) + _mm(rw_ref[...], wout_ref[A_WIDTH:, :])
    x1 = x_ref[...] + _rms(m, g_ref[NORM_MIX_POST:NORM_MIX_POST + 1, :])
    q = _mm(_rms(x1, g_ref[NORM_X_PRE:NORM_X_PRE + 1, :]), wq_ref[...])
    scale = MEM_HEAD_DIM ** -0.5
    jobs = [(bi, hd) for bi in range(nb) for hd in range(MEM_HEADS)]
    cols = lambda hd: slice(hd * MEM_HEAD_DIM, (hd + 1) * MEM_HEAD_DIM)
    head_rows = lambda hd: pl.ds(hd, MEM_LEN, stride=MEM_HEADS)
    scores = [_mm_nt(q[bi * t:(bi + 1) * t, cols(hd)], mk_ref[bi, head_rows(hd), :]) * scale for bi, hd in jobs]
    exps = [jnp.exp(s - jnp.max(s, axis=-1, keepdims=True)) for s in scores]
    outs = [_mm(e, mv_ref[bi, head_rows(hd), :]) / jnp.sum(e, axis=-1, keepdims=True)
            for (bi, hd), e in zip(jobs, exps)]
    row_blocks = [jnp.concatenate(outs[bi * MEM_HEADS:(bi + 1) * MEM_HEADS], axis=1) for bi in range(nb)]
    o = row_blocks[0] if nb == 1 else jnp.concatenate(row_blocks, axis=0)
    c = _mm(o, wo_ref[...])
    o_ref[...] = x1 + _rms(c, g_ref[NORM_X_POST:NORM_X_POST + 1, :])


def _post(attn, rw, x2d, l, gains, w_out, w_mq, w_mo, lm, mk, mv, nb, t, tiles_per_batch):
    n = x2d.shape[0]
    tm = nb * t
    row = lambda i: (i, 0)
    if nb == 1:
        mmap = lambda i: (lm, i // tiles_per_batch, 0, 0)
    else:
        mmap = lambda i: (lm, i, 0, 0)
    mem_rows = MEM_LEN * MEM_HEADS
    return pl.pallas_call(
        functools.partial(_post_kernel, nb=nb, t=t),
        grid=(n // tm,),
        in_specs=[pl.BlockSpec((tm, A_WIDTH), row), pl.BlockSpec((tm, R_WIDTH), row),
                  pl.BlockSpec((tm, D_MODEL), row), _layer_block((8, D_MODEL), l),
                  _layer_block((D_MODEL, D_MODEL), l), _layer_block((D_MODEL, MEM_WIDTH), l),
                  _layer_block((MEM_WIDTH, D_MODEL), l),
                  pl.BlockSpec((None, nb, mem_rows, MEM_HEAD_DIM), mmap),
                  pl.BlockSpec((None, nb, mem_rows, MEM_HEAD_DIM), mmap)],
        out_specs=pl.BlockSpec((tm, D_MODEL), row),
        out_shape=jax.ShapeDtypeStruct((n, D_MODEL), F32),
        compiler_params=_params(("parallel",)),
        name="post_mix",
    )(attn, rw, x2d, gains, w_out, w_mq, w_mo, mk, mv)


FFN_CHUNK = 256


def _ffn_kernel(x_ref, g_ref, wgu_ref, wd_ref, o_ref):
    x = x_ref[...]
    h = _rms(x, g_ref[NORM_FFN_PRE:NORM_FFN_PRE + 1, :]).astype(BF16)
    acc = None
    for j in range(D_FF // FFN_CHUNK):
        cols = slice(j * FFN_CHUNK, (j + 1) * FFN_CHUNK)
        gate = jnp.dot(h, wgu_ref[:, cols], preferred_element_type=F32)
        up = jnp.dot(h, wgu_ref[:, D_FF + j * FFN_CHUNK:D_FF + (j + 1) * FFN_CHUNK], preferred_element_type=F32)
        act = (gate * jax.nn.sigmoid(gate)) * up
        part = _mm(act, wd_ref[cols, :])
        acc = part if acc is None else acc + part
    o_ref[...] = x + _rms(acc, g_ref[NORM_FFN_POST:NORM_FFN_POST + 1, :])


def _ffn(x2d, l, gains, w_gu, w_dn, tm):
    n = x2d.shape[0]
    row = lambda i: (i, 0)
    resident = dict(pipeline_mode=pl.Buffered(1))
    return pl.pallas_call(
        _ffn_kernel,
        grid=(n // tm,),
        in_specs=[pl.BlockSpec((tm, D_MODEL), row), _layer_block((8, D_MODEL), l),
                  pl.BlockSpec((None, D_MODEL, 2 * D_FF), lambda i: (l, 0, 0), **resident),
                  pl.BlockSpec((None, D_FF, D_MODEL), lambda i: (l, 0, 0), **resident)],
        out_specs=pl.BlockSpec((tm, D_MODEL), row),
        out_shape=jax.ShapeDtypeStruct((n, D_MODEL), F32),
        compiler_params=_params(("parallel",)),
        name="ffn",
    )(x2d, gains, w_gu, w_dn)


def _memkv_kernel(x_ref, g_ref, w_ref, k_ref, v_ref):
    kv = _mm(_rms(x_ref[...], g_ref[NORM_MEM:NORM_MEM + 1, :]), w_ref[...])
    for hd in range(MEM_HEADS):
        rows = pl.ds(hd, MEM_LEN, stride=MEM_HEADS)
        k_ref[0, rows, :] = kv[:, hd * MEM_HEAD_DIM:(hd + 1) * MEM_HEAD_DIM]
        v_ref[0, rows, :] = kv[:, MEM_WIDTH + hd * MEM_HEAD_DIM:MEM_WIDTH + (hd + 1) * MEM_HEAD_DIM]


def _memkv(mem2d, l, gains, w_kv):
    n = mem2d.shape[0]
    nbatch = n // MEM_LEN
    mem_rows = MEM_LEN * MEM_HEADS
    out = jax.ShapeDtypeStruct((nbatch, mem_rows, MEM_HEAD_DIM), F32)
    ospec = pl.BlockSpec((1, mem_rows, MEM_HEAD_DIM), lambda i: (i, 0, 0))
    return pl.pallas_call(
        _memkv_kernel,
        grid=(nbatch,),
        in_specs=[pl.BlockSpec((MEM_LEN, D_MODEL), lambda i: (i, 0)), _layer_block((8, D_MODEL), l),
                  _layer_block((D_MODEL, 2 * MEM_WIDTH), l)],
        out_specs=[ospec, ospec],
        out_shape=[out, out],
        compiler_params=_params(("parallel",)),
        name="mem_kv",
    )(mem2d, gains, w_kv)


def _rope_tables(pos):
    half = HEAD_DIM // 2
    inv = ROPE_THETA ** (-jnp.arange(half, dtype=F32) / half)
    ang = pos.astype(F32)[:, None] * inv[None, :]
    cos = jnp.cos(ang)
    sin = jnp.sin(ang)
    cos_t = jnp.tile(cos, (1, LANES // half))
    sin_t = jnp.tile(jnp.concatenate([-sin, sin], axis=1), (1, LANES // HEAD_DIM))
    return cos_t, sin_t


def _stacked_params(norm_gains, w_in, attn_sink, shift_mu, rwkv_vecs, rwkv_rk, rwkv_w2, rwkv_a2, rwkv_g2,
                    w_out, w_mem_q, w_mem_kv, w_mem_o, w_gate_up, w_down):
    depth = w_in.shape[0]
    return dict(
        gains=jnp.pad(norm_gains, ((0, 0), (0, 8 - norm_gains.shape[1]), (0, 0))),
        w_in=jnp.pad(w_in, ((0, 0), (0, 0), (0, IN_COLS_PAD - w_in.shape[2]))).astype(BF16),
        sink=jnp.broadcast_to(attn_sink[:, :, None], attn_sink.shape + (LANES,)).astype(F32),
        mu=jnp.pad(shift_mu, ((0, 0), (0, RWKV_PAD - RWKV_PROJ)))[:, None, :],
        vecs=rwkv_vecs, rk=rwkv_rk.reshape(depth, 1, R_WIDTH),
        w2=jnp.pad(rwkv_w2, ((0, 0), (0, LANES - DECAY_LORA), (0, 0))).astype(BF16),
        a2=jnp.pad(rwkv_a2, ((0, 0), (DECAY_LORA, LANES - DECAY_LORA - AAA_LORA), (0, 0))).astype(BF16),
        g2=jnp.pad(rwkv_g2, ((0, 0), (0, 2 * LANES - GATE_LORA), (0, 0))).astype(BF16),
        w_out=w_out.astype(BF16), w_mq=w_mem_q.astype(BF16), w_mkv=w_mem_kv.astype(BF16),
        w_mo=w_mem_o.astype(BF16), w_gu=w_gate_up.astype(BF16), w_dn=w_down.astype(BF16))


def _head_sum_matrix():
    i = jnp.arange(2 * LANES)
    return ((i[:, None] // HEAD_DIM) == (i[None, :] // HEAD_DIM)).astype(BF16)


def _layer(x2d, l, b, t, tabs, lm, mk, mv, swa_cache, ls, s0, shift0, sp, hsum, in_tm, rw_tt, post_nb, post_t,
           ffn_tm):
    q, k, v, pr = _in_proj(x2d, l, sp["gains"], sp["w_in"], tabs[0], tabs[1], in_tm)
    if swa_cache is None:
        attn = _swa_prompt(q, k, v, l, sp["sink"], b, t)
    else:
        attn = _swa_sample(q, k, v, l, swa_cache[0], swa_cache[1], sp["sink"], b, t)
    pr3 = pr.reshape(b, t, RWKV_PAD)
    rw, s_fin = _rwkv(pr3, l, ls, shift0, s0, sp["mu"], sp["vecs"], sp["rk"], sp["w2"], sp["a2"], sp["g2"], hsum,
                      rw_tt[0], min(CHUNK, t), rw_tt[1])
    x2 = _post(attn, rw.reshape(b * t, R_WIDTH), x2d, l, sp["gains"], sp["w_out"], sp["w_mq"], sp["w_mo"],
               lm, mk, mv, post_nb, post_t, t // post_t)
    x3 = _ffn(x2, l, sp["gains"], sp["w_gu"], sp["w_dn"], ffn_tm)
    return x3, k, v, s_fin, pr3[:, t - 1:t, :RWKV_PROJ]


def kernel(x_prompt, mem_prompt, x_sample, cache_swa_k, cache_swa_v, cache_mem_k, cache_mem_v, state_rwkv,
           state_shift, norm_gains, w_in, attn_sink, shift_mu, rwkv_vecs, rwkv_rk, rwkv_w2, rwkv_a2, rwkv_g2,
           w_out, w_mem_q, w_mem_kv, w_mem_o, w_gate_up, w_down):
    b, t, _ = x_prompt.shape
    bd, tn, _ = x_sample.shape
    depth = w_in.shape[0]
    m_len = mem_prompt.shape[1]
    cache_len = cache_swa_k.shape[2]
    tabs_p = _rope_tables(jnp.arange(t, dtype=jnp.int32))
    cs, sn = _rope_tables(PAST_LEN + jnp.arange(tn, dtype=jnp.int32))
    tabs_s = (jnp.tile(cs, (bd, 1)), jnp.tile(sn, (bd, 1)))
    hsum = _head_sum_matrix()
    sp = _stacked_params(norm_gains, w_in, attn_sink, shift_mu, rwkv_vecs, rwkv_rk, rwkv_w2, rwkv_a2, rwkv_g2,
                         w_out, w_mem_q, w_mem_kv, w_mem_o, w_gate_up, w_down)
    s0_p = jnp.zeros((1, b, N_HEADS_R, HEAD_DIM, HEAD_DIM), F32)
    sh0_p = jnp.zeros((1, b, 1, RWKV_PAD), F32)
    sh0_s = jnp.pad(state_shift, ((0, 0), (0, 0), (0, 0), (0, RWKV_PAD - RWKV_PROJ)))
    swa_cache = (cache_swa_k.reshape(depth, bd, cache_len, KV_WIDTH_A),
                 cache_swa_v.reshape(depth, bd, cache_len, KV_WIDTH_A))
    mem_rows = m_len * MEM_HEADS
    cmk = cache_mem_k.reshape(depth, bd, mem_rows, MEM_HEAD_DIM)
    cmv = cache_mem_v.reshape(depth, bd, mem_rows, MEM_HEAD_DIM)
    xp = x_prompt.reshape(b * t, D_MODEL)
    xs = x_sample.reshape(bd * tn, D_MODEL)
    mem2d = mem_prompt.reshape(b * m_len, D_MODEL)
    in_tm_p = min(512, t)
    rw_tt_p = (min(512, t), 1)
    rw_tt_s = (tn, 4 if bd % 4 == 0 else 1)
    post_t_p = min(512, t)
    ffn_tm_p = min(512, b * t)
    keep = t - min(WINDOW, t)
    pk, pv, pmk, pmv, pS, psh = [], [], [], [], [], []
    sk, sv, sS, ssh = [], [], [], []
    for l in range(depth):
        mk_l, mv_l = _memkv(mem2d, l, sp["gains"], sp["w_mkv"])
        xp, k_l, v_l, s_l, sh_l = _layer(xp, l, b, t, tabs_p, 0, mk_l[None], mv_l[None], None, 0, s0_p, sh0_p, sp,
                                         hsum, in_tm_p, rw_tt_p, 1, post_t_p, ffn_tm_p)
        pk.append(k_l.reshape(b, t, KV_WIDTH_A)[:, keep:].reshape(b, t - keep, N_KV_A, HEAD_DIM))
        pv.append(v_l.reshape(b, t, KV_WIDTH_A)[:, keep:].reshape(b, t - keep, N_KV_A, HEAD_DIM))
        pmk.append(mk_l.reshape(b, m_len, MEM_HEADS, MEM_HEAD_DIM))
        pmv.append(mv_l.reshape(b, m_len, MEM_HEADS, MEM_HEAD_DIM))
        pS.append(s_l)
        psh.append(sh_l)

        xs, k2, v2, s2, sh2 = _layer(xs, l, bd, tn, tabs_s, l, cmk, cmv, swa_cache, l, state_rwkv, sh0_s, sp,
                                     hsum, bd * tn, rw_tt_s, bd, tn, bd * tn)
        sk.append(k2.reshape(bd, tn, N_KV_A, HEAD_DIM))
        sv.append(v2.reshape(bd, tn, N_KV_A, HEAD_DIM))
        sS.append(s2)
        ssh.append(sh2)
    return (xp.reshape(b, t, D_MODEL), xs.reshape(bd, tn, D_MODEL), jnp.stack(pk), jnp.stack(pv),
            jnp.stack(pmk), jnp.stack(pmv), jnp.stack(pS), jnp.stack(psh),
            jnp.stack(sk), jnp.stack(sv), jnp.stack(sS), jnp.stack(ssh))
```

```python
import functools
import math

import jax
import jax.numpy as jnp
from jax import lax
from jax.experimental import pallas as pl
from jax.experimental.pallas import tpu as pltpu

F32 = jnp.float32
BF16 = jnp.bfloat16

D_MODEL = 1024
HEAD_DIM = 64
CHUNK = 64
A_WIDTH = 512
KV_WIDTH_A = 128
N_KV_A = 2
WINDOW = 128
PAST_LEN = 4096
ROPE_THETA = 10000.0
R_WIDTH = 512
N_HEADS_R = 8
DECAY_LORA = 64
AAA_LORA = 64
GATE_LORA = 160
RWKV_PROJ = 3 * R_WIDTH + DECAY_LORA + AAA_LORA + GATE_LORA
RWKV_PAD = 1920
IN_COLS_PAD = A_WIDTH + 2 * KV_WIDTH_A + RWKV_PAD
GN_EPS = 6.4e-4
MEM_LEN = 256
MEM_HEADS = 4
MEM_HEAD_DIM = 128
MEM_WIDTH = 512
D_FF = 2816
RMS_EPS = 1e-6
NEG_INF = -1e30
NORM_MIX_PRE, NORM_MIX_POST, NORM_X_PRE, NORM_X_POST, NORM_MEM, NORM_FFN_PRE, NORM_FFN_POST = range(7)
EXP_M05 = math.exp(-0.5)

LANES = 128
VMEM_LIMIT = 56 * 1024 * 1024


def _params(sem):
    return pltpu.CompilerParams(dimension_semantics=sem, vmem_limit_bytes=VMEM_LIMIT)


def _rms(x, g):
    ms = jnp.mean(x * x, axis=-1, keepdims=True)
    return x * lax.rsqrt(ms + RMS_EPS) * g


def _mm(a, b):
    return jnp.dot(a.astype(BF16), b.astype(BF16), preferred_element_type=F32)


def _mm_nt(a, b):
    return lax.dot_general(a.astype(BF16), b.astype(BF16), (((1,), (1,)), ((), ())),
                           preferred_element_type=F32)


def _split(x):
    hi = x.astype(BF16)
    lo = (x - hi.astype(F32)).astype(BF16)
    return hi, lo


def _mm_lsplit(a_exact_bf16, x):
    hi, lo = _split(x)
    return (jnp.dot(a_exact_bf16, hi, preferred_element_type=F32)
            + jnp.dot(a_exact_bf16, lo, preferred_element_type=F32))


def _mm_rsplit(x, b_exact_bf16):
    hi, lo = _split(x)
    return (jnp.dot(hi, b_exact_bf16, preferred_element_type=F32)
            + jnp.dot(lo, b_exact_bf16, preferred_element_type=F32))


def _in_kernel(x_ref, g_ref, w_ref, cos_ref, sin_ref, q_ref, k_ref, v_ref, pr_ref):
    h = _rms(x_ref[...], g_ref[NORM_MIX_PRE:NORM_MIX_PRE + 1, :]).astype(BF16)
    p = jnp.dot(h, w_ref[...], preferred_element_type=F32)
    cos = cos_ref[...]
    sin = sin_ref[...]
    lane = lax.broadcasted_iota(jnp.int32, cos.shape, 1)
    first_half = (lane & (HEAD_DIM // 2)) == 0

    def rope(xc):
        sw = jnp.where(first_half, pltpu.roll(xc, LANES - HEAD_DIM // 2, 1), pltpu.roll(xc, HEAD_DIM // 2, 1))
        return xc * cos + sw * sin

    for j in range(A_WIDTH // LANES):
        q_ref[:, j * LANES:(j + 1) * LANES] = rope(p[:, j * LANES:(j + 1) * LANES])
    k_ref[...] = rope(p[:, A_WIDTH:A_WIDTH + KV_WIDTH_A])
    v_ref[...] = p[:, A_WIDTH + KV_WIDTH_A:A_WIDTH + 2 * KV_WIDTH_A]
    pr_ref[...] = p[:, A_WIDTH + 2 * KV_WIDTH_A:]


def _layer_block(shape, l):
    nd = len(shape)
    return pl.BlockSpec((None,) + tuple(shape), lambda *_: (l,) + (0,) * nd)


def _in_proj(x2d, l, gains, w_in_p, cos_t, sin_t, tm):
    n = x2d.shape[0]
    tab_blocks = cos_t.shape[0] // tm
    row = lambda i: (i, 0)
    tab = lambda i: (i % tab_blocks, 0)
    return pl.pallas_call(
        _in_kernel,
        grid=(n // tm,),
        in_specs=[pl.BlockSpec((tm, D_MODEL), row), _layer_block((8, D_MODEL), l),
                  _layer_block((D_MODEL, IN_COLS_PAD), l),
                  pl.BlockSpec((tm, LANES), tab), pl.BlockSpec((tm, LANES), tab)],
        out_specs=[pl.BlockSpec((tm, A_WIDTH), row), pl.BlockSpec((tm, KV_WIDTH_A), row),
                   pl.BlockSpec((tm, KV_WIDTH_A), row), pl.BlockSpec((tm, RWKV_PAD), row)],
        out_shape=[jax.ShapeDtypeStruct((n, A_WIDTH), F32), jax.ShapeDtypeStruct((n, KV_WIDTH_A), F32),
                   jax.ShapeDtypeStruct((n, KV_WIDTH_A), F32), jax.ShapeDtypeStruct((n, RWKV_PAD), F32)],
        compiler_params=_params(("parallel",)),
        name="in_proj",
    )(x2d, gains, w_in_p, cos_t, sin_t)


def _sink_attend(jobs, sink_ref, t):
    nk = jobs[0][1].shape[0]
    nkp = 2 * LANES
    lane = lax.broadcasted_iota(jnp.int32, (t, LANES), 1)
    m0 = lane < HEAD_DIM
    coli = lax.broadcasted_iota(jnp.int32, (1, nkp), 1)
    fills = []
    for kv in range(N_KV_A):
        blocks = []
        for g in range(4):
            sg = sink_ref[4 * kv + g:4 * kv + g + 1, :]
            blocks.append(jnp.broadcast_to(jnp.where(coli == nk, jnp.concatenate([sg, sg], axis=1), NEG_INF),
                                           (t, nkp)))
        fills.append(jnp.concatenate(blocks, axis=0))
    k_tail = jnp.zeros((nkp - nk, LANES), F32)
    tail_row = lax.broadcasted_iota(jnp.int32, (nkp - nk, nkp), 0)
    tail_col = lax.broadcasted_iota(jnp.int32, (nkp - nk, nkp), 1)
    v_tail = jnp.where((tail_row == 0) & (tail_col >= LANES), 1.0, 0.0).astype(F32)
    ones = jnp.ones((nk, LANES), F32)
    scores = []
    for q_rows, kdup, _, kv, valid in jobs:
        parts = []
        for p in (2 * kv, 2 * kv + 1):
            qp = q_rows[:, p * LANES:(p + 1) * LANES]
            parts.append(jnp.where(m0, qp, 0.0))
            parts.append(jnp.where(m0, 0.0, qp))
        lhs = jnp.concatenate(parts, axis=0)
        s = _mm_nt(lhs, jnp.concatenate([kdup, k_tail], axis=0))
        keep = coli < nk
        if valid is not None:
            keep = keep & valid
        scores.append(jnp.where(keep, s, fills[kv]))
    exps = [jnp.exp(s - jnp.max(s, axis=-1, keepdims=True)) for s in scores]
    outs = []
    for (_, _, vdup, _, _), e in zip(jobs, exps):
        v2 = jnp.concatenate([jnp.concatenate([vdup, ones], axis=1), v_tail], axis=0)
        o2 = _mm(e, v2)
        o = o2[:, :LANES] / o2[:, LANES:]
        outs.append([jnp.where(m0, o[(2 * pi) * t:(2 * pi + 1) * t], o[(2 * pi + 1) * t:(2 * pi + 2) * t])
                     for pi in range(2)])
    return outs


SWA_SCALE = HEAD_DIM ** -0.5


def _dup_heads(x):
    lane = lax.broadcasted_iota(jnp.int32, x.shape, 1)
    m0 = lane < HEAD_DIM
    xs = pltpu.roll(x, HEAD_DIM, 1)
    return [jnp.where(m0, x, xs), jnp.where(m0, xs, x)]


SWA_TQ = 512


def _swa_prompt_kernel(q_ref, kp_ref, kc_ref, vp_ref, vc_ref, sink_ref, o_ref):
    i = pl.program_id(1)
    k = jnp.concatenate([kp_ref[...], kc_ref[...]], axis=0)
    v = jnp.concatenate([vp_ref[...], vc_ref[...]], axis=0)
    kd = _dup_heads(k * SWA_SCALE)
    vd = _dup_heads(v)
    nk = 3 * CHUNK
    slot = lax.broadcasted_iota(jnp.int32, (1, 2 * LANES), 1) // CHUNK
    jobs = []
    for j in range(SWA_TQ // CHUNK):
        qj = q_ref[j * CHUNK:(j + 1) * CHUNK, :]
        valid = (slot + (i * (SWA_TQ // CHUNK) + j - 2)) >= 0
        for kv in range(N_KV_A):
            jobs.append((qj, kd[kv][j * CHUNK:j * CHUNK + nk], vd[kv][j * CHUNK:j * CHUNK + nk], kv, valid))
    outs = _sink_attend(jobs, sink_ref, CHUNK)
    for n, out in enumerate(outs):
        j, kv = divmod(n, N_KV_A)
        for pi in range(2):
            p = 2 * kv + pi
            o_ref[j * CHUNK:(j + 1) * CHUNK, p * LANES:(p + 1) * LANES] = out[pi]


def _swa_prompt(q, k, v, l, sink_b, b, t):
    n = b * t
    nq = t // SWA_TQ
    qmap = lambda bi, i: (bi * nq + i, 0)
    pmap = lambda bi, i: (jnp.maximum(bi * (t // WINDOW) + (SWA_TQ // WINDOW) * i - 1, 0), 0)
    const = lambda bi, i: (0, 0)
    return pl.pallas_call(
        _swa_prompt_kernel,
        grid=(b, nq),
        in_specs=[pl.BlockSpec((SWA_TQ, A_WIDTH), qmap),
                  pl.BlockSpec((WINDOW, KV_WIDTH_A), pmap), pl.BlockSpec((SWA_TQ, KV_WIDTH_A), qmap),
                  pl.BlockSpec((WINDOW, KV_WIDTH_A), pmap), pl.BlockSpec((SWA_TQ, KV_WIDTH_A), qmap),
                  _layer_block((8, LANES), l)],
        out_specs=pl.BlockSpec((SWA_TQ, A_WIDTH), qmap),
        out_shape=jax.ShapeDtypeStruct((n, A_WIDTH), F32),
        compiler_params=_params(("parallel", "parallel")),
        name="swa_prompt",
    )(q, k, k, v, v, sink_b)


def _swa_sample_kernel(q_ref, kc_ref, kn_ref, vc_ref, vn_ref, sink_ref, o_ref, *, t, nb):
    jobs = []
    for bi in range(nb):
        rows = slice(bi * t, (bi + 1) * t)
        kd = _dup_heads(jnp.concatenate([kc_ref[bi], kn_ref[rows, :]], axis=0) * SWA_SCALE)
        vd = _dup_heads(jnp.concatenate([vc_ref[bi], vn_ref[rows, :]], axis=0))
        for kv in range(N_KV_A):
            jobs.append((q_ref[rows, :], kd[kv], vd[kv], kv, None))
    outs = _sink_attend(jobs, sink_ref, t)
    for n, out in enumerate(outs):
        bi, kv = divmod(n, N_KV_A)
        for pi in range(2):
            p = 2 * kv + pi
            o_ref[bi * t:(bi + 1) * t, p * LANES:(p + 1) * LANES] = out[pi]


def _swa_sample(q, k, v, l, kc, vc, sink_b, b, t):
    n = b * t
    cache = kc.shape[2]
    nb = 4 if b % 4 == 0 else 1
    row = lambda bi: (bi, 0)
    cmap = lambda bi: (l, bi, 0, 0)
    return pl.pallas_call(
        functools.partial(_swa_sample_kernel, t=t, nb=nb),
        grid=(b // nb,),
        in_specs=[pl.BlockSpec((nb * t, A_WIDTH), row),
                  pl.BlockSpec((None, nb, cache, KV_WIDTH_A), cmap), pl.BlockSpec((nb * t, KV_WIDTH_A), row),
                  pl.BlockSpec((None, nb, cache, KV_WIDTH_A), cmap), pl.BlockSpec((nb * t, KV_WIDTH_A), row),
                  _layer_block((8, LANES), l)],
        out_specs=pl.BlockSpec((nb * t, A_WIDTH), row),
        out_shape=jax.ShapeDtypeStruct((n, A_WIDTH), F32),
        compiler_params=_params(("parallel",)),
        name="swa_sample",
    )(q, kc, k, vc, v, sink_b)


def _tri_inverse(nmat, masks, eye):
    p = eye + jnp.where(masks[0], nmat, 0.0)
    for m in masks[1:]:
        e = jnp.where(m, nmat, 0.0)
        p = p + _mm(_mm(p, e), p)
    return p


def _rwkv_kernel_old(pr_ref, sh0_ref, s0_ref, mu_ref, vec_ref, rk_ref, w2_ref, a2_ref, g2_ref, hsum_ref,
                     ltri_ref, out_ref, sfin_ref,
                     s_scr, carry_scr, rt_s, at_s, bt_s, kt_s, bb_s, kb_s, v_s, et_s, y_s, *, c_len, tt, nb):
    i = pl.program_id(1)
    n_i = pl.num_programs(1)
    n_rows = nb * tt

    @pl.when(i == 0)
    def _():
        zero = jnp.zeros((HEAD_DIM, HEAD_DIM), F32)
        for bi in range(nb):
            for p in range(4):
                s_scr[bi, p] = jnp.concatenate(
                    [jnp.concatenate([s0_ref[bi, 2 * p].astype(F32), zero], axis=1),
                     jnp.concatenate([zero, s0_ref[bi, 2 * p + 1].astype(F32)], axis=1)], axis=0)
        carry_scr[...] = sh0_ref[...]

    rowi = lax.broadcasted_iota(jnp.int32, (n_rows, 1), 0)

    def shifted(a, b):
        cur = pr_ref[:, :, a:b].reshape(n_rows, b - a)
        prev = pltpu.roll(cur, 1, 0)
        for bi in range(nb):
            prev = jnp.where(rowi == bi * tt, carry_scr[bi, :, a:b], prev)
        return cur + (prev - cur) * mu_ref[:, a:b]

    r = shifted(0, 512)
    k = shifted(512, 1024)
    v = shifted(1024, 1536)
    xwa = shifted(1536, 1664)
    xg = shifted(1664, RWKV_PAD)
    for bi in range(nb):
        carry_scr[bi] = pr_ref[bi, tt - 1:tt, :]

    w0 = vec_ref[0:1, :]
    a0 = vec_ref[1:2, :]
    k_k = vec_ref[2:3, :]
    k_a = vec_ref[3:4, :]
    gn_g = vec_ref[4:5, :]
    gn_b = vec_ref[5:6, :]
    hsum = hsum_ref[...]
    hw = hsum.shape[0]

    def head_sum(x):
        return jnp.concatenate([_mm_rsplit(x[:, j * hw:(j + 1) * hw], hsum) for j in range(R_WIDTH // hw)], axis=1)

    z = w0 + _mm(jnp.tanh(xwa), w2_ref[...])
    wlog = -EXP_M05 * jax.nn.sigmoid(z)
    a = jax.nn.sigmoid(a0 + _mm(xwa, a2_ref[...]))
    g = _mm(jax.nn.sigmoid(xg), g2_ref[...])
    kk = k * k_k
    kk = kk * lax.rsqrt(jnp.maximum(head_sum(kk * kk), 1e-24))
    k_f = k * (1.0 + (a - 1.0) * k_a)
    bvec = kk * a
    bonus = head_sum(r * k_f * rk_ref[...]) * v

    cum = _mm_lsplit(ltri_ref[...], wlog)
    tot = jnp.concatenate([jnp.broadcast_to(cum[r1 - 1:r1, :], (c_len, R_WIDTH))
                           for r1 in range(c_len, n_rows + 1, c_len)], axis=0)
    e_in = jnp.exp(cum)
    e_inv = jnp.exp(-cum)
    e_end = jnp.exp(tot - cum)
    rt_s[...] = r * e_in
    at_s[...] = -kk * jnp.exp(cum - wlog)
    bt_s[...] = bvec * e_inv
    kt_s[...] = k_f * e_inv
    bb_s[...] = bvec * e_end
    kb_s[...] = k_f * e_end
    v_s[...] = v
    et_s[...] = jnp.exp(tot)

    n2 = 2 * c_len
    ti = lax.broadcasted_iota(jnp.int32, (c_len, n2), 0)
    lane2 = lax.broadcasted_iota(jnp.int32, (c_len, n2), 1)
    si = lane2 & (c_len - 1)
    first_c = lane2 < c_len
    strict = ti > si
    incl = ti >= si
    eye = jnp.where(ti == si, 1.0, 0.0).astype(F32)
    masks = []
    half = 1
    while half < c_len:
        blk = 2 * half
        masks.append(((ti & ~(blk - 1)) == (si & ~(blk - 1))) & ((ti & half) != 0) & ((si & half) == 0))
        half = blk
    m0 = lax.broadcasted_iota(jnp.int32, (c_len, LANES), 1) < HEAD_DIM
    rl = lax.broadcasted_iota(jnp.int32, (LANES, LANES), 0)
    cl = lax.broadcasted_iota(jnp.int32, (LANES, LANES), 1)
    same_head = (rl >= HEAD_DIM) == (cl >= HEAD_DIM)
    fused = n2 == LANES
    zeros_c = jnp.zeros((c_len, LANES), F32)

    def bd(x):
        return jnp.concatenate([jnp.where(m0, x, 0.0), jnp.where(m0, 0.0, x)], axis=0)

    def bd_t(x):
        return jnp.concatenate([jnp.where(first_c, x, 0.0), jnp.where(first_c, 0.0, x)], axis=0)

    chains = [(bi, c, p) for bi in range(nb) for c in range(tt // c_len) for p in range(4)]

    def rows_lanes(ch):
        bi, c, p = ch
        r0 = bi * tt + c * c_len
        return slice(r0, r0 + c_len), slice(p * LANES, (p + 1) * LANES)

    def cat(ref, ch):
        rows, ls = rows_lanes(ch)
        return ref[rows, ls]

    a_ab, a_ak, a_rb, a_rk = {}, {}, {}, {}
    for ch in chains:
        at, rt, rb, rkt = cat(at_s, ch), cat(rt_s, ch), bd(cat(bt_s, ch)), bd(cat(kt_s, ch))
        if fused:
            amat = _mm_nt(jnp.concatenate([at, rt], axis=0), jnp.concatenate([rb, rkt], axis=0))
            q_ab, q_ak = amat[:c_len, :n2], amat[:c_len, n2:]
            q_rb, q_rk = amat[c_len:, :n2], amat[c_len:, n2:]
        else:
            q_ab, q_ak, q_rb, q_rk = _mm_nt(at, rb), _mm_nt(at, rkt), _mm_nt(rt, rb), _mm_nt(rt, rkt)
        a_ab[ch] = jnp.where(strict, q_ab, 0.0)
        a_ak[ch] = jnp.where(strict, q_ak, 0.0)
        a_rb[ch] = jnp.where(incl, q_rb, 0.0)
        a_rk[ch] = jnp.where(incl, q_rk, 0.0)

    tinv = {ch: eye + jnp.where(masks[0], a_ab[ch], 0.0) for ch in chains}
    for m in masks[1:]:
        pe = {ch: _mm(tinv[ch], bd_t(jnp.where(m, a_ab[ch], 0.0))) for ch in chains}
        tinv = {ch: tinv[ch] + _mm(pe[ch], bd_t(tinv[ch])) for ch in chains}

    zv = {ch: _mm(a_ak[ch], bd(cat(v_s, ch))) for ch in chains}
    r_hat, y_hat, m_mat, g_mat = {}, {}, {}, {}
    if fused:
        w = {ch: _mm(tinv[ch], jnp.concatenate([bd(cat(at_s, ch)), bd(zv[ch])], axis=1)) for ch in chains}
        for ch in chains:
            rhs2 = jnp.concatenate([jnp.concatenate([bd(w[ch][:, :LANES]), bd(w[ch][:, LANES:])], axis=1),
                                    jnp.concatenate([jnp.zeros((n2, LANES), F32), bd(cat(v_s, ch))], axis=1)], axis=0)
            ry = _mm(jnp.concatenate([a_rb[ch], a_rk[ch]], axis=1), rhs2)
            r_hat[ch] = cat(rt_s, ch) + ry[:, :LANES]
            y_hat[ch] = ry[:, LANES:]
        for ch in chains:
            lhs_t = jnp.concatenate([w[ch], jnp.concatenate([zeros_c, cat(v_s, ch)], axis=1)], axis=0)
            mg = _mm(lhs_t.T, jnp.concatenate([cat(bb_s, ch), cat(kb_s, ch)], axis=0))
            m_mat[ch] = jnp.where(same_head, mg[:LANES], 0.0)
            g_mat[ch] = jnp.where(same_head, mg[LANES:], 0.0)
    else:
        a_hat = {ch: _mm(tinv[ch], bd(cat(at_s, ch))) for ch in chains}
        u_hat = {ch: _mm(tinv[ch], bd(zv[ch])) for ch in chains}
        for ch in chains:
            r_hat[ch] = cat(rt_s, ch) + _mm(a_rb[ch], bd(a_hat[ch]))
            y_hat[ch] = _mm(a_rb[ch], bd(u_hat[ch])) + _mm(a_rk[ch], bd(cat(v_s, ch)))
        for ch in chains:
            m_mat[ch] = jnp.where(same_head, _mm(a_hat[ch].T, cat(bb_s, ch)), 0.0)
            g_mat[ch] = jnp.where(same_head, _mm(u_hat[ch].T, cat(bb_s, ch)) + _mm(cat(v_s, ch).T, cat(kb_s, ch)), 0.0)

    for bi in range(nb):
        s_cur = [s_scr[bi, p] for p in range(4)]
        for c in range(tt // c_len):
            for p in range(4):
                ch = (bi, c, p)
                rows, ls = rows_lanes(ch)
                e_last = et_s[rows.start:rows.start + 1, ls]
                y_s[rows, ls] = y_hat[ch] + _mm_nt(r_hat[ch], s_cur[p])
                s_cur[p] = s_cur[p] * e_last + _mm(s_cur[p], m_mat[ch]) + g_mat[ch]
        for p in range(4):
            s_scr[bi, p] = s_cur[p]

    y = y_s[...]
    mean = head_sum(y) * (1.0 / HEAD_DIM)
    yc = y - mean
    var = head_sum(yc * yc) * (1.0 / HEAD_DIM)
    yn = yc * lax.rsqrt(var + GN_EPS) * gn_g + gn_b
    res = (yn + bonus) * g
    for bi in range(nb):
        out_ref[bi] = res[bi * tt:(bi + 1) * tt]

    @pl.when(i == n_i - 1)
    def _():
        for bi in range(nb):
            for p in range(4):
                s_pair = s_scr[bi, p]
                sfin_ref[bi, 2 * p] = s_pair[:HEAD_DIM, :HEAD_DIM].astype(sfin_ref.dtype)
                sfin_ref[bi, 2 * p + 1] = s_pair[HEAD_DIM:, HEAD_DIM:].astype(sfin_ref.dtype)


RWKV_SUB = 256


def _interleave(gens):
    gens = list(gens)
    while gens:
        for g in list(gens):
            try:
                next(g)
            except StopIteration:
                gens.remove(g)


def _rwkv_kernel(pr_ref, sh0_ref, s0_ref, mu_ref, vec_ref, rk_ref, w2_ref, a2_ref, g2_ref, hsum_ref,
                 ltri_ref, out_ref, sfin_ref,
                 s_scr, carry_scr, rt_s, at_s, bt_s, kt_s, bb_s, kb_s, v_s, et_s, y_s, bonus_s, gate_s,
                 *, c_len, tt, nb, sub):
    i = pl.program_id(1)
    n_i = pl.num_programs(1)
    n_rows = nb * tt

    @pl.when(i == 0)
    def _():
        zero = jnp.zeros((HEAD_DIM, HEAD_DIM), F32)
        for bi in range(nb):
            for p in range(4):
                s_scr[bi, p] = jnp.concatenate(
                    [jnp.concatenate([s0_ref[bi, 2 * p].astype(F32), zero], axis=1),
                     jnp.concatenate([zero, s0_ref[bi, 2 * p + 1].astype(F32)], axis=1)], axis=0)
        carry_scr[...] = sh0_ref[...]

    w0 = vec_ref[0:1, :]
    a0 = vec_ref[1:2, :]
    k_k = vec_ref[2:3, :]
    k_a = vec_ref[3:4, :]
    gn_g = vec_ref[4:5, :]
    gn_b = vec_ref[5:6, :]
    hsum = hsum_ref[...]
    hw = hsum.shape[0]
    halves = [slice(j * hw, (j + 1) * hw) for j in range(R_WIDTH // hw)]

    def head_sum(x):
        return _mm(x, hsum)

    n2 = 2 * c_len
    ti = lax.broadcasted_iota(jnp.int32, (c_len, n2), 0)
    lane2 = lax.broadcasted_iota(jnp.int32, (c_len, n2), 1)
    si = lane2 & (c_len - 1)
    first_c = lane2 < c_len
    strict = ti > si
    incl = ti >= si
    eye = jnp.where(ti == si, 1.0, 0.0).astype(F32)
    masks = []
    half = 1
    while half < c_len:
        blk = 2 * half
        masks.append(((ti & ~(blk - 1)) == (si & ~(blk - 1))) & ((ti & half) != 0) & ((si & half) == 0))
        half = blk
    m0 = lax.broadcasted_iota(jnp.int32, (c_len, LANES), 1) < HEAD_DIM
    rl = lax.broadcasted_iota(jnp.int32, (LANES, LANES), 0)
    cl = lax.broadcasted_iota(jnp.int32, (LANES, LANES), 1)
    same_head = (rl >= HEAD_DIM) == (cl >= HEAD_DIM)
    fused = n2 == LANES
    zeros_c = jnp.zeros((c_len, LANES), F32)

    def bd(x):
        return jnp.concatenate([jnp.where(m0, x, 0.0), jnp.where(m0, 0.0, x)], axis=0)

    def bd_t(x):
        return jnp.concatenate([jnp.where(first_c, x, 0.0), jnp.where(first_c, 0.0, x)], axis=0)

    def prologue(lo, hi):
        n = hi - lo
        bi0 = lo // tt
        rowi = lax.broadcasted_iota(jnp.int32, (n, 1), 0)

        def shifted(a, b):
            if hi - lo <= tt:
                cur = pr_ref[bi0, lo - bi0 * tt:hi - bi0 * tt, a:b]
            else:
                cur = pr_ref[bi0:hi // tt, :, a:b].reshape(n, b - a)
            prev = pltpu.roll(cur, 1, 0)
            for r in range(lo, hi):
                if r % tt == 0:
                    prev = jnp.where(rowi == r - lo, carry_scr[r // tt, :, a:b], prev)
            if lo % tt != 0:
                prev = jnp.where(rowi == 0, pr_ref[bi0, lo - bi0 * tt - 1:lo - bi0 * tt, a:b], prev)
            return cur + (prev - cur) * mu_ref[:, a:b]

        xwa = shifted(3 * R_WIDTH, 3 * R_WIDTH + LANES)
        xg = shifted(3 * R_WIDTH + LANES, RWKV_PAD)
        th = jnp.tanh(xwa)
        sg = jax.nn.sigmoid(xg)
        ltri = ltri_ref[...]
        yield
        for cs in halves:
            r = shifted(cs.start, cs.stop)
            k = shifted(R_WIDTH + cs.start, R_WIDTH + cs.stop)
            v = shifted(2 * R_WIDTH + cs.start, 2 * R_WIDTH + cs.stop)
            yield
            z = w0[:, cs] + _mm(th, w2_ref[:, cs])
            wlog = -EXP_M05 * jax.nn.sigmoid(z)
            a = jax.nn.sigmoid(a0[:, cs] + _mm(xwa, a2_ref[:, cs]))
            gate_s[lo:hi, cs] = _mm(sg, g2_ref[:, cs])
            yield
            kk = k * k_k[:, cs]
            kk = kk * lax.rsqrt(jnp.maximum(head_sum(kk * kk), 1e-24))
            k_f = k * (1.0 + (a - 1.0) * k_a[:, cs])
            bvec = kk * a
            bonus_s[lo:hi, cs] = head_sum(r * k_f * rk_ref[:, cs]) * v
            yield
            cum = _mm_lsplit(ltri, wlog)
            tot = jnp.concatenate([jnp.broadcast_to(cum[r1 - 1:r1, :], (c_len, hw))
                                   for r1 in range(c_len, n + 1, c_len)], axis=0)
            e_in = jnp.exp(cum)
            e_inv = jnp.exp(-cum)
            e_end = jnp.exp(tot - cum)
            rt_s[lo:hi, cs] = r * e_in
            at_s[lo:hi, cs] = -kk * jnp.exp(cum - wlog)
            yield
            bt_s[lo:hi, cs] = bvec * e_inv
            kt_s[lo:hi, cs] = k_f * e_inv
            bb_s[lo:hi, cs] = bvec * e_end
            kb_s[lo:hi, cs] = k_f * e_end
            v_s[lo:hi, cs] = v
            et_s[lo:hi, cs] = jnp.exp(tot)
            yield

    s_state = [[s_scr[bi, p] for p in range(4)] for bi in range(nb)]

    def chains(lo, hi):
        chs = [(r0, p) for r0 in range(lo, hi, c_len) for p in range(4)]

        def cat(ref, ch):
            return ref[ch[0]:ch[0] + c_len, ch[1] * LANES:(ch[1] + 1) * LANES]

        a_ab, a_ak, a_rb, a_rk = {}, {}, {}, {}
        for ch in chs:
            at, rt, rb, rkt = cat(at_s, ch), cat(rt_s, ch), bd(cat(bt_s, ch)), bd(cat(kt_s, ch))
            if fused:
                amat = _mm_nt(jnp.concatenate([at, rt], axis=0), jnp.concatenate([rb, rkt], axis=0))
                q_ab, q_ak = amat[:c_len, :n2], amat[:c_len, n2:]
                q_rb, q_rk = amat[c_len:, :n2], amat[c_len:, n2:]
            else:
                q_ab, q_ak, q_rb, q_rk = _mm_nt(at, rb), _mm_nt(at, rkt), _mm_nt(rt, rb), _mm_nt(rt, rkt)
            a_ab[ch] = jnp.where(strict, q_ab, 0.0)
            a_ak[ch] = jnp.where(strict, q_ak, 0.0)
            a_rb[ch] = jnp.where(incl, q_rb, 0.0)
            a_rk[ch] = jnp.where(incl, q_rk, 0.0)
        yield
        tinv = {ch: eye + jnp.where(masks[0], a_ab[ch], 0.0) for ch in chs}
        for m in masks[1:]:
            pe = {ch: _mm(tinv[ch], bd_t(jnp.where(m, a_ab[ch], 0.0))) for ch in chs}
            yield
            tinv = {ch: tinv[ch] + _mm(pe[ch], bd_t(tinv[ch])) for ch in chs}
            yield
        zv = {ch: _mm(a_ak[ch], bd(cat(v_s, ch))) for ch in chs}
        yield
        r_hat, y_hat, m_mat, g_mat = {}, {}, {}, {}
        if fused:
            w = {ch: _mm(tinv[ch], jnp.concatenate([bd(cat(at_s, ch)), bd(zv[ch])], axis=1)) for ch in chs}
            yield
            for ch in chs:
                rhs2 = jnp.concatenate(
                    [jnp.concatenate([bd(w[ch][:, :LANES]), bd(w[ch][:, LANES:])], axis=1),
                     jnp.concatenate([jnp.zeros((n2, LANES), F32), bd(cat(v_s, ch))], axis=1)], axis=0)
                ry = _mm(jnp.concatenate([a_rb[ch], a_rk[ch]], axis=1), rhs2)
                r_hat[ch] = cat(rt_s, ch) + ry[:, :LANES]
                y_hat[ch] = ry[:, LANES:]
            yield
            for ch in chs:
                lhs_t = jnp.concatenate([w[ch], jnp.concatenate([zeros_c, cat(v_s, ch)], axis=1)], axis=0)
                mg = _mm(lhs_t.T, jnp.concatenate([cat(bb_s, ch), cat(kb_s, ch)], axis=0))
                m_mat[ch] = jnp.where(same_head, mg[:LANES], 0.0)
                g_mat[ch] = jnp.where(same_head, mg[LANES:], 0.0)
            yield
        else:
            a_hat = {ch: _mm(tinv[ch], bd(cat(at_s, ch))) for ch in chs}
            u_hat = {ch: _mm(tinv[ch], bd(zv[ch])) for ch in chs}
            yield
            for ch in chs:
                r_hat[ch] = cat(rt_s, ch) + _mm(a_rb[ch], bd(a_hat[ch]))
                y_hat[ch] = _mm(a_rb[ch], bd(u_hat[ch])) + _mm(a_rk[ch], bd(cat(v_s, ch)))
            yield
            for ch in chs:
                m_mat[ch] = jnp.where(same_head, _mm(a_hat[ch].T, cat(bb_s, ch)), 0.0)
                g_mat[ch] = jnp.where(same_head,
                                      _mm(u_hat[ch].T, cat(bb_s, ch)) + _mm(cat(v_s, ch).T, cat(kb_s, ch)), 0.0)
            yield
        for r0 in range(lo, hi, c_len):
            s_cur = s_state[r0 // tt]
            for p in range(4):
                ch = (r0, p)
                ls = slice(p * LANES, (p + 1) * LANES)
                e_last = et_s[r0:r0 + 1, ls]
                y_s[r0:r0 + c_len, ls] = y_hat[ch] + _mm_nt(r_hat[ch], s_cur[p])
                s_cur[p] = s_cur[p] * e_last + _mm(s_cur[p], m_mat[ch]) + g_mat[ch]
            yield

    def epilogue(lo, hi):
        for cs in halves:
            y = y_s[lo:hi, cs]
            mean = head_sum(y) * (1.0 / HEAD_DIM)
            yc = y - mean
            yield
            var = head_sum(yc * yc) * (1.0 / HEAD_DIM)
            yn = yc * lax.rsqrt(var + GN_EPS) * gn_g[:, cs] + gn_b[:, cs]
            res = (yn + bonus_s[lo:hi, cs]) * gate_s[lo:hi, cs]
            for r0 in range(lo, hi, tt) if hi - lo > tt else [lo]:
                r1 = min(r0 + tt, hi)
                out_ref[r0 // tt, r0 % tt:r0 % tt + (r1 - r0), cs] = res[r0 - lo:r1 - lo]
            yield

    subs = [(lo, lo + sub) for lo in range(0, n_rows, sub)]
    _interleave([prologue(*subs[0])])
    for n, sb in enumerate(subs):
        phases = [chains(*sb)]
        if n + 1 < len(subs):
            phases.append(prologue(*subs[n + 1]))
        if n > 0:
            phases.append(epilogue(*subs[n - 1]))
        _interleave(phases)
    _interleave([epilogue(*subs[-1])])

    for bi in range(nb):
        carry_scr[bi] = pr_ref[bi, tt - 1:tt, :]
        for p in range(4):
            s_scr[bi, p] = s_state[bi][p]

    @pl.when(i == n_i - 1)
    def _():
        for bi in range(nb):
            for p in range(4):
                s_pair = s_state[bi][p]
                sfin_ref[bi, 2 * p] = s_pair[:HEAD_DIM, :HEAD_DIM].astype(sfin_ref.dtype)
                sfin_ref[bi, 2 * p + 1] = s_pair[HEAD_DIM:, HEAD_DIM:].astype(sfin_ref.dtype)


def _rwkv(pr3, l, ls, shift0, s0, mu_p, vecs, rk_flat, w2p, a2p, g2p, hsum, tt, c_len, nb):
    b, t, _ = pr3.shape
    n_rows = nb * tt
    sub = min(RWKV_SUB, n_rows)
    ri = jnp.arange(sub)[:, None]
    ci = jnp.arange(sub)[None, :]
    same = (ri // c_len) == (ci // c_len)
    ltri = (same & (ri >= ci)).astype(BF16)
    hw = hsum.shape[0]
    blk = lambda bi, i: (bi, i, 0)
    const = lambda bi, i: (0, 0)
    big = lambda: pltpu.VMEM((n_rows, R_WIDTH), F32)
    return pl.pallas_call(
        functools.partial(_rwkv_kernel, c_len=c_len, tt=tt, nb=nb, sub=sub),
        grid=(b // nb, t // tt),
        in_specs=[pl.BlockSpec((nb, tt, RWKV_PAD), blk),
                  pl.BlockSpec((None, nb, 1, RWKV_PAD), lambda bi, i: (ls, bi, 0, 0)),
                  pl.BlockSpec((None, nb, N_HEADS_R, HEAD_DIM, HEAD_DIM), lambda bi, i: (ls, bi, 0, 0, 0)),
                  _layer_block((1, RWKV_PAD), l), _layer_block((6, R_WIDTH), l), _layer_block((1, R_WIDTH), l),
                  _layer_block((LANES, R_WIDTH), l), _layer_block((LANES, R_WIDTH), l),
                  _layer_block((2 * LANES, R_WIDTH), l), pl.BlockSpec((hw, hw), const),
                  pl.BlockSpec((sub, sub), const)],
        out_specs=[pl.BlockSpec((nb, tt, R_WIDTH), blk),
                   pl.BlockSpec((nb, N_HEADS_R, HEAD_DIM, HEAD_DIM), lambda bi, i: (bi, 0, 0, 0))],
        out_shape=[jax.ShapeDtypeStruct((b, t, R_WIDTH), F32),
                   jax.ShapeDtypeStruct((b, N_HEADS_R, HEAD_DIM, HEAD_DIM), s0.dtype)],
        scratch_shapes=[pltpu.VMEM((nb, 4, LANES, LANES), F32), pltpu.VMEM((nb, 1, RWKV_PAD), F32),
                        big(), big(), big(), big(), big(), big(), big(), big(), big(), big(), big()],
        compiler_params=_params(("parallel", "arbitrary")),
        name="rwkv_mix",
    )(pr3, shift0, s0, mu_p, vecs, rk_flat, w2p, a2p, g2p, hsum, ltri)


def _post_kernel(attn_ref, rw_ref, x_ref, g_ref, wout_ref, wq_ref, wo_ref, mk_ref, mv_ref, o_ref, *, nb, t):
    m = _mm(attn_ref[...], wout_ref[0:A_WIDTH, :]) + _mm(rw_ref[...], wout_ref[A_WIDTH:, :])
    x1 = x_ref[...] + _rms(m, g_ref[NORM_MIX_POST:NORM_MIX_POST + 1, :])
    q = _mm(_rms(x1, g_ref[NORM_X_PRE:NORM_X_PRE + 1, :]), wq_ref[...])
    scale = MEM_HEAD_DIM ** -0.5
    jobs = [(bi, hd) for bi in range(nb) for hd in range(MEM_HEADS)]
    cols = lambda hd: slice(hd * MEM_HEAD_DIM, (hd + 1) * MEM_HEAD_DIM)
    head_rows = lambda hd: pl.ds(hd, MEM_LEN, stride=MEM_HEADS)
    scores = [_mm_nt(q[bi * t:(bi + 1) * t, cols(hd)], mk_ref[bi, head_rows(hd), :]) * scale for bi, hd in jobs]
    exps = [jnp.exp(s - jnp.max(s, axis=-1, keepdims=True)) for s in scores]
    ones = jnp.ones((MEM_LEN, MEM_HEAD_DIM), F32)
    outs = []
    for (bi, hd), e in zip(jobs, exps):
        o2 = _mm(e, jnp.concatenate([mv_ref[bi, head_rows(hd), :], ones], axis=1))
        outs.append(o2[:, :MEM_HEAD_DIM] / o2[:, MEM_HEAD_DIM:])
    row_blocks = [jnp.concatenate(outs[bi * MEM_HEADS:(bi + 1) * MEM_HEADS], axis=1) for bi in range(nb)]
    o = row_blocks[0] if nb == 1 else jnp.concatenate(row_blocks, axis=0)
    c = _mm(o, wo_ref[...])
    o_ref[...] = x1 + _rms(c, g_ref[NORM_X_POST:NORM_X_POST + 1, :])


def _post(attn, rw, x2d, l, gains, w_out, w_mq, w_mo, lm, mk, mv, nb, t, tiles_per_batch):
    n = x2d.shape[0]
    tm = nb * t
    row = lambda i: (i, 0)
    if nb == 1:
        mmap = lambda i: (lm, i // tiles_per_batch, 0, 0)
    else:
        mmap = lambda i: (lm, i, 0, 0)
    mem_rows = MEM_LEN * MEM_HEADS
    return pl.pallas_call(
        functools.partial(_post_kernel, nb=nb, t=t),
        grid=(n // tm,),
        in_specs=[pl.BlockSpec((tm, A_WIDTH), row), pl.BlockSpec((tm, R_WIDTH), row),
                  pl.BlockSpec((tm, D_MODEL), row), _layer_block((8, D_MODEL), l),
                  _layer_block((D_MODEL, D_MODEL), l), _layer_block((D_MODEL, MEM_WIDTH), l),
                  _layer_block((MEM_WIDTH, D_MODEL), l),
                  pl.BlockSpec((None, nb, mem_rows, MEM_HEAD_DIM), mmap),
                  pl.BlockSpec((None, nb, mem_rows, MEM_HEAD_DIM), mmap)],
        out_specs=pl.BlockSpec((tm, D_MODEL), row),
        out_shape=jax.ShapeDtypeStruct((n, D_MODEL), F32),
        compiler_params=_params(("parallel",)),
        name="post_mix",
    )(attn, rw, x2d, gains, w_out, w_mq, w_mo, mk, mv)


FFN_CHUNK = 256


def _ffn_kernel(x_ref, g_ref, wgu_ref, wd_ref, o_ref):
    x = x_ref[...]
    h = _rms(x, g_ref[NORM_FFN_PRE:NORM_FFN_PRE + 1, :]).astype(BF16)
    acc = None
    for j in range(D_FF // FFN_CHUNK):
        cols = slice(j * FFN_CHUNK, (j + 1) * FFN_CHUNK)
        gate = jnp.dot(h, wgu_ref[:, cols], preferred_element_type=F32)
        up = jnp.dot(h, wgu_ref[:, D_FF + j * FFN_CHUNK:D_FF + (j + 1) * FFN_CHUNK], preferred_element_type=F32)
        act = (gate * jax.nn.sigmoid(gate)) * up
        part = _mm(act, wd_ref[cols, :])
        acc = part if acc is None else acc + part
    o_ref[...] = x + _rms(acc, g_ref[NORM_FFN_POST:NORM_FFN_POST + 1, :])


def _ffn(x2d, l, gains, w_gu, w_dn, tm):
    n = x2d.shape[0]
    row = lambda i: (i, 0)
    resident = dict(pipeline_mode=pl.Buffered(1))
    return pl.pallas_call(
        _ffn_kernel,
        grid=(n // tm,),
        in_specs=[pl.BlockSpec((tm, D_MODEL), row), _layer_block((8, D_MODEL), l),
                  pl.BlockSpec((None, D_MODEL, 2 * D_FF), lambda i: (l, 0, 0), **resident),
                  pl.BlockSpec((None, D_FF, D_MODEL), lambda i: (l, 0, 0), **resident)],
        out_specs=pl.BlockSpec((tm, D_MODEL), row),
        out_shape=jax.ShapeDtypeStruct((n, D_MODEL), F32),
        compiler_params=_params(("parallel",)),
        name="ffn",
    )(x2d, gains, w_gu, w_dn)


def _memkv_kernel(x_ref, g_ref, w_ref, k_ref, v_ref):
    kv = _mm(_rms(x_ref[...], g_ref[NORM_MEM:NORM_MEM + 1, :]), w_ref[...])
    for hd in range(MEM_HEADS):
        rows = pl.ds(hd, MEM_LEN, stride=MEM_HEADS)
        k_ref[0, rows, :] = kv[:, hd * MEM_HEAD_DIM:(hd + 1) * MEM_HEAD_DIM]
        v_ref[0, rows, :] = kv[:, MEM_WIDTH + hd * MEM_HEAD_DIM:MEM_WIDTH + (hd + 1) * MEM_HEAD_DIM]


def _memkv(mem2d, l, gains, w_kv):
    n = mem2d.shape[0]
    nbatch = n // MEM_LEN
    mem_rows = MEM_LEN * MEM_HEADS
    out = jax.ShapeDtypeStruct((nbatch, mem_rows, MEM_HEAD_DIM), F32)
    ospec = pl.BlockSpec((1, mem_rows, MEM_HEAD_DIM), lambda i: (i, 0, 0))
    return pl.pallas_call(
        _memkv_kernel,
        grid=(nbatch,),
        in_specs=[pl.BlockSpec((MEM_LEN, D_MODEL), lambda i: (i, 0)), _layer_block((8, D_MODEL), l),
                  _layer_block((D_MODEL, 2 * MEM_WIDTH), l)],
        out_specs=[ospec, ospec],
        out_shape=[out, out],
        compiler_params=_params(("parallel",)),
        name="mem_kv",
    )(mem2d, gains, w_kv)


def _rope_tables(pos):
    half = HEAD_DIM // 2
    inv = ROPE_THETA ** (-jnp.arange(half, dtype=F32) / half)
    ang = pos.astype(F32)[:, None] * inv[None, :]
    cos = jnp.cos(ang)
    sin = jnp.sin(ang)
    cos_t = jnp.tile(cos, (1, LANES // half))
    sin_t = jnp.tile(jnp.concatenate([-sin, sin], axis=1), (1, LANES // HEAD_DIM))
    return cos_t, sin_t


def _stacked_params(norm_gains, w_in, attn_sink, shift_mu, rwkv_vecs, rwkv_rk, rwkv_w2, rwkv_a2, rwkv_g2,
                    w_out, w_mem_q, w_mem_kv, w_mem_o, w_gate_up, w_down):
    depth = w_in.shape[0]
    return dict(
        gains=jnp.pad(norm_gains, ((0, 0), (0, 8 - norm_gains.shape[1]), (0, 0))),
        w_in=jnp.pad(w_in, ((0, 0), (0, 0), (0, IN_COLS_PAD - w_in.shape[2]))).astype(BF16),
        sink=jnp.broadcast_to(attn_sink[:, :, None], attn_sink.shape + (LANES,)).astype(F32),
        mu=jnp.pad(shift_mu, ((0, 0), (0, RWKV_PAD - RWKV_PROJ)))[:, None, :],
        vecs=rwkv_vecs, rk=rwkv_rk.reshape(depth, 1, R_WIDTH),
        w2=jnp.pad(rwkv_w2, ((0, 0), (0, LANES - DECAY_LORA), (0, 0))).astype(BF16),
        a2=jnp.pad(rwkv_a2, ((0, 0), (DECAY_LORA, LANES - DECAY_LORA - AAA_LORA), (0, 0))).astype(BF16),
        g2=jnp.pad(rwkv_g2, ((0, 0), (0, 2 * LANES - GATE_LORA), (0, 0))).astype(BF16),
        w_out=w_out.astype(BF16), w_mq=w_mem_q.astype(BF16), w_mkv=w_mem_kv.astype(BF16),
        w_mo=w_mem_o.astype(BF16), w_gu=w_gate_up.astype(BF16), w_dn=w_down.astype(BF16))


def _head_sum_matrix():
    i = jnp.arange(2 * LANES)
    return ((i[:, None] // HEAD_DIM) == (i[None, :] // HEAD_DIM)).astype(BF16)


def _layer(x2d, l, b, t, tabs, lm, mk, mv, swa_cache, ls, s0, shift0, sp, hsum, in_tm, rw_tt, post_nb, post_t,
           ffn_tm):
    q, k, v, pr = _in_proj(x2d, l, sp["gains"], sp["w_in"], tabs[0], tabs[1], in_tm)
    if swa_cache is None:
        attn = _swa_prompt(q, k, v, l, sp["sink"], b, t)
    else:
        attn = _swa_sample(q, k, v, l, swa_cache[0], swa_cache[1], sp["sink"], b, t)
    pr3 = pr.reshape(b, t, RWKV_PAD)
    rw, s_fin = _rwkv(pr3, l, ls, shift0, s0, sp["mu"], sp["vecs"], sp["rk"], sp["w2"], sp["a2"], sp["g2"], hsum,
                      rw_tt[0], min(CHUNK, t), rw_tt[1])
    x2 = _post(attn, rw.reshape(b * t, R_WIDTH), x2d, l, sp["gains"], sp["w_out"], sp["w_mq"], sp["w_mo"],
               lm, mk, mv, post_nb, post_t, t // post_t)
    x3 = _ffn(x2, l, sp["gains"], sp["w_gu"], sp["w_dn"], ffn_tm)
    return x3, k, v, s_fin, pr3[:, t - 1:t, :RWKV_PROJ]


def kernel(x_prompt, mem_prompt, x_sample, cache_swa_k, cache_swa_v, cache_mem_k, cache_mem_v, state_rwkv,
           state_shift, norm_gains, w_in, attn_sink, shift_mu, rwkv_vecs, rwkv_rk, rwkv_w2, rwkv_a2, rwkv_g2,
           w_out, w_mem_q, w_mem_kv, w_mem_o, w_gate_up, w_down):
    b, t, _ = x_prompt.shape
    bd, tn, _ = x_sample.shape
    depth = w_in.shape[0]
    m_len = mem_prompt.shape[1]
    cache_len = cache_swa_k.shape[2]
    tabs_p = _rope_tables(jnp.arange(t, dtype=jnp.int32))
    cs, sn = _rope_tables(PAST_LEN + jnp.arange(tn, dtype=jnp.int32))
    tabs_s = (jnp.tile(cs, (bd, 1)), jnp.tile(sn, (bd, 1)))
    hsum = _head_sum_matrix()
    sp = _stacked_params(norm_gains, w_in, attn_sink, shift_mu, rwkv_vecs, rwkv_rk, rwkv_w2, rwkv_a2, rwkv_g2,
                         w_out, w_mem_q, w_mem_kv, w_mem_o, w_gate_up, w_down)
    s0_p = jnp.zeros((1, b, N_HEADS_R, HEAD_DIM, HEAD_DIM), F32)
    sh0_p = jnp.zeros((1, b, 1, RWKV_PAD), F32)
    sh0_s = jnp.pad(state_shift, ((0, 0), (0, 0), (0, 0), (0, RWKV_PAD - RWKV_PROJ)))
    swa_cache = (cache_swa_k.reshape(depth, bd, cache_len, KV_WIDTH_A),
                 cache_swa_v.reshape(depth, bd, cache_len, KV_WIDTH_A))
    mem_rows = m_len * MEM_HEADS
    cmk = cache_mem_k.reshape(depth, bd, mem_rows, MEM_HEAD_DIM)
    cmv = cache_mem_v.reshape(depth, bd, mem_rows, MEM_HEAD_DIM)
    xp = x_prompt.reshape(b * t, D_MODEL)
    xs = x_sample.reshape(bd * tn, D_MODEL)
    mem2d = mem_prompt.reshape(b * m_len, D_MODEL)
    in_tm_p = min(512, t)
    rw_tt_p = (min(512, t), 1)
    rw_tt_s = (tn, 4 if bd % 4 == 0 else 1)
    post_t_p = min(512, t)
    ffn_tm_p = min(512, b * t)
    keep = t - min(WINDOW, t)
    pk, pv, pmk, pmv, pS, psh = [], [], [], [], [], []
    sk, sv, sS, ssh = [], [], [], []
    for l in range(depth):
        mk_l, mv_l = _memkv(mem2d, l, sp["gains"], sp["w_mkv"])
        xp, k_l, v_l, s_l, sh_l = _layer(xp, l, b, t, tabs_p, 0, mk_l[None], mv_l[None], None, 0, s0_p, sh0_p, sp,
                                         hsum, in_tm_p, rw_tt_p, 1, post_t_p, ffn_tm_p)
        pk.append(k_l.reshape(b, t, KV_WIDTH_A)[:, keep:].reshape(b, t - keep, N_KV_A, HEAD_DIM))
        pv.append(v_l.reshape(b, t, KV_WIDTH_A)[:, keep:].reshape(b, t - keep, N_KV_A, HEAD_DIM))
        pmk.append(mk_l.reshape(b, m_len, MEM_HEADS, MEM_HEAD_DIM))
        pmv.append(mv_l.reshape(b, m_len, MEM_HEADS, MEM_HEAD_DIM))
        pS.append(s_l)
        psh.append(sh_l)

        xs, k2, v2, s2, sh2 = _layer(xs, l, bd, tn, tabs_s, l, cmk, cmv, swa_cache, l, state_rwkv, sh0_s, sp,
                                     hsum, bd * tn, rw_tt_s, bd, tn, bd * tn)
        sk.append(k2.reshape(bd, tn, N_KV_A, HEAD_DIM))
        sv.append(v2.reshape(bd, tn, N_KV_A, HEAD_DIM))
        sS.append(s2)
        ssh.append(sh2)
    return (xp.reshape(b, t, D_MODEL), xs.reshape(bd, tn, D_MODEL), jnp.stack(pk), jnp.stack(pv),
            jnp.stack(pmk), jnp.stack(pmv), jnp.stack(pS), jnp.stack(psh),
            jnp.stack(sk), jnp.stack(sv), jnp.stack(sS), jnp.stack(ssh))
```

```python
import functools
import math

import jax
import jax.numpy as jnp
from jax import lax
from jax.experimental import pallas as pl
from jax.experimental.pallas import tpu as pltpu

F32 = jnp.float32
BF16 = jnp.bfloat16

D_MODEL = 1024
HEAD_DIM = 64
CHUNK = 64
A_WIDTH = 512
KV_WIDTH_A = 128
N_KV_A = 2
WINDOW = 128
PAST_LEN = 4096
ROPE_THETA = 10000.0
R_WIDTH = 512
N_HEADS_R = 8
DECAY_LORA = 64
AAA_LORA = 64
GATE_LORA = 160
RWKV_PROJ = 3 * R_WIDTH + DECAY_LORA + AAA_LORA + GATE_LORA
RWKV_PAD = 1920
IN_COLS_PAD = A_WIDTH + 2 * KV_WIDTH_A + RWKV_PAD
GN_EPS = 6.4e-4
MEM_LEN = 256
MEM_HEADS = 4
MEM_HEAD_DIM = 128
MEM_WIDTH = 512
D_FF = 2816
RMS_EPS = 1e-6
NEG_INF = -1e30
NORM_MIX_PRE, NORM_MIX_POST, NORM_X_PRE, NORM_X_POST, NORM_MEM, NORM_FFN_PRE, NORM_FFN_POST = range(7)
EXP_M05 = math.exp(-0.5)

LANES = 128
VMEM_LIMIT = 56 * 1024 * 1024


def _params(sem):
    return pltpu.CompilerParams(dimension_semantics=sem, vmem_limit_bytes=VMEM_LIMIT)


def _rms(x, g):
    ms = jnp.mean(x * x, axis=-1, keepdims=True)
    return x * lax.rsqrt(ms + RMS_EPS) * g


def _mm(a, b):
    return jnp.dot(a.astype(BF16), b.astype(BF16), preferred_element_type=F32)


def _mm_nt(a, b):
    return lax.dot_general(a.astype(BF16), b.astype(BF16), (((1,), (1,)), ((), ())),
                           preferred_element_type=F32)


def _split(x):
    hi = x.astype(BF16)
    lo = (x - hi.astype(F32)).astype(BF16)
    return hi, lo


def _mm_lsplit(a_exact_bf16, x):
    hi, lo = _split(x)
    return (jnp.dot(a_exact_bf16, hi, preferred_element_type=F32)
            + jnp.dot(a_exact_bf16, lo, preferred_element_type=F32))


def _mm_rsplit(x, b_exact_bf16):
    hi, lo = _split(x)
    return (jnp.dot(hi, b_exact_bf16, preferred_element_type=F32)
            + jnp.dot(lo, b_exact_bf16, preferred_element_type=F32))


def _in_kernel(x_ref, g_ref, w_ref, cos_ref, sin_ref, mu_ref, sh0_ref, q_ref, k_ref, v_ref, pr_ref, shl_ref,
               carry_scr, *, tm, t_seq):
    h = _rms(x_ref[...], g_ref[NORM_MIX_PRE:NORM_MIX_PRE + 1, :]).astype(BF16)
    p = jnp.dot(h, w_ref[...], preferred_element_type=F32)
    cos = cos_ref[...]
    sin = sin_ref[...]
    lane = lax.broadcasted_iota(jnp.int32, cos.shape, 1)
    first_half = (lane & (HEAD_DIM // 2)) == 0

    def rope(xc):
        sw = jnp.where(first_half, pltpu.roll(xc, LANES - HEAD_DIM // 2, 1), pltpu.roll(xc, HEAD_DIM // 2, 1))
        return xc * cos + sw * sin

    for j in range(A_WIDTH // LANES):
        q_ref[:, j * LANES:(j + 1) * LANES] = rope(p[:, j * LANES:(j + 1) * LANES])
    k_ref[...] = rope(p[:, A_WIDTH:A_WIDTH + KV_WIDTH_A])
    v_ref[...] = p[:, A_WIDTH + KV_WIDTH_A:A_WIDTH + 2 * KV_WIDTH_A]
    raw = p[:, A_WIDTH + 2 * KV_WIDTH_A:]
    rowi = lax.broadcasted_iota(jnp.int32, (tm, 1), 0)
    prev = pltpu.roll(raw, 1, 0)
    if tm <= t_seq:
        first = (pl.program_id(0) % (t_seq // tm)) == 0
        prev = jnp.where(rowi == 0, jnp.where(first, sh0_ref[0], carry_scr[...]), prev)
        carry_scr[...] = raw[tm - 1:tm, :]
        shl_ref[0] = raw[tm - 1:tm, :]
    else:
        for s in range(tm // t_seq):
            prev = jnp.where(rowi == s * t_seq, sh0_ref[s], prev)
            shl_ref[s] = raw[(s + 1) * t_seq - 1:(s + 1) * t_seq, :]
    pr_ref[...] = raw + (prev - raw) * mu_ref[...]


def _layer_block(shape, l):
    nd = len(shape)
    return pl.BlockSpec((None,) + tuple(shape), lambda *_: (l,) + (0,) * nd)


def _in_proj(x2d, l, gains, w_in_p, cos_t, sin_t, mu_p, ls, shift0, t_seq, tm):
    n = x2d.shape[0]
    n_seq = n // t_seq
    tab_blocks = cos_t.shape[0] // tm
    row = lambda i: (i, 0)
    tab = lambda i: (i % tab_blocks, 0)
    if tm <= t_seq:
        seqs, seq_of = 1, lambda i: i // (t_seq // tm)
    else:
        seqs, seq_of = tm // t_seq, lambda i: i
    return pl.pallas_call(
        functools.partial(_in_kernel, tm=tm, t_seq=t_seq),
        grid=(n // tm,),
        in_specs=[pl.BlockSpec((tm, D_MODEL), row), _layer_block((8, D_MODEL), l),
                  _layer_block((D_MODEL, IN_COLS_PAD), l),
                  pl.BlockSpec((tm, LANES), tab), pl.BlockSpec((tm, LANES), tab),
                  _layer_block((1, RWKV_PAD), l),
                  pl.BlockSpec((None, seqs, 1, RWKV_PAD), lambda i: (ls, seq_of(i), 0, 0))],
        out_specs=[pl.BlockSpec((tm, A_WIDTH), row), pl.BlockSpec((tm, KV_WIDTH_A), row),
                   pl.BlockSpec((tm, KV_WIDTH_A), row), pl.BlockSpec((tm, RWKV_PAD), row),
                   pl.BlockSpec((seqs, 1, RWKV_PAD), lambda i: (seq_of(i), 0, 0))],
        out_shape=[jax.ShapeDtypeStruct((n, A_WIDTH), F32), jax.ShapeDtypeStruct((n, KV_WIDTH_A), F32),
                   jax.ShapeDtypeStruct((n, KV_WIDTH_A), F32), jax.ShapeDtypeStruct((n, RWKV_PAD), F32),
                   jax.ShapeDtypeStruct((n_seq, 1, RWKV_PAD), F32)],
        scratch_shapes=[pltpu.VMEM((1, RWKV_PAD), F32)],
        compiler_params=_params(("arbitrary",)),
        name="in_proj",
    )(x2d, gains, w_in_p, cos_t, sin_t, mu_p, shift0)


def _sink_attend(jobs, sink_ref, t):
    nk = jobs[0][1].shape[0]
    nkp = 2 * LANES
    lane = lax.broadcasted_iota(jnp.int32, (t, LANES), 1)
    m0 = lane < HEAD_DIM
    coli = lax.broadcasted_iota(jnp.int32, (1, nkp), 1)
    fills = []
    for kv in range(N_KV_A):
        blocks = []
        for g in range(4):
            sg = sink_ref[4 * kv + g:4 * kv + g + 1, :]
            blocks.append(jnp.broadcast_to(jnp.where(coli == nk, jnp.concatenate([sg, sg], axis=1), NEG_INF),
                                           (t, nkp)))
        fills.append(jnp.concatenate(blocks, axis=0))
    k_tail = jnp.zeros((nkp - nk, LANES), F32)
    tail_row = lax.broadcasted_iota(jnp.int32, (nkp - nk, nkp), 0)
    tail_col = lax.broadcasted_iota(jnp.int32, (nkp - nk, nkp), 1)
    v_tail = jnp.where((tail_row == 0) & (tail_col >= LANES), 1.0, 0.0).astype(F32)
    ones = jnp.ones((nk, LANES), F32)
    scores = []
    for q_rows, kdup, _, kv, valid in jobs:
        parts = []
        for p in (2 * kv, 2 * kv + 1):
            qp = q_rows[:, p * LANES:(p + 1) * LANES]
            parts.append(jnp.where(m0, qp, 0.0))
            parts.append(jnp.where(m0, 0.0, qp))
        lhs = jnp.concatenate(parts, axis=0)
        s = _mm_nt(lhs, jnp.concatenate([kdup, k_tail], axis=0))
        keep = coli < nk
        if valid is not None:
            keep = keep & valid
        scores.append(jnp.where(keep, s, fills[kv]))
    exps = [jnp.exp(s - jnp.max(s, axis=-1, keepdims=True)) for s in scores]
    outs = []
    for (_, _, vdup, _, _), e in zip(jobs, exps):
        v2 = jnp.concatenate([jnp.concatenate([vdup, ones], axis=1), v_tail], axis=0)
        o2 = _mm(e, v2)
        o = o2[:, :LANES] / o2[:, LANES:]
        outs.append([jnp.where(m0, o[(2 * pi) * t:(2 * pi + 1) * t], o[(2 * pi + 1) * t:(2 * pi + 2) * t])
                     for pi in range(2)])
    return outs


SWA_SCALE = HEAD_DIM ** -0.5


def _dup_heads(x):
    lane = lax.broadcasted_iota(jnp.int32, x.shape, 1)
    m0 = lane < HEAD_DIM
    xs = pltpu.roll(x, HEAD_DIM, 1)
    return [jnp.where(m0, x, xs), jnp.where(m0, xs, x)]


SWA_TQ = 512


def _swa_prompt_kernel(q_ref, kp_ref, kc_ref, vp_ref, vc_ref, sink_ref, o_ref):
    i = pl.program_id(1)
    k = jnp.concatenate([kp_ref[...], kc_ref[...]], axis=0)
    v = jnp.concatenate([vp_ref[...], vc_ref[...]], axis=0)
    kd = _dup_heads(k * SWA_SCALE)
    vd = _dup_heads(v)
    nk = 3 * CHUNK
    slot = lax.broadcasted_iota(jnp.int32, (1, 2 * LANES), 1) // CHUNK
    jobs = []
    for j in range(SWA_TQ // CHUNK):
        qj = q_ref[j * CHUNK:(j + 1) * CHUNK, :]
        valid = (slot + (i * (SWA_TQ // CHUNK) + j - 2)) >= 0
        for kv in range(N_KV_A):
            jobs.append((qj, kd[kv][j * CHUNK:j * CHUNK + nk], vd[kv][j * CHUNK:j * CHUNK + nk], kv, valid))
    outs = _sink_attend(jobs, sink_ref, CHUNK)
    for n, out in enumerate(outs):
        j, kv = divmod(n, N_KV_A)
        for pi in range(2):
            p = 2 * kv + pi
            o_ref[j * CHUNK:(j + 1) * CHUNK, p * LANES:(p + 1) * LANES] = out[pi]


def _swa_prompt(q, k, v, l, sink_b, b, t):
    n = b * t
    nq = t // SWA_TQ
    qmap = lambda bi, i: (bi * nq + i, 0)
    pmap = lambda bi, i: (jnp.maximum(bi * (t // WINDOW) + (SWA_TQ // WINDOW) * i - 1, 0), 0)
    const = lambda bi, i: (0, 0)
    return pl.pallas_call(
        _swa_prompt_kernel,
        grid=(b, nq),
        in_specs=[pl.BlockSpec((SWA_TQ, A_WIDTH), qmap),
                  pl.BlockSpec((WINDOW, KV_WIDTH_A), pmap), pl.BlockSpec((SWA_TQ, KV_WIDTH_A), qmap),
                  pl.BlockSpec((WINDOW, KV_WIDTH_A), pmap), pl.BlockSpec((SWA_TQ, KV_WIDTH_A), qmap),
                  _layer_block((8, LANES), l)],
        out_specs=pl.BlockSpec((SWA_TQ, A_WIDTH), qmap),
        out_shape=jax.ShapeDtypeStruct((n, A_WIDTH), F32),
        compiler_params=_params(("parallel", "parallel")),
        name="swa_prompt",
    )(q, k, k, v, v, sink_b)


def _swa_sample_kernel(q_ref, kc_ref, kn_ref, vc_ref, vn_ref, sink_ref, o_ref, *, t, nb):
    jobs = []
    for bi in range(nb):
        rows = slice(bi * t, (bi + 1) * t)
        kd = _dup_heads(jnp.concatenate([kc_ref[bi], kn_ref[rows, :]], axis=0) * SWA_SCALE)
        vd = _dup_heads(jnp.concatenate([vc_ref[bi], vn_ref[rows, :]], axis=0))
        for kv in range(N_KV_A):
            jobs.append((q_ref[rows, :], kd[kv], vd[kv], kv, None))
    outs = _sink_attend(jobs, sink_ref, t)
    for n, out in enumerate(outs):
        bi, kv = divmod(n, N_KV_A)
        for pi in range(2):
            p = 2 * kv + pi
            o_ref[bi * t:(bi + 1) * t, p * LANES:(p + 1) * LANES] = out[pi]


def _swa_sample(q, k, v, l, kc, vc, sink_b, b, t):
    n = b * t
    cache = kc.shape[2]
    nb = 4 if b % 4 == 0 else 1
    row = lambda bi: (bi, 0)
    cmap = lambda bi: (l, bi, 0, 0)
    return pl.pallas_call(
        functools.partial(_swa_sample_kernel, t=t, nb=nb),
        grid=(b // nb,),
        in_specs=[pl.BlockSpec((nb * t, A_WIDTH), row),
                  pl.BlockSpec((None, nb, cache, KV_WIDTH_A), cmap), pl.BlockSpec((nb * t, KV_WIDTH_A), row),
                  pl.BlockSpec((None, nb, cache, KV_WIDTH_A), cmap), pl.BlockSpec((nb * t, KV_WIDTH_A), row),
                  _layer_block((8, LANES), l)],
        out_specs=pl.BlockSpec((nb * t, A_WIDTH), row),
        out_shape=jax.ShapeDtypeStruct((n, A_WIDTH), F32),
        compiler_params=_params(("parallel",)),
        name="swa_sample",
    )(q, kc, k, vc, v, sink_b)


def _tri_inverse(nmat, masks, eye):
    p = eye + jnp.where(masks[0], nmat, 0.0)
    for m in masks[1:]:
        e = jnp.where(m, nmat, 0.0)
        p = p + _mm(_mm(p, e), p)
    return p


def _rwkv_kernel_old(pr_ref, sh0_ref, s0_ref, mu_ref, vec_ref, rk_ref, w2_ref, a2_ref, g2_ref, hsum_ref,
                     ltri_ref, out_ref, sfin_ref,
                     s_scr, carry_scr, rt_s, at_s, bt_s, kt_s, bb_s, kb_s, v_s, et_s, y_s, *, c_len, tt, nb):
    i = pl.program_id(1)
    n_i = pl.num_programs(1)
    n_rows = nb * tt

    @pl.when(i == 0)
    def _():
        zero = jnp.zeros((HEAD_DIM, HEAD_DIM), F32)
        for bi in range(nb):
            for p in range(4):
                s_scr[bi, p] = jnp.concatenate(
                    [jnp.concatenate([s0_ref[bi, 2 * p].astype(F32), zero], axis=1),
                     jnp.concatenate([zero, s0_ref[bi, 2 * p + 1].astype(F32)], axis=1)], axis=0)
        carry_scr[...] = sh0_ref[...]

    rowi = lax.broadcasted_iota(jnp.int32, (n_rows, 1), 0)

    def shifted(a, b):
        cur = pr_ref[:, :, a:b].reshape(n_rows, b - a)
        prev = pltpu.roll(cur, 1, 0)
        for bi in range(nb):
            prev = jnp.where(rowi == bi * tt, carry_scr[bi, :, a:b], prev)
        return cur + (prev - cur) * mu_ref[:, a:b]

    r = shifted(0, 512)
    k = shifted(512, 1024)
    v = shifted(1024, 1536)
    xwa = shifted(1536, 1664)
    xg = shifted(1664, RWKV_PAD)
    for bi in range(nb):
        carry_scr[bi] = pr_ref[bi, tt - 1:tt, :]

    w0 = vec_ref[0:1, :]
    a0 = vec_ref[1:2, :]
    k_k = vec_ref[2:3, :]
    k_a = vec_ref[3:4, :]
    gn_g = vec_ref[4:5, :]
    gn_b = vec_ref[5:6, :]
    hsum = hsum_ref[...]
    hw = hsum.shape[0]

    def head_sum(x):
        return jnp.concatenate([_mm_rsplit(x[:, j * hw:(j + 1) * hw], hsum) for j in range(R_WIDTH // hw)], axis=1)

    z = w0 + _mm(jnp.tanh(xwa), w2_ref[...])
    wlog = -EXP_M05 * jax.nn.sigmoid(z)
    a = jax.nn.sigmoid(a0 + _mm(xwa, a2_ref[...]))
    g = _mm(jax.nn.sigmoid(xg), g2_ref[...])
    kk = k * k_k
    kk = kk * lax.rsqrt(jnp.maximum(head_sum(kk * kk), 1e-24))
    k_f = k * (1.0 + (a - 1.0) * k_a)
    bvec = kk * a
    bonus = head_sum(r * k_f * rk_ref[...]) * v

    cum = _mm_lsplit(ltri_ref[...], wlog)
    tot = jnp.concatenate([jnp.broadcast_to(cum[r1 - 1:r1, :], (c_len, R_WIDTH))
                           for r1 in range(c_len, n_rows + 1, c_len)], axis=0)
    e_in = jnp.exp(cum)
    e_inv = jnp.exp(-cum)
    e_end = jnp.exp(tot - cum)
    rt_s[...] = r * e_in
    at_s[...] = -kk * jnp.exp(cum - wlog)
    bt_s[...] = bvec * e_inv
    kt_s[...] = k_f * e_inv
    bb_s[...] = bvec * e_end
    kb_s[...] = k_f * e_end
    v_s[...] = v
    et_s[...] = jnp.exp(tot)

    n2 = 2 * c_len
    ti = lax.broadcasted_iota(jnp.int32, (c_len, n2), 0)
    lane2 = lax.broadcasted_iota(jnp.int32, (c_len, n2), 1)
    si = lane2 & (c_len - 1)
    first_c = lane2 < c_len
    strict = ti > si
    incl = ti >= si
    eye = jnp.where(ti == si, 1.0, 0.0).astype(F32)
    masks = []
    half = 1
    while half < c_len:
        blk = 2 * half
        masks.append(((ti & ~(blk - 1)) == (si & ~(blk - 1))) & ((ti & half) != 0) & ((si & half) == 0))
        half = blk
    m0 = lax.broadcasted_iota(jnp.int32, (c_len, LANES), 1) < HEAD_DIM
    rl = lax.broadcasted_iota(jnp.int32, (LANES, LANES), 0)
    cl = lax.broadcasted_iota(jnp.int32, (LANES, LANES), 1)
    same_head = (rl >= HEAD_DIM) == (cl >= HEAD_DIM)
    fused = n2 == LANES
    zeros_c = jnp.zeros((c_len, LANES), F32)

    def bd(x):
        return jnp.concatenate([jnp.where(m0, x, 0.0), jnp.where(m0, 0.0, x)], axis=0)

    def bd_t(x):
        return jnp.concatenate([jnp.where(first_c, x, 0.0), jnp.where(first_c, 0.0, x)], axis=0)

    chains = [(bi, c, p) for bi in range(nb) for c in range(tt // c_len) for p in range(4)]

    def rows_lanes(ch):
        bi, c, p = ch
        r0 = bi * tt + c * c_len
        return slice(r0, r0 + c_len), slice(p * LANES, (p + 1) * LANES)

    def cat(ref, ch):
        rows, ls = rows_lanes(ch)
        return ref[rows, ls]

    a_ab, a_ak, a_rb, a_rk = {}, {}, {}, {}
    for ch in chains:
        at, rt, rb, rkt = cat(at_s, ch), cat(rt_s, ch), bd(cat(bt_s, ch)), bd(cat(kt_s, ch))
        if fused:
            amat = _mm_nt(jnp.concatenate([at, rt], axis=0), jnp.concatenate([rb, rkt], axis=0))
            q_ab, q_ak = amat[:c_len, :n2], amat[:c_len, n2:]
            q_rb, q_rk = amat[c_len:, :n2], amat[c_len:, n2:]
        else:
            q_ab, q_ak, q_rb, q_rk = _mm_nt(at, rb), _mm_nt(at, rkt), _mm_nt(rt, rb), _mm_nt(rt, rkt)
        a_ab[ch] = jnp.where(strict, q_ab, 0.0)
        a_ak[ch] = jnp.where(strict, q_ak, 0.0)
        a_rb[ch] = jnp.where(incl, q_rb, 0.0)
        a_rk[ch] = jnp.where(incl, q_rk, 0.0)

    tinv = {ch: eye + jnp.where(masks[0], a_ab[ch], 0.0) for ch in chains}
    for m in masks[1:]:
        pe = {ch: _mm(tinv[ch], bd_t(jnp.where(m, a_ab[ch], 0.0))) for ch in chains}
        tinv = {ch: tinv[ch] + _mm(pe[ch], bd_t(tinv[ch])) for ch in chains}

    zv = {ch: _mm(a_ak[ch], bd(cat(v_s, ch))) for ch in chains}
    r_hat, y_hat, m_mat, g_mat = {}, {}, {}, {}
    if fused:
        w = {ch: _mm(tinv[ch], jnp.concatenate([bd(cat(at_s, ch)), bd(zv[ch])], axis=1)) for ch in chains}
        for ch in chains:
            rhs2 = jnp.concatenate([jnp.concatenate([bd(w[ch][:, :LANES]), bd(w[ch][:, LANES:])], axis=1),
                                    jnp.concatenate([jnp.zeros((n2, LANES), F32), bd(cat(v_s, ch))], axis=1)], axis=0)
            ry = _mm(jnp.concatenate([a_rb[ch], a_rk[ch]], axis=1), rhs2)
            r_hat[ch] = cat(rt_s, ch) + ry[:, :LANES]
            y_hat[ch] = ry[:, LANES:]
        for ch in chains:
            lhs_t = jnp.concatenate([w[ch], jnp.concatenate([zeros_c, cat(v_s, ch)], axis=1)], axis=0)
            mg = _mm(lhs_t.T, jnp.concatenate([cat(bb_s, ch), cat(kb_s, ch)], axis=0))
            m_mat[ch] = jnp.where(same_head, mg[:LANES], 0.0)
            g_mat[ch] = jnp.where(same_head, mg[LANES:], 0.0)
    else:
        a_hat = {ch: _mm(tinv[ch], bd(cat(at_s, ch))) for ch in chains}
        u_hat = {ch: _mm(tinv[ch], bd(zv[ch])) for ch in chains}
        for ch in chains:
            r_hat[ch] = cat(rt_s, ch) + _mm(a_rb[ch], bd(a_hat[ch]))
            y_hat[ch] = _mm(a_rb[ch], bd(u_hat[ch])) + _mm(a_rk[ch], bd(cat(v_s, ch)))
        for ch in chains:
            m_mat[ch] = jnp.where(same_head, _mm(a_hat[ch].T, cat(bb_s, ch)), 0.0)
            g_mat[ch] = jnp.where(same_head, _mm(u_hat[ch].T, cat(bb_s, ch)) + _mm(cat(v_s, ch).T, cat(kb_s, ch)), 0.0)

    for bi in range(nb):
        s_cur = [s_scr[bi, p] for p in range(4)]
        for c in range(tt // c_len):
            for p in range(4):
                ch = (bi, c, p)
                rows, ls = rows_lanes(ch)
                e_last = et_s[rows.start:rows.start + 1, ls]
                y_s[rows, ls] = y_hat[ch] + _mm_nt(r_hat[ch], s_cur[p])
                s_cur[p] = s_cur[p] * e_last + _mm(s_cur[p], m_mat[ch]) + g_mat[ch]
        for p in range(4):
            s_scr[bi, p] = s_cur[p]

    y = y_s[...]
    mean = head_sum(y) * (1.0 / HEAD_DIM)
    yc = y - mean
    var = head_sum(yc * yc) * (1.0 / HEAD_DIM)
    yn = yc * lax.rsqrt(var + GN_EPS) * gn_g + gn_b
    res = (yn + bonus) * g
    for bi in range(nb):
        out_ref[bi] = res[bi * tt:(bi + 1) * tt]

    @pl.when(i == n_i - 1)
    def _():
        for bi in range(nb):
            for p in range(4):
                s_pair = s_scr[bi, p]
                sfin_ref[bi, 2 * p] = s_pair[:HEAD_DIM, :HEAD_DIM].astype(sfin_ref.dtype)
                sfin_ref[bi, 2 * p + 1] = s_pair[HEAD_DIM:, HEAD_DIM:].astype(sfin_ref.dtype)


RWKV_SUB = 256


def _interleave(gens):
    gens = list(gens)
    while gens:
        for g in list(gens):
            try:
                next(g)
            except StopIteration:
                gens.remove(g)


def _rwkv_kernel(pr_ref, s0_ref, vec_ref, rk_ref, w2_ref, a2_ref, g2_ref, hsum_ref,
                 ltri_ref, out_ref, sfin_ref,
                 s_scr, rt_s, at_s, bt_s, kt_s, bb_s, kb_s, v_s, et_s, y_s, bonus_s, gate_s,
                 *, c_len, tt, nb, sub):
    i = pl.program_id(1)
    n_i = pl.num_programs(1)
    n_rows = nb * tt

    @pl.when(i == 0)
    def _():
        zero = jnp.zeros((HEAD_DIM, HEAD_DIM), F32)
        for bi in range(nb):
            for p in range(4):
                s_scr[bi, p] = jnp.concatenate(
                    [jnp.concatenate([s0_ref[bi, 2 * p].astype(F32), zero], axis=1),
                     jnp.concatenate([zero, s0_ref[bi, 2 * p + 1].astype(F32)], axis=1)], axis=0)

    w0 = vec_ref[0:1, :]
    a0 = vec_ref[1:2, :]
    k_k = vec_ref[2:3, :]
    k_a = vec_ref[3:4, :]
    gn_g = vec_ref[4:5, :]
    gn_b = vec_ref[5:6, :]
    hsum = hsum_ref[...]
    hw = hsum.shape[0]
    halves = [slice(j * hw, (j + 1) * hw) for j in range(R_WIDTH // hw)]

    def head_sum(x):
        return _mm(x, hsum)

    n2 = 2 * c_len
    ti = lax.broadcasted_iota(jnp.int32, (c_len, n2), 0)
    lane2 = lax.broadcasted_iota(jnp.int32, (c_len, n2), 1)
    si = lane2 & (c_len - 1)
    first_c = lane2 < c_len
    strict = ti > si
    incl = ti >= si
    eye = jnp.where(ti == si, 1.0, 0.0).astype(F32)
    masks = []
    half = 1
    while half < c_len:
        blk = 2 * half
        masks.append(((ti & ~(blk - 1)) == (si & ~(blk - 1))) & ((ti & half) != 0) & ((si & half) == 0))
        half = blk
    m0 = lax.broadcasted_iota(jnp.int32, (c_len, LANES), 1) < HEAD_DIM
    rl = lax.broadcasted_iota(jnp.int32, (LANES, LANES), 0)
    cl = lax.broadcasted_iota(jnp.int32, (LANES, LANES), 1)
    same_head = (rl >= HEAD_DIM) == (cl >= HEAD_DIM)
    fused = n2 == LANES
    zeros_c = jnp.zeros((c_len, LANES), F32)

    def bd(x):
        return jnp.concatenate([jnp.where(m0, x, 0.0), jnp.where(m0, 0.0, x)], axis=0)

    def bd_t(x):
        return jnp.concatenate([jnp.where(first_c, x, 0.0), jnp.where(first_c, 0.0, x)], axis=0)

    def prologue(lo, hi):
        n = hi - lo
        bi0 = lo // tt

        def shifted(a, b):
            if hi - lo <= tt:
                return pr_ref[bi0, lo - bi0 * tt:hi - bi0 * tt, a:b]
            return pr_ref[bi0:hi // tt, :, a:b].reshape(n, b - a)

        xwa = shifted(3 * R_WIDTH, 3 * R_WIDTH + LANES)
        xg = shifted(3 * R_WIDTH + LANES, RWKV_PAD)
        th = jnp.tanh(xwa)
        sg = jax.nn.sigmoid(xg)
        ltri = ltri_ref[...]
        yield
        for cs in halves:
            r = shifted(cs.start, cs.stop)
            k = shifted(R_WIDTH + cs.start, R_WIDTH + cs.stop)
            v = shifted(2 * R_WIDTH + cs.start, 2 * R_WIDTH + cs.stop)
            yield
            z = w0[:, cs] + _mm(th, w2_ref[:, cs])
            wlog = -EXP_M05 * jax.nn.sigmoid(z)
            a = jax.nn.sigmoid(a0[:, cs] + _mm(xwa, a2_ref[:, cs]))
            gate_s[lo:hi, cs] = _mm(sg, g2_ref[:, cs])
            yield
            kk = k * k_k[:, cs]
            kk = kk * lax.rsqrt(jnp.maximum(head_sum(kk * kk), 1e-24))
            k_f = k * (1.0 + (a - 1.0) * k_a[:, cs])
            bvec = kk * a
            bonus_s[lo:hi, cs] = head_sum(r * k_f * rk_ref[:, cs]) * v
            yield
            cum = _mm_lsplit(ltri, wlog)
            tot = jnp.concatenate([jnp.broadcast_to(cum[r1 - 1:r1, :], (c_len, hw))
                                   for r1 in range(c_len, n + 1, c_len)], axis=0)
            e_in = jnp.exp(cum)
            e_inv = jnp.exp(-cum)
            e_end = jnp.exp(tot - cum)
            rt_s[lo:hi, cs] = r * e_in
            at_s[lo:hi, cs] = -kk * jnp.exp(cum - wlog)
            yield
            bt_s[lo:hi, cs] = bvec * e_inv
            kt_s[lo:hi, cs] = k_f * e_inv
            bb_s[lo:hi, cs] = bvec * e_end
            kb_s[lo:hi, cs] = k_f * e_end
            v_s[lo:hi, cs] = v
            et_s[lo:hi, cs] = jnp.exp(tot)
            yield

    s_state = [[s_scr[bi, p] for p in range(4)] for bi in range(nb)]

    def chains(lo, hi):
        chs = [(r0, p) for r0 in range(lo, hi, c_len) for p in range(4)]

        def cat(ref, ch):
            return ref[ch[0]:ch[0] + c_len, ch[1] * LANES:(ch[1] + 1) * LANES]

        a_ab, a_ak, a_rb, a_rk = {}, {}, {}, {}
        for ch in chs:
            at, rt, rb, rkt = cat(at_s, ch), cat(rt_s, ch), bd(cat(bt_s, ch)), bd(cat(kt_s, ch))
            if fused:
                amat = _mm_nt(jnp.concatenate([at, rt], axis=0), jnp.concatenate([rb, rkt], axis=0))
                q_ab, q_ak = amat[:c_len, :n2], amat[:c_len, n2:]
                q_rb, q_rk = amat[c_len:, :n2], amat[c_len:, n2:]
            else:
                q_ab, q_ak, q_rb, q_rk = _mm_nt(at, rb), _mm_nt(at, rkt), _mm_nt(rt, rb), _mm_nt(rt, rkt)
            a_ab[ch] = jnp.where(strict, q_ab, 0.0)
            a_ak[ch] = jnp.where(strict, q_ak, 0.0)
            a_rb[ch] = jnp.where(incl, q_rb, 0.0)
            a_rk[ch] = jnp.where(incl, q_rk, 0.0)
        yield
        tinv = {ch: eye + jnp.where(masks[0], a_ab[ch], 0.0) for ch in chs}
        for m in masks[1:]:
            pe = {ch: _mm(tinv[ch], bd_t(jnp.where(m, a_ab[ch], 0.0))) for ch in chs}
            yield
            tinv = {ch: tinv[ch] + _mm(pe[ch], bd_t(tinv[ch])) for ch in chs}
            yield
        zv = {ch: _mm(a_ak[ch], bd(cat(v_s, ch))) for ch in chs}
        yield
        r_hat, y_hat, m_mat, g_mat = {}, {}, {}, {}
        if fused:
            w = {ch: _mm(tinv[ch], jnp.concatenate([bd(cat(at_s, ch)), bd(zv[ch])], axis=1)) for ch in chs}
            yield
            for ch in chs:
                rhs2 = jnp.concatenate(
                    [jnp.concatenate([bd(w[ch][:, :LANES]), bd(w[ch][:, LANES:])], axis=1),
                     jnp.concatenate([jnp.zeros((n2, LANES), F32), bd(cat(v_s, ch))], axis=1)], axis=0)
                ry = _mm(jnp.concatenate([a_rb[ch], a_rk[ch]], axis=1), rhs2)
                r_hat[ch] = cat(rt_s, ch) + ry[:, :LANES]
                y_hat[ch] = ry[:, LANES:]
            yield
            for ch in chs:
                lhs_t = jnp.concatenate([w[ch], jnp.concatenate([zeros_c, cat(v_s, ch)], axis=1)], axis=0)
                mg = _mm(lhs_t.T, jnp.concatenate([cat(bb_s, ch), cat(kb_s, ch)], axis=0))
                m_mat[ch] = jnp.where(same_head, mg[:LANES], 0.0)
                g_mat[ch] = jnp.where(same_head, mg[LANES:], 0.0)
            yield
        else:
            a_hat = {ch: _mm(tinv[ch], bd(cat(at_s, ch))) for ch in chs}
            u_hat = {ch: _mm(tinv[ch], bd(zv[ch])) for ch in chs}
            yield
            for ch in chs:
                r_hat[ch] = cat(rt_s, ch) + _mm(a_rb[ch], bd(a_hat[ch]))
                y_hat[ch] = _mm(a_rb[ch], bd(u_hat[ch])) + _mm(a_rk[ch], bd(cat(v_s, ch)))
            yield
            for ch in chs:
                m_mat[ch] = jnp.where(same_head, _mm(a_hat[ch].T, cat(bb_s, ch)), 0.0)
                g_mat[ch] = jnp.where(same_head,
                                      _mm(u_hat[ch].T, cat(bb_s, ch)) + _mm(cat(v_s, ch).T, cat(kb_s, ch)), 0.0)
            yield
        for r0 in range(lo, hi, c_len):
            s_cur = s_state[r0 // tt]
            for p in range(4):
                ch = (r0, p)
                ls = slice(p * LANES, (p + 1) * LANES)
                e_last = et_s[r0:r0 + 1, ls]
                y_s[r0:r0 + c_len, ls] = y_hat[ch] + _mm_nt(r_hat[ch], s_cur[p])
                s_cur[p] = s_cur[p] * e_last + _mm(s_cur[p], m_mat[ch]) + g_mat[ch]
            yield

    def epilogue(lo, hi):
        for cs in halves:
            y = y_s[lo:hi, cs]
            mean = head_sum(y) * (1.0 / HEAD_DIM)
            yc = y - mean
            yield
            var = head_sum(yc * yc) * (1.0 / HEAD_DIM)
            yn = yc * lax.rsqrt(var + GN_EPS) * gn_g[:, cs] + gn_b[:, cs]
            res = (yn + bonus_s[lo:hi, cs]) * gate_s[lo:hi, cs]
            for r0 in range(lo, hi, tt) if hi - lo > tt else [lo]:
                r1 = min(r0 + tt, hi)
                out_ref[r0 // tt, r0 % tt:r0 % tt + (r1 - r0), cs] = res[r0 - lo:r1 - lo]
            yield

    subs = [(lo, lo + sub) for lo in range(0, n_rows, sub)]
    _interleave([prologue(*subs[0])])
    for n, sb in enumerate(subs):
        phases = [chains(*sb)]
        if n + 1 < len(subs):
            phases.append(prologue(*subs[n + 1]))
        if n > 0:
            phases.append(epilogue(*subs[n - 1]))
        _interleave(phases)
    _interleave([epilogue(*subs[-1])])

    for bi in range(nb):
        for p in range(4):
            s_scr[bi, p] = s_state[bi][p]

    @pl.when(i == n_i - 1)
    def _():
        for bi in range(nb):
            for p in range(4):
                s_pair = s_state[bi][p]
                sfin_ref[bi, 2 * p] = s_pair[:HEAD_DIM, :HEAD_DIM].astype(sfin_ref.dtype)
                sfin_ref[bi, 2 * p + 1] = s_pair[HEAD_DIM:, HEAD_DIM:].astype(sfin_ref.dtype)


def _rwkv(pr3, l, ls, s0, vecs, rk_flat, w2p, a2p, g2p, hsum, tt, c_len, nb):
    b, t, _ = pr3.shape
    n_rows = nb * tt
    sub = min(RWKV_SUB, n_rows)
    ri = jnp.arange(sub)[:, None]
    ci = jnp.arange(sub)[None, :]
    same = (ri // c_len) == (ci // c_len)
    ltri = (same & (ri >= ci)).astype(BF16)
    hw = hsum.shape[0]
    blk = lambda bi, i: (bi, i, 0)
    const = lambda bi, i: (0, 0)
    big = lambda: pltpu.VMEM((n_rows, R_WIDTH), F32)
    return pl.pallas_call(
        functools.partial(_rwkv_kernel, c_len=c_len, tt=tt, nb=nb, sub=sub),
        grid=(b // nb, t // tt),
        in_specs=[pl.BlockSpec((nb, tt, RWKV_PAD), blk),
                  pl.BlockSpec((None, nb, N_HEADS_R, HEAD_DIM, HEAD_DIM), lambda bi, i: (ls, bi, 0, 0, 0)),
                  _layer_block((6, R_WIDTH), l), _layer_block((1, R_WIDTH), l),
                  _layer_block((LANES, R_WIDTH), l), _layer_block((LANES, R_WIDTH), l),
                  _layer_block((2 * LANES, R_WIDTH), l), pl.BlockSpec((hw, hw), const),
                  pl.BlockSpec((sub, sub), const)],
        out_specs=[pl.BlockSpec((nb, tt, R_WIDTH), blk),
                   pl.BlockSpec((nb, N_HEADS_R, HEAD_DIM, HEAD_DIM), lambda bi, i: (bi, 0, 0, 0))],
        out_shape=[jax.ShapeDtypeStruct((b, t, R_WIDTH), F32),
                   jax.ShapeDtypeStruct((b, N_HEADS_R, HEAD_DIM, HEAD_DIM), s0.dtype)],
        scratch_shapes=[pltpu.VMEM((nb, 4, LANES, LANES), F32),
                        big(), big(), big(), big(), big(), big(), big(), big(), big(), big(), big()],
        compiler_params=_params(("parallel", "arbitrary")),
        name="rwkv_mix",
    )(pr3, s0, vecs, rk_flat, w2p, a2p, g2p, hsum, ltri)


POST_PARTS = 2


def _post_kernel(attn_ref, rw_ref, x_ref, g_ref, wout_ref, wq_ref, wo_ref, mk_ref, mv_ref, o_ref, *, nb, t):
    scale = MEM_HEAD_DIM ** -0.5
    cols = lambda hd: slice(hd * MEM_HEAD_DIM, (hd + 1) * MEM_HEAD_DIM)
    head_rows = lambda hd: pl.ds(hd, MEM_LEN, stride=MEM_HEADS)
    ones = jnp.ones((MEM_LEN, MEM_HEAD_DIM), F32)

    def part(lo, hi):
        rows = slice(lo, hi)
        m = _mm(attn_ref[rows, :], wout_ref[0:A_WIDTH, :]) + _mm(rw_ref[rows, :], wout_ref[A_WIDTH:, :])
        yield
        x1 = x_ref[rows, :] + _rms(m, g_ref[NORM_MIX_POST:NORM_MIX_POST + 1, :])
        hq = _rms(x1, g_ref[NORM_X_PRE:NORM_X_PRE + 1, :])
        yield
        q = _mm(hq, wq_ref[...])
        yield
        if nb == 1:
            jobs = [(0, lo, hi, hd) for hd in range(MEM_HEADS)]
        else:
            jobs = [(r0 // t, r0, r0 + t, hd) for r0 in range(lo, hi, t) for hd in range(MEM_HEADS)]
        scores = [_mm_nt(q[r0 - lo:r1 - lo, cols(hd)], mk_ref[bi, head_rows(hd), :]) * scale
                  for bi, r0, r1, hd in jobs]
        yield
        exps = [jnp.exp(s - jnp.max(s, axis=-1, keepdims=True)) for s in scores]
        yield
        outs = []
        for (bi, _, _, hd), e in zip(jobs, exps):
            o2 = _mm(e, jnp.concatenate([mv_ref[bi, head_rows(hd), :], ones], axis=1))
            outs.append(o2[:, :MEM_HEAD_DIM] / o2[:, MEM_HEAD_DIM:])
        blocks = [jnp.concatenate(outs[j:j + MEM_HEADS], axis=1) for j in range(0, len(outs), MEM_HEADS)]
        o = blocks[0] if len(blocks) == 1 else jnp.concatenate(blocks, axis=0)
        yield
        c = _mm(o, wo_ref[...])
        yield
        o_ref[rows, :] = x1 + _rms(c, g_ref[NORM_X_POST:NORM_X_POST + 1, :])

    n_rows = nb * t
    n_parts = POST_PARTS if (nb == 1 or nb % POST_PARTS == 0) and n_rows % (8 * POST_PARTS) == 0 else 1
    step = n_rows // n_parts
    gens = [part(j * step, (j + 1) * step) for j in range(n_parts)]
    for j, gen in enumerate(gens):
        for _ in range(n_parts - 1 - j):
            next(gen)
    _interleave(gens)


def _post(attn, rw, x2d, l, gains, w_out, w_mq, w_mo, lm, mk, mv, nb, t, tiles_per_batch):
    n = x2d.shape[0]
    tm = nb * t
    row = lambda i: (i, 0)
    if nb == 1:
        mmap = lambda i: (lm, i // tiles_per_batch, 0, 0)
    else:
        mmap = lambda i: (lm, i, 0, 0)
    mem_rows = MEM_LEN * MEM_HEADS
    return pl.pallas_call(
        functools.partial(_post_kernel, nb=nb, t=t),
        grid=(n // tm,),
        in_specs=[pl.BlockSpec((tm, A_WIDTH), row), pl.BlockSpec((tm, R_WIDTH), row),
                  pl.BlockSpec((tm, D_MODEL), row), _layer_block((8, D_MODEL), l),
                  _layer_block((D_MODEL, D_MODEL), l), _layer_block((D_MODEL, MEM_WIDTH), l),
                  _layer_block((MEM_WIDTH, D_MODEL), l),
                  pl.BlockSpec((None, nb, mem_rows, MEM_HEAD_DIM), mmap),
                  pl.BlockSpec((None, nb, mem_rows, MEM_HEAD_DIM), mmap)],
        out_specs=pl.BlockSpec((tm, D_MODEL), row),
        out_shape=jax.ShapeDtypeStruct((n, D_MODEL), F32),
        compiler_params=_params(("parallel",)),
        name="post_mix",
    )(attn, rw, x2d, gains, w_out, w_mq, w_mo, mk, mv)


FFN_CHUNK = 256


def _ffn_kernel(x_ref, g_ref, wgu_ref, wd_ref, o_ref):
    x = x_ref[...]
    h = _rms(x, g_ref[NORM_FFN_PRE:NORM_FFN_PRE + 1, :]).astype(BF16)
    acc = None
    for j in range(D_FF // FFN_CHUNK):
        cols = slice(j * FFN_CHUNK, (j + 1) * FFN_CHUNK)
        gate = jnp.dot(h, wgu_ref[:, cols], preferred_element_type=F32)
        up = jnp.dot(h, wgu_ref[:, D_FF + j * FFN_CHUNK:D_FF + (j + 1) * FFN_CHUNK], preferred_element_type=F32)
        act = (gate * jax.nn.sigmoid(gate)) * up
        part = _mm(act, wd_ref[cols, :])
        acc = part if acc is None else acc + part
    o_ref[...] = x + _rms(acc, g_ref[NORM_FFN_POST:NORM_FFN_POST + 1, :])


def _ffn(x2d, l, gains, w_gu, w_dn, tm):
    n = x2d.shape[0]
    row = lambda i: (i, 0)
    resident = dict(pipeline_mode=pl.Buffered(1))
    return pl.pallas_call(
        _ffn_kernel,
        grid=(n // tm,),
        in_specs=[pl.BlockSpec((tm, D_MODEL), row), _layer_block((8, D_MODEL), l),
                  pl.BlockSpec((None, D_MODEL, 2 * D_FF), lambda i: (l, 0, 0), **resident),
                  pl.BlockSpec((None, D_FF, D_MODEL), lambda i: (l, 0, 0), **resident)],
        out_specs=pl.BlockSpec((tm, D_MODEL), row),
        out_shape=jax.ShapeDtypeStruct((n, D_MODEL), F32),
        compiler_params=_params(("parallel",)),
        name="ffn",
    )(x2d, gains, w_gu, w_dn)


def _memkv_kernel(x_ref, g_ref, w_ref, k_ref, v_ref):
    kv = _mm(_rms(x_ref[...], g_ref[NORM_MEM:NORM_MEM + 1, :]), w_ref[...])
    for hd in range(MEM_HEADS):
        rows = pl.ds(hd, MEM_LEN, stride=MEM_HEADS)
        k_ref[0, rows, :] = kv[:, hd * MEM_HEAD_DIM:(hd + 1) * MEM_HEAD_DIM]
        v_ref[0, rows, :] = kv[:, MEM_WIDTH + hd * MEM_HEAD_DIM:MEM_WIDTH + (hd + 1) * MEM_HEAD_DIM]


def _memkv(mem2d, l, gains, w_kv):
    n = mem2d.shape[0]
    nbatch = n // MEM_LEN
    mem_rows = MEM_LEN * MEM_HEADS
    out = jax.ShapeDtypeStruct((nbatch, mem_rows, MEM_HEAD_DIM), F32)
    ospec = pl.BlockSpec((1, mem_rows, MEM_HEAD_DIM), lambda i: (i, 0, 0))
    return pl.pallas_call(
        _memkv_kernel,
        grid=(nbatch,),
        in_specs=[pl.BlockSpec((MEM_LEN, D_MODEL), lambda i: (i, 0)), _layer_block((8, D_MODEL), l),
                  _layer_block((D_MODEL, 2 * MEM_WIDTH), l)],
        out_specs=[ospec, ospec],
        out_shape=[out, out],
        compiler_params=_params(("parallel",)),
        name="mem_kv",
    )(mem2d, gains, w_kv)


def _rope_tables(pos):
    half = HEAD_DIM // 2
    inv = ROPE_THETA ** (-jnp.arange(half, dtype=F32) / half)
    ang = pos.astype(F32)[:, None] * inv[None, :]
    cos = jnp.cos(ang)
    sin = jnp.sin(ang)
    cos_t = jnp.tile(cos, (1, LANES // half))
    sin_t = jnp.tile(jnp.concatenate([-sin, sin], axis=1), (1, LANES // HEAD_DIM))
    return cos_t, sin_t


def _stacked_params(norm_gains, w_in, attn_sink, shift_mu, rwkv_vecs, rwkv_rk, rwkv_w2, rwkv_a2, rwkv_g2,
                    w_out, w_mem_q, w_mem_kv, w_mem_o, w_gate_up, w_down):
    depth = w_in.shape[0]
    return dict(
        gains=jnp.pad(norm_gains, ((0, 0), (0, 8 - norm_gains.shape[1]), (0, 0))),
        w_in=jnp.pad(w_in, ((0, 0), (0, 0), (0, IN_COLS_PAD - w_in.shape[2]))).astype(BF16),
        sink=jnp.broadcast_to(attn_sink[:, :, None], attn_sink.shape + (LANES,)).astype(F32),
        mu=jnp.pad(shift_mu, ((0, 0), (0, RWKV_PAD - RWKV_PROJ)))[:, None, :],
        vecs=rwkv_vecs, rk=rwkv_rk.reshape(depth, 1, R_WIDTH),
        w2=jnp.pad(rwkv_w2, ((0, 0), (0, LANES - DECAY_LORA), (0, 0))).astype(BF16),
        a2=jnp.pad(rwkv_a2, ((0, 0), (DECAY_LORA, LANES - DECAY_LORA - AAA_LORA), (0, 0))).astype(BF16),
        g2=jnp.pad(rwkv_g2, ((0, 0), (0, 2 * LANES - GATE_LORA), (0, 0))).astype(BF16),
        w_out=w_out.astype(BF16), w_mq=w_mem_q.astype(BF16), w_mkv=w_mem_kv.astype(BF16),
        w_mo=w_mem_o.astype(BF16), w_gu=w_gate_up.astype(BF16), w_dn=w_down.astype(BF16))


def _head_sum_matrix():
    i = jnp.arange(2 * LANES)
    return ((i[:, None] // HEAD_DIM) == (i[None, :] // HEAD_DIM)).astype(BF16)


def _layer(x2d, l, b, t, tabs, lm, mk, mv, swa_cache, ls, s0, shift0, sp, hsum, in_tm, rw_tt, post_nb, post_t,
           ffn_tm):
    q, k, v, pr, shift_new = _in_proj(x2d, l, sp["gains"], sp["w_in"], tabs[0], tabs[1], sp["mu"], ls, shift0, t,
                                      in_tm)
    if swa_cache is None:
        attn = _swa_prompt(q, k, v, l, sp["sink"], b, t)
    else:
        attn = _swa_sample(q, k, v, l, swa_cache[0], swa_cache[1], sp["sink"], b, t)
    pr3 = pr.reshape(b, t, RWKV_PAD)
    rw, s_fin = _rwkv(pr3, l, ls, s0, sp["vecs"], sp["rk"], sp["w2"], sp["a2"], sp["g2"], hsum,
                      rw_tt[0], min(CHUNK, t), rw_tt[1])
    x2 = _post(attn, rw.reshape(b * t, R_WIDTH), x2d, l, sp["gains"], sp["w_out"], sp["w_mq"], sp["w_mo"],
               lm, mk, mv, post_nb, post_t, t // post_t)
    x3 = _ffn(x2, l, sp["gains"], sp["w_gu"], sp["w_dn"], ffn_tm)
    return x3, k, v, s_fin, shift_new[:, :, :RWKV_PROJ]


def kernel(x_prompt, mem_prompt, x_sample, cache_swa_k, cache_swa_v, cache_mem_k, cache_mem_v, state_rwkv,
           state_shift, norm_gains, w_in, attn_sink, shift_mu, rwkv_vecs, rwkv_rk, rwkv_w2, rwkv_a2, rwkv_g2,
           w_out, w_mem_q, w_mem_kv, w_mem_o, w_gate_up, w_down):
    b, t, _ = x_prompt.shape
    bd, tn, _ = x_sample.shape
    depth = w_in.shape[0]
    m_len = mem_prompt.shape[1]
    cache_len = cache_swa_k.shape[2]
    tabs_p = _rope_tables(jnp.arange(t, dtype=jnp.int32))
    cs, sn = _rope_tables(PAST_LEN + jnp.arange(tn, dtype=jnp.int32))
    tabs_s = (jnp.tile(cs, (bd, 1)), jnp.tile(sn, (bd, 1)))
    hsum = _head_sum_matrix()
    sp = _stacked_params(norm_gains, w_in, attn_sink, shift_mu, rwkv_vecs, rwkv_rk, rwkv_w2, rwkv_a2, rwkv_g2,
                         w_out, w_mem_q, w_mem_kv, w_mem_o, w_gate_up, w_down)
    s0_p = jnp.zeros((1, b, N_HEADS_R, HEAD_DIM, HEAD_DIM), F32)
    sh0_p = jnp.zeros((1, b, 1, RWKV_PAD), F32)
    sh0_s = jnp.pad(state_shift, ((0, 0), (0, 0), (0, 0), (0, RWKV_PAD - RWKV_PROJ)))
    swa_cache = (cache_swa_k.reshape(depth, bd, cache_len, KV_WIDTH_A),
                 cache_swa_v.reshape(depth, bd, cache_len, KV_WIDTH_A))
    mem_rows = m_len * MEM_HEADS
    cmk = cache_mem_k.reshape(depth, bd, mem_rows, MEM_HEAD_DIM)
    cmv = cache_mem_v.reshape(depth, bd, mem_rows, MEM_HEAD_DIM)
    xp = x_prompt.reshape(b * t, D_MODEL)
    xs = x_sample.reshape(bd * tn, D_MODEL)
    mem2d = mem_prompt.reshape(b * m_len, D_MODEL)
    in_tm_p = min(512, t)
    rw_tt_p = (min(512, t), 1)
    rw_tt_s = (tn, 4 if bd % 4 == 0 else 1)
    post_t_p = min(1024, t)
    ffn_tm_p = min(512, b * t)
    keep = t - min(WINDOW, t)
    pk, pv, pmk, pmv, pS, psh = [], [], [], [], [], []
    sk, sv, sS, ssh = [], [], [], []
    for l in range(depth):
        mk_l, mv_l = _memkv(mem2d, l, sp["gains"], sp["w_mkv"])
        xp, k_l, v_l, s_l, sh_l = _layer(xp, l, b, t, tabs_p, 0, mk_l[None], mv_l[None], None, 0, s0_p, sh0_p, sp,
                                         hsum, in_tm_p, rw_tt_p, 1, post_t_p, ffn_tm_p)
        pk.append(k_l.reshape(b, t, KV_WIDTH_A)[:, keep:].reshape(b, t - keep, N_KV_A, HEAD_DIM))
        pv.append(v_l.reshape(b, t, KV_WIDTH_A)[:, keep:].reshape(b, t - keep, N_KV_A, HEAD_DIM))
        pmk.append(mk_l.reshape(b, m_len, MEM_HEADS, MEM_HEAD_DIM))
        pmv.append(mv_l.reshape(b, m_len, MEM_HEADS, MEM_HEAD_DIM))
        pS.append(s_l)
        psh.append(sh_l)

        xs, k2, v2, s2, sh2 = _layer(xs, l, bd, tn, tabs_s, l, cmk, cmv, swa_cache, l, state_rwkv, sh0_s, sp,
                                     hsum, bd * tn, rw_tt_s, bd, tn, bd * tn)
        sk.append(k2.reshape(bd, tn, N_KV_A, HEAD_DIM))
        sv.append(v2.reshape(bd, tn, N_KV_A, HEAD_DIM))
        sS.append(s2)
        ssh.append(sh2)
    return (xp.reshape(b, t, D_MODEL), xs.reshape(bd, tn, D_MODEL), jnp.stack(pk), jnp.stack(pv),
            jnp.stack(pmk), jnp.stack(pmv), jnp.stack(pS), jnp.stack(psh),
            jnp.stack(sk), jnp.stack(sv), jnp.stack(sS), jnp.stack(ssh))
```

```python
import functools
import math

import jax
import jax.numpy as jnp
from jax import lax
from jax.experimental import pallas as pl
from jax.experimental.pallas import tpu as pltpu

F32 = jnp.float32
BF16 = jnp.bfloat16

D_MODEL = 1024
HEAD_DIM = 64
CHUNK = 64
A_WIDTH = 512
KV_WIDTH_A = 128
N_KV_A = 2
WINDOW = 128
PAST_LEN = 4096
ROPE_THETA = 10000.0
R_WIDTH = 512
N_HEADS_R = 8
DECAY_LORA = 64
AAA_LORA = 64
GATE_LORA = 160
RWKV_PROJ = 3 * R_WIDTH + DECAY_LORA + AAA_LORA + GATE_LORA
RWKV_PAD = 1920
IN_COLS_PAD = A_WIDTH + 2 * KV_WIDTH_A + RWKV_PAD
GN_EPS = 6.4e-4
MEM_LEN = 256
MEM_HEADS = 4
MEM_HEAD_DIM = 128
MEM_WIDTH = 512
D_FF = 2816
RMS_EPS = 1e-6
NEG_INF = -1e30
NORM_MIX_PRE, NORM_MIX_POST, NORM_X_PRE, NORM_X_POST, NORM_MEM, NORM_FFN_PRE, NORM_FFN_POST = range(7)
EXP_M05 = math.exp(-0.5)

LANES = 128
VMEM_LIMIT = 56 * 1024 * 1024


def _params(sem):
    return pltpu.CompilerParams(dimension_semantics=sem, vmem_limit_bytes=VMEM_LIMIT)


def _rms(x, g):
    ms = jnp.mean(x * x, axis=-1, keepdims=True)
    return x * lax.rsqrt(ms + RMS_EPS) * g


def _mm(a, b):
    return jnp.dot(a.astype(BF16), b.astype(BF16), preferred_element_type=F32)


def _mm_nt(a, b):
    return lax.dot_general(a.astype(BF16), b.astype(BF16), (((1,), (1,)), ((), ())),
                           preferred_element_type=F32)


def _split(x):
    hi = x.astype(BF16)
    lo = (x - hi.astype(F32)).astype(BF16)
    return hi, lo


def _mm_lsplit(a_exact_bf16, x):
    hi, lo = _split(x)
    return (jnp.dot(a_exact_bf16, hi, preferred_element_type=F32)
            + jnp.dot(a_exact_bf16, lo, preferred_element_type=F32))


def _in_kernel(x_ref, g_ref, w_ref, cos_ref, sin_ref, mu_ref, sh0_ref, q_ref, k_ref, v_ref, pr_ref, shl_ref,
               carry_scr, *, tm, t_seq):
    h = _rms(x_ref[...], g_ref[NORM_MIX_PRE:NORM_MIX_PRE + 1, :]).astype(BF16)
    p = jnp.dot(h, w_ref[...], preferred_element_type=F32)
    cos = cos_ref[...]
    sin = sin_ref[...]
    lane = lax.broadcasted_iota(jnp.int32, cos.shape, 1)
    first_half = (lane & (HEAD_DIM // 2)) == 0

    def rope(xc):
        sw = jnp.where(first_half, pltpu.roll(xc, LANES - HEAD_DIM // 2, 1), pltpu.roll(xc, HEAD_DIM // 2, 1))
        return xc * cos + sw * sin

    for j in range(A_WIDTH // LANES):
        q_ref[:, j * LANES:(j + 1) * LANES] = rope(p[:, j * LANES:(j + 1) * LANES])
    k_ref[...] = rope(p[:, A_WIDTH:A_WIDTH + KV_WIDTH_A])
    v_ref[...] = p[:, A_WIDTH + KV_WIDTH_A:A_WIDTH + 2 * KV_WIDTH_A]
    raw = p[:, A_WIDTH + 2 * KV_WIDTH_A:]
    rowi = lax.broadcasted_iota(jnp.int32, (tm, 1), 0)
    prev = pltpu.roll(raw, 1, 0)
    if tm <= t_seq:
        first = (pl.program_id(0) % (t_seq // tm)) == 0
        prev = jnp.where(rowi == 0, jnp.where(first, sh0_ref[0], carry_scr[...]), prev)
        carry_scr[...] = raw[tm - 1:tm, :]
        shl_ref[0] = raw[tm - 1:tm, :]
    else:
        for s in range(tm // t_seq):
            prev = jnp.where(rowi == s * t_seq, sh0_ref[s], prev)
            shl_ref[s] = raw[(s + 1) * t_seq - 1:(s + 1) * t_seq, :]
    pr_ref[...] = raw + (prev - raw) * mu_ref[...]


def _layer_block(shape, l):
    nd = len(shape)
    return pl.BlockSpec((None,) + tuple(shape), lambda *_: (l,) + (0,) * nd)


def _in_proj(x2d, l, gains, w_in_p, cos_t, sin_t, mu_p, ls, shift0, t_seq, tm):
    n = x2d.shape[0]
    n_seq = n // t_seq
    tab_blocks = cos_t.shape[0] // tm
    row = lambda i: (i, 0)
    tab = lambda i: (i % tab_blocks, 0)
    if tm <= t_seq:
        seqs, seq_of = 1, lambda i: i // (t_seq // tm)
    else:
        seqs, seq_of = tm // t_seq, lambda i: i
    return pl.pallas_call(
        functools.partial(_in_kernel, tm=tm, t_seq=t_seq),
        grid=(n // tm,),
        in_specs=[pl.BlockSpec((tm, D_MODEL), row), _layer_block((8, D_MODEL), l),
                  _layer_block((D_MODEL, IN_COLS_PAD), l),
                  pl.BlockSpec((tm, LANES), tab), pl.BlockSpec((tm, LANES), tab),
                  _layer_block((1, RWKV_PAD), l),
                  pl.BlockSpec((None, seqs, 1, RWKV_PAD), lambda i: (ls, seq_of(i), 0, 0))],
        out_specs=[pl.BlockSpec((tm, A_WIDTH), row), pl.BlockSpec((tm, KV_WIDTH_A), row),
                   pl.BlockSpec((tm, KV_WIDTH_A), row), pl.BlockSpec((tm, RWKV_PAD), row),
                   pl.BlockSpec((seqs, 1, RWKV_PAD), lambda i: (seq_of(i), 0, 0))],
        out_shape=[jax.ShapeDtypeStruct((n, A_WIDTH), F32), jax.ShapeDtypeStruct((n, KV_WIDTH_A), F32),
                   jax.ShapeDtypeStruct((n, KV_WIDTH_A), F32), jax.ShapeDtypeStruct((n, RWKV_PAD), F32),
                   jax.ShapeDtypeStruct((n_seq, 1, RWKV_PAD), F32)],
        scratch_shapes=[pltpu.VMEM((1, RWKV_PAD), F32)],
        compiler_params=_params(("arbitrary",)),
        name="in_proj",
    )(x2d, gains, w_in_p, cos_t, sin_t, mu_p, shift0)


def _sink_attend(jobs, sink_ref, t):
    nk = jobs[0][1].shape[0]
    nkp = 2 * LANES
    lane = lax.broadcasted_iota(jnp.int32, (t, LANES), 1)
    m0 = lane < HEAD_DIM
    coli = lax.broadcasted_iota(jnp.int32, (1, nkp), 1)
    fills = []
    for kv in range(N_KV_A):
        blocks = []
        for g in range(4):
            sg = sink_ref[4 * kv + g:4 * kv + g + 1, :]
            blocks.append(jnp.broadcast_to(jnp.where(coli == nk, jnp.concatenate([sg, sg], axis=1), NEG_INF),
                                           (t, nkp)))
        fills.append(jnp.concatenate(blocks, axis=0))
    k_tail = jnp.zeros((nkp - nk, LANES), F32)
    tail_row = lax.broadcasted_iota(jnp.int32, (nkp - nk, nkp), 0)
    tail_col = lax.broadcasted_iota(jnp.int32, (nkp - nk, nkp), 1)
    v_tail = jnp.where((tail_row == 0) & (tail_col >= LANES), 1.0, 0.0).astype(F32)
    ones = jnp.ones((nk, LANES), F32)
    scores = []
    for q_rows, kdup, _, kv, valid in jobs:
        parts = []
        for p in (2 * kv, 2 * kv + 1):
            qp = q_rows[:, p * LANES:(p + 1) * LANES]
            parts.append(jnp.where(m0, qp, 0.0))
            parts.append(jnp.where(m0, 0.0, qp))
        lhs = jnp.concatenate(parts, axis=0)
        s = _mm_nt(lhs, jnp.concatenate([kdup, k_tail], axis=0))
        keep = coli < nk
        if valid is not None:
            keep = keep & valid
        scores.append(jnp.where(keep, s, fills[kv]))
    exps = [jnp.exp(s - jnp.max(s, axis=-1, keepdims=True)) for s in scores]
    outs = []
    for (_, _, vdup, _, _), e in zip(jobs, exps):
        v2 = jnp.concatenate([jnp.concatenate([vdup, ones], axis=1), v_tail], axis=0)
        o2 = _mm(e, v2)
        o = o2[:, :LANES] / o2[:, LANES:]
        outs.append([jnp.where(m0, o[(2 * pi) * t:(2 * pi + 1) * t], o[(2 * pi + 1) * t:(2 * pi + 2) * t])
                     for pi in range(2)])
    return outs


SWA_SCALE = HEAD_DIM ** -0.5


def _dup_heads(x):
    lane = lax.broadcasted_iota(jnp.int32, x.shape, 1)
    m0 = lane < HEAD_DIM
    xs = pltpu.roll(x, HEAD_DIM, 1)
    return [jnp.where(m0, x, xs), jnp.where(m0, xs, x)]


SWA_TQ = 512


def _swa_prompt_kernel(q_ref, kp_ref, kc_ref, vp_ref, vc_ref, sink_ref, o_ref):
    i = pl.program_id(1)
    k = jnp.concatenate([kp_ref[...], kc_ref[...]], axis=0)
    v = jnp.concatenate([vp_ref[...], vc_ref[...]], axis=0)
    kd = _dup_heads(k * SWA_SCALE)
    vd = _dup_heads(v)
    nk = 3 * CHUNK
    slot = lax.broadcasted_iota(jnp.int32, (1, 2 * LANES), 1) // CHUNK
    jobs = []
    for j in range(SWA_TQ // CHUNK):
        qj = q_ref[j * CHUNK:(j + 1) * CHUNK, :]
        valid = (slot + (i * (SWA_TQ // CHUNK) + j - 2)) >= 0
        for kv in range(N_KV_A):
            jobs.append((qj, kd[kv][j * CHUNK:j * CHUNK + nk], vd[kv][j * CHUNK:j * CHUNK + nk], kv, valid))
    outs = _sink_attend(jobs, sink_ref, CHUNK)
    for n, out in enumerate(outs):
        j, kv = divmod(n, N_KV_A)
        for pi in range(2):
            p = 2 * kv + pi
            o_ref[j * CHUNK:(j + 1) * CHUNK, p * LANES:(p + 1) * LANES] = out[pi]


def _swa_prompt(q, k, v, l, sink_b, b, t):
    n = b * t
    nq = t // SWA_TQ
    qmap = lambda bi, i: (bi * nq + i, 0)
    pmap = lambda bi, i: (jnp.maximum(bi * (t // WINDOW) + (SWA_TQ // WINDOW) * i - 1, 0), 0)
    return pl.pallas_call(
        _swa_prompt_kernel,
        grid=(b, nq),
        in_specs=[pl.BlockSpec((SWA_TQ, A_WIDTH), qmap),
                  pl.BlockSpec((WINDOW, KV_WIDTH_A), pmap), pl.BlockSpec((SWA_TQ, KV_WIDTH_A), qmap),
                  pl.BlockSpec((WINDOW, KV_WIDTH_A), pmap), pl.BlockSpec((SWA_TQ, KV_WIDTH_A), qmap),
                  _layer_block((8, LANES), l)],
        out_specs=pl.BlockSpec((SWA_TQ, A_WIDTH), qmap),
        out_shape=jax.ShapeDtypeStruct((n, A_WIDTH), F32),
        compiler_params=_params(("parallel", "parallel")),
        name="swa_prompt",
    )(q, k, k, v, v, sink_b)


def _swa_sample_kernel(q_ref, kc_ref, kn_ref, vc_ref, vn_ref, sink_ref, o_ref, *, t, nb):
    jobs = []
    for bi in range(nb):
        rows = slice(bi * t, (bi + 1) * t)
        kd = _dup_heads(jnp.concatenate([kc_ref[bi], kn_ref[rows, :]], axis=0) * SWA_SCALE)
        vd = _dup_heads(jnp.concatenate([vc_ref[bi], vn_ref[rows, :]], axis=0))
        for kv in range(N_KV_A):
            jobs.append((q_ref[rows, :], kd[kv], vd[kv], kv, None))
    outs = _sink_attend(jobs, sink_ref, t)
    for n, out in enumerate(outs):
        bi, kv = divmod(n, N_KV_A)
        for pi in range(2):
            p = 2 * kv + pi
            o_ref[bi * t:(bi + 1) * t, p * LANES:(p + 1) * LANES] = out[pi]


def _swa_sample(q, k, v, l, kc, vc, sink_b, b, t):
    n = b * t
    cache = kc.shape[2]
    nb = 4 if b % 4 == 0 else 1
    row = lambda bi: (bi, 0)
    cmap = lambda bi: (l, bi, 0, 0)
    return pl.pallas_call(
        functools.partial(_swa_sample_kernel, t=t, nb=nb),
        grid=(b // nb,),
        in_specs=[pl.BlockSpec((nb * t, A_WIDTH), row),
                  pl.BlockSpec((None, nb, cache, KV_WIDTH_A), cmap), pl.BlockSpec((nb * t, KV_WIDTH_A), row),
                  pl.BlockSpec((None, nb, cache, KV_WIDTH_A), cmap), pl.BlockSpec((nb * t, KV_WIDTH_A), row),
                  _layer_block((8, LANES), l)],
        out_specs=pl.BlockSpec((nb * t, A_WIDTH), row),
        out_shape=jax.ShapeDtypeStruct((n, A_WIDTH), F32),
        compiler_params=_params(("parallel",)),
        name="swa_sample",
    )(q, kc, k, vc, v, sink_b)


RWKV_SUB = 256


def _interleave(gens):
    gens = list(gens)
    while gens:
        for g in list(gens):
            try:
                next(g)
            except StopIteration:
                gens.remove(g)


def _rwkv_kernel(pr_ref, s0_ref, vec_ref, rk_ref, w2_ref, a2_ref, g2_ref, hsum_ref,
                 ltri_ref, out_ref, sfin_ref,
                 s_scr, rt_s, at_s, bt_s, kt_s, bb_s, kb_s, v_s, et_s, y_s, bonus_s, gate_s,
                 *, c_len, tt, nb, sub):
    i = pl.program_id(1)
    n_i = pl.num_programs(1)
    n_rows = nb * tt

    @pl.when(i == 0)
    def _():
        zero = jnp.zeros((HEAD_DIM, HEAD_DIM), F32)
        for bi in range(nb):
            for p in range(4):
                s_scr[bi, p] = jnp.concatenate(
                    [jnp.concatenate([s0_ref[bi, 2 * p].astype(F32), zero], axis=1),
                     jnp.concatenate([zero, s0_ref[bi, 2 * p + 1].astype(F32)], axis=1)], axis=0)

    w0 = vec_ref[0:1, :]
    a0 = vec_ref[1:2, :]
    k_k = vec_ref[2:3, :]
    k_a = vec_ref[3:4, :]
    gn_g = vec_ref[4:5, :]
    gn_b = vec_ref[5:6, :]
    hsum = hsum_ref[...]
    hw = hsum.shape[0]
    halves = [slice(j * hw, (j + 1) * hw) for j in range(R_WIDTH // hw)]

    def head_sum(x):
        return _mm(x, hsum)

    n2 = 2 * c_len
    ti = lax.broadcasted_iota(jnp.int32, (c_len, n2), 0)
    lane2 = lax.broadcasted_iota(jnp.int32, (c_len, n2), 1)
    si = lane2 & (c_len - 1)
    first_c = lane2 < c_len
    strict = ti > si
    incl = ti >= si
    eye = jnp.where(ti == si, 1.0, 0.0).astype(F32)
    masks = []
    half = 1
    while half < c_len:
        blk = 2 * half
        masks.append(((ti & ~(blk - 1)) == (si & ~(blk - 1))) & ((ti & half) != 0) & ((si & half) == 0))
        half = blk
    m0 = lax.broadcasted_iota(jnp.int32, (c_len, LANES), 1) < HEAD_DIM
    rl = lax.broadcasted_iota(jnp.int32, (LANES, LANES), 0)
    cl = lax.broadcasted_iota(jnp.int32, (LANES, LANES), 1)
    same_head = (rl >= HEAD_DIM) == (cl >= HEAD_DIM)
    fused = n2 == LANES
    zeros_c = jnp.zeros((c_len, LANES), F32)

    def bd(x):
        return jnp.concatenate([jnp.where(m0, x, 0.0), jnp.where(m0, 0.0, x)], axis=0)

    def bd_t(x):
        return jnp.concatenate([jnp.where(first_c, x, 0.0), jnp.where(first_c, 0.0, x)], axis=0)

    def prologue(lo, hi):
        n = hi - lo
        bi0 = lo // tt

        def shifted(a, b):
            if hi - lo <= tt:
                return pr_ref[bi0, lo - bi0 * tt:hi - bi0 * tt, a:b]
            return pr_ref[bi0:hi // tt, :, a:b].reshape(n, b - a)

        xwa = shifted(3 * R_WIDTH, 3 * R_WIDTH + LANES)
        xg = shifted(3 * R_WIDTH + LANES, RWKV_PAD)
        th = jnp.tanh(xwa)
        sg = jax.nn.sigmoid(xg)
        ltri = ltri_ref[...]
        yield
        for cs in halves:
            r = shifted(cs.start, cs.stop)
            k = shifted(R_WIDTH + cs.start, R_WIDTH + cs.stop)
            v = shifted(2 * R_WIDTH + cs.start, 2 * R_WIDTH + cs.stop)
            yield
            z = w0[:, cs] + _mm(th, w2_ref[:, cs])
            wlog = -EXP_M05 * jax.nn.sigmoid(z)
            a = jax.nn.sigmoid(a0[:, cs] + _mm(xwa, a2_ref[:, cs]))
            gate_s[lo:hi, cs] = _mm(sg, g2_ref[:, cs])
            yield
            kk = k * k_k[:, cs]
            kk = kk * lax.rsqrt(jnp.maximum(head_sum(kk * kk), 1e-24))
            k_f = k * (1.0 + (a - 1.0) * k_a[:, cs])
            bvec = kk * a
            bonus_s[lo:hi, cs] = head_sum(r * k_f * rk_ref[:, cs]) * v
            yield
            cum = _mm_lsplit(ltri, wlog)
            tot = jnp.concatenate([jnp.broadcast_to(cum[r1 - 1:r1, :], (c_len, hw))
                                   for r1 in range(c_len, n + 1, c_len)], axis=0)
            e_in = jnp.exp(cum)
            e_inv = jnp.exp(-cum)
            e_end = jnp.exp(tot - cum)
            rt_s[lo:hi, cs] = r * e_in
            at_s[lo:hi, cs] = -kk * jnp.exp(cum - wlog)
            yield
            bt_s[lo:hi, cs] = bvec * e_inv
            kt_s[lo:hi, cs] = k_f * e_inv
            bb_s[lo:hi, cs] = bvec * e_end
            kb_s[lo:hi, cs] = k_f * e_end
            v_s[lo:hi, cs] = v
            et_s[lo:hi, cs] = jnp.exp(tot)
            yield

    s_state = [[s_scr[bi, p] for p in range(4)] for bi in range(nb)]

    def chains(lo, hi):
        chs = [(r0, p) for r0 in range(lo, hi, c_len) for p in range(4)]

        def cat(ref, ch):
            return ref[ch[0]:ch[0] + c_len, ch[1] * LANES:(ch[1] + 1) * LANES]

        a_ab, a_ak, a_rb, a_rk = {}, {}, {}, {}
        for ch in chs:
            at, rt, rb, rkt = cat(at_s, ch), cat(rt_s, ch), bd(cat(bt_s, ch)), bd(cat(kt_s, ch))
            if fused:
                amat = _mm_nt(jnp.concatenate([at, rt], axis=0), jnp.concatenate([rb, rkt], axis=0))
                q_ab, q_ak = amat[:c_len, :n2], amat[:c_len, n2:]
                q_rb, q_rk = amat[c_len:, :n2], amat[c_len:, n2:]
            else:
                q_ab, q_ak, q_rb, q_rk = _mm_nt(at, rb), _mm_nt(at, rkt), _mm_nt(rt, rb), _mm_nt(rt, rkt)
            a_ab[ch] = jnp.where(strict, q_ab, 0.0)
            a_ak[ch] = jnp.where(strict, q_ak, 0.0)
            a_rb[ch] = jnp.where(incl, q_rb, 0.0)
            a_rk[ch] = jnp.where(incl, q_rk, 0.0)
        yield
        tinv = {ch: eye + jnp.where(masks[0], a_ab[ch], 0.0) for ch in chs}
        for m in masks[1:]:
            pe = {ch: _mm(tinv[ch], bd_t(jnp.where(m, a_ab[ch], 0.0))) for ch in chs}
            yield
            tinv = {ch: tinv[ch] + _mm(pe[ch], bd_t(tinv[ch])) for ch in chs}
            yield
        zv = {ch: _mm(a_ak[ch], bd(cat(v_s, ch))) for ch in chs}
        yield
        r_hat, y_hat, m_mat, g_mat = {}, {}, {}, {}
        if fused:
            w = {ch: _mm(tinv[ch], jnp.concatenate([bd(cat(at_s, ch)), bd(zv[ch])], axis=1)) for ch in chs}
            yield
            for ch in chs:
                rhs2 = jnp.concatenate(
                    [jnp.concatenate([bd(w[ch][:, :LANES]), bd(w[ch][:, LANES:])], axis=1),
                     jnp.concatenate([jnp.zeros((n2, LANES), F32), bd(cat(v_s, ch))], axis=1)], axis=0)
                ry = _mm(jnp.concatenate([a_rb[ch], a_rk[ch]], axis=1), rhs2)
                r_hat[ch] = cat(rt_s, ch) + ry[:, :LANES]
                y_hat[ch] = ry[:, LANES:]
            yield
            for ch in chs:
                lhs_t = jnp.concatenate([w[ch], jnp.concatenate([zeros_c, cat(v_s, ch)], axis=1)], axis=0)
                mg = _mm(lhs_t.T, jnp.concatenate([cat(bb_s, ch), cat(kb_s, ch)], axis=0))
                m_mat[ch] = jnp.where(same_head, mg[:LANES], 0.0)
                g_mat[ch] = jnp.where(same_head, mg[LANES:], 0.0)
            yield
        else:
            a_hat = {ch: _mm(tinv[ch], bd(cat(at_s, ch))) for ch in chs}
            u_hat = {ch: _mm(tinv[ch], bd(zv[ch])) for ch in chs}
            yield
            for ch in chs:
                r_hat[ch] = cat(rt_s, ch) + _mm(a_rb[ch], bd(a_hat[ch]))
                y_hat[ch] = _mm(a_rb[ch], bd(u_hat[ch])) + _mm(a_rk[ch], bd(cat(v_s, ch)))
            yield
            for ch in chs:
                m_mat[ch] = jnp.where(same_head, _mm(a_hat[ch].T, cat(bb_s, ch)), 0.0)
                g_mat[ch] = jnp.where(same_head,
                                      _mm(u_hat[ch].T, cat(bb_s, ch)) + _mm(cat(v_s, ch).T, cat(kb_s, ch)), 0.0)
            yield
        for r0 in range(lo, hi, c_len):
            s_cur = s_state[r0 // tt]
            for p in range(4):
                ch = (r0, p)
                ls = slice(p * LANES, (p + 1) * LANES)
                e_last = et_s[r0:r0 + 1, ls]
                y_s[r0:r0 + c_len, ls] = y_hat[ch] + _mm_nt(r_hat[ch], s_cur[p])
                s_cur[p] = s_cur[p] * e_last + _mm(s_cur[p], m_mat[ch]) + g_mat[ch]
            yield

    def epilogue(lo, hi):
        for cs in halves:
            y = y_s[lo:hi, cs]
            mean = head_sum(y) * (1.0 / HEAD_DIM)
            yc = y - mean
            yield
            var = head_sum(yc * yc) * (1.0 / HEAD_DIM)
            yn = yc * lax.rsqrt(var + GN_EPS) * gn_g[:, cs] + gn_b[:, cs]
            res = (yn + bonus_s[lo:hi, cs]) * gate_s[lo:hi, cs]
            for r0 in range(lo, hi, tt) if hi - lo > tt else [lo]:
                r1 = min(r0 + tt, hi)
                out_ref[r0 // tt, r0 % tt:r0 % tt + (r1 - r0), cs] = res[r0 - lo:r1 - lo]
            yield

    subs = [(lo, lo + sub) for lo in range(0, n_rows, sub)]
    _interleave([prologue(*subs[0])])
    for n, sb in enumerate(subs):
        phases = [chains(*sb)]
        if n + 1 < len(subs):
            phases.append(prologue(*subs[n + 1]))
        if n > 0:
            phases.append(epilogue(*subs[n - 1]))
        _interleave(phases)
    _interleave([epilogue(*subs[-1])])

    for bi in range(nb):
        for p in range(4):
            s_scr[bi, p] = s_state[bi][p]

    @pl.when(i == n_i - 1)
    def _():
        for bi in range(nb):
            for p in range(4):
                s_pair = s_state[bi][p]
                sfin_ref[bi, 2 * p] = s_pair[:HEAD_DIM, :HEAD_DIM].astype(sfin_ref.dtype)
                sfin_ref[bi, 2 * p + 1] = s_pair[HEAD_DIM:, HEAD_DIM:].astype(sfin_ref.dtype)


def _rwkv(pr3, l, ls, s0, vecs, rk_flat, w2p, a2p, g2p, hsum, tt, c_len, nb):
    b, t, _ = pr3.shape
    n_rows = nb * tt
    sub = min(RWKV_SUB, n_rows)
    ri = jnp.arange(sub)[:, None]
    ci = jnp.arange(sub)[None, :]
    same = (ri // c_len) == (ci // c_len)
    ltri = (same & (ri >= ci)).astype(BF16)
    hw = hsum.shape[0]
    blk = lambda bi, i: (bi, i, 0)
    const = lambda bi, i: (0, 0)
    big = lambda: pltpu.VMEM((n_rows, R_WIDTH), F32)
    return pl.pallas_call(
        functools.partial(_rwkv_kernel, c_len=c_len, tt=tt, nb=nb, sub=sub),
        grid=(b // nb, t // tt),
        in_specs=[pl.BlockSpec((nb, tt, RWKV_PAD), blk),
                  pl.BlockSpec((None, nb, N_HEADS_R, HEAD_DIM, HEAD_DIM), lambda bi, i: (ls, bi, 0, 0, 0)),
                  _layer_block((6, R_WIDTH), l), _layer_block((1, R_WIDTH), l),
                  _layer_block((LANES, R_WIDTH), l), _layer_block((LANES, R_WIDTH), l),
                  _layer_block((2 * LANES, R_WIDTH), l), pl.BlockSpec((hw, hw), const),
                  pl.BlockSpec((sub, sub), const)],
        out_specs=[pl.BlockSpec((nb, tt, R_WIDTH), blk),
                   pl.BlockSpec((nb, N_HEADS_R, HEAD_DIM, HEAD_DIM), lambda bi, i: (bi, 0, 0, 0))],
        out_shape=[jax.ShapeDtypeStruct((b, t, R_WIDTH), F32),
                   jax.ShapeDtypeStruct((b, N_HEADS_R, HEAD_DIM, HEAD_DIM), s0.dtype)],
        scratch_shapes=[pltpu.VMEM((nb, 4, LANES, LANES), F32),
                        big(), big(), big(), big(), big(), big(), big(), big(), big(), big(), big()],
        compiler_params=_params(("parallel", "arbitrary")),
        name="rwkv_mix",
    )(pr3, s0, vecs, rk_flat, w2p, a2p, g2p, hsum, ltri)


POST_PARTS = 2


def _post_kernel(attn_ref, rw_ref, x_ref, g_ref, wout_ref, wq_ref, wo_ref, mk_ref, mv_ref, o_ref, *, nb, t):
    scale = MEM_HEAD_DIM ** -0.5
    cols = lambda hd: slice(hd * MEM_HEAD_DIM, (hd + 1) * MEM_HEAD_DIM)
    head_rows = lambda hd: pl.ds(hd, MEM_LEN, stride=MEM_HEADS)
    ones = jnp.ones((MEM_LEN, MEM_HEAD_DIM), F32)

    def part(lo, hi):
        rows = slice(lo, hi)
        m = _mm(attn_ref[rows, :], wout_ref[0:A_WIDTH, :]) + _mm(rw_ref[rows, :], wout_ref[A_WIDTH:, :])
        yield
        x1 = x_ref[rows, :] + _rms(m, g_ref[NORM_MIX_POST:NORM_MIX_POST + 1, :])
        hq = _rms(x1, g_ref[NORM_X_PRE:NORM_X_PRE + 1, :])
        yield
        q = _mm(hq, wq_ref[...])
        yield
        if nb == 1:
            jobs = [(0, lo, hi, hd) for hd in range(MEM_HEADS)]
        else:
            jobs = [(r0 // t, r0, r0 + t, hd) for r0 in range(lo, hi, t) for hd in range(MEM_HEADS)]
        scores = [_mm_nt(q[r0 - lo:r1 - lo, cols(hd)], mk_ref[bi, head_rows(hd), :]) * scale
                  for bi, r0, r1, hd in jobs]
        yield
        exps = [jnp.exp(s - jnp.max(s, axis=-1, keepdims=True)) for s in scores]
        yield
        outs = []
        for (bi, _, _, hd), e in zip(jobs, exps):
            o2 = _mm(e, jnp.concatenate([mv_ref[bi, head_rows(hd), :], ones], axis=1))
            outs.append(o2[:, :MEM_HEAD_DIM] / o2[:, MEM_HEAD_DIM:])
        blocks = [jnp.concatenate(outs[j:j + MEM_HEADS], axis=1) for j in range(0, len(outs), MEM_HEADS)]
        o = blocks[0] if len(blocks) == 1 else jnp.concatenate(blocks, axis=0)
        yield
        c = _mm(o, wo_ref[...])
        yield
        o_ref[rows, :] = x1 + _rms(c, g_ref[NORM_X_POST:NORM_X_POST + 1, :])

    n_rows = nb * t
    n_parts = POST_PARTS if (nb == 1 or nb % POST_PARTS == 0) and n_rows % (8 * POST_PARTS) == 0 else 1
    step = n_rows // n_parts
    gens = [part(j * step, (j + 1) * step) for j in range(n_parts)]
    for j, gen in enumerate(gens):
        for _ in range(n_parts - 1 - j):
            next(gen)
    _interleave(gens)


def _post(attn, rw, x2d, l, gains, w_out, w_mq, w_mo, lm, mk, mv, nb, t, tiles_per_batch):
    n = x2d.shape[0]
    tm = nb * t
    row = lambda i: (i, 0)
    if nb == 1:
        mmap = lambda i: (lm, i // tiles_per_batch, 0, 0)
    else:
        mmap = lambda i: (lm, i, 0, 0)
    mem_rows = MEM_LEN * MEM_HEADS
    return pl.pallas_call(
        functools.partial(_post_kernel, nb=nb, t=t),
        grid=(n // tm,),
        in_specs=[pl.BlockSpec((tm, A_WIDTH), row), pl.BlockSpec((tm, R_WIDTH), row),
                  pl.BlockSpec((tm, D_MODEL), row), _layer_block((8, D_MODEL), l),
                  _layer_block((D_MODEL, D_MODEL), l), _layer_block((D_MODEL, MEM_WIDTH), l),
                  _layer_block((MEM_WIDTH, D_MODEL), l),
                  pl.BlockSpec((None, nb, mem_rows, MEM_HEAD_DIM), mmap),
                  pl.BlockSpec((None, nb, mem_rows, MEM_HEAD_DIM), mmap)],
        out_specs=pl.BlockSpec((tm, D_MODEL), row),
        out_shape=jax.ShapeDtypeStruct((n, D_MODEL), F32),
        compiler_params=_params(("parallel",)),
        name="post_mix",
    )(attn, rw, x2d, gains, w_out, w_mq, w_mo, mk, mv)


FFN_CHUNK = 256


def _ffn_kernel(x_ref, g_ref, wgu_ref, wd_ref, o_ref):
    x = x_ref[...]
    h = _rms(x, g_ref[NORM_FFN_PRE:NORM_FFN_PRE + 1, :]).astype(BF16)
    acc = None
    for j in range(D_FF // FFN_CHUNK):
        cols = slice(j * FFN_CHUNK, (j + 1) * FFN_CHUNK)
        gate = jnp.dot(h, wgu_ref[:, cols], preferred_element_type=F32)
        up = jnp.dot(h, wgu_ref[:, D_FF + j * FFN_CHUNK:D_FF + (j + 1) * FFN_CHUNK], preferred_element_type=F32)
        act = (gate * jax.nn.sigmoid(gate)) * up
        part = _mm(act, wd_ref[cols, :])
        acc = part if acc is None else acc + part
    o_ref[...] = x + _rms(acc, g_ref[NORM_FFN_POST:NORM_FFN_POST + 1, :])


def _ffn(x2d, l, gains, w_gu, w_dn, tm):
    n = x2d.shape[0]
    row = lambda i: (i, 0)
    resident = dict(pipeline_mode=pl.Buffered(1))
    return pl.pallas_call(
        _ffn_kernel,
        grid=(n // tm,),
        in_specs=[pl.BlockSpec((tm, D_MODEL), row), _layer_block((8, D_MODEL), l),
                  pl.BlockSpec((None, D_MODEL, 2 * D_FF), lambda i: (l, 0, 0), **resident),
                  pl.BlockSpec((None, D_FF, D_MODEL), lambda i: (l, 0, 0), **resident)],
        out_specs=pl.BlockSpec((tm, D_MODEL), row),
        out_shape=jax.ShapeDtypeStruct((n, D_MODEL), F32),
        compiler_params=_params(("parallel",)),
        name="ffn",
    )(x2d, gains, w_gu, w_dn)


def _memkv_kernel(x_ref, g_ref, w_ref, k_ref, v_ref):
    kv = _mm(_rms(x_ref[...], g_ref[NORM_MEM:NORM_MEM + 1, :]), w_ref[...])
    for hd in range(MEM_HEADS):
        rows = pl.ds(hd, MEM_LEN, stride=MEM_HEADS)
        k_ref[0, rows, :] = kv[:, hd * MEM_HEAD_DIM:(hd + 1) * MEM_HEAD_DIM]
        v_ref[0, rows, :] = kv[:, MEM_WIDTH + hd * MEM_HEAD_DIM:MEM_WIDTH + (hd + 1) * MEM_HEAD_DIM]


def _memkv(mem2d, l, gains, w_kv):
    n = mem2d.shape[0]
    nbatch = n // MEM_LEN
    mem_rows = MEM_LEN * MEM_HEADS
    out = jax.ShapeDtypeStruct((nbatch, mem_rows, MEM_HEAD_DIM), F32)
    ospec = pl.BlockSpec((1, mem_rows, MEM_HEAD_DIM), lambda i: (i, 0, 0))
    return pl.pallas_call(
        _memkv_kernel,
        grid=(nbatch,),
        in_specs=[pl.BlockSpec((MEM_LEN, D_MODEL), lambda i: (i, 0)), _layer_block((8, D_MODEL), l),
                  _layer_block((D_MODEL, 2 * MEM_WIDTH), l)],
        out_specs=[ospec, ospec],
        out_shape=[out, out],
        compiler_params=_params(("parallel",)),
        name="mem_kv",
    )(mem2d, gains, w_kv)


def _rope_tables(pos):
    half = HEAD_DIM // 2
    inv = ROPE_THETA ** (-jnp.arange(half, dtype=F32) / half)
    ang = pos.astype(F32)[:, None] * inv[None, :]
    cos = jnp.cos(ang)
    sin = jnp.sin(ang)
    cos_t = jnp.tile(cos, (1, LANES // half))
    sin_t = jnp.tile(jnp.concatenate([-sin, sin], axis=1), (1, LANES // HEAD_DIM))
    return cos_t, sin_t


def _stacked_params(norm_gains, w_in, attn_sink, shift_mu, rwkv_vecs, rwkv_rk, rwkv_w2, rwkv_a2, rwkv_g2,
                    w_out, w_mem_q, w_mem_kv, w_mem_o, w_gate_up, w_down):
    depth = w_in.shape[0]
    return dict(
        gains=jnp.pad(norm_gains, ((0, 0), (0, 8 - norm_gains.shape[1]), (0, 0))),
        w_in=jnp.pad(w_in, ((0, 0), (0, 0), (0, IN_COLS_PAD - w_in.shape[2]))).astype(BF16),
        sink=jnp.broadcast_to(attn_sink[:, :, None], attn_sink.shape + (LANES,)).astype(F32),
        mu=jnp.pad(shift_mu, ((0, 0), (0, RWKV_PAD - RWKV_PROJ)))[:, None, :],
        vecs=rwkv_vecs, rk=rwkv_rk.reshape(depth, 1, R_WIDTH),
        w2=jnp.pad(rwkv_w2, ((0, 0), (0, LANES - DECAY_LORA), (0, 0))).astype(BF16),
        a2=jnp.pad(rwkv_a2, ((0, 0), (DECAY_LORA, LANES - DECAY_LORA - AAA_LORA), (0, 0))).astype(BF16),
        g2=jnp.pad(rwkv_g2, ((0, 0), (0, 2 * LANES - GATE_LORA), (0, 0))).astype(BF16),
        w_out=w_out.astype(BF16), w_mq=w_mem_q.astype(BF16), w_mkv=w_mem_kv.astype(BF16),
        w_mo=w_mem_o.astype(BF16), w_gu=w_gate_up.astype(BF16), w_dn=w_down.astype(BF16))


def _head_sum_matrix():
    i = jnp.arange(2 * LANES)
    return ((i[:, None] // HEAD_DIM) == (i[None, :] // HEAD_DIM)).astype(BF16)


def _layer(x2d, l, b, t, tabs, lm, mk, mv, swa_cache, ls, s0, shift0, sp, hsum, in_tm, rw_tt, post_nb, post_t,
           ffn_tm):
    q, k, v, pr, shift_new = _in_proj(x2d, l, sp["gains"], sp["w_in"], tabs[0], tabs[1], sp["mu"], ls, shift0, t,
                                      in_tm)
    if swa_cache is None:
        attn = _swa_prompt(q, k, v, l, sp["sink"], b, t)
    else:
        attn = _swa_sample(q, k, v, l, swa_cache[0], swa_cache[1], sp["sink"], b, t)
    pr3 = pr.reshape(b, t, RWKV_PAD)
    rw, s_fin = _rwkv(pr3, l, ls, s0, sp["vecs"], sp["rk"], sp["w2"], sp["a2"], sp["g2"], hsum,
                      rw_tt[0], min(CHUNK, t), rw_tt[1])
    x2 = _post(attn, rw.reshape(b * t, R_WIDTH), x2d, l, sp["gains"], sp["w_out"], sp["w_mq"], sp["w_mo"],
               lm, mk, mv, post_nb, post_t, t // post_t)
    x3 = _ffn(x2, l, sp["gains"], sp["w_gu"], sp["w_dn"], ffn_tm)
    return x3, k, v, s_fin, shift_new[:, :, :RWKV_PROJ]


def kernel(x_prompt, mem_prompt, x_sample, cache_swa_k, cache_swa_v, cache_mem_k, cache_mem_v, state_rwkv,
           state_shift, norm_gains, w_in, attn_sink, shift_mu, rwkv_vecs, rwkv_rk, rwkv_w2, rwkv_a2, rwkv_g2,
           w_out, w_mem_q, w_mem_kv, w_mem_o, w_gate_up, w_down):
    b, t, _ = x_prompt.shape
    bd, tn, _ = x_sample.shape
    depth = w_in.shape[0]
    m_len = mem_prompt.shape[1]
    cache_len = cache_swa_k.shape[2]
    tabs_p = _rope_tables(jnp.arange(t, dtype=jnp.int32))
    cs, sn = _rope_tables(PAST_LEN + jnp.arange(tn, dtype=jnp.int32))
    tabs_s = (jnp.tile(cs, (bd, 1)), jnp.tile(sn, (bd, 1)))
    hsum = _head_sum_matrix()
    sp = _stacked_params(norm_gains, w_in, attn_sink, shift_mu, rwkv_vecs, rwkv_rk, rwkv_w2, rwkv_a2, rwkv_g2,
                         w_out, w_mem_q, w_mem_kv, w_mem_o, w_gate_up, w_down)
    s0_p = jnp.zeros((1, b, N_HEADS_R, HEAD_DIM, HEAD_DIM), F32)
    sh0_p = jnp.zeros((1, b, 1, RWKV_PAD), F32)
    sh0_s = jnp.pad(state_shift, ((0, 0), (0, 0), (0, 0), (0, RWKV_PAD - RWKV_PROJ)))
    swa_cache = (cache_swa_k.reshape(depth, bd, cache_len, KV_WIDTH_A),
                 cache_swa_v.reshape(depth, bd, cache_len, KV_WIDTH_A))
    mem_rows = m_len * MEM_HEADS
    cmk = cache_mem_k.reshape(depth, bd, mem_rows, MEM_HEAD_DIM)
    cmv = cache_mem_v.reshape(depth, bd, mem_rows, MEM_HEAD_DIM)
    xp = x_prompt.reshape(b * t, D_MODEL)
    xs = x_sample.reshape(bd * tn, D_MODEL)
    mem2d = mem_prompt.reshape(b * m_len, D_MODEL)
    in_tm_p = min(1024, t)
    rw_tt_p = (min(512, t), 1)
    rw_tt_s = (tn, 4 if bd % 4 == 0 else 1)
    post_t_p = min(1024, t)
    ffn_tm_p = min(512, b * t)
    keep = t - min(WINDOW, t)
    pk, pv, pmk, pmv, pS, psh = [], [], [], [], [], []
    sk, sv, sS, ssh = [], [], [], []
    for l in range(depth):
        mk_l, mv_l = _memkv(mem2d, l, sp["gains"], sp["w_mkv"])
        xp, k_l, v_l, s_l, sh_l = _layer(xp, l, b, t, tabs_p, 0, mk_l[None], mv_l[None], None, 0, s0_p, sh0_p, sp,
                                         hsum, in_tm_p, rw_tt_p, 1, post_t_p, ffn_tm_p)
        pk.append(k_l.reshape(b, t, KV_WIDTH_A)[:, keep:].reshape(b, t - keep, N_KV_A, HEAD_DIM))
        pv.append(v_l.reshape(b, t, KV_WIDTH_A)[:, keep:].reshape(b, t - keep, N_KV_A, HEAD_DIM))
        pmk.append(mk_l.reshape(b, m_len, MEM_HEADS, MEM_HEAD_DIM))
        pmv.append(mv_l.reshape(b, m_len, MEM_HEADS, MEM_HEAD_DIM))
        pS.append(s_l)
        psh.append(sh_l)

        xs, k2, v2, s2, sh2 = _layer(xs, l, bd, tn, tabs_s, l, cmk, cmv, swa_cache, l, state_rwkv, sh0_s, sp,
                                     hsum, bd * tn, rw_tt_s, bd, tn, bd * tn)
        sk.append(k2.reshape(bd, tn, N_KV_A, HEAD_DIM))
        sv.append(v2.reshape(bd, tn, N_KV_A, HEAD_DIM))
        sS.append(s2)
        ssh.append(sh2)
    return (xp.reshape(b, t, D_MODEL), xs.reshape(bd, tn, D_MODEL), jnp.stack(pk), jnp.stack(pv),
            jnp.stack(pmk), jnp.stack(pmv), jnp.stack(pS), jnp.stack(psh),
            jnp.stack(sk), jnp.stack(sv), jnp.stack(sS), jnp.stack(ssh))
```

```python
import functools
import math

import jax
import jax.numpy as jnp
from jax import lax
from jax.experimental import pallas as pl
from jax.experimental.pallas import tpu as pltpu

F32 = jnp.float32
BF16 = jnp.bfloat16

D_MODEL = 1024
HEAD_DIM = 64
CHUNK = 64
A_WIDTH = 512
KV_WIDTH_A = 128
N_KV_A = 2
WINDOW = 128
PAST_LEN = 4096
ROPE_THETA = 10000.0
R_WIDTH = 512
N_HEADS_R = 8
DECAY_LORA = 64
AAA_LORA = 64
GATE_LORA = 160
RWKV_PROJ = 3 * R_WIDTH + DECAY_LORA + AAA_LORA + GATE_LORA
RWKV_PAD = 1920
IN_COLS_PAD = A_WIDTH + 2 * KV_WIDTH_A + RWKV_PAD
GN_EPS = 6.4e-4
MEM_LEN = 256
MEM_HEADS = 4
MEM_HEAD_DIM = 128
MEM_WIDTH = 512
D_FF = 2816
RMS_EPS = 1e-6
NEG_INF = -1e30
NORM_MIX_PRE, NORM_MIX_POST, NORM_X_PRE, NORM_X_POST, NORM_MEM, NORM_FFN_PRE, NORM_FFN_POST = range(7)
EXP_M05 = math.exp(-0.5)

LANES = 128
VMEM_LIMIT = 56 * 1024 * 1024


def _params(sem):
    return pltpu.CompilerParams(dimension_semantics=sem, vmem_limit_bytes=VMEM_LIMIT)


def _rms(x, g):
    ms = jnp.mean(x * x, axis=-1, keepdims=True)
    return x * lax.rsqrt(ms + RMS_EPS) * g


def _mm(a, b):
    return jnp.dot(a.astype(BF16), b.astype(BF16), preferred_element_type=F32)


def _mm_nt(a, b):
    return lax.dot_general(a.astype(BF16), b.astype(BF16), (((1,), (1,)), ((), ())),
                           preferred_element_type=F32)


def _split(x):
    hi = x.astype(BF16)
    lo = (x - hi.astype(F32)).astype(BF16)
    return hi, lo


def _mm_lsplit(a_exact_bf16, x):
    hi, lo = _split(x)
    return (jnp.dot(a_exact_bf16, hi, preferred_element_type=F32)
            + jnp.dot(a_exact_bf16, lo, preferred_element_type=F32))


def _in_kernel(*refs, tm, t_seq, n_cast):
    x_ref, g_ref, w_ref, cos_ref, sin_ref, mu_ref, sh0_ref = refs[:7]
    cast_in = refs[7:7 + n_cast]
    q_ref, k_ref, v_ref, pr_ref, shl_ref = refs[7 + n_cast:12 + n_cast]
    cast_out = refs[12 + n_cast:12 + 2 * n_cast]
    carry_scr = refs[-1]
    for src, dst in zip(cast_in, cast_out):
        dst[...] = src[...].astype(BF16)
    h = _rms(x_ref[...], g_ref[NORM_MIX_PRE:NORM_MIX_PRE + 1, :]).astype(BF16)
    p = jnp.dot(h, w_ref[...], preferred_element_type=F32)
    cos = cos_ref[...]
    sin = sin_ref[...]
    lane = lax.broadcasted_iota(jnp.int32, cos.shape, 1)
    first_half = (lane & (HEAD_DIM // 2)) == 0

    def rope(xc):
        sw = jnp.where(first_half, pltpu.roll(xc, LANES - HEAD_DIM // 2, 1), pltpu.roll(xc, HEAD_DIM // 2, 1))
        return xc * cos + sw * sin

    for j in range(A_WIDTH // LANES):
        q_ref[:, j * LANES:(j + 1) * LANES] = rope(p[:, j * LANES:(j + 1) * LANES])
    k_ref[...] = rope(p[:, A_WIDTH:A_WIDTH + KV_WIDTH_A])
    v_ref[...] = p[:, A_WIDTH + KV_WIDTH_A:A_WIDTH + 2 * KV_WIDTH_A]
    raw = p[:, A_WIDTH + 2 * KV_WIDTH_A:]
    rowi = lax.broadcasted_iota(jnp.int32, (tm, 1), 0)
    prev = pltpu.roll(raw, 1, 0)
    if tm <= t_seq:
        first = (pl.program_id(0) % (t_seq // tm)) == 0
        prev = jnp.where(rowi == 0, jnp.where(first, sh0_ref[0], carry_scr[...]), prev)
        carry_scr[...] = raw[tm - 1:tm, :]
        shl_ref[0] = raw[tm - 1:tm, :]
    else:
        for s in range(tm // t_seq):
            prev = jnp.where(rowi == s * t_seq, sh0_ref[s], prev)
            shl_ref[s] = raw[(s + 1) * t_seq - 1:(s + 1) * t_seq, :]
    pr_ref[...] = raw + (prev - raw) * mu_ref[...]


def _layer_block(shape, l):
    nd = len(shape)
    return pl.BlockSpec((None,) + tuple(shape), lambda *_: (l,) + (0,) * nd)


def _in_proj(x2d, l, gains, w_in_p, cos_t, sin_t, mu_p, ls, shift0, t_seq, tm, casts=()):
    n = x2d.shape[0]
    n_seq = n // t_seq
    steps = n // tm
    tab_blocks = cos_t.shape[0] // tm
    row = lambda i: (i, 0)
    tab = lambda i: (i % tab_blocks, 0)
    if tm <= t_seq:
        seqs, seq_of = 1, lambda i: i // (t_seq // tm)
    else:
        seqs, seq_of = tm // t_seq, lambda i: i
    cast_in, cast_out, cast_shapes = [], [], []
    for w in casts:
        rows = w.shape[1] // steps
        cast_in.append(pl.BlockSpec((None, rows, w.shape[2]), lambda i: (l, i, 0)))
        cast_out.append(pl.BlockSpec((None, rows, w.shape[2]), lambda i: (0, i, 0)))
        cast_shapes.append(jax.ShapeDtypeStruct((1,) + w.shape[1:], BF16))
    outs = pl.pallas_call(
        functools.partial(_in_kernel, tm=tm, t_seq=t_seq, n_cast=len(casts)),
        grid=(steps,),
        in_specs=[pl.BlockSpec((tm, D_MODEL), row), _layer_block((8, D_MODEL), l),
                  _layer_block((D_MODEL, IN_COLS_PAD), l),
                  pl.BlockSpec((tm, LANES), tab), pl.BlockSpec((tm, LANES), tab),
                  _layer_block((1, RWKV_PAD), l),
                  pl.BlockSpec((None, seqs, 1, RWKV_PAD), lambda i: (ls, seq_of(i), 0, 0))] + cast_in,
        out_specs=[pl.BlockSpec((tm, A_WIDTH), row), pl.BlockSpec((tm, KV_WIDTH_A), row),
                   pl.BlockSpec((tm, KV_WIDTH_A), row), pl.BlockSpec((tm, RWKV_PAD), row),
                   pl.BlockSpec((seqs, 1, RWKV_PAD), lambda i: (seq_of(i), 0, 0))] + cast_out,
        out_shape=[jax.ShapeDtypeStruct((n, A_WIDTH), F32), jax.ShapeDtypeStruct((n, KV_WIDTH_A), F32),
                   jax.ShapeDtypeStruct((n, KV_WIDTH_A), F32), jax.ShapeDtypeStruct((n, RWKV_PAD), F32),
                   jax.ShapeDtypeStruct((n_seq, 1, RWKV_PAD), F32)] + cast_shapes,
        scratch_shapes=[pltpu.VMEM((1, RWKV_PAD), F32)],
        compiler_params=_params(("arbitrary",)),
        name="in_proj",
    )(x2d, gains, w_in_p, cos_t, sin_t, mu_p, shift0, *casts)
    return outs[:5], outs[5:]


def _sink_attend(jobs, sink_ref, t):
    nk = jobs[0][1].shape[0]
    nkp = 2 * LANES
    lane = lax.broadcasted_iota(jnp.int32, (t, LANES), 1)
    m0 = lane < HEAD_DIM
    coli = lax.broadcasted_iota(jnp.int32, (1, nkp), 1)
    fills = []
    for kv in range(N_KV_A):
        blocks = []
        for g in range(4):
            sg = sink_ref[4 * kv + g:4 * kv + g + 1, :]
            blocks.append(jnp.broadcast_to(jnp.where(coli == nk, jnp.concatenate([sg, sg], axis=1), NEG_INF),
                                           (t, nkp)))
        fills.append(jnp.concatenate(blocks, axis=0))
    k_tail = jnp.zeros((nkp - nk, LANES), F32)
    tail_row = lax.broadcasted_iota(jnp.int32, (nkp - nk, nkp), 0)
    tail_col = lax.broadcasted_iota(jnp.int32, (nkp - nk, nkp), 1)
    v_tail = jnp.where((tail_row == 0) & (tail_col >= LANES), 1.0, 0.0).astype(F32)
    ones = jnp.ones((nk, LANES), F32)
    scores = []
    for q_rows, kdup, _, kv, valid in jobs:
        parts = []
        for p in (2 * kv, 2 * kv + 1):
            qp = q_rows[:, p * LANES:(p + 1) * LANES]
            parts.append(jnp.where(m0, qp, 0.0))
            parts.append(jnp.where(m0, 0.0, qp))
        lhs = jnp.concatenate(parts, axis=0)
        s = _mm_nt(lhs, jnp.concatenate([kdup, k_tail], axis=0))
        keep = coli < nk
        if valid is not None:
            keep = keep & valid
        scores.append(jnp.where(keep, s, fills[kv]))
    exps = [jnp.exp(s - jnp.max(s, axis=-1, keepdims=True)) for s in scores]
    outs = []
    for (_, _, vdup, _, _), e in zip(jobs, exps):
        v2 = jnp.concatenate([jnp.concatenate([vdup, ones], axis=1), v_tail], axis=0)
        o2 = _mm(e, v2)
        o = o2[:, :LANES] / o2[:, LANES:]
        outs.append([jnp.where(m0, o[(2 * pi) * t:(2 * pi + 1) * t], o[(2 * pi + 1) * t:(2 * pi + 2) * t])
                     for pi in range(2)])
    return outs


SWA_SCALE = HEAD_DIM ** -0.5


def _dup_heads(x):
    lane = lax.broadcasted_iota(jnp.int32, x.shape, 1)
    m0 = lane < HEAD_DIM
    xs = pltpu.roll(x, HEAD_DIM, 1)
    return [jnp.where(m0, x, xs), jnp.where(m0, xs, x)]


SWA_TQ = 512


def _swa_prompt_kernel(q_ref, kp_ref, kc_ref, vp_ref, vc_ref, sink_ref, o_ref):
    i = pl.program_id(1)
    k = jnp.concatenate([kp_ref[...], kc_ref[...]], axis=0)
    v = jnp.concatenate([vp_ref[...], vc_ref[...]], axis=0)
    kd = _dup_heads(k * SWA_SCALE)
    vd = _dup_heads(v)
    nk = 3 * CHUNK
    slot = lax.broadcasted_iota(jnp.int32, (1, 2 * LANES), 1) // CHUNK
    jobs = []
    for j in range(SWA_TQ // CHUNK):
        qj = q_ref[j * CHUNK:(j + 1) * CHUNK, :]
        valid = (slot + (i * (SWA_TQ // CHUNK) + j - 2)) >= 0
        for kv in range(N_KV_A):
            jobs.append((qj, kd[kv][j * CHUNK:j * CHUNK + nk], vd[kv][j * CHUNK:j * CHUNK + nk], kv, valid))
    outs = _sink_attend(jobs, sink_ref, CHUNK)
    for n, out in enumerate(outs):
        j, kv = divmod(n, N_KV_A)
        for pi in range(2):
            p = 2 * kv + pi
            o_ref[j * CHUNK:(j + 1) * CHUNK, p * LANES:(p + 1) * LANES] = out[pi]


def _swa_prompt(q, k, v, l, sink_b, b, t):
    n = b * t
    nq = t // SWA_TQ
    qmap = lambda bi, i: (bi * nq + i, 0)
    pmap = lambda bi, i: (jnp.maximum(bi * (t // WINDOW) + (SWA_TQ // WINDOW) * i - 1, 0), 0)
    return pl.pallas_call(
        _swa_prompt_kernel,
        grid=(b, nq),
        in_specs=[pl.BlockSpec((SWA_TQ, A_WIDTH), qmap),
                  pl.BlockSpec((WINDOW, KV_WIDTH_A), pmap), pl.BlockSpec((SWA_TQ, KV_WIDTH_A), qmap),
                  pl.BlockSpec((WINDOW, KV_WIDTH_A), pmap), pl.BlockSpec((SWA_TQ, KV_WIDTH_A), qmap),
                  _layer_block((8, LANES), l)],
        out_specs=pl.BlockSpec((SWA_TQ, A_WIDTH), qmap),
        out_shape=jax.ShapeDtypeStruct((n, A_WIDTH), F32),
        compiler_params=_params(("parallel", "parallel")),
        name="swa_prompt",
    )(q, k, k, v, v, sink_b)


def _swa_sample_kernel(q_ref, kc_ref, kn_ref, vc_ref, vn_ref, sink_ref, o_ref, *, t, nb):
    jobs = []
    for bi in range(nb):
        rows = slice(bi * t, (bi + 1) * t)
        kd = _dup_heads(jnp.concatenate([kc_ref[bi], kn_ref[rows, :]], axis=0) * SWA_SCALE)
        vd = _dup_heads(jnp.concatenate([vc_ref[bi], vn_ref[rows, :]], axis=0))
        for kv in range(N_KV_A):
            jobs.append((q_ref[rows, :], kd[kv], vd[kv], kv, None))
    outs = _sink_attend(jobs, sink_ref, t)
    for n, out in enumerate(outs):
        bi, kv = divmod(n, N_KV_A)
        for pi in range(2):
            p = 2 * kv + pi
            o_ref[bi * t:(bi + 1) * t, p * LANES:(p + 1) * LANES] = out[pi]


def _swa_sample(q, k, v, l, kc, vc, sink_b, b, t):
    n = b * t
    cache = kc.shape[2]
    nb = 8 if b % 8 == 0 else 1
    row = lambda bi: (bi, 0)
    cmap = lambda bi: (l, bi, 0, 0)
    return pl.pallas_call(
        functools.partial(_swa_sample_kernel, t=t, nb=nb),
        grid=(b // nb,),
        in_specs=[pl.BlockSpec((nb * t, A_WIDTH), row),
                  pl.BlockSpec((None, nb, cache, KV_WIDTH_A), cmap), pl.BlockSpec((nb * t, KV_WIDTH_A), row),
                  pl.BlockSpec((None, nb, cache, KV_WIDTH_A), cmap), pl.BlockSpec((nb * t, KV_WIDTH_A), row),
                  _layer_block((8, LANES), l)],
        out_specs=pl.BlockSpec((nb * t, A_WIDTH), row),
        out_shape=jax.ShapeDtypeStruct((n, A_WIDTH), F32),
        compiler_params=_params(("parallel",)),
        name="swa_sample",
    )(q, kc, k, vc, v, sink_b)


RWKV_SUB = 256


def _interleave(gens):
    gens = list(gens)
    while gens:
        for g in list(gens):
            try:
                next(g)
            except StopIteration:
                gens.remove(g)


def _rwkv_kernel(pr_ref, s0_ref, vec_ref, rk_ref, w2_ref, a2_ref, g2_ref, hsum_ref,
                 ltri_ref, out_ref, sfin_ref,
                 s_scr, rt_s, at_s, bt_s, kt_s, bb_s, kb_s, v_s, et_s, y_s, bonus_s, gate_s,
                 *, c_len, tt, nb, sub):
    i = pl.program_id(1)
    n_i = pl.num_programs(1)
    n_rows = nb * tt

    @pl.when(i == 0)
    def _():
        zero = jnp.zeros((HEAD_DIM, HEAD_DIM), F32)
        for bi in range(nb):
            for p in range(4):
                s_scr[bi, p] = jnp.concatenate(
                    [jnp.concatenate([s0_ref[bi, 2 * p].astype(F32), zero], axis=1),
                     jnp.concatenate([zero, s0_ref[bi, 2 * p + 1].astype(F32)], axis=1)], axis=0)

    w0 = vec_ref[0:1, :]
    a0 = vec_ref[1:2, :]
    k_k = vec_ref[2:3, :]
    k_a = vec_ref[3:4, :]
    gn_g = vec_ref[4:5, :]
    gn_b = vec_ref[5:6, :]
    hsum = hsum_ref[...]
    hw = hsum.shape[0]
    halves = [slice(j * hw, (j + 1) * hw) for j in range(R_WIDTH // hw)]

    def head_sum(x):
        return _mm(x, hsum)

    n2 = 2 * c_len
    ti = lax.broadcasted_iota(jnp.int32, (c_len, n2), 0)
    lane2 = lax.broadcasted_iota(jnp.int32, (c_len, n2), 1)
    si = lane2 & (c_len - 1)
    first_c = lane2 < c_len
    strict = ti > si
    incl = ti >= si
    eye = jnp.where(ti == si, 1.0, 0.0).astype(F32)
    masks = []
    half = 1
    while half < c_len:
        blk = 2 * half
        masks.append(((ti & ~(blk - 1)) == (si & ~(blk - 1))) & ((ti & half) != 0) & ((si & half) == 0))
        half = blk
    m0 = lax.broadcasted_iota(jnp.int32, (c_len, LANES), 1) < HEAD_DIM
    rl = lax.broadcasted_iota(jnp.int32, (LANES, LANES), 0)
    cl = lax.broadcasted_iota(jnp.int32, (LANES, LANES), 1)
    same_head = (rl >= HEAD_DIM) == (cl >= HEAD_DIM)
    fused = n2 == LANES
    zeros_c = jnp.zeros((c_len, LANES), F32)

    def bd(x):
        return jnp.concatenate([jnp.where(m0, x, 0.0), jnp.where(m0, 0.0, x)], axis=0)

    def bd_t(x):
        return jnp.concatenate([jnp.where(first_c, x, 0.0), jnp.where(first_c, 0.0, x)], axis=0)

    def prologue(lo, hi):
        n = hi - lo
        bi0 = lo // tt

        def shifted(a, b):
            if hi - lo <= tt:
                return pr_ref[bi0, lo - bi0 * tt:hi - bi0 * tt, a:b]
            return pr_ref[bi0:hi // tt, :, a:b].reshape(n, b - a)

        xwa = shifted(3 * R_WIDTH, 3 * R_WIDTH + LANES)
        xg = shifted(3 * R_WIDTH + LANES, RWKV_PAD)
        th = jnp.tanh(xwa)
        sg = jax.nn.sigmoid(xg)
        ltri = ltri_ref[...]
        yield
        for cs in halves:
            r = shifted(cs.start, cs.stop)
            k = shifted(R_WIDTH + cs.start, R_WIDTH + cs.stop)
            v = shifted(2 * R_WIDTH + cs.start, 2 * R_WIDTH + cs.stop)
            yield
            z = w0[:, cs] + _mm(th, w2_ref[:, cs])
            wlog = -EXP_M05 * jax.nn.sigmoid(z)
            a = jax.nn.sigmoid(a0[:, cs] + _mm(xwa, a2_ref[:, cs]))
            gate_s[lo:hi, cs] = _mm(sg, g2_ref[:, cs])
            yield
            kk = k * k_k[:, cs]
            kk = kk * lax.rsqrt(jnp.maximum(head_sum(kk * kk), 1e-24))
            k_f = k * (1.0 + (a - 1.0) * k_a[:, cs])
            bvec = kk * a
            bonus_s[lo:hi, cs] = head_sum(r * k_f * rk_ref[:, cs]) * v
            yield
            cum = _mm_lsplit(ltri, wlog)
            tot = jnp.concatenate([jnp.broadcast_to(cum[r1 - 1:r1, :], (c_len, hw))
                                   for r1 in range(c_len, n + 1, c_len)], axis=0)
            e_in = jnp.exp(cum)
            e_inv = jnp.exp(-cum)
            e_end = jnp.exp(tot - cum)
            rt_s[lo:hi, cs] = r * e_in
            at_s[lo:hi, cs] = -kk * jnp.exp(cum - wlog)
            yield
            bt_s[lo:hi, cs] = bvec * e_inv
            kt_s[lo:hi, cs] = k_f * e_inv
            bb_s[lo:hi, cs] = bvec * e_end
            kb_s[lo:hi, cs] = k_f * e_end
            v_s[lo:hi, cs] = v
            et_s[lo:hi, cs] = jnp.exp(tot)
            yield

    s_state = [[s_scr[bi, p] for p in range(4)] for bi in range(nb)]

    def chains(lo, hi):
        chs = [(r0, p) for r0 in range(lo, hi, c_len) for p in range(4)]

        def cat(ref, ch):
            return ref[ch[0]:ch[0] + c_len, ch[1] * LANES:(ch[1] + 1) * LANES]

        a_ab, a_ak, a_rb, a_rk = {}, {}, {}, {}
        for ch in chs:
            at, rt, rb, rkt = cat(at_s, ch), cat(rt_s, ch), bd(cat(bt_s, ch)), bd(cat(kt_s, ch))
            if fused:
                amat = _mm_nt(jnp.concatenate([at, rt], axis=0), jnp.concatenate([rb, rkt], axis=0))
                q_ab, q_ak = amat[:c_len, :n2], amat[:c_len, n2:]
                q_rb, q_rk = amat[c_len:, :n2], amat[c_len:, n2:]
            else:
                q_ab, q_ak, q_rb, q_rk = _mm_nt(at, rb), _mm_nt(at, rkt), _mm_nt(rt, rb), _mm_nt(rt, rkt)
            a_ab[ch] = jnp.where(strict, q_ab, 0.0)
            a_ak[ch] = jnp.where(strict, q_ak, 0.0)
            a_rb[ch] = jnp.where(incl, q_rb, 0.0)
            a_rk[ch] = jnp.where(incl, q_rk, 0.0)
        yield
        tinv = {ch: eye + jnp.where(masks[0], a_ab[ch], 0.0) for ch in chs}
        for m in masks[1:]:
            pe = {ch: _mm(tinv[ch], bd_t(jnp.where(m, a_ab[ch], 0.0))) for ch in chs}
            yield
            tinv = {ch: tinv[ch] + _mm(pe[ch], bd_t(tinv[ch])) for ch in chs}
            yield
        zv = {ch: _mm(a_ak[ch], bd(cat(v_s, ch))) for ch in chs}
        yield
        r_hat, y_hat, m_mat, g_mat = {}, {}, {}, {}
        if fused:
            w = {ch: _mm(tinv[ch], jnp.concatenate([bd(cat(at_s, ch)), bd(zv[ch])], axis=1)) for ch in chs}
            yield
            for ch in chs:
                rhs2 = jnp.concatenate(
                    [jnp.concatenate([bd(w[ch][:, :LANES]), bd(w[ch][:, LANES:])], axis=1),
                     jnp.concatenate([jnp.zeros((n2, LANES), F32), bd(cat(v_s, ch))], axis=1)], axis=0)
                ry = _mm(jnp.concatenate([a_rb[ch], a_rk[ch]], axis=1), rhs2)
                r_hat[ch] = cat(rt_s, ch) + ry[:, :LANES]
                y_hat[ch] = ry[:, LANES:]
            yield
            for ch in chs:
                lhs_t = jnp.concatenate([w[ch], jnp.concatenate([zeros_c, cat(v_s, ch)], axis=1)], axis=0)
                mg = _mm(lhs_t.T, jnp.concatenate([cat(bb_s, ch), cat(kb_s, ch)], axis=0))
                m_mat[ch] = jnp.where(same_head, mg[:LANES], 0.0)
                g_mat[ch] = jnp.where(same_head, mg[LANES:], 0.0)
            yield
        else:
            a_hat = {ch: _mm(tinv[ch], bd(cat(at_s, ch))) for ch in chs}
            u_hat = {ch: _mm(tinv[ch], bd(zv[ch])) for ch in chs}
            yield
            for ch in chs:
                r_hat[ch] = cat(rt_s, ch) + _mm(a_rb[ch], bd(a_hat[ch]))
                y_hat[ch] = _mm(a_rb[ch], bd(u_hat[ch])) + _mm(a_rk[ch], bd(cat(v_s, ch)))
            yield
            for ch in chs:
                m_mat[ch] = jnp.where(same_head, _mm(a_hat[ch].T, cat(bb_s, ch)), 0.0)
                g_mat[ch] = jnp.where(same_head,
                                      _mm(u_hat[ch].T, cat(bb_s, ch)) + _mm(cat(v_s, ch).T, cat(kb_s, ch)), 0.0)
            yield
        for r0 in range(lo, hi, c_len):
            s_cur = s_state[r0 // tt]
            for p in range(4):
                ch = (r0, p)
                ls = slice(p * LANES, (p + 1) * LANES)
                e_last = et_s[r0:r0 + 1, ls]
                y_s[r0:r0 + c_len, ls] = y_hat[ch] + _mm_nt(r_hat[ch], s_cur[p])
                s_cur[p] = s_cur[p] * e_last + _mm(s_cur[p], m_mat[ch]) + g_mat[ch]
            yield

    def epilogue(lo, hi):
        for cs in halves:
            y = y_s[lo:hi, cs]
            mean = head_sum(y) * (1.0 / HEAD_DIM)
            yc = y - mean
            yield
            var = head_sum(yc * yc) * (1.0 / HEAD_DIM)
            yn = yc * lax.rsqrt(var + GN_EPS) * gn_g[:, cs] + gn_b[:, cs]
            res = (yn + bonus_s[lo:hi, cs]) * gate_s[lo:hi, cs]
            for r0 in range(lo, hi, tt) if hi - lo > tt else [lo]:
                r1 = min(r0 + tt, hi)
                out_ref[r0 // tt, r0 % tt:r0 % tt + (r1 - r0), cs] = res[r0 - lo:r1 - lo]
            yield

    subs = [(lo, lo + sub) for lo in range(0, n_rows, sub)]
    _interleave([prologue(*subs[0])])
    for n, sb in enumerate(subs):
        phases = [chains(*sb)]
        if n + 1 < len(subs):
            phases.append(prologue(*subs[n + 1]))
        if n > 0:
            phases.append(epilogue(*subs[n - 1]))
        _interleave(phases)
    _interleave([epilogue(*subs[-1])])

    for bi in range(nb):
        for p in range(4):
            s_scr[bi, p] = s_state[bi][p]

    @pl.when(i == n_i - 1)
    def _():
        for bi in range(nb):
            for p in range(4):
                s_pair = s_state[bi][p]
                sfin_ref[bi, 2 * p] = s_pair[:HEAD_DIM, :HEAD_DIM].astype(sfin_ref.dtype)
                sfin_ref[bi, 2 * p + 1] = s_pair[HEAD_DIM:, HEAD_DIM:].astype(sfin_ref.dtype)


def _rwkv(pr3, l, ls, s0, vecs, rk_flat, w2p, a2p, g2p, hsum, tt, c_len, nb):
    b, t, _ = pr3.shape
    n_rows = nb * tt
    sub = min(RWKV_SUB, n_rows)
    ri = jnp.arange(sub)[:, None]
    ci = jnp.arange(sub)[None, :]
    same = (ri // c_len) == (ci // c_len)
    ltri = (same & (ri >= ci)).astype(BF16)
    hw = hsum.shape[0]
    blk = lambda bi, i: (bi, i, 0)
    const = lambda bi, i: (0, 0)
    big = lambda: pltpu.VMEM((n_rows, R_WIDTH), F32)
    return pl.pallas_call(
        functools.partial(_rwkv_kernel, c_len=c_len, tt=tt, nb=nb, sub=sub),
        grid=(b // nb, t // tt),
        in_specs=[pl.BlockSpec((nb, tt, RWKV_PAD), blk),
                  pl.BlockSpec((None, nb, N_HEADS_R, HEAD_DIM, HEAD_DIM), lambda bi, i: (ls, bi, 0, 0, 0)),
                  _layer_block((6, R_WIDTH), l), _layer_block((1, R_WIDTH), l),
                  _layer_block((LANES, R_WIDTH), l), _layer_block((LANES, R_WIDTH), l),
                  _layer_block((2 * LANES, R_WIDTH), l), pl.BlockSpec((hw, hw), const),
                  pl.BlockSpec((sub, sub), const)],
        out_specs=[pl.BlockSpec((nb, tt, R_WIDTH), blk),
                   pl.BlockSpec((nb, N_HEADS_R, HEAD_DIM, HEAD_DIM), lambda bi, i: (bi, 0, 0, 0))],
        out_shape=[jax.ShapeDtypeStruct((b, t, R_WIDTH), F32),
                   jax.ShapeDtypeStruct((b, N_HEADS_R, HEAD_DIM, HEAD_DIM), s0.dtype)],
        scratch_shapes=[pltpu.VMEM((nb, 4, LANES, LANES), F32),
                        big(), big(), big(), big(), big(), big(), big(), big(), big(), big(), big()],
        compiler_params=_params(("parallel", "arbitrary")),
        name="rwkv_mix",
    )(pr3, s0, vecs, rk_flat, w2p, a2p, g2p, hsum, ltri)


POST_PARTS = 2


def _post_kernel(attn_ref, rw_ref, x_ref, g_ref, wout_ref, wq_ref, wo_ref, mk_ref, mv_ref, o_ref, *, nb, t):
    scale = MEM_HEAD_DIM ** -0.5
    cols = lambda hd: slice(hd * MEM_HEAD_DIM, (hd + 1) * MEM_HEAD_DIM)
    head_rows = lambda hd: pl.ds(hd, MEM_LEN, stride=MEM_HEADS)
    ones = jnp.ones((MEM_LEN, MEM_HEAD_DIM), F32)

    def part(lo, hi):
        rows = slice(lo, hi)
        m = _mm(attn_ref[rows, :], wout_ref[0:A_WIDTH, :]) + _mm(rw_ref[rows, :], wout_ref[A_WIDTH:, :])
        yield
        x1 = x_ref[rows, :] + _rms(m, g_ref[NORM_MIX_POST:NORM_MIX_POST + 1, :])
        hq = _rms(x1, g_ref[NORM_X_PRE:NORM_X_PRE + 1, :])
        yield
        q = _mm(hq, wq_ref[...])
        yield
        if nb == 1:
            jobs = [(0, lo, hi, hd) for hd in range(MEM_HEADS)]
        else:
            jobs = [(r0 // t, r0, r0 + t, hd) for r0 in range(lo, hi, t) for hd in range(MEM_HEADS)]
        scores = [_mm_nt(q[r0 - lo:r1 - lo, cols(hd)], mk_ref[bi, head_rows(hd), :]) * scale
                  for bi, r0, r1, hd in jobs]
        yield
        exps = [jnp.exp(s - jnp.max(s, axis=-1, keepdims=True)) for s in scores]
        yield
        outs = []
        for (bi, _, _, hd), e in zip(jobs, exps):
            o2 = _mm(e, jnp.concatenate([mv_ref[bi, head_rows(hd), :], ones], axis=1))
            outs.append(o2[:, :MEM_HEAD_DIM] / o2[:, MEM_HEAD_DIM:])
        blocks = [jnp.concatenate(outs[j:j + MEM_HEADS], axis=1) for j in range(0, len(outs), MEM_HEADS)]
        o = blocks[0] if len(blocks) == 1 else jnp.concatenate(blocks, axis=0)
        yield
        c = _mm(o, wo_ref[...])
        yield
        o_ref[rows, :] = x1 + _rms(c, g_ref[NORM_X_POST:NORM_X_POST + 1, :])

    n_rows = nb * t
    n_parts = POST_PARTS if (nb == 1 or nb % POST_PARTS == 0) and n_rows % (8 * POST_PARTS) == 0 else 1
    step = n_rows // n_parts
    gens = [part(j * step, (j + 1) * step) for j in range(n_parts)]
    for j, gen in enumerate(gens):
        for _ in range(n_parts - 1 - j):
            next(gen)
    _interleave(gens)


def _post(attn, rw, x2d, l, gains, w_out, w_mq, w_mo, lm, mk, mv, nb, t, tiles_per_batch):
    n = x2d.shape[0]
    tm = nb * t
    row = lambda i: (i, 0)
    if nb == 1:
        mmap = lambda i: (lm, i // tiles_per_batch, 0, 0)
    else:
        mmap = lambda i: (lm, i, 0, 0)
    mem_rows = MEM_LEN * MEM_HEADS
    return pl.pallas_call(
        functools.partial(_post_kernel, nb=nb, t=t),
        grid=(n // tm,),
        in_specs=[pl.BlockSpec((tm, A_WIDTH), row), pl.BlockSpec((tm, R_WIDTH), row),
                  pl.BlockSpec((tm, D_MODEL), row), _layer_block((8, D_MODEL), l),
                  _layer_block((D_MODEL, D_MODEL), 0), _layer_block((D_MODEL, MEM_WIDTH), 0),
                  _layer_block((MEM_WIDTH, D_MODEL), 0),
                  pl.BlockSpec((None, nb, mem_rows, MEM_HEAD_DIM), mmap),
                  pl.BlockSpec((None, nb, mem_rows, MEM_HEAD_DIM), mmap)],
        out_specs=pl.BlockSpec((tm, D_MODEL), row),
        out_shape=jax.ShapeDtypeStruct((n, D_MODEL), F32),
        compiler_params=_params(("parallel",)),
        name="post_mix",
    )(attn, rw, x2d, gains, w_out, w_mq, w_mo, mk, mv)


FFN_CHUNK = 256


def _ffn_kernel(x_ref, g_ref, wgu_ref, wd_ref, o_ref):
    x = x_ref[...]
    h = _rms(x, g_ref[NORM_FFN_PRE:NORM_FFN_PRE + 1, :]).astype(BF16)
    acc = None
    for j in range(D_FF // FFN_CHUNK):
        cols = slice(j * FFN_CHUNK, (j + 1) * FFN_CHUNK)
        gate = jnp.dot(h, wgu_ref[:, cols], preferred_element_type=F32)
        up = jnp.dot(h, wgu_ref[:, D_FF + j * FFN_CHUNK:D_FF + (j + 1) * FFN_CHUNK], preferred_element_type=F32)
        act = (gate * jax.nn.sigmoid(gate)) * up
        part = _mm(act, wd_ref[cols, :])
        acc = part if acc is None else acc + part
    o_ref[...] = x + _rms(acc, g_ref[NORM_FFN_POST:NORM_FFN_POST + 1, :])


def _ffn(x2d, l, gains, w_gu, w_dn, tm):
    n = x2d.shape[0]
    row = lambda i: (i, 0)
    resident = dict(pipeline_mode=pl.Buffered(1))
    return pl.pallas_call(
        _ffn_kernel,
        grid=(n // tm,),
        in_specs=[pl.BlockSpec((tm, D_MODEL), row), _layer_block((8, D_MODEL), l),
                  pl.BlockSpec((None, D_MODEL, 2 * D_FF), lambda i: (0, 0, 0), **resident),
                  pl.BlockSpec((None, D_FF, D_MODEL), lambda i: (0, 0, 0), **resident)],
        out_specs=pl.BlockSpec((tm, D_MODEL), row),
        out_shape=jax.ShapeDtypeStruct((n, D_MODEL), F32),
        compiler_params=_params(("parallel",)),
        name="ffn",
    )(x2d, gains, w_gu, w_dn)


def _memkv_kernel(x_ref, g_ref, w_ref, k_ref, v_ref):
    kv = _mm(_rms(x_ref[...], g_ref[NORM_MEM:NORM_MEM + 1, :]), w_ref[...])
    for hd in range(MEM_HEADS):
        rows = pl.ds(hd, MEM_LEN, stride=MEM_HEADS)
        k_ref[0, rows, :] = kv[:, hd * MEM_HEAD_DIM:(hd + 1) * MEM_HEAD_DIM]
        v_ref[0, rows, :] = kv[:, MEM_WIDTH + hd * MEM_HEAD_DIM:MEM_WIDTH + (hd + 1) * MEM_HEAD_DIM]


def _memkv(mem2d, l, gains, w_kv):
    n = mem2d.shape[0]
    nbatch = n // MEM_LEN
    mem_rows = MEM_LEN * MEM_HEADS
    out = jax.ShapeDtypeStruct((nbatch, mem_rows, MEM_HEAD_DIM), F32)
    ospec = pl.BlockSpec((1, mem_rows, MEM_HEAD_DIM), lambda i: (i, 0, 0))
    return pl.pallas_call(
        _memkv_kernel,
        grid=(nbatch,),
        in_specs=[pl.BlockSpec((MEM_LEN, D_MODEL), lambda i: (i, 0)), _layer_block((8, D_MODEL), l),
                  _layer_block((D_MODEL, 2 * MEM_WIDTH), l)],
        out_specs=[ospec, ospec],
        out_shape=[out, out],
        compiler_params=_params(("parallel",)),
        name="mem_kv",
    )(mem2d, gains, w_kv)


def _rope_tables(pos):
    half = HEAD_DIM // 2
    inv = ROPE_THETA ** (-jnp.arange(half, dtype=F32) / half)
    ang = pos.astype(F32)[:, None] * inv[None, :]
    cos = jnp.cos(ang)
    sin = jnp.sin(ang)
    cos_t = jnp.tile(cos, (1, LANES // half))
    sin_t = jnp.tile(jnp.concatenate([-sin, sin], axis=1), (1, LANES // HEAD_DIM))
    return cos_t, sin_t


def _stacked_params(norm_gains, w_in, attn_sink, shift_mu, rwkv_vecs, rwkv_rk, rwkv_w2, rwkv_a2, rwkv_g2,
                    w_out, w_mem_q, w_mem_kv, w_mem_o, w_gate_up, w_down):
    depth = w_in.shape[0]
    return dict(
        gains=jnp.pad(norm_gains, ((0, 0), (0, 8 - norm_gains.shape[1]), (0, 0))),
        w_in=jnp.pad(w_in, ((0, 0), (0, 0), (0, IN_COLS_PAD - w_in.shape[2]))).astype(BF16),
        sink=jnp.broadcast_to(attn_sink[:, :, None], attn_sink.shape + (LANES,)).astype(F32),
        mu=jnp.pad(shift_mu, ((0, 0), (0, RWKV_PAD - RWKV_PROJ)))[:, None, :],
        vecs=rwkv_vecs, rk=rwkv_rk.reshape(depth, 1, R_WIDTH),
        w2=jnp.pad(rwkv_w2, ((0, 0), (0, LANES - DECAY_LORA), (0, 0))).astype(BF16),
        a2=jnp.pad(rwkv_a2, ((0, 0), (DECAY_LORA, LANES - DECAY_LORA - AAA_LORA), (0, 0))).astype(BF16),
        g2=jnp.pad(rwkv_g2, ((0, 0), (0, 2 * LANES - GATE_LORA), (0, 0))).astype(BF16),
        w_mkv=w_mem_kv.astype(BF16),
        w_out=w_out, w_mq=w_mem_q, w_mo=w_mem_o, w_gu=w_gate_up, w_dn=w_down)


def _head_sum_matrix():
    i = jnp.arange(2 * LANES)
    return ((i[:, None] // HEAD_DIM) == (i[None, :] // HEAD_DIM)).astype(BF16)


def _layer(x2d, l, b, t, tabs, lm, mk, mv, swa_cache, ls, s0, shift0, sp, hsum, in_tm, rw_tt, post_nb, post_t,
           ffn_tm, late_w=None):
    to_cast = () if late_w else tuple(sp[name] for name in ("w_out", "w_mq", "w_mo", "w_gu", "w_dn"))
    (q, k, v, pr, shift_new), cast = _in_proj(x2d, l, sp["gains"], sp["w_in"], tabs[0], tabs[1], sp["mu"], ls,
                                              shift0, t, in_tm, to_cast)
    late_w = late_w or tuple(cast)
    if swa_cache is None:
        attn = _swa_prompt(q, k, v, l, sp["sink"], b, t)
    else:
        attn = _swa_sample(q, k, v, l, swa_cache[0], swa_cache[1], sp["sink"], b, t)
    pr3 = pr.reshape(b, t, RWKV_PAD)
    rw, s_fin = _rwkv(pr3, l, ls, s0, sp["vecs"], sp["rk"], sp["w2"], sp["a2"], sp["g2"], hsum,
                      rw_tt[0], min(CHUNK, t), rw_tt[1])
    w_out, w_mq, w_mo, w_gu, w_dn = late_w
    x2 = _post(attn, rw.reshape(b * t, R_WIDTH), x2d, l, sp["gains"], w_out, w_mq, w_mo,
               lm, mk, mv, post_nb, post_t, t // post_t)
    x3 = _ffn(x2, l, sp["gains"], w_gu, w_dn, ffn_tm)
    return x3, k, v, s_fin, shift_new[:, :, :RWKV_PROJ], late_w


def kernel(x_prompt, mem_prompt, x_sample, cache_swa_k, cache_swa_v, cache_mem_k, cache_mem_v, state_rwkv,
           state_shift, norm_gains, w_in, attn_sink, shift_mu, rwkv_vecs, rwkv_rk, rwkv_w2, rwkv_a2, rwkv_g2,
           w_out, w_mem_q, w_mem_kv, w_mem_o, w_gate_up, w_down):
    b, t, _ = x_prompt.shape
    bd, tn, _ = x_sample.shape
    depth = w_in.shape[0]
    m_len = mem_prompt.shape[1]
    cache_len = cache_swa_k.shape[2]
    tabs_p = _rope_tables(jnp.arange(t, dtype=jnp.int32))
    cs, sn = _rope_tables(PAST_LEN + jnp.arange(tn, dtype=jnp.int32))
    tabs_s = (jnp.tile(cs, (bd, 1)), jnp.tile(sn, (bd, 1)))
    hsum = _head_sum_matrix()
    sp = _stacked_params(norm_gains, w_in, attn_sink, shift_mu, rwkv_vecs, rwkv_rk, rwkv_w2, rwkv_a2, rwkv_g2,
                         w_out, w_mem_q, w_mem_kv, w_mem_o, w_gate_up, w_down)
    s0_p = jnp.zeros((1, b, N_HEADS_R, HEAD_DIM, HEAD_DIM), F32)
    sh0_p = jnp.zeros((1, b, 1, RWKV_PAD), F32)
    sh0_s = jnp.pad(state_shift, ((0, 0), (0, 0), (0, 0), (0, RWKV_PAD - RWKV_PROJ)))
    swa_cache = (cache_swa_k.reshape(depth, bd, cache_len, KV_WIDTH_A),
                 cache_swa_v.reshape(depth, bd, cache_len, KV_WIDTH_A))
    mem_rows = m_len * MEM_HEADS
    cmk = cache_mem_k.reshape(depth, bd, mem_rows, MEM_HEAD_DIM)
    cmv = cache_mem_v.reshape(depth, bd, mem_rows, MEM_HEAD_DIM)
    xp = x_prompt.reshape(b * t, D_MODEL)
    xs = x_sample.reshape(bd * tn, D_MODEL)
    mem2d = mem_prompt.reshape(b * m_len, D_MODEL)
    in_tm_p = min(1024, t)
    rw_tt_p = (min(512, t), 1)
    rw_tt_s = (tn, 16 if bd % 16 == 0 else 1)
    post_t_p = min(1024, t)
    ffn_tm_p = min(512, b * t)
    keep = t - min(WINDOW, t)
    pk, pv, pmk, pmv, pS, psh = [], [], [], [], [], []
    sk, sv, sS, ssh = [], [], [], []
    for l in range(depth):
        mk_l, mv_l = _memkv(mem2d, l, sp["gains"], sp["w_mkv"])
        xp, k_l, v_l, s_l, sh_l, late_w = _layer(xp, l, b, t, tabs_p, 0, mk_l[None], mv_l[None], None, 0, s0_p, sh0_p,
                                                sp, hsum, in_tm_p, rw_tt_p, 1, post_t_p, ffn_tm_p)
        pk.append(k_l.reshape(b, t, KV_WIDTH_A)[:, keep:].reshape(b, t - keep, N_KV_A, HEAD_DIM))
        pv.append(v_l.reshape(b, t, KV_WIDTH_A)[:, keep:].reshape(b, t - keep, N_KV_A, HEAD_DIM))
        pmk.append(mk_l.reshape(b, m_len, MEM_HEADS, MEM_HEAD_DIM))
        pmv.append(mv_l.reshape(b, m_len, MEM_HEADS, MEM_HEAD_DIM))
        pS.append(s_l)
        psh.append(sh_l)

        xs, k2, v2, s2, sh2, _ = _layer(xs, l, bd, tn, tabs_s, l, cmk, cmv, swa_cache, l, state_rwkv, sh0_s, sp,
                                        hsum, bd * tn, rw_tt_s, bd, tn, bd * tn, late_w)
        sk.append(k2.reshape(bd, tn, N_KV_A, HEAD_DIM))
        sv.append(v2.reshape(bd, tn, N_KV_A, HEAD_DIM))
        sS.append(s2)
        ssh.append(sh2)
    return (xp.reshape(b, t, D_MODEL), xs.reshape(bd, tn, D_MODEL), jnp.stack(pk), jnp.stack(pv),
            jnp.stack(pmk), jnp.stack(pmv), jnp.stack(pS), jnp.stack(psh),
            jnp.stack(sk), jnp.stack(sv), jnp.stack(sS), jnp.stack(ssh))
```

```python
import functools
import math

import jax
import jax.numpy as jnp
from jax import lax
from jax.experimental import pallas as pl
from jax.experimental.pallas import tpu as pltpu

F32 = jnp.float32
BF16 = jnp.bfloat16

D_MODEL = 1024
HEAD_DIM = 64
CHUNK = 64
A_WIDTH = 512
KV_WIDTH_A = 128
N_KV_A = 2
WINDOW = 128
PAST_LEN = 4096
ROPE_THETA = 10000.0
R_WIDTH = 512
N_HEADS_R = 8
DECAY_LORA = 64
AAA_LORA = 64
GATE_LORA = 160
RWKV_PROJ = 3 * R_WIDTH + DECAY_LORA + AAA_LORA + GATE_LORA
RWKV_PAD = 1920
IN_COLS_PAD = A_WIDTH + 2 * KV_WIDTH_A + RWKV_PAD
GN_EPS = 6.4e-4
MEM_LEN = 256
MEM_HEADS = 4
MEM_HEAD_DIM = 128
MEM_WIDTH = 512
D_FF = 2816
RMS_EPS = 1e-6
NEG_INF = -1e30
NORM_MIX_PRE, NORM_MIX_POST, NORM_X_PRE, NORM_X_POST, NORM_MEM, NORM_FFN_PRE, NORM_FFN_POST = range(7)
EXP_M05 = math.exp(-0.5)

LANES = 128
IN_MAIN = IN_COLS_PAD - LANES
VMEM_LIMIT = 56 * 1024 * 1024


def _params(sem):
    return pltpu.CompilerParams(dimension_semantics=sem, vmem_limit_bytes=VMEM_LIMIT)


def _rms(x, g):
    ms = jnp.mean(x * x, axis=-1, keepdims=True)
    return x * lax.rsqrt(ms + RMS_EPS) * g


def _mm(a, b):
    return jnp.dot(a.astype(BF16), b.astype(BF16), preferred_element_type=F32)


def _mm_nt(a, b):
    return lax.dot_general(a.astype(BF16), b.astype(BF16), (((1,), (1,)), ((), ())),
                           preferred_element_type=F32)


def _split(x):
    hi = x.astype(BF16)
    lo = (x - hi.astype(F32)).astype(BF16)
    return hi, lo


def _mm_lsplit(a_exact_bf16, x):
    hi, lo = _split(x)
    return (jnp.dot(a_exact_bf16, hi, preferred_element_type=F32)
            + jnp.dot(a_exact_bf16, lo, preferred_element_type=F32))


def _in_kernel(*refs, tm, t_seq, n_cast):
    x_ref, g_ref, w_ref, wt_ref, cos_ref, sin_ref, mu_ref, sh0_ref = refs[:8]
    cast_in = refs[8:8 + n_cast]
    q_ref, k_ref, v_ref, pr_ref, shl_ref = refs[8 + n_cast:13 + n_cast]
    cast_out = refs[13 + n_cast:13 + 2 * n_cast]
    carry_scr, wb_scr, wtb_scr = refs[-3:]
    for src, dst in zip(cast_in, cast_out):
        dst[...] = src[...].astype(BF16)

    @pl.when(pl.program_id(0) == 0)
    def _():
        wb_scr[...] = w_ref[...].astype(BF16)
        wtb_scr[...] = wt_ref[...].astype(BF16)

    h = _rms(x_ref[...], g_ref[NORM_MIX_PRE:NORM_MIX_PRE + 1, :]).astype(BF16)
    p = jnp.concatenate([jnp.dot(h, wb_scr[...], preferred_element_type=F32),
                         jnp.dot(h, wtb_scr[...], preferred_element_type=F32)], axis=1)
    cos = cos_ref[...]
    sin = sin_ref[...]
    lane = lax.broadcasted_iota(jnp.int32, cos.shape, 1)
    first_half = (lane & (HEAD_DIM // 2)) == 0

    def rope(xc):
        sw = jnp.where(first_half, pltpu.roll(xc, LANES - HEAD_DIM // 2, 1), pltpu.roll(xc, HEAD_DIM // 2, 1))
        return xc * cos + sw * sin

    for j in range(A_WIDTH // LANES):
        q_ref[:, j * LANES:(j + 1) * LANES] = rope(p[:, j * LANES:(j + 1) * LANES])
    k_ref[...] = rope(p[:, A_WIDTH:A_WIDTH + KV_WIDTH_A])
    v_ref[...] = p[:, A_WIDTH + KV_WIDTH_A:A_WIDTH + 2 * KV_WIDTH_A]
    raw = p[:, A_WIDTH + 2 * KV_WIDTH_A:]
    rowi = lax.broadcasted_iota(jnp.int32, (tm, 1), 0)
    prev = pltpu.roll(raw, 1, 0)
    if tm <= t_seq:
        first = (pl.program_id(0) % (t_seq // tm)) == 0
        prev = jnp.where(rowi == 0, jnp.where(first, sh0_ref[0], carry_scr[...]), prev)
        carry_scr[...] = raw[tm - 1:tm, :]
        shl_ref[0] = raw[tm - 1:tm, :]
    else:
        for s in range(tm // t_seq):
            prev = jnp.where(rowi == s * t_seq, sh0_ref[s], prev)
            shl_ref[s] = raw[(s + 1) * t_seq - 1:(s + 1) * t_seq, :]
    pr_ref[...] = raw + (prev - raw) * mu_ref[...]


def _layer_block(shape, l):
    nd = len(shape)
    return pl.BlockSpec((None,) + tuple(shape), lambda *_: (l,) + (0,) * nd)


def _in_proj(x2d, l, gains, w_in, w_tail, cos_t, sin_t, mu_p, ls, shift0, t_seq, tm, casts=()):
    n = x2d.shape[0]
    n_seq = n // t_seq
    steps = n // tm
    tab_blocks = cos_t.shape[0] // tm
    row = lambda i: (i, 0)
    tab = lambda i: (i % tab_blocks, 0)
    if tm <= t_seq:
        seqs, seq_of = 1, lambda i: i // (t_seq // tm)
    else:
        seqs, seq_of = tm // t_seq, lambda i: i
    cast_in, cast_out, cast_shapes = [], [], []
    for w in casts:
        rows = w.shape[1] // steps
        cast_in.append(pl.BlockSpec((None, rows, w.shape[2]), lambda i: (l, i, 0)))
        cast_out.append(pl.BlockSpec((None, rows, w.shape[2]), lambda i: (0, i, 0)))
        cast_shapes.append(jax.ShapeDtypeStruct((1,) + w.shape[1:], BF16))
    outs = pl.pallas_call(
        functools.partial(_in_kernel, tm=tm, t_seq=t_seq, n_cast=len(casts)),
        grid=(steps,),
        in_specs=[pl.BlockSpec((tm, D_MODEL), row), _layer_block((8, D_MODEL), l),
                  pl.BlockSpec((None, D_MODEL, IN_MAIN), lambda i: (l, 0, 0), pipeline_mode=pl.Buffered(1)),
                  _layer_block((D_MODEL, LANES), l),
                  pl.BlockSpec((tm, LANES), tab), pl.BlockSpec((tm, LANES), tab),
                  _layer_block((1, RWKV_PAD), l),
                  pl.BlockSpec((None, seqs, 1, RWKV_PAD), lambda i: (ls, seq_of(i), 0, 0))] + cast_in,
        out_specs=[pl.BlockSpec((tm, A_WIDTH), row), pl.BlockSpec((tm, KV_WIDTH_A), row),
                   pl.BlockSpec((tm, KV_WIDTH_A), row), pl.BlockSpec((tm, RWKV_PAD), row),
                   pl.BlockSpec((seqs, 1, RWKV_PAD), lambda i: (seq_of(i), 0, 0))] + cast_out,
        out_shape=[jax.ShapeDtypeStruct((n, A_WIDTH), F32), jax.ShapeDtypeStruct((n, KV_WIDTH_A), F32),
                   jax.ShapeDtypeStruct((n, KV_WIDTH_A), F32), jax.ShapeDtypeStruct((n, RWKV_PAD), F32),
                   jax.ShapeDtypeStruct((n_seq, 1, RWKV_PAD), F32)] + cast_shapes,
        scratch_shapes=[pltpu.VMEM((1, RWKV_PAD), F32), pltpu.VMEM((D_MODEL, IN_MAIN), BF16),
                        pltpu.VMEM((D_MODEL, LANES), BF16)],
        compiler_params=_params(("arbitrary",)),
        name="in_proj",
    )(x2d, gains, w_in, w_tail, cos_t, sin_t, mu_p, shift0, *casts)
    return outs[:5], outs[5:]


def _sink_attend(jobs, sink_ref, t):
    nk = jobs[0][1].shape[0]
    nkp = 2 * LANES
    lane = lax.broadcasted_iota(jnp.int32, (t, LANES), 1)
    m0 = lane < HEAD_DIM
    coli = lax.broadcasted_iota(jnp.int32, (1, nkp), 1)
    fills = []
    for kv in range(N_KV_A):
        blocks = []
        for g in range(4):
            sg = sink_ref[4 * kv + g:4 * kv + g + 1, :]
            blocks.append(jnp.broadcast_to(jnp.where(coli == nk, jnp.concatenate([sg, sg], axis=1), NEG_INF),
                                           (t, nkp)))
        fills.append(jnp.concatenate(blocks, axis=0))
    k_tail = jnp.zeros((nkp - nk, LANES), F32)
    tail_row = lax.broadcasted_iota(jnp.int32, (nkp - nk, nkp), 0)
    tail_col = lax.broadcasted_iota(jnp.int32, (nkp - nk, nkp), 1)
    v_tail = jnp.where((tail_row == 0) & (tail_col >= LANES), 1.0, 0.0).astype(F32)
    ones = jnp.ones((nk, LANES), F32)
    scores = []
    for q_rows, kdup, _, kv, valid in jobs:
        parts = []
        for p in (2 * kv, 2 * kv + 1):
            qp = q_rows[:, p * LANES:(p + 1) * LANES]
            parts.append(jnp.where(m0, qp, 0.0))
            parts.append(jnp.where(m0, 0.0, qp))
        lhs = jnp.concatenate(parts, axis=0)
        s = _mm_nt(lhs, jnp.concatenate([kdup, k_tail], axis=0))
        keep = coli < nk
        if valid is not None:
            keep = keep & valid
        scores.append(jnp.where(keep, s, fills[kv]))
    exps = [jnp.exp(s - jnp.max(s, axis=-1, keepdims=True)) for s in scores]
    outs = []
    for (_, _, vdup, _, _), e in zip(jobs, exps):
        v2 = jnp.concatenate([jnp.concatenate([vdup, ones], axis=1), v_tail], axis=0)
        o2 = _mm(e, v2)
        o = o2[:, :LANES] / o2[:, LANES:]
        outs.append([jnp.where(m0, o[(2 * pi) * t:(2 * pi + 1) * t], o[(2 * pi + 1) * t:(2 * pi + 2) * t])
                     for pi in range(2)])
    return outs


SWA_SCALE = HEAD_DIM ** -0.5


def _dup_heads(x):
    lane = lax.broadcasted_iota(jnp.int32, x.shape, 1)
    m0 = lane < HEAD_DIM
    xs = pltpu.roll(x, HEAD_DIM, 1)
    return [jnp.where(m0, x, xs), jnp.where(m0, xs, x)]


SWA_TQ = 512


def _swa_prompt_kernel(q_ref, kp_ref, kc_ref, vp_ref, vc_ref, sink_ref, o_ref):
    i = pl.program_id(1)
    k = jnp.concatenate([kp_ref[...], kc_ref[...]], axis=0)
    v = jnp.concatenate([vp_ref[...], vc_ref[...]], axis=0)
    kd = _dup_heads(k * SWA_SCALE)
    vd = _dup_heads(v)
    nk = 3 * CHUNK
    slot = lax.broadcasted_iota(jnp.int32, (1, 2 * LANES), 1) // CHUNK
    jobs = []
    for j in range(SWA_TQ // CHUNK):
        qj = q_ref[j * CHUNK:(j + 1) * CHUNK, :]
        valid = (slot + (i * (SWA_TQ // CHUNK) + j - 2)) >= 0
        for kv in range(N_KV_A):
            jobs.append((qj, kd[kv][j * CHUNK:j * CHUNK + nk], vd[kv][j * CHUNK:j * CHUNK + nk], kv, valid))
    outs = _sink_attend(jobs, sink_ref, CHUNK)
    for n, out in enumerate(outs):
        j, kv = divmod(n, N_KV_A)
        for pi in range(2):
            p = 2 * kv + pi
            o_ref[j * CHUNK:(j + 1) * CHUNK, p * LANES:(p + 1) * LANES] = out[pi]


def _swa_prompt(q, k, v, l, sink_b, b, t):
    n = b * t
    nq = t // SWA_TQ
    qmap = lambda bi, i: (bi * nq + i, 0)
    pmap = lambda bi, i: (jnp.maximum(bi * (t // WINDOW) + (SWA_TQ // WINDOW) * i - 1, 0), 0)
    return pl.pallas_call(
        _swa_prompt_kernel,
        grid=(b, nq),
        in_specs=[pl.BlockSpec((SWA_TQ, A_WIDTH), qmap),
                  pl.BlockSpec((WINDOW, KV_WIDTH_A), pmap), pl.BlockSpec((SWA_TQ, KV_WIDTH_A), qmap),
                  pl.BlockSpec((WINDOW, KV_WIDTH_A), pmap), pl.BlockSpec((SWA_TQ, KV_WIDTH_A), qmap),
                  _layer_block((8, LANES), l)],
        out_specs=pl.BlockSpec((SWA_TQ, A_WIDTH), qmap),
        out_shape=jax.ShapeDtypeStruct((n, A_WIDTH), F32),
        compiler_params=_params(("parallel", "parallel")),
        name="swa_prompt",
    )(q, k, k, v, v, sink_b)


def _swa_sample_kernel(q_ref, kc_ref, kn_ref, vc_ref, vn_ref, sink_ref, o_ref, *, t, nb):
    jobs = []
    for bi in range(nb):
        rows = slice(bi * t, (bi + 1) * t)
        kd = _dup_heads(jnp.concatenate([kc_ref[bi], kn_ref[rows, :]], axis=0) * SWA_SCALE)
        vd = _dup_heads(jnp.concatenate([vc_ref[bi], vn_ref[rows, :]], axis=0))
        for kv in range(N_KV_A):
            jobs.append((q_ref[rows, :], kd[kv], vd[kv], kv, None))
    outs = _sink_attend(jobs, sink_ref, t)
    for n, out in enumerate(outs):
        bi, kv = divmod(n, N_KV_A)
        for pi in range(2):
            p = 2 * kv + pi
            o_ref[bi * t:(bi + 1) * t, p * LANES:(p + 1) * LANES] = out[pi]


def _swa_sample(q, k, v, l, kc, vc, sink_b, b, t):
    n = b * t
    cache = kc.shape[2]
    nb = 8 if b % 8 == 0 else 1
    row = lambda bi: (bi, 0)
    cmap = lambda bi: (l, bi, 0, 0)
    return pl.pallas_call(
        functools.partial(_swa_sample_kernel, t=t, nb=nb),
        grid=(b // nb,),
        in_specs=[pl.BlockSpec((nb * t, A_WIDTH), row),
                  pl.BlockSpec((None, nb, cache, KV_WIDTH_A), cmap), pl.BlockSpec((nb * t, KV_WIDTH_A), row),
                  pl.BlockSpec((None, nb, cache, KV_WIDTH_A), cmap), pl.BlockSpec((nb * t, KV_WIDTH_A), row),
                  _layer_block((8, LANES), l)],
        out_specs=pl.BlockSpec((nb * t, A_WIDTH), row),
        out_shape=jax.ShapeDtypeStruct((n, A_WIDTH), F32),
        compiler_params=_params(("parallel",)),
        name="swa_sample",
    )(q, kc, k, vc, v, sink_b)


RWKV_SUB = 256


def _interleave(gens):
    gens = list(gens)
    while gens:
        for g in list(gens):
            try:
                next(g)
            except StopIteration:
                gens.remove(g)


def _rwkv_kernel(pr_ref, s0_ref, vec_ref, rk_ref, w2_ref, a2_ref, g2_ref, hsum_ref,
                 ltri_ref, out_ref, sfin_ref,
                 s_scr, rt_s, at_s, bt_s, kt_s, bb_s, kb_s, v_s, et_s, y_s, bonus_s, gate_s,
                 *, c_len, tt, nb, sub):
    i = pl.program_id(1)
    n_i = pl.num_programs(1)
    n_rows = nb * tt

    @pl.when(i == 0)
    def _():
        zero = jnp.zeros((HEAD_DIM, HEAD_DIM), F32)
        for bi in range(nb):
            for p in range(4):
                s_scr[bi, p] = jnp.concatenate(
                    [jnp.concatenate([s0_ref[bi, 2 * p].astype(F32), zero], axis=1),
                     jnp.concatenate([zero, s0_ref[bi, 2 * p + 1].astype(F32)], axis=1)], axis=0)

    w0 = vec_ref[0:1, :]
    a0 = vec_ref[1:2, :]
    k_k = vec_ref[2:3, :]
    k_a = vec_ref[3:4, :]
    gn_g = vec_ref[4:5, :]
    gn_b = vec_ref[5:6, :]
    hsum = hsum_ref[...]
    hw = hsum.shape[0]
    halves = [slice(j * hw, (j + 1) * hw) for j in range(R_WIDTH // hw)]

    def head_sum(x):
        return _mm(x, hsum)

    n2 = 2 * c_len
    ti = lax.broadcasted_iota(jnp.int32, (c_len, n2), 0)
    lane2 = lax.broadcasted_iota(jnp.int32, (c_len, n2), 1)
    si = lane2 & (c_len - 1)
    first_c = lane2 < c_len
    strict = ti > si
    incl = ti >= si
    eye = jnp.where(ti == si, 1.0, 0.0).astype(F32)
    masks = []
    half = 1
    while half < c_len:
        blk = 2 * half
        masks.append(((ti & ~(blk - 1)) == (si & ~(blk - 1))) & ((ti & half) != 0) & ((si & half) == 0))
        half = blk
    m0 = lax.broadcasted_iota(jnp.int32, (c_len, LANES), 1) < HEAD_DIM
    rl = lax.broadcasted_iota(jnp.int32, (LANES, LANES), 0)
    cl = lax.broadcasted_iota(jnp.int32, (LANES, LANES), 1)
    same_head = (rl >= HEAD_DIM) == (cl >= HEAD_DIM)
    fused = n2 == LANES
    zeros_c = jnp.zeros((c_len, LANES), F32)

    def bd(x):
        return jnp.concatenate([jnp.where(m0, x, 0.0), jnp.where(m0, 0.0, x)], axis=0)

    def bd_t(x):
        return jnp.concatenate([jnp.where(first_c, x, 0.0), jnp.where(first_c, 0.0, x)], axis=0)

    def prologue(lo, hi):
        n = hi - lo
        bi0 = lo // tt

        def shifted(a, b):
            if hi - lo <= tt:
                return pr_ref[bi0, lo - bi0 * tt:hi - bi0 * tt, a:b]
            return pr_ref[bi0:hi // tt, :, a:b].reshape(n, b - a)

        xwa = shifted(3 * R_WIDTH, 3 * R_WIDTH + LANES)
        xg = shifted(3 * R_WIDTH + LANES, RWKV_PAD)
        th = jnp.tanh(xwa)
        sg = jax.nn.sigmoid(xg)
        ltri = ltri_ref[...]
        yield
        for cs in halves:
            r = shifted(cs.start, cs.stop)
            k = shifted(R_WIDTH + cs.start, R_WIDTH + cs.stop)
            v = shifted(2 * R_WIDTH + cs.start, 2 * R_WIDTH + cs.stop)
            yield
            z = w0[:, cs] + _mm(th, w2_ref[:, cs])
            wlog = -EXP_M05 * jax.nn.sigmoid(z)
            a = jax.nn.sigmoid(a0[:, cs] + _mm(xwa, a2_ref[:, cs]))
            gate_s[lo:hi, cs] = _mm(sg, g2_ref[:, cs])
            yield
            kk = k * k_k[:, cs]
            kk = kk * lax.rsqrt(jnp.maximum(head_sum(kk * kk), 1e-24))
            k_f = k * (1.0 + (a - 1.0) * k_a[:, cs])
            bvec = kk * a
            bonus_s[lo:hi, cs] = head_sum(r * k_f * rk_ref[:, cs]) * v
            yield
            cum = _mm_lsplit(ltri, wlog)
            tot = jnp.concatenate([jnp.broadcast_to(cum[r1 - 1:r1, :], (c_len, hw))
                                   for r1 in range(c_len, n + 1, c_len)], axis=0)
            e_in = jnp.exp(cum)
            e_inv = jnp.exp(-cum)
            e_end = jnp.exp(tot - cum)
            rt_s[lo:hi, cs] = r * e_in
            at_s[lo:hi, cs] = -kk * jnp.exp(cum - wlog)
            yield
            bt_s[lo:hi, cs] = bvec * e_inv
            kt_s[lo:hi, cs] = k_f * e_inv
            bb_s[lo:hi, cs] = bvec * e_end
            kb_s[lo:hi, cs] = k_f * e_end
            v_s[lo:hi, cs] = v
            et_s[lo:hi, cs] = jnp.exp(tot)
            yield

    s_state = [[s_scr[bi, p] for p in range(4)] for bi in range(nb)]

    def chains(lo, hi):
        chs = [(r0, p) for r0 in range(lo, hi, c_len) for p in range(4)]

        def cat(ref, ch):
            return ref[ch[0]:ch[0] + c_len, ch[1] * LANES:(ch[1] + 1) * LANES]

        a_ab, a_ak, a_rb, a_rk = {}, {}, {}, {}
        for ch in chs:
            at, rt, rb, rkt = cat(at_s, ch), cat(rt_s, ch), bd(cat(bt_s, ch)), bd(cat(kt_s, ch))
            if fused:
                amat = _mm_nt(jnp.concatenate([at, rt], axis=0), jnp.concatenate([rb, rkt], axis=0))
                q_ab, q_ak = amat[:c_len, :n2], amat[:c_len, n2:]
                q_rb, q_rk = amat[c_len:, :n2], amat[c_len:, n2:]
            else:
                q_ab, q_ak, q_rb, q_rk = _mm_nt(at, rb), _mm_nt(at, rkt), _mm_nt(rt, rb), _mm_nt(rt, rkt)
            a_ab[ch] = jnp.where(strict, q_ab, 0.0)
            a_ak[ch] = jnp.where(strict, q_ak, 0.0)
            a_rb[ch] = jnp.where(incl, q_rb, 0.0)
            a_rk[ch] = jnp.where(incl, q_rk, 0.0)
        yield
        tinv = {ch: eye + jnp.where(masks[0], a_ab[ch], 0.0) for ch in chs}
        for m in masks[1:]:
            pe = {ch: _mm(tinv[ch], bd_t(jnp.where(m, a_ab[ch], 0.0))) for ch in chs}
            yield
            tinv = {ch: tinv[ch] + _mm(pe[ch], bd_t(tinv[ch])) for ch in chs}
            yield
        zv = {ch: _mm(a_ak[ch], bd(cat(v_s, ch))) for ch in chs}
        yield
        r_hat, y_hat, m_mat, g_mat = {}, {}, {}, {}
        if fused:
            w = {ch: _mm(tinv[ch], jnp.concatenate([bd(cat(at_s, ch)), bd(zv[ch])], axis=1)) for ch in chs}
            yield
            for ch in chs:
                rhs2 = jnp.concatenate(
                    [jnp.concatenate([bd(w[ch][:, :LANES]), bd(w[ch][:, LANES:])], axis=1),
                     jnp.concatenate([jnp.zeros((n2, LANES), F32), bd(cat(v_s, ch))], axis=1)], axis=0)
                ry = _mm(jnp.concatenate([a_rb[ch], a_rk[ch]], axis=1), rhs2)
                r_hat[ch] = cat(rt_s, ch) + ry[:, :LANES]
                y_hat[ch] = ry[:, LANES:]
            yield
            for ch in chs:
                lhs_t = jnp.concatenate([w[ch], jnp.concatenate([zeros_c, cat(v_s, ch)], axis=1)], axis=0)
                mg = _mm(lhs_t.T, jnp.concatenate([cat(bb_s, ch), cat(kb_s, ch)], axis=0))
                m_mat[ch] = jnp.where(same_head, mg[:LANES], 0.0)
                g_mat[ch] = jnp.where(same_head, mg[LANES:], 0.0)
            yield
        else:
            a_hat = {ch: _mm(tinv[ch], bd(cat(at_s, ch))) for ch in chs}
            u_hat = {ch: _mm(tinv[ch], bd(zv[ch])) for ch in chs}
            yield
            for ch in chs:
                r_hat[ch] = cat(rt_s, ch) + _mm(a_rb[ch], bd(a_hat[ch]))
                y_hat[ch] = _mm(a_rb[ch], bd(u_hat[ch])) + _mm(a_rk[ch], bd(cat(v_s, ch)))
            yield
            for ch in chs:
                m_mat[ch] = jnp.where(same_head, _mm(a_hat[ch].T, cat(bb_s, ch)), 0.0)
                g_mat[ch] = jnp.where(same_head,
                                      _mm(u_hat[ch].T, cat(bb_s, ch)) + _mm(cat(v_s, ch).T, cat(kb_s, ch)), 0.0)
            yield
        for r0 in range(lo, hi, c_len):
            s_cur = s_state[r0 // tt]
            for p in range(4):
                ch = (r0, p)
                ls = slice(p * LANES, (p + 1) * LANES)
                e_last = et_s[r0:r0 + 1, ls]
                y_s[r0:r0 + c_len, ls] = y_hat[ch] + _mm_nt(r_hat[ch], s_cur[p])
                s_cur[p] = s_cur[p] * e_last + _mm(s_cur[p], m_mat[ch]) + g_mat[ch]
            yield

    def epilogue(lo, hi):
        for cs in halves:
            y = y_s[lo:hi, cs]
            mean = head_sum(y) * (1.0 / HEAD_DIM)
            yc = y - mean
            yield
            var = head_sum(yc * yc) * (1.0 / HEAD_DIM)
            yn = yc * lax.rsqrt(var + GN_EPS) * gn_g[:, cs] + gn_b[:, cs]
            res = (yn + bonus_s[lo:hi, cs]) * gate_s[lo:hi, cs]
            for r0 in range(lo, hi, tt) if hi - lo > tt else [lo]:
                r1 = min(r0 + tt, hi)
                out_ref[r0 // tt, r0 % tt:r0 % tt + (r1 - r0), cs] = res[r0 - lo:r1 - lo]
            yield

    subs = [(lo, lo + sub) for lo in range(0, n_rows, sub)]
    _interleave([prologue(*subs[0])])
    for n, sb in enumerate(subs):
        phases = [chains(*sb)]
        if n + 1 < len(subs):
            phases.append(prologue(*subs[n + 1]))
        if n > 0:
            phases.append(epilogue(*subs[n - 1]))
        _interleave(phases)
    _interleave([epilogue(*subs[-1])])

    for bi in range(nb):
        for p in range(4):
            s_scr[bi, p] = s_state[bi][p]

    @pl.when(i == n_i - 1)
    def _():
        for bi in range(nb):
            for p in range(4):
                s_pair = s_state[bi][p]
                sfin_ref[bi, 2 * p] = s_pair[:HEAD_DIM, :HEAD_DIM].astype(sfin_ref.dtype)
                sfin_ref[bi, 2 * p + 1] = s_pair[HEAD_DIM:, HEAD_DIM:].astype(sfin_ref.dtype)


def _rwkv(pr3, l, ls, s0, vecs, rk_flat, w2p, a2p, g2p, hsum, tt, c_len, nb):
    b, t, _ = pr3.shape
    n_rows = nb * tt
    sub = min(RWKV_SUB, n_rows)
    ri = jnp.arange(sub)[:, None]
    ci = jnp.arange(sub)[None, :]
    same = (ri // c_len) == (ci // c_len)
    ltri = (same & (ri >= ci)).astype(BF16)
    hw = hsum.shape[0]
    blk = lambda bi, i: (bi, i, 0)
    const = lambda bi, i: (0, 0)
    big = lambda: pltpu.VMEM((n_rows, R_WIDTH), F32)
    return pl.pallas_call(
        functools.partial(_rwkv_kernel, c_len=c_len, tt=tt, nb=nb, sub=sub),
        grid=(b // nb, t // tt),
        in_specs=[pl.BlockSpec((nb, tt, RWKV_PAD), blk),
                  pl.BlockSpec((None, nb, N_HEADS_R, HEAD_DIM, HEAD_DIM), lambda bi, i: (ls, bi, 0, 0, 0)),
                  _layer_block((6, R_WIDTH), l), _layer_block((1, R_WIDTH), l),
                  _layer_block((LANES, R_WIDTH), l), _layer_block((LANES, R_WIDTH), l),
                  _layer_block((2 * LANES, R_WIDTH), l), pl.BlockSpec((hw, hw), const),
                  pl.BlockSpec((sub, sub), const)],
        out_specs=[pl.BlockSpec((nb, tt, R_WIDTH), blk),
                   pl.BlockSpec((nb, N_HEADS_R, HEAD_DIM, HEAD_DIM), lambda bi, i: (bi, 0, 0, 0))],
        out_shape=[jax.ShapeDtypeStruct((b, t, R_WIDTH), F32),
                   jax.ShapeDtypeStruct((b, N_HEADS_R, HEAD_DIM, HEAD_DIM), s0.dtype)],
        scratch_shapes=[pltpu.VMEM((nb, 4, LANES, LANES), F32),
                        big(), big(), big(), big(), big(), big(), big(), big(), big(), big(), big()],
        compiler_params=_params(("parallel", "arbitrary")),
        name="rwkv_mix",
    )(pr3, s0, vecs, rk_flat, w2p, a2p, g2p, hsum, ltri)


POST_PARTS = 2


def _post_kernel(attn_ref, rw_ref, x_ref, g_ref, wout_ref, wq_ref, wo_ref, mk_ref, mv_ref, o_ref, *, nb, t):
    scale = MEM_HEAD_DIM ** -0.5
    cols = lambda hd: slice(hd * MEM_HEAD_DIM, (hd + 1) * MEM_HEAD_DIM)
    head_rows = lambda hd: pl.ds(hd, MEM_LEN, stride=MEM_HEADS)
    ones = jnp.ones((MEM_LEN, MEM_HEAD_DIM), F32)

    def part(lo, hi):
        rows = slice(lo, hi)
        m = _mm(attn_ref[rows, :], wout_ref[0:A_WIDTH, :]) + _mm(rw_ref[rows, :], wout_ref[A_WIDTH:, :])
        yield
        x1 = x_ref[rows, :] + _rms(m, g_ref[NORM_MIX_POST:NORM_MIX_POST + 1, :])
        hq = _rms(x1, g_ref[NORM_X_PRE:NORM_X_PRE + 1, :])
        yield
        q = _mm(hq, wq_ref[...])
        yield
        if nb == 1:
            jobs = [(0, lo, hi, hd) for hd in range(MEM_HEADS)]
        else:
            jobs = [(r0 // t, r0, r0 + t, hd) for r0 in range(lo, hi, t) for hd in range(MEM_HEADS)]
        scores = [_mm_nt(q[r0 - lo:r1 - lo, cols(hd)], mk_ref[bi, head_rows(hd), :]) * scale
                  for bi, r0, r1, hd in jobs]
        yield
        exps = [jnp.exp(s - jnp.max(s, axis=-1, keepdims=True)) for s in scores]
        yield
        outs = []
        for (bi, _, _, hd), e in zip(jobs, exps):
            o2 = _mm(e, jnp.concatenate([mv_ref[bi, head_rows(hd), :], ones], axis=1))
            outs.append(o2[:, :MEM_HEAD_DIM] / o2[:, MEM_HEAD_DIM:])
        blocks = [jnp.concatenate(outs[j:j + MEM_HEADS], axis=1) for j in range(0, len(outs), MEM_HEADS)]
        o = blocks[0] if len(blocks) == 1 else jnp.concatenate(blocks, axis=0)
        yield
        c = _mm(o, wo_ref[...])
        yield
        o_ref[rows, :] = x1 + _rms(c, g_ref[NORM_X_POST:NORM_X_POST + 1, :])

    n_rows = nb * t
    n_parts = POST_PARTS if (nb == 1 or nb % POST_PARTS == 0) and n_rows % (8 * POST_PARTS) == 0 else 1
    step = n_rows // n_parts
    gens = [part(j * step, (j + 1) * step) for j in range(n_parts)]
    for j, gen in enumerate(gens):
        for _ in range(n_parts - 1 - j):
            next(gen)
    _interleave(gens)


def _post(attn, rw, x2d, l, gains, w_out, w_mq, w_mo, lm, mk, mv, nb, t, tiles_per_batch):
    n = x2d.shape[0]
    tm = nb * t
    row = lambda i: (i, 0)
    if nb == 1:
        mmap = lambda i: (lm, i // tiles_per_batch, 0, 0)
    else:
        mmap = lambda i: (lm, i, 0, 0)
    mem_rows = MEM_LEN * MEM_HEADS
    return pl.pallas_call(
        functools.partial(_post_kernel, nb=nb, t=t),
        grid=(n // tm,),
        in_specs=[pl.BlockSpec((tm, A_WIDTH), row), pl.BlockSpec((tm, R_WIDTH), row),
                  pl.BlockSpec((tm, D_MODEL), row), _layer_block((8, D_MODEL), l),
                  _layer_block((D_MODEL, D_MODEL), 0), _layer_block((D_MODEL, MEM_WIDTH), 0),
                  _layer_block((MEM_WIDTH, D_MODEL), 0),
                  pl.BlockSpec((None, nb, mem_rows, MEM_HEAD_DIM), mmap),
                  pl.BlockSpec((None, nb, mem_rows, MEM_HEAD_DIM), mmap)],
        out_specs=pl.BlockSpec((tm, D_MODEL), row),
        out_shape=jax.ShapeDtypeStruct((n, D_MODEL), F32),
        compiler_params=_params(("parallel",)),
        name="post_mix",
    )(attn, rw, x2d, gains, w_out, w_mq, w_mo, mk, mv)


FFN_CHUNK = 256


def _ffn_kernel(x_ref, g_ref, wgu_ref, wd_ref, o_ref):
    x = x_ref[...]
    h = _rms(x, g_ref[NORM_FFN_PRE:NORM_FFN_PRE + 1, :]).astype(BF16)
    acc = None
    for j in range(D_FF // FFN_CHUNK):
        cols = slice(j * FFN_CHUNK, (j + 1) * FFN_CHUNK)
        gate = jnp.dot(h, wgu_ref[:, cols], preferred_element_type=F32)
        up = jnp.dot(h, wgu_ref[:, D_FF + j * FFN_CHUNK:D_FF + (j + 1) * FFN_CHUNK], preferred_element_type=F32)
        act = (gate * jax.nn.sigmoid(gate)) * up
        part = _mm(act, wd_ref[cols, :])
        acc = part if acc is None else acc + part
    o_ref[...] = x + _rms(acc, g_ref[NORM_FFN_POST:NORM_FFN_POST + 1, :])


def _ffn(x2d, l, gains, w_gu, w_dn, tm):
    n = x2d.shape[0]
    row = lambda i: (i, 0)
    resident = dict(pipeline_mode=pl.Buffered(1))
    return pl.pallas_call(
        _ffn_kernel,
        grid=(n // tm,),
        in_specs=[pl.BlockSpec((tm, D_MODEL), row), _layer_block((8, D_MODEL), l),
                  pl.BlockSpec((None, D_MODEL, 2 * D_FF), lambda i: (0, 0, 0), **resident),
                  pl.BlockSpec((None, D_FF, D_MODEL), lambda i: (0, 0, 0), **resident)],
        out_specs=pl.BlockSpec((tm, D_MODEL), row),
        out_shape=jax.ShapeDtypeStruct((n, D_MODEL), F32),
        compiler_params=_params(("parallel",)),
        name="ffn",
    )(x2d, gains, w_gu, w_dn)


def _memkv_kernel(x_ref, g_ref, w_ref, k_ref, v_ref):
    kv = _mm(_rms(x_ref[...], g_ref[NORM_MEM:NORM_MEM + 1, :]), w_ref[...])
    for hd in range(MEM_HEADS):
        rows = pl.ds(hd, MEM_LEN, stride=MEM_HEADS)
        k_ref[0, rows, :] = kv[:, hd * MEM_HEAD_DIM:(hd + 1) * MEM_HEAD_DIM]
        v_ref[0, rows, :] = kv[:, MEM_WIDTH + hd * MEM_HEAD_DIM:MEM_WIDTH + (hd + 1) * MEM_HEAD_DIM]


def _memkv(mem2d, l, gains, w_kv):
    n = mem2d.shape[0]
    nbatch = n // MEM_LEN
    mem_rows = MEM_LEN * MEM_HEADS
    out = jax.ShapeDtypeStruct((nbatch, mem_rows, MEM_HEAD_DIM), F32)
    ospec = pl.BlockSpec((1, mem_rows, MEM_HEAD_DIM), lambda i: (i, 0, 0))
    return pl.pallas_call(
        _memkv_kernel,
        grid=(nbatch,),
        in_specs=[pl.BlockSpec((MEM_LEN, D_MODEL), lambda i: (i, 0)), _layer_block((8, D_MODEL), l),
                  _layer_block((D_MODEL, 2 * MEM_WIDTH), l)],
        out_specs=[ospec, ospec],
        out_shape=[out, out],
        compiler_params=_params(("parallel",)),
        name="mem_kv",
    )(mem2d, gains, w_kv)


def _rope_tables(pos):
    half = HEAD_DIM // 2
    inv = ROPE_THETA ** (-jnp.arange(half, dtype=F32) / half)
    ang = pos.astype(F32)[:, None] * inv[None, :]
    cos = jnp.cos(ang)
    sin = jnp.sin(ang)
    cos_t = jnp.tile(cos, (1, LANES // half))
    sin_t = jnp.tile(jnp.concatenate([-sin, sin], axis=1), (1, LANES // HEAD_DIM))
    return cos_t, sin_t


def _stacked_params(norm_gains, w_in, attn_sink, shift_mu, rwkv_vecs, rwkv_rk, rwkv_w2, rwkv_a2, rwkv_g2,
                    w_out, w_mem_q, w_mem_kv, w_mem_o, w_gate_up, w_down):
    depth = w_in.shape[0]
    return dict(
        gains=jnp.pad(norm_gains, ((0, 0), (0, 8 - norm_gains.shape[1]), (0, 0))),
        w_in=w_in,
        w_tail=jnp.pad(w_in[:, :, IN_MAIN:], ((0, 0), (0, 0), (0, IN_MAIN + LANES - w_in.shape[2]))),
        sink=jnp.broadcast_to(attn_sink[:, :, None], attn_sink.shape + (LANES,)).astype(F32),
        mu=jnp.pad(shift_mu, ((0, 0), (0, RWKV_PAD - RWKV_PROJ)))[:, None, :],
        vecs=rwkv_vecs, rk=rwkv_rk.reshape(depth, 1, R_WIDTH),
        w2=jnp.pad(rwkv_w2, ((0, 0), (0, LANES - DECAY_LORA), (0, 0))).astype(BF16),
        a2=jnp.pad(rwkv_a2, ((0, 0), (DECAY_LORA, LANES - DECAY_LORA - AAA_LORA), (0, 0))).astype(BF16),
        g2=jnp.pad(rwkv_g2, ((0, 0), (0, 2 * LANES - GATE_LORA), (0, 0))).astype(BF16),
        w_mkv=w_mem_kv,
        w_out=w_out, w_mq=w_mem_q, w_mo=w_mem_o, w_gu=w_gate_up, w_dn=w_down)


def _head_sum_matrix():
    i = jnp.arange(2 * LANES)
    return ((i[:, None] // HEAD_DIM) == (i[None, :] // HEAD_DIM)).astype(BF16)


def _layer(x2d, l, b, t, tabs, lm, mk, mv, swa_cache, ls, s0, shift0, sp, hsum, in_tm, rw_tt, post_nb, post_t,
           ffn_tm, late_w=None):
    to_cast = () if late_w else tuple(sp[name] for name in ("w_out", "w_mq", "w_mo", "w_gu", "w_dn"))
    (q, k, v, pr, shift_new), cast = _in_proj(x2d, l, sp["gains"], sp["w_in"], sp["w_tail"], tabs[0], tabs[1], sp["mu"], ls,
                                              shift0, t, in_tm, to_cast)
    late_w = late_w or tuple(cast)
    if swa_cache is None:
        attn = _swa_prompt(q, k, v, l, sp["sink"], b, t)
    else:
        attn = _swa_sample(q, k, v, l, swa_cache[0], swa_cache[1], sp["sink"], b, t)
    pr3 = pr.reshape(b, t, RWKV_PAD)
    rw, s_fin = _rwkv(pr3, l, ls, s0, sp["vecs"], sp["rk"], sp["w2"], sp["a2"], sp["g2"], hsum,
                      rw_tt[0], min(CHUNK, t), rw_tt[1])
    w_out, w_mq, w_mo, w_gu, w_dn = late_w
    x2 = _post(attn, rw.reshape(b * t, R_WIDTH), x2d, l, sp["gains"], w_out, w_mq, w_mo,
               lm, mk, mv, post_nb, post_t, t // post_t)
    x3 = _ffn(x2, l, sp["gains"], w_gu, w_dn, ffn_tm)
    return x3, k, v, s_fin, shift_new[:, :, :RWKV_PROJ], late_w


def kernel(x_prompt, mem_prompt, x_sample, cache_swa_k, cache_swa_v, cache_mem_k, cache_mem_v, state_rwkv,
           state_shift, norm_gains, w_in, attn_sink, shift_mu, rwkv_vecs, rwkv_rk, rwkv_w2, rwkv_a2, rwkv_g2,
           w_out, w_mem_q, w_mem_kv, w_mem_o, w_gate_up, w_down):
    b, t, _ = x_prompt.shape
    bd, tn, _ = x_sample.shape
    depth = w_in.shape[0]
    m_len = mem_prompt.shape[1]
    cache_len = cache_swa_k.shape[2]
    tabs_p = _rope_tables(jnp.arange(t, dtype=jnp.int32))
    cs, sn = _rope_tables(PAST_LEN + jnp.arange(tn, dtype=jnp.int32))
    tabs_s = (jnp.tile(cs, (bd, 1)), jnp.tile(sn, (bd, 1)))
    hsum = _head_sum_matrix()
    sp = _stacked_params(norm_gains, w_in, attn_sink, shift_mu, rwkv_vecs, rwkv_rk, rwkv_w2, rwkv_a2, rwkv_g2,
                         w_out, w_mem_q, w_mem_kv, w_mem_o, w_gate_up, w_down)
    s0_p = jnp.zeros((1, b, N_HEADS_R, HEAD_DIM, HEAD_DIM), F32)
    sh0_p = jnp.zeros((1, b, 1, RWKV_PAD), F32)
    sh0_s = jnp.pad(state_shift, ((0, 0), (0, 0), (0, 0), (0, RWKV_PAD - RWKV_PROJ)))
    swa_cache = (cache_swa_k.reshape(depth, bd, cache_len, KV_WIDTH_A),
                 cache_swa_v.reshape(depth, bd, cache_len, KV_WIDTH_A))
    mem_rows = m_len * MEM_HEADS
    cmk = cache_mem_k.reshape(depth, bd, mem_rows, MEM_HEAD_DIM)
    cmv = cache_mem_v.reshape(depth, bd, mem_rows, MEM_HEAD_DIM)
    xp = x_prompt.reshape(b * t, D_MODEL)
    xs = x_sample.reshape(bd * tn, D_MODEL)
    mem2d = mem_prompt.reshape(b * m_len, D_MODEL)
    in_tm_p = min(512, t)
    rw_tt_p = (min(512, t), 1)
    rw_tt_s = (tn, 16 if bd % 16 == 0 else 1)
    post_t_p = min(1024, t)
    ffn_tm_p = min(512, b * t)
    keep = t - min(WINDOW, t)
    pk, pv, pmk, pmv, pS, psh = [], [], [], [], [], []
    sk, sv, sS, ssh = [], [], [], []
    for l in range(depth):
        mk_l, mv_l = _memkv(mem2d, l, sp["gains"], sp["w_mkv"])
        xp, k_l, v_l, s_l, sh_l, late_w = _layer(xp, l, b, t, tabs_p, 0, mk_l[None], mv_l[None], None, 0, s0_p, sh0_p,
                                                sp, hsum, in_tm_p, rw_tt_p, 1, post_t_p, ffn_tm_p)
        pk.append(k_l.reshape(b, t, KV_WIDTH_A)[:, keep:].reshape(b, t - keep, N_KV_A, HEAD_DIM))
        pv.append(v_l.reshape(b, t, KV_WIDTH_A)[:, keep:].reshape(b, t - keep, N_KV_A, HEAD_DIM))
        pmk.append(mk_l.reshape(b, m_len, MEM_HEADS, MEM_HEAD_DIM))
        pmv.append(mv_l.reshape(b, m_len, MEM_HEADS, MEM_HEAD_DIM))
        pS.append(s_l)
        psh.append(sh_l)

        xs, k2, v2, s2, sh2, _ = _layer(xs, l, bd, tn, tabs_s, l, cmk, cmv, swa_cache, l, state_rwkv, sh0_s, sp,
                                        hsum, bd * tn, rw_tt_s, bd, tn, bd * tn, late_w)
        sk.append(k2.reshape(bd, tn, N_KV_A, HEAD_DIM))
        sv.append(v2.reshape(bd, tn, N_KV_A, HEAD_DIM))
        sS.append(s2)
        ssh.append(sh2)
    return (xp.reshape(b, t, D_MODEL), xs.reshape(bd, tn, D_MODEL), jnp.stack(pk), jnp.stack(pv),
            jnp.stack(pmk), jnp.stack(pmv), jnp.stack(pS), jnp.stack(psh),
            jnp.stack(sk), jnp.stack(sv), jnp.stack(sS), jnp.stack(ssh))
```

```python
import functools
import math

import jax
import jax.numpy as jnp
from jax import lax
from jax.experimental import pallas as pl
from jax.experimental.pallas import tpu as pltpu

F32 = jnp.float32
BF16 = jnp.bfloat16

D_MODEL = 1024
HEAD_DIM = 64
CHUNK = 64
A_WIDTH = 512
KV_WIDTH_A = 128
N_KV_A = 2
WINDOW = 128
PAST_LEN = 4096
ROPE_THETA = 10000.0
R_WIDTH = 512
N_HEADS_R = 8
DECAY_LORA = 64
AAA_LORA = 64
GATE_LORA = 160
RWKV_PROJ = 3 * R_WIDTH + DECAY_LORA + AAA_LORA + GATE_LORA
RWKV_PAD = 1920
IN_COLS_PAD = A_WIDTH + 2 * KV_WIDTH_A + RWKV_PAD
GN_EPS = 6.4e-4
MEM_LEN = 256
MEM_HEADS = 4
MEM_HEAD_DIM = 128
MEM_WIDTH = 512
D_FF = 2816
RMS_EPS = 1e-6
NEG_INF = -1e30
NORM_MIX_PRE, NORM_MIX_POST, NORM_X_PRE, NORM_X_POST, NORM_MEM, NORM_FFN_PRE, NORM_FFN_POST = range(7)
EXP_M05 = math.exp(-0.5)

LANES = 128
IN_MAIN = IN_COLS_PAD - LANES
VMEM_LIMIT = 58 * 1024 * 1024


def _params(sem):
    return pltpu.CompilerParams(dimension_semantics=sem, vmem_limit_bytes=VMEM_LIMIT)


def _rms(x, g):
    ms = jnp.mean(x * x, axis=-1, keepdims=True)
    return x * lax.rsqrt(ms + RMS_EPS) * g


def _mm(a, b):
    return jnp.dot(a.astype(BF16), b.astype(BF16), preferred_element_type=F32)


def _mm_nt(a, b):
    return lax.dot_general(a.astype(BF16), b.astype(BF16), (((1,), (1,)), ((), ())),
                           preferred_element_type=F32)


def _split(x):
    hi = x.astype(BF16)
    lo = (x - hi.astype(F32)).astype(BF16)
    return hi, lo


def _mm_lsplit(a_exact_bf16, x):
    hi, lo = _split(x)
    return (jnp.dot(a_exact_bf16, hi, preferred_element_type=F32)
            + jnp.dot(a_exact_bf16, lo, preferred_element_type=F32))


def _in_kernel(*refs, tm, t_seq, n_cast):
    x_ref, g_ref, w_ref, wt_ref, cos_ref, sin_ref, mu_ref, sh0_ref = refs[:8]
    cast_in = refs[8:8 + n_cast]
    q_ref, k_ref, v_ref, pr_ref, shl_ref = refs[8 + n_cast:13 + n_cast]
    cast_out = refs[13 + n_cast:13 + 2 * n_cast]
    carry_scr, wb_scr, wtb_scr = refs[-3:]
    for src, dst in zip(cast_in, cast_out):
        dst[...] = src[...].astype(BF16)

    @pl.when(pl.program_id(0) == 0)
    def _():
        wb_scr[...] = w_ref[...].astype(BF16)
        wtb_scr[...] = wt_ref[...].astype(BF16)

    h = _rms(x_ref[...], g_ref[NORM_MIX_PRE:NORM_MIX_PRE + 1, :]).astype(BF16)
    p = jnp.concatenate([jnp.dot(h, wb_scr[...], preferred_element_type=F32),
                         jnp.dot(h, wtb_scr[...], preferred_element_type=F32)], axis=1)
    cos = cos_ref[...]
    sin = sin_ref[...]
    lane = lax.broadcasted_iota(jnp.int32, cos.shape, 1)
    first_half = (lane & (HEAD_DIM // 2)) == 0

    def rope(xc):
        sw = jnp.where(first_half, pltpu.roll(xc, LANES - HEAD_DIM // 2, 1), pltpu.roll(xc, HEAD_DIM // 2, 1))
        return xc * cos + sw * sin

    for j in range(A_WIDTH // LANES):
        q_ref[:, j * LANES:(j + 1) * LANES] = rope(p[:, j * LANES:(j + 1) * LANES])
    k_ref[...] = rope(p[:, A_WIDTH:A_WIDTH + KV_WIDTH_A])
    v_ref[...] = p[:, A_WIDTH + KV_WIDTH_A:A_WIDTH + 2 * KV_WIDTH_A]
    raw = p[:, A_WIDTH + 2 * KV_WIDTH_A:]
    rowi = lax.broadcasted_iota(jnp.int32, (tm, 1), 0)
    prev = pltpu.roll(raw, 1, 0)
    if tm <= t_seq:
        first = (pl.program_id(0) % (t_seq // tm)) == 0
        prev = jnp.where(rowi == 0, jnp.where(first, sh0_ref[0], carry_scr[...]), prev)
        carry_scr[...] = raw[tm - 1:tm, :]
        shl_ref[0] = raw[tm - 1:tm, :]
    else:
        for s in range(tm // t_seq):
            prev = jnp.where(rowi == s * t_seq, sh0_ref[s], prev)
            shl_ref[s] = raw[(s + 1) * t_seq - 1:(s + 1) * t_seq, :]
    pr_ref[...] = raw + (prev - raw) * mu_ref[...]


def _layer_block(shape, l):
    nd = len(shape)
    return pl.BlockSpec((None,) + tuple(shape), lambda *_: (l,) + (0,) * nd)


def _in_proj(x2d, l, gains, w_in, w_tail, cos_t, sin_t, mu_p, ls, shift0, t_seq, tm, casts=()):
    n = x2d.shape[0]
    n_seq = n // t_seq
    steps = n // tm
    tab_blocks = cos_t.shape[0] // tm
    row = lambda i: (i, 0)
    tab = lambda i: (i % tab_blocks, 0)
    if tm <= t_seq:
        seqs, seq_of = 1, lambda i: i // (t_seq // tm)
    else:
        seqs, seq_of = tm // t_seq, lambda i: i
    cast_in, cast_out, cast_shapes = [], [], []
    for w in casts:
        rows = w.shape[1] // steps
        cast_in.append(pl.BlockSpec((None, rows, w.shape[2]), lambda i: (l, i, 0)))
        cast_out.append(pl.BlockSpec((None, rows, w.shape[2]), lambda i: (0, i, 0)))
        cast_shapes.append(jax.ShapeDtypeStruct((1,) + w.shape[1:], BF16))
    outs = pl.pallas_call(
        functools.partial(_in_kernel, tm=tm, t_seq=t_seq, n_cast=len(casts)),
        grid=(steps,),
        in_specs=[pl.BlockSpec((tm, D_MODEL), row), _layer_block((8, D_MODEL), l),
                  pl.BlockSpec((None, D_MODEL, IN_MAIN), lambda i: (l, 0, 0), pipeline_mode=pl.Buffered(1)),
                  _layer_block((D_MODEL, LANES), l),
                  pl.BlockSpec((tm, LANES), tab), pl.BlockSpec((tm, LANES), tab),
                  _layer_block((1, RWKV_PAD), l),
                  pl.BlockSpec((None, seqs, 1, RWKV_PAD), lambda i: (ls, seq_of(i), 0, 0))] + cast_in,
        out_specs=[pl.BlockSpec((tm, A_WIDTH), row), pl.BlockSpec((tm, KV_WIDTH_A), row),
                   pl.BlockSpec((tm, KV_WIDTH_A), row), pl.BlockSpec((tm, RWKV_PAD), row),
                   pl.BlockSpec((seqs, 1, RWKV_PAD), lambda i: (seq_of(i), 0, 0))] + cast_out,
        out_shape=[jax.ShapeDtypeStruct((n, A_WIDTH), F32), jax.ShapeDtypeStruct((n, KV_WIDTH_A), F32),
                   jax.ShapeDtypeStruct((n, KV_WIDTH_A), F32), jax.ShapeDtypeStruct((n, RWKV_PAD), F32),
                   jax.ShapeDtypeStruct((n_seq, 1, RWKV_PAD), F32)] + cast_shapes,
        scratch_shapes=[pltpu.VMEM((1, RWKV_PAD), F32), pltpu.VMEM((D_MODEL, IN_MAIN), BF16),
                        pltpu.VMEM((D_MODEL, LANES), BF16)],
        compiler_params=_params(("arbitrary",)),
        name="in_proj",
    )(x2d, gains, w_in, w_tail, cos_t, sin_t, mu_p, shift0, *casts)
    return outs[:5], outs[5:]


def _sink_attend(jobs, sink_ref, t):
    nk = jobs[0][1].shape[0]
    nkp = 2 * LANES
    lane = lax.broadcasted_iota(jnp.int32, (t, LANES), 1)
    m0 = lane < HEAD_DIM
    coli = lax.broadcasted_iota(jnp.int32, (1, nkp), 1)
    fills = []
    for kv in range(N_KV_A):
        blocks = []
        for g in range(4):
            sg = sink_ref[4 * kv + g:4 * kv + g + 1, :]
            blocks.append(jnp.broadcast_to(jnp.where(coli == nk, jnp.concatenate([sg, sg], axis=1), NEG_INF),
                                           (t, nkp)))
        fills.append(jnp.concatenate(blocks, axis=0))
    k_tail = jnp.zeros((nkp - nk, LANES), F32)
    tail_row = lax.broadcasted_iota(jnp.int32, (nkp - nk, nkp), 0)
    tail_col = lax.broadcasted_iota(jnp.int32, (nkp - nk, nkp), 1)
    v_tail = jnp.where((tail_row == 0) & (tail_col >= LANES), 1.0, 0.0).astype(F32)
    ones = jnp.ones((nk, LANES), F32)
    scores = []
    for q_rows, kdup, _, kv, valid in jobs:
        parts = []
        for p in (2 * kv, 2 * kv + 1):
            qp = q_rows[:, p * LANES:(p + 1) * LANES]
            parts.append(jnp.where(m0, qp, 0.0))
            parts.append(jnp.where(m0, 0.0, qp))
        lhs = jnp.concatenate(parts, axis=0)
        s = _mm_nt(lhs, jnp.concatenate([kdup, k_tail], axis=0))
        keep = coli < nk
        if valid is not None:
            keep = keep & valid
        scores.append(jnp.where(keep, s, fills[kv]))
    exps = [jnp.exp(s - jnp.max(s, axis=-1, keepdims=True)) for s in scores]
    outs = []
    for (_, _, vdup, _, _), e in zip(jobs, exps):
        v2 = jnp.concatenate([jnp.concatenate([vdup, ones], axis=1), v_tail], axis=0)
        o2 = _mm(e, v2)
        o = o2[:, :LANES] / o2[:, LANES:]
        outs.append([jnp.where(m0, o[(2 * pi) * t:(2 * pi + 1) * t], o[(2 * pi + 1) * t:(2 * pi + 2) * t])
                     for pi in range(2)])
    return outs


SWA_SCALE = HEAD_DIM ** -0.5


def _dup_heads(x):
    lane = lax.broadcasted_iota(jnp.int32, x.shape, 1)
    m0 = lane < HEAD_DIM
    xs = pltpu.roll(x, HEAD_DIM, 1)
    return [jnp.where(m0, x, xs), jnp.where(m0, xs, x)]


SWA_TQ = 512


def _swa_prompt_kernel(q_ref, kp_ref, kc_ref, vp_ref, vc_ref, sink_ref, o_ref):
    i = pl.program_id(1)
    k = jnp.concatenate([kp_ref[...], kc_ref[...]], axis=0)
    v = jnp.concatenate([vp_ref[...], vc_ref[...]], axis=0)
    kd = _dup_heads(k * SWA_SCALE)
    vd = _dup_heads(v)
    nk = 3 * CHUNK
    slot = lax.broadcasted_iota(jnp.int32, (1, 2 * LANES), 1) // CHUNK
    jobs = []
    for j in range(SWA_TQ // CHUNK):
        qj = q_ref[j * CHUNK:(j + 1) * CHUNK, :]
        valid = (slot + (i * (SWA_TQ // CHUNK) + j - 2)) >= 0
        for kv in range(N_KV_A):
            jobs.append((qj, kd[kv][j * CHUNK:j * CHUNK + nk], vd[kv][j * CHUNK:j * CHUNK + nk], kv, valid))
    outs = _sink_attend(jobs, sink_ref, CHUNK)
    for n, out in enumerate(outs):
        j, kv = divmod(n, N_KV_A)
        for pi in range(2):
            p = 2 * kv + pi
            o_ref[j * CHUNK:(j + 1) * CHUNK, p * LANES:(p + 1) * LANES] = out[pi]


def _swa_prompt(q, k, v, l, sink_b, b, t):
    n = b * t
    nq = t // SWA_TQ
    qmap = lambda bi, i: (bi * nq + i, 0)
    pmap = lambda bi, i: (jnp.maximum(bi * (t // WINDOW) + (SWA_TQ // WINDOW) * i - 1, 0), 0)
    return pl.pallas_call(
        _swa_prompt_kernel,
        grid=(b, nq),
        in_specs=[pl.BlockSpec((SWA_TQ, A_WIDTH), qmap),
                  pl.BlockSpec((WINDOW, KV_WIDTH_A), pmap), pl.BlockSpec((SWA_TQ, KV_WIDTH_A), qmap),
                  pl.BlockSpec((WINDOW, KV_WIDTH_A), pmap), pl.BlockSpec((SWA_TQ, KV_WIDTH_A), qmap),
                  _layer_block((8, LANES), l)],
        out_specs=pl.BlockSpec((SWA_TQ, A_WIDTH), qmap),
        out_shape=jax.ShapeDtypeStruct((n, A_WIDTH), F32),
        compiler_params=_params(("parallel", "parallel")),
        name="swa_prompt",
    )(q, k, k, v, v, sink_b)


def _swa_sample_kernel(q_ref, kc_ref, kn_ref, vc_ref, vn_ref, sink_ref, o_ref, *, t, nb):
    jobs = []
    for bi in range(nb):
        rows = slice(bi * t, (bi + 1) * t)
        kd = _dup_heads(jnp.concatenate([kc_ref[bi], kn_ref[rows, :]], axis=0) * SWA_SCALE)
        vd = _dup_heads(jnp.concatenate([vc_ref[bi], vn_ref[rows, :]], axis=0))
        for kv in range(N_KV_A):
            jobs.append((q_ref[rows, :], kd[kv], vd[kv], kv, None))
    outs = _sink_attend(jobs, sink_ref, t)
    for n, out in enumerate(outs):
        bi, kv = divmod(n, N_KV_A)
        for pi in range(2):
            p = 2 * kv + pi
            o_ref[bi * t:(bi + 1) * t, p * LANES:(p + 1) * LANES] = out[pi]


def _swa_sample(q, k, v, l, kc, vc, sink_b, b, t):
    n = b * t
    cache = kc.shape[2]
    nb = 8 if b % 8 == 0 else 1
    row = lambda bi: (bi, 0)
    cmap = lambda bi: (l, bi, 0, 0)
    return pl.pallas_call(
        functools.partial(_swa_sample_kernel, t=t, nb=nb),
        grid=(b // nb,),
        in_specs=[pl.BlockSpec((nb * t, A_WIDTH), row),
                  pl.BlockSpec((None, nb, cache, KV_WIDTH_A), cmap), pl.BlockSpec((nb * t, KV_WIDTH_A), row),
                  pl.BlockSpec((None, nb, cache, KV_WIDTH_A), cmap), pl.BlockSpec((nb * t, KV_WIDTH_A), row),
                  _layer_block((8, LANES), l)],
        out_specs=pl.BlockSpec((nb * t, A_WIDTH), row),
        out_shape=jax.ShapeDtypeStruct((n, A_WIDTH), F32),
        compiler_params=_params(("parallel",)),
        name="swa_sample",
    )(q, kc, k, vc, v, sink_b)


RWKV_SUB = 256


def _interleave(gens):
    gens = list(gens)
    while gens:
        for g in list(gens):
            try:
                next(g)
            except StopIteration:
                gens.remove(g)


def _rwkv_kernel(pr_ref, s0_ref, vec_ref, rk_ref, w2_ref, a2_ref, g2_ref, hsum_ref,
                 ltri_ref, out_ref, sfin_ref,
                 s_scr, rt_s, at_s, bt_s, kt_s, bb_s, kb_s, v_s, et_s, y_s, bonus_s, gate_s,
                 *, c_len, tt, nb, sub):
    i = pl.program_id(1)
    n_i = pl.num_programs(1)
    n_rows = nb * tt

    @pl.when(i == 0)
    def _():
        zero = jnp.zeros((HEAD_DIM, HEAD_DIM), F32)
        for bi in range(nb):
            for p in range(4):
                s_scr[bi, p] = jnp.concatenate(
                    [jnp.concatenate([s0_ref[bi, 2 * p].astype(F32), zero], axis=1),
                     jnp.concatenate([zero, s0_ref[bi, 2 * p + 1].astype(F32)], axis=1)], axis=0)

    w0 = vec_ref[0:1, :]
    a0 = vec_ref[1:2, :]
    k_k = vec_ref[2:3, :]
    k_a = vec_ref[3:4, :]
    gn_g = vec_ref[4:5, :]
    gn_b = vec_ref[5:6, :]
    hsum = hsum_ref[...]
    hw = hsum.shape[0]
    halves = [slice(j * hw, (j + 1) * hw) for j in range(R_WIDTH // hw)]

    def head_sum(x):
        return _mm(x, hsum)

    n2 = 2 * c_len
    ti = lax.broadcasted_iota(jnp.int32, (c_len, n2), 0)
    lane2 = lax.broadcasted_iota(jnp.int32, (c_len, n2), 1)
    si = lane2 & (c_len - 1)
    first_c = lane2 < c_len
    strict = ti > si
    incl = ti >= si
    eye = jnp.where(ti == si, 1.0, 0.0).astype(F32)
    masks = []
    half = 1
    while half < c_len:
        blk = 2 * half
        masks.append(((ti & ~(blk - 1)) == (si & ~(blk - 1))) & ((ti & half) != 0) & ((si & half) == 0))
        half = blk
    m0 = lax.broadcasted_iota(jnp.int32, (c_len, LANES), 1) < HEAD_DIM
    rl = lax.broadcasted_iota(jnp.int32, (LANES, LANES), 0)
    cl = lax.broadcasted_iota(jnp.int32, (LANES, LANES), 1)
    same_head = (rl >= HEAD_DIM) == (cl >= HEAD_DIM)
    fused = n2 == LANES
    zeros_c = jnp.zeros((c_len, LANES), F32)

    def bd(x):
        return jnp.concatenate([jnp.where(m0, x, 0.0), jnp.where(m0, 0.0, x)], axis=0)

    def bd_t(x):
        return jnp.concatenate([jnp.where(first_c, x, 0.0), jnp.where(first_c, 0.0, x)], axis=0)

    def prologue(lo, hi):
        n = hi - lo
        bi0 = lo // tt

        def shifted(a, b):
            if hi - lo <= tt:
                return pr_ref[bi0, lo - bi0 * tt:hi - bi0 * tt, a:b]
            return pr_ref[bi0:hi // tt, :, a:b].reshape(n, b - a)

        xwa = shifted(3 * R_WIDTH, 3 * R_WIDTH + LANES)
        xg = shifted(3 * R_WIDTH + LANES, RWKV_PAD)
        th = jnp.tanh(xwa)
        sg = jax.nn.sigmoid(xg)
        ltri = ltri_ref[...]
        yield
        for cs in halves:
            r = shifted(cs.start, cs.stop)
            k = shifted(R_WIDTH + cs.start, R_WIDTH + cs.stop)
            v = shifted(2 * R_WIDTH + cs.start, 2 * R_WIDTH + cs.stop)
            yield
            z = w0[:, cs] + _mm(th, w2_ref[:, cs])
            wlog = -EXP_M05 * jax.nn.sigmoid(z)
            a = jax.nn.sigmoid(a0[:, cs] + _mm(xwa, a2_ref[:, cs]))
            gate_s[lo:hi, cs] = _mm(sg, g2_ref[:, cs])
            yield
            kk = k * k_k[:, cs]
            kk = kk * lax.rsqrt(jnp.maximum(head_sum(kk * kk), 1e-24))
            k_f = k * (1.0 + (a - 1.0) * k_a[:, cs])
            bvec = kk * a
            bonus_s[lo:hi, cs] = head_sum(r * k_f * rk_ref[:, cs]) * v
            yield
            cum = _mm_lsplit(ltri, wlog)
            tot = jnp.concatenate([jnp.broadcast_to(cum[r1 - 1:r1, :], (c_len, hw))
                                   for r1 in range(c_len, n + 1, c_len)], axis=0)
            e_in = jnp.exp(cum)
            e_inv = jnp.exp(-cum)
            e_end = jnp.exp(tot - cum)
            rt_s[lo:hi, cs] = r * e_in
            at_s[lo:hi, cs] = -kk * jnp.exp(cum - wlog)
            yield
            bt_s[lo:hi, cs] = bvec * e_inv
            kt_s[lo:hi, cs] = k_f * e_inv
            bb_s[lo:hi, cs] = bvec * e_end
            kb_s[lo:hi, cs] = k_f * e_end
            v_s[lo:hi, cs] = v
            et_s[lo:hi, cs] = jnp.exp(tot)
            yield

    s_state = [[s_scr[bi, p] for p in range(4)] for bi in range(nb)]

    def chains(lo, hi):
        chs = [(r0, p) for r0 in range(lo, hi, c_len) for p in range(4)]

        def cat(ref, ch):
            return ref[ch[0]:ch[0] + c_len, ch[1] * LANES:(ch[1] + 1) * LANES]

        a_ab, a_ak, a_rb, a_rk = {}, {}, {}, {}
        for ch in chs:
            at, rt, rb, rkt = cat(at_s, ch), cat(rt_s, ch), bd(cat(bt_s, ch)), bd(cat(kt_s, ch))
            if fused:
                amat = _mm_nt(jnp.concatenate([at, rt], axis=0), jnp.concatenate([rb, rkt], axis=0))
                q_ab, q_ak = amat[:c_len, :n2], amat[:c_len, n2:]
                q_rb, q_rk = amat[c_len:, :n2], amat[c_len:, n2:]
            else:
                q_ab, q_ak, q_rb, q_rk = _mm_nt(at, rb), _mm_nt(at, rkt), _mm_nt(rt, rb), _mm_nt(rt, rkt)
            a_ab[ch] = jnp.where(strict, q_ab, 0.0)
            a_ak[ch] = jnp.where(strict, q_ak, 0.0)
            a_rb[ch] = jnp.where(incl, q_rb, 0.0)
            a_rk[ch] = jnp.where(incl, q_rk, 0.0)
        yield
        tinv = {ch: eye + jnp.where(masks[0], a_ab[ch], 0.0) for ch in chs}
        for m in masks[1:]:
            pe = {ch: _mm(tinv[ch], bd_t(jnp.where(m, a_ab[ch], 0.0))) for ch in chs}
            yield
            tinv = {ch: tinv[ch] + _mm(pe[ch], bd_t(tinv[ch])) for ch in chs}
            yield
        zv = {ch: _mm(a_ak[ch], bd(cat(v_s, ch))) for ch in chs}
        yield
        r_hat, y_hat, m_mat, g_mat = {}, {}, {}, {}
        if fused:
            w = {ch: _mm(tinv[ch], jnp.concatenate([bd(cat(at_s, ch)), bd(zv[ch])], axis=1)) for ch in chs}
            yield
            for ch in chs:
                rhs2 = jnp.concatenate(
                    [jnp.concatenate([bd(w[ch][:, :LANES]), bd(w[ch][:, LANES:])], axis=1),
                     jnp.concatenate([jnp.zeros((n2, LANES), F32), bd(cat(v_s, ch))], axis=1)], axis=0)
                ry = _mm(jnp.concatenate([a_rb[ch], a_rk[ch]], axis=1), rhs2)
                r_hat[ch] = cat(rt_s, ch) + ry[:, :LANES]
                y_hat[ch] = ry[:, LANES:]
            yield
            for ch in chs:
                lhs_t = jnp.concatenate([w[ch], jnp.concatenate([zeros_c, cat(v_s, ch)], axis=1)], axis=0)
                mg = _mm(lhs_t.T, jnp.concatenate([cat(bb_s, ch), cat(kb_s, ch)], axis=0))
                m_mat[ch] = jnp.where(same_head, mg[:LANES], 0.0)
                g_mat[ch] = jnp.where(same_head, mg[LANES:], 0.0)
            yield
        else:
            a_hat = {ch: _mm(tinv[ch], bd(cat(at_s, ch))) for ch in chs}
            u_hat = {ch: _mm(tinv[ch], bd(zv[ch])) for ch in chs}
            yield
            for ch in chs:
                r_hat[ch] = cat(rt_s, ch) + _mm(a_rb[ch], bd(a_hat[ch]))
                y_hat[ch] = _mm(a_rb[ch], bd(u_hat[ch])) + _mm(a_rk[ch], bd(cat(v_s, ch)))
            yield
            for ch in chs:
                m_mat[ch] = jnp.where(same_head, _mm(a_hat[ch].T, cat(bb_s, ch)), 0.0)
                g_mat[ch] = jnp.where(same_head,
                                      _mm(u_hat[ch].T, cat(bb_s, ch)) + _mm(cat(v_s, ch).T, cat(kb_s, ch)), 0.0)
            yield
        for r0 in range(lo, hi, c_len):
            s_cur = s_state[r0 // tt]
            for p in range(4):
                ch = (r0, p)
                ls = slice(p * LANES, (p + 1) * LANES)
                e_last = et_s[r0:r0 + 1, ls]
                y_s[r0:r0 + c_len, ls] = y_hat[ch] + _mm_nt(r_hat[ch], s_cur[p])
                s_cur[p] = s_cur[p] * e_last + _mm(s_cur[p], m_mat[ch]) + g_mat[ch]
            yield

    def epilogue(lo, hi):
        for cs in halves:
            y = y_s[lo:hi, cs]
            mean = head_sum(y) * (1.0 / HEAD_DIM)
            yc = y - mean
            yield
            var = head_sum(yc * yc) * (1.0 / HEAD_DIM)
            yn = yc * lax.rsqrt(var + GN_EPS) * gn_g[:, cs] + gn_b[:, cs]
            res = (yn + bonus_s[lo:hi, cs]) * gate_s[lo:hi, cs]
            for r0 in range(lo, hi, tt) if hi - lo > tt else [lo]:
                r1 = min(r0 + tt, hi)
                out_ref[r0 // tt, r0 % tt:r0 % tt + (r1 - r0), cs] = res[r0 - lo:r1 - lo]
            yield

    subs = [(lo, lo + sub) for lo in range(0, n_rows, sub)]
    _interleave([prologue(*subs[0])])
    for n, sb in enumerate(subs):
        phases = [chains(*sb)]
        if n + 1 < len(subs):
            phases.append(prologue(*subs[n + 1]))
        if n > 0:
            phases.append(epilogue(*subs[n - 1]))
        _interleave(phases)
    _interleave([epilogue(*subs[-1])])

    for bi in range(nb):
        for p in range(4):
            s_scr[bi, p] = s_state[bi][p]

    @pl.when(i == n_i - 1)
    def _():
        for bi in range(nb):
            for p in range(4):
                s_pair = s_state[bi][p]
                sfin_ref[bi, 2 * p] = s_pair[:HEAD_DIM, :HEAD_DIM].astype(sfin_ref.dtype)
                sfin_ref[bi, 2 * p + 1] = s_pair[HEAD_DIM:, HEAD_DIM:].astype(sfin_ref.dtype)


def _rwkv(pr3, l, ls, s0, vecs, rk_flat, w2p, a2p, g2p, hsum, tt, c_len, nb):
    b, t, _ = pr3.shape
    n_rows = nb * tt
    sub = min(RWKV_SUB, n_rows)
    ri = jnp.arange(sub)[:, None]
    ci = jnp.arange(sub)[None, :]
    same = (ri // c_len) == (ci // c_len)
    ltri = (same & (ri >= ci)).astype(BF16)
    hw = hsum.shape[0]
    blk = lambda bi, i: (bi, i, 0)
    const = lambda bi, i: (0, 0)
    big = lambda: pltpu.VMEM((n_rows, R_WIDTH), F32)
    return pl.pallas_call(
        functools.partial(_rwkv_kernel, c_len=c_len, tt=tt, nb=nb, sub=sub),
        grid=(b // nb, t // tt),
        in_specs=[pl.BlockSpec((nb, tt, RWKV_PAD), blk),
                  pl.BlockSpec((None, nb, N_HEADS_R, HEAD_DIM, HEAD_DIM), lambda bi, i: (ls, bi, 0, 0, 0)),
                  _layer_block((6, R_WIDTH), l), _layer_block((1, R_WIDTH), l),
                  _layer_block((LANES, R_WIDTH), l), _layer_block((LANES, R_WIDTH), l),
                  _layer_block((2 * LANES, R_WIDTH), l), pl.BlockSpec((hw, hw), const),
                  pl.BlockSpec((sub, sub), const)],
        out_specs=[pl.BlockSpec((nb, tt, R_WIDTH), blk),
                   pl.BlockSpec((nb, N_HEADS_R, HEAD_DIM, HEAD_DIM), lambda bi, i: (bi, 0, 0, 0))],
        out_shape=[jax.ShapeDtypeStruct((b, t, R_WIDTH), F32),
                   jax.ShapeDtypeStruct((b, N_HEADS_R, HEAD_DIM, HEAD_DIM), s0.dtype)],
        scratch_shapes=[pltpu.VMEM((nb, 4, LANES, LANES), F32),
                        big(), big(), big(), big(), big(), big(), big(), big(), big(), big(), big()],
        compiler_params=_params(("parallel", "arbitrary")),
        name="rwkv_mix",
    )(pr3, s0, vecs, rk_flat, w2p, a2p, g2p, hsum, ltri)


POST_PARTS = 2


def _post_kernel(attn_ref, rw_ref, x_ref, g_ref, wout_ref, wq_ref, wo_ref, mk_ref, mv_ref, o_ref, *, nb, t):
    scale = MEM_HEAD_DIM ** -0.5
    cols = lambda hd: slice(hd * MEM_HEAD_DIM, (hd + 1) * MEM_HEAD_DIM)
    head_rows = lambda hd: pl.ds(hd, MEM_LEN, stride=MEM_HEADS)
    ones = jnp.ones((MEM_LEN, MEM_HEAD_DIM), F32)

    def part(lo, hi):
        rows = slice(lo, hi)
        m = _mm(attn_ref[rows, :], wout_ref[0:A_WIDTH, :]) + _mm(rw_ref[rows, :], wout_ref[A_WIDTH:, :])
        yield
        x1 = x_ref[rows, :] + _rms(m, g_ref[NORM_MIX_POST:NORM_MIX_POST + 1, :])
        hq = _rms(x1, g_ref[NORM_X_PRE:NORM_X_PRE + 1, :])
        yield
        q = _mm(hq, wq_ref[...])
        yield
        if nb == 1:
            jobs = [(0, lo, hi, hd) for hd in range(MEM_HEADS)]
        else:
            jobs = [(r0 // t, r0, r0 + t, hd) for r0 in range(lo, hi, t) for hd in range(MEM_HEADS)]
        scores = [_mm_nt(q[r0 - lo:r1 - lo, cols(hd)], mk_ref[bi, head_rows(hd), :]) * scale
                  for bi, r0, r1, hd in jobs]
        yield
        exps = [jnp.exp(s - jnp.max(s, axis=-1, keepdims=True)) for s in scores]
        yield
        outs = []
        for (bi, _, _, hd), e in zip(jobs, exps):
            o2 = _mm(e, jnp.concatenate([mv_ref[bi, head_rows(hd), :], ones], axis=1))
            outs.append(o2[:, :MEM_HEAD_DIM] / o2[:, MEM_HEAD_DIM:])
        blocks = [jnp.concatenate(outs[j:j + MEM_HEADS], axis=1) for j in range(0, len(outs), MEM_HEADS)]
        o = blocks[0] if len(blocks) == 1 else jnp.concatenate(blocks, axis=0)
        yield
        c = _mm(o, wo_ref[...])
        yield
        o_ref[rows, :] = x1 + _rms(c, g_ref[NORM_X_POST:NORM_X_POST + 1, :])

    n_rows = nb * t
    n_parts = POST_PARTS if (nb == 1 or nb % POST_PARTS == 0) and n_rows % (8 * POST_PARTS) == 0 else 1
    step = n_rows // n_parts
    gens = [part(j * step, (j + 1) * step) for j in range(n_parts)]
    for j, gen in enumerate(gens):
        for _ in range(n_parts - 1 - j):
            next(gen)
    _interleave(gens)


def _post(attn, rw, x2d, l, gains, w_out, w_mq, w_mo, lm, mk, mv, nb, t, tiles_per_batch):
    n = x2d.shape[0]
    tm = nb * t
    row = lambda i: (i, 0)
    if nb == 1:
        mmap = lambda i: (lm, i // tiles_per_batch, 0, 0)
    else:
        mmap = lambda i: (lm, i, 0, 0)
    mem_rows = MEM_LEN * MEM_HEADS
    return pl.pallas_call(
        functools.partial(_post_kernel, nb=nb, t=t),
        grid=(n // tm,),
        in_specs=[pl.BlockSpec((tm, A_WIDTH), row), pl.BlockSpec((tm, R_WIDTH), row),
                  pl.BlockSpec((tm, D_MODEL), row), _layer_block((8, D_MODEL), l),
                  _layer_block((D_MODEL, D_MODEL), 0), _layer_block((D_MODEL, MEM_WIDTH), 0),
                  _layer_block((MEM_WIDTH, D_MODEL), 0),
                  pl.BlockSpec((None, nb, mem_rows, MEM_HEAD_DIM), mmap),
                  pl.BlockSpec((None, nb, mem_rows, MEM_HEAD_DIM), mmap)],
        out_specs=pl.BlockSpec((tm, D_MODEL), row),
        out_shape=jax.ShapeDtypeStruct((n, D_MODEL), F32),
        compiler_params=_params(("parallel",)),
        name="post_mix",
    )(attn, rw, x2d, gains, w_out, w_mq, w_mo, mk, mv)


FFN_CHUNK = 256


def _ffn_kernel(x_ref, g_ref, wgu_ref, wd_ref, o_ref):
    x = x_ref[...]
    h = _rms(x, g_ref[NORM_FFN_PRE:NORM_FFN_PRE + 1, :]).astype(BF16)
    acc = None
    for j in range(D_FF // FFN_CHUNK):
        cols = slice(j * FFN_CHUNK, (j + 1) * FFN_CHUNK)
        gate = jnp.dot(h, wgu_ref[:, cols], preferred_element_type=F32)
        up = jnp.dot(h, wgu_ref[:, D_FF + j * FFN_CHUNK:D_FF + (j + 1) * FFN_CHUNK], preferred_element_type=F32)
        act = (gate * jax.nn.sigmoid(gate)) * up
        part = _mm(act, wd_ref[cols, :])
        acc = part if acc is None else acc + part
    o_ref[...] = x + _rms(acc, g_ref[NORM_FFN_POST:NORM_FFN_POST + 1, :])


def _ffn(x2d, l, gains, w_gu, w_dn, tm):
    n = x2d.shape[0]
    row = lambda i: (i, 0)
    resident = dict(pipeline_mode=pl.Buffered(1))
    return pl.pallas_call(
        _ffn_kernel,
        grid=(n // tm,),
        in_specs=[pl.BlockSpec((tm, D_MODEL), row), _layer_block((8, D_MODEL), l),
                  pl.BlockSpec((None, D_MODEL, 2 * D_FF), lambda i: (0, 0, 0), **resident),
                  pl.BlockSpec((None, D_FF, D_MODEL), lambda i: (0, 0, 0), **resident)],
        out_specs=pl.BlockSpec((tm, D_MODEL), row),
        out_shape=jax.ShapeDtypeStruct((n, D_MODEL), F32),
        compiler_params=_params(("parallel",)),
        name="ffn",
    )(x2d, gains, w_gu, w_dn)


def _memkv_kernel(x_ref, g_ref, w_ref, k_ref, v_ref):
    kv = _mm(_rms(x_ref[...], g_ref[NORM_MEM:NORM_MEM + 1, :]), w_ref[...])
    for hd in range(MEM_HEADS):
        rows = pl.ds(hd, MEM_LEN, stride=MEM_HEADS)
        k_ref[0, rows, :] = kv[:, hd * MEM_HEAD_DIM:(hd + 1) * MEM_HEAD_DIM]
        v_ref[0, rows, :] = kv[:, MEM_WIDTH + hd * MEM_HEAD_DIM:MEM_WIDTH + (hd + 1) * MEM_HEAD_DIM]


def _memkv(mem2d, l, gains, w_kv):
    n = mem2d.shape[0]
    nbatch = n // MEM_LEN
    mem_rows = MEM_LEN * MEM_HEADS
    out = jax.ShapeDtypeStruct((nbatch, mem_rows, MEM_HEAD_DIM), F32)
    ospec = pl.BlockSpec((1, mem_rows, MEM_HEAD_DIM), lambda i: (i, 0, 0))
    return pl.pallas_call(
        _memkv_kernel,
        grid=(nbatch,),
        in_specs=[pl.BlockSpec((MEM_LEN, D_MODEL), lambda i: (i, 0)), _layer_block((8, D_MODEL), l),
                  _layer_block((D_MODEL, 2 * MEM_WIDTH), l)],
        out_specs=[ospec, ospec],
        out_shape=[out, out],
        compiler_params=_params(("parallel",)),
        name="mem_kv",
    )(mem2d, gains, w_kv)


def _rope_tables(pos):
    half = HEAD_DIM // 2
    inv = ROPE_THETA ** (-jnp.arange(half, dtype=F32) / half)
    ang = pos.astype(F32)[:, None] * inv[None, :]
    cos = jnp.cos(ang)
    sin = jnp.sin(ang)
    cos_t = jnp.tile(cos, (1, LANES // half))
    sin_t = jnp.tile(jnp.concatenate([-sin, sin], axis=1), (1, LANES // HEAD_DIM))
    return cos_t, sin_t


def _stacked_params(norm_gains, w_in, attn_sink, shift_mu, rwkv_vecs, rwkv_rk, rwkv_w2, rwkv_a2, rwkv_g2,
                    w_out, w_mem_q, w_mem_kv, w_mem_o, w_gate_up, w_down):
    depth = w_in.shape[0]
    return dict(
        gains=jnp.pad(norm_gains, ((0, 0), (0, 8 - norm_gains.shape[1]), (0, 0))),
        w_in=w_in,
        w_tail=jnp.pad(w_in[:, :, IN_MAIN:], ((0, 0), (0, 0), (0, IN_MAIN + LANES - w_in.shape[2]))),
        sink=jnp.broadcast_to(attn_sink[:, :, None], attn_sink.shape + (LANES,)).astype(F32),
        mu=jnp.pad(shift_mu, ((0, 0), (0, RWKV_PAD - RWKV_PROJ)))[:, None, :],
        vecs=rwkv_vecs, rk=rwkv_rk.reshape(depth, 1, R_WIDTH),
        w2=jnp.pad(rwkv_w2, ((0, 0), (0, LANES - DECAY_LORA), (0, 0))).astype(BF16),
        a2=jnp.pad(rwkv_a2, ((0, 0), (DECAY_LORA, LANES - DECAY_LORA - AAA_LORA), (0, 0))).astype(BF16),
        g2=jnp.pad(rwkv_g2, ((0, 0), (0, 2 * LANES - GATE_LORA), (0, 0))).astype(BF16),
        w_mkv=w_mem_kv,
        w_out=w_out, w_mq=w_mem_q, w_mo=w_mem_o, w_gu=w_gate_up, w_dn=w_down)


def _head_sum_matrix():
    i = jnp.arange(2 * LANES)
    return ((i[:, None] // HEAD_DIM) == (i[None, :] // HEAD_DIM)).astype(BF16)


def _layer(x2d, l, b, t, tabs, lm, mk, mv, swa_cache, ls, s0, shift0, sp, hsum, in_tm, rw_tt, post_nb, post_t,
           ffn_tm, late_w=None):
    to_cast = () if late_w else tuple(sp[name] for name in ("w_out", "w_mq", "w_mo", "w_gu", "w_dn"))
    (q, k, v, pr, shift_new), cast = _in_proj(x2d, l, sp["gains"], sp["w_in"], sp["w_tail"], tabs[0], tabs[1], sp["mu"], ls,
                                              shift0, t, in_tm, to_cast)
    late_w = late_w or tuple(cast)
    if swa_cache is None:
        attn = _swa_prompt(q, k, v, l, sp["sink"], b, t)
    else:
        attn = _swa_sample(q, k, v, l, swa_cache[0], swa_cache[1], sp["sink"], b, t)
    pr3 = pr.reshape(b, t, RWKV_PAD)
    rw, s_fin = _rwkv(pr3, l, ls, s0, sp["vecs"], sp["rk"], sp["w2"], sp["a2"], sp["g2"], hsum,
                      rw_tt[0], min(CHUNK, t), rw_tt[1])
    w_out, w_mq, w_mo, w_gu, w_dn = late_w
    x2 = _post(attn, rw.reshape(b * t, R_WIDTH), x2d, l, sp["gains"], w_out, w_mq, w_mo,
               lm, mk, mv, post_nb, post_t, t // post_t)
    x3 = _ffn(x2, l, sp["gains"], w_gu, w_dn, ffn_tm)
    return x3, k, v, s_fin, shift_new[:, :, :RWKV_PROJ], late_w


def kernel(x_prompt, mem_prompt, x_sample, cache_swa_k, cache_swa_v, cache_mem_k, cache_mem_v, state_rwkv,
           state_shift, norm_gains, w_in, attn_sink, shift_mu, rwkv_vecs, rwkv_rk, rwkv_w2, rwkv_a2, rwkv_g2,
           w_out, w_mem_q, w_mem_kv, w_mem_o, w_gate_up, w_down):
    b, t, _ = x_prompt.shape
    bd, tn, _ = x_sample.shape
    depth = w_in.shape[0]
    m_len = mem_prompt.shape[1]
    cache_len = cache_swa_k.shape[2]
    tabs_p = _rope_tables(jnp.arange(t, dtype=jnp.int32))
    cs, sn = _rope_tables(PAST_LEN + jnp.arange(tn, dtype=jnp.int32))
    tabs_s = (jnp.tile(cs, (bd, 1)), jnp.tile(sn, (bd, 1)))
    hsum = _head_sum_matrix()
    sp = _stacked_params(norm_gains, w_in, attn_sink, shift_mu, rwkv_vecs, rwkv_rk, rwkv_w2, rwkv_a2, rwkv_g2,
                         w_out, w_mem_q, w_mem_kv, w_mem_o, w_gate_up, w_down)
    s0_p = jnp.zeros((1, b, N_HEADS_R, HEAD_DIM, HEAD_DIM), F32)
    sh0_p = jnp.zeros((1, b, 1, RWKV_PAD), F32)
    sh0_s = jnp.pad(state_shift, ((0, 0), (0, 0), (0, 0), (0, RWKV_PAD - RWKV_PROJ)))
    swa_cache = (cache_swa_k.reshape(depth, bd, cache_len, KV_WIDTH_A),
                 cache_swa_v.reshape(depth, bd, cache_len, KV_WIDTH_A))
    mem_rows = m_len * MEM_HEADS
    cmk = cache_mem_k.reshape(depth, bd, mem_rows, MEM_HEAD_DIM)
    cmv = cache_mem_v.reshape(depth, bd, mem_rows, MEM_HEAD_DIM)
    xp = x_prompt.reshape(b * t, D_MODEL)
    xs = x_sample.reshape(bd * tn, D_MODEL)
    mem2d = mem_prompt.reshape(b * m_len, D_MODEL)
    in_tm_p = min(1024, t)
    rw_tt_p = (min(512, t), 1)
    rw_tt_s = (tn, 16 if bd % 16 == 0 else 1)
    post_t_p = min(1024, t)
    post_nb_s = 8 if bd % 8 == 0 else bd
    ffn_tm_p = min(512, b * t)
    keep = t - min(WINDOW, t)
    pk, pv, pmk, pmv, pS, psh = [], [], [], [], [], []
    sk, sv, sS, ssh = [], [], [], []
    for l in range(depth):
        mk_l, mv_l = _memkv(mem2d, l, sp["gains"], sp["w_mkv"])
        xp, k_l, v_l, s_l, sh_l, late_w = _layer(xp, l, b, t, tabs_p, 0, mk_l[None], mv_l[None], None, 0, s0_p, sh0_p,
                                                sp, hsum, in_tm_p, rw_tt_p, 1, post_t_p, ffn_tm_p)
        pk.append(k_l.reshape(b, t, KV_WIDTH_A)[:, keep:].reshape(b, t - keep, N_KV_A, HEAD_DIM))
        pv.append(v_l.reshape(b, t, KV_WIDTH_A)[:, keep:].reshape(b, t - keep, N_KV_A, HEAD_DIM))
        pmk.append(mk_l.reshape(b, m_len, MEM_HEADS, MEM_HEAD_DIM))
        pmv.append(mv_l.reshape(b, m_len, MEM_HEADS, MEM_HEAD_DIM))
        pS.append(s_l)
        psh.append(sh_l)

        xs, k2, v2, s2, sh2, _ = _layer(xs, l, bd, tn, tabs_s, l, cmk, cmv, swa_cache, l, state_rwkv, sh0_s, sp,
                                        hsum, bd * tn, rw_tt_s, post_nb_s, tn, bd * tn, late_w)
        sk.append(k2.reshape(bd, tn, N_KV_A, HEAD_DIM))
        sv.append(v2.reshape(bd, tn, N_KV_A, HEAD_DIM))
        sS.append(s2)
        ssh.append(sh2)
    return (xp.reshape(b, t, D_MODEL), xs.reshape(bd, tn, D_MODEL), jnp.stack(pk), jnp.stack(pv),
            jnp.stack(pmk), jnp.stack(pmv), jnp.stack(pS), jnp.stack(psh),
            jnp.stack(sk), jnp.stack(sv), jnp.stack(sS), jnp.stack(ssh))
```

```python
import functools
import math

import jax
import jax.numpy as jnp
from jax import lax
from jax.experimental import pallas as pl
from jax.experimental.pallas import tpu as pltpu

F32 = jnp.float32
BF16 = jnp.bfloat16

D_MODEL = 1024
HEAD_DIM = 64
CHUNK = 64
A_WIDTH = 512
KV_WIDTH_A = 128
N_KV_A = 2
WINDOW = 128
PAST_LEN = 4096
ROPE_THETA = 10000.0
R_WIDTH = 512
N_HEADS_R = 8
DECAY_LORA = 64
AAA_LORA = 64
GATE_LORA = 160
RWKV_PROJ = 3 * R_WIDTH + DECAY_LORA + AAA_LORA + GATE_LORA
RWKV_PAD = 1920
IN_COLS_PAD = A_WIDTH + 2 * KV_WIDTH_A + RWKV_PAD
GN_EPS = 6.4e-4
MEM_LEN = 256
MEM_HEADS = 4
MEM_HEAD_DIM = 128
MEM_WIDTH = 512
D_FF = 2816
RMS_EPS = 1e-6
NEG_INF = -1e30
NORM_MIX_PRE, NORM_MIX_POST, NORM_X_PRE, NORM_X_POST, NORM_MEM, NORM_FFN_PRE, NORM_FFN_POST = range(7)
EXP_M05 = math.exp(-0.5)

LANES = 128
IN_MAIN = IN_COLS_PAD - LANES
VMEM_LIMIT = 58 * 1024 * 1024


def _params(sem):
    return pltpu.CompilerParams(dimension_semantics=sem, vmem_limit_bytes=VMEM_LIMIT)


def _rms(x, g):
    ms = jnp.mean(x * x, axis=-1, keepdims=True)
    return x * lax.rsqrt(ms + RMS_EPS) * g


def _mm(a, b):
    return jnp.dot(a.astype(BF16), b.astype(BF16), preferred_element_type=F32)


def _mm_nt(a, b):
    return lax.dot_general(a.astype(BF16), b.astype(BF16), (((1,), (1,)), ((), ())),
                           preferred_element_type=F32)


def _split(x):
    hi = x.astype(BF16)
    lo = (x - hi.astype(F32)).astype(BF16)
    return hi, lo


def _mm_lsplit(a_exact_bf16, x):
    hi, lo = _split(x)
    return (jnp.dot(a_exact_bf16, hi, preferred_element_type=F32)
            + jnp.dot(a_exact_bf16, lo, preferred_element_type=F32))


def _in_kernel(*refs, tm, t_seq, n_cast):
    x_ref, g_ref, w_ref, wt_ref, cos_ref, sin_ref, mu_ref, sh0_ref = refs[:8]
    cast_in = refs[8:8 + n_cast]
    q_ref, k_ref, v_ref, pr_ref, shl_ref = refs[8 + n_cast:13 + n_cast]
    cast_out = refs[13 + n_cast:13 + 2 * n_cast]
    carry_scr, wb_scr, wtb_scr = refs[-3:]
    for src, dst in zip(cast_in, cast_out):
        dst[...] = src[...].astype(BF16)

    @pl.when(pl.program_id(0) == 0)
    def _():
        wb_scr[...] = w_ref[...].astype(BF16)
        wtb_scr[...] = wt_ref[...].astype(BF16)

    h = _rms(x_ref[...], g_ref[NORM_MIX_PRE:NORM_MIX_PRE + 1, :]).astype(BF16)
    nt = (((1,), (1,)), ((), ()))
    p = jnp.concatenate([lax.dot_general(h, wb_scr[...], nt, preferred_element_type=F32),
                         lax.dot_general(h, wtb_scr[...], nt, preferred_element_type=F32)], axis=1)
    cos = cos_ref[...]
    sin = sin_ref[...]
    lane = lax.broadcasted_iota(jnp.int32, cos.shape, 1)
    first_half = (lane & (HEAD_DIM // 2)) == 0

    def rope(xc):
        sw = jnp.where(first_half, pltpu.roll(xc, LANES - HEAD_DIM // 2, 1), pltpu.roll(xc, HEAD_DIM // 2, 1))
        return xc * cos + sw * sin

    for j in range(A_WIDTH // LANES):
        q_ref[:, j * LANES:(j + 1) * LANES] = rope(p[:, j * LANES:(j + 1) * LANES])
    k_ref[...] = rope(p[:, A_WIDTH:A_WIDTH + KV_WIDTH_A])
    v_ref[...] = p[:, A_WIDTH + KV_WIDTH_A:A_WIDTH + 2 * KV_WIDTH_A]
    raw = p[:, A_WIDTH + 2 * KV_WIDTH_A:]
    rowi = lax.broadcasted_iota(jnp.int32, (tm, 1), 0)
    prev = pltpu.roll(raw, 1, 0)
    if tm <= t_seq:
        first = (pl.program_id(0) % (t_seq // tm)) == 0
        prev = jnp.where(rowi == 0, jnp.where(first, sh0_ref[0], carry_scr[...]), prev)
        carry_scr[...] = raw[tm - 1:tm, :]
        shl_ref[0] = raw[tm - 1:tm, :]
    else:
        for s in range(tm // t_seq):
            prev = jnp.where(rowi == s * t_seq, sh0_ref[s], prev)
            shl_ref[s] = raw[(s + 1) * t_seq - 1:(s + 1) * t_seq, :]
    pr_ref[...] = raw + (prev - raw) * mu_ref[...]


def _layer_block(shape, l):
    nd = len(shape)
    return pl.BlockSpec((None,) + tuple(shape), lambda *_: (l,) + (0,) * nd)


def _in_proj(x2d, l, gains, w_in, w_tail, cos_t, sin_t, mu_p, ls, shift0, t_seq, tm, casts=()):
    n = x2d.shape[0]
    n_seq = n // t_seq
    steps = n // tm
    tab_blocks = cos_t.shape[0] // tm
    row = lambda i: (i, 0)
    tab = lambda i: (i % tab_blocks, 0)
    if tm <= t_seq:
        seqs, seq_of = 1, lambda i: i // (t_seq // tm)
    else:
        seqs, seq_of = tm // t_seq, lambda i: i
    cast_in, cast_out, cast_shapes = [], [], []
    for w in casts:
        rows = w.shape[1] // steps
        cast_in.append(pl.BlockSpec((None, rows, w.shape[2]), lambda i: (l, i, 0)))
        cast_out.append(pl.BlockSpec((None, rows, w.shape[2]), lambda i: (0, i, 0)))
        cast_shapes.append(jax.ShapeDtypeStruct((1,) + w.shape[1:], BF16))
    outs = pl.pallas_call(
        functools.partial(_in_kernel, tm=tm, t_seq=t_seq, n_cast=len(casts)),
        grid=(steps,),
        in_specs=[pl.BlockSpec((tm, D_MODEL), row), _layer_block((8, D_MODEL), l),
                  pl.BlockSpec((None, IN_MAIN, D_MODEL), lambda i: (l, 0, 0), pipeline_mode=pl.Buffered(1)),
                  _layer_block((LANES, D_MODEL), l),
                  pl.BlockSpec((tm, LANES), tab), pl.BlockSpec((tm, LANES), tab),
                  _layer_block((1, RWKV_PAD), l),
                  pl.BlockSpec((None, seqs, 1, RWKV_PAD), lambda i: (ls, seq_of(i), 0, 0))] + cast_in,
        out_specs=[pl.BlockSpec((tm, A_WIDTH), row), pl.BlockSpec((tm, KV_WIDTH_A), row),
                   pl.BlockSpec((tm, KV_WIDTH_A), row), pl.BlockSpec((tm, RWKV_PAD), row),
                   pl.BlockSpec((seqs, 1, RWKV_PAD), lambda i: (seq_of(i), 0, 0))] + cast_out,
        out_shape=[jax.ShapeDtypeStruct((n, A_WIDTH), F32), jax.ShapeDtypeStruct((n, KV_WIDTH_A), F32),
                   jax.ShapeDtypeStruct((n, KV_WIDTH_A), F32), jax.ShapeDtypeStruct((n, RWKV_PAD), F32),
                   jax.ShapeDtypeStruct((n_seq, 1, RWKV_PAD), F32)] + cast_shapes,
        scratch_shapes=[pltpu.VMEM((1, RWKV_PAD), F32), pltpu.VMEM((IN_MAIN, D_MODEL), BF16),
                        pltpu.VMEM((LANES, D_MODEL), BF16)],
        compiler_params=_params(("arbitrary",)),
        name="in_proj",
    )(x2d, gains, w_in, w_tail, cos_t, sin_t, mu_p, shift0, *casts)
    return outs[:5], outs[5:]


def _sink_attend(jobs, sink_ref, t):
    nk = jobs[0][1].shape[0]
    nkp = 2 * LANES
    lane = lax.broadcasted_iota(jnp.int32, (t, LANES), 1)
    m0 = lane < HEAD_DIM
    coli = lax.broadcasted_iota(jnp.int32, (1, nkp), 1)
    fills = []
    for kv in range(N_KV_A):
        blocks = []
        for g in range(4):
            sg = sink_ref[4 * kv + g:4 * kv + g + 1, :]
            blocks.append(jnp.broadcast_to(jnp.where(coli == nk, jnp.concatenate([sg, sg], axis=1), NEG_INF),
                                           (t, nkp)))
        fills.append(jnp.concatenate(blocks, axis=0))
    k_tail = jnp.zeros((nkp - nk, LANES), F32)
    tail_row = lax.broadcasted_iota(jnp.int32, (nkp - nk, nkp), 0)
    tail_col = lax.broadcasted_iota(jnp.int32, (nkp - nk, nkp), 1)
    v_tail = jnp.where((tail_row == 0) & (tail_col >= LANES), 1.0, 0.0).astype(F32)
    ones = jnp.ones((nk, LANES), F32)
    scores = []
    for q_rows, kdup, _, kv, valid in jobs:
        parts = []
        for p in (2 * kv, 2 * kv + 1):
            qp = q_rows[:, p * LANES:(p + 1) * LANES]
            parts.append(jnp.where(m0, qp, 0.0))
            parts.append(jnp.where(m0, 0.0, qp))
        lhs = jnp.concatenate(parts, axis=0)
        s = _mm_nt(lhs, jnp.concatenate([kdup, k_tail], axis=0))
        keep = coli < nk
        if valid is not None:
            keep = keep & valid
        scores.append(jnp.where(keep, s, fills[kv]))
    exps = [jnp.exp(s - jnp.max(s, axis=-1, keepdims=True)) for s in scores]
    outs = []
    for (_, _, vdup, _, _), e in zip(jobs, exps):
        v2 = jnp.concatenate([jnp.concatenate([vdup, ones], axis=1), v_tail], axis=0)
        o2 = _mm(e, v2)
        o = o2[:, :LANES] / o2[:, LANES:]
        outs.append([jnp.where(m0, o[(2 * pi) * t:(2 * pi + 1) * t], o[(2 * pi + 1) * t:(2 * pi + 2) * t])
                     for pi in range(2)])
    return outs


SWA_SCALE = HEAD_DIM ** -0.5


def _dup_heads(x):
    lane = lax.broadcasted_iota(jnp.int32, x.shape, 1)
    m0 = lane < HEAD_DIM
    xs = pltpu.roll(x, HEAD_DIM, 1)
    return [jnp.where(m0, x, xs), jnp.where(m0, xs, x)]


SWA_TQ = 512


def _swa_prompt_kernel(q_ref, kp_ref, kc_ref, vp_ref, vc_ref, sink_ref, o_ref):
    i = pl.program_id(1)
    k = jnp.concatenate([kp_ref[...], kc_ref[...]], axis=0)
    v = jnp.concatenate([vp_ref[...], vc_ref[...]], axis=0)
    kd = _dup_heads(k * SWA_SCALE)
    vd = _dup_heads(v)
    nk = 3 * CHUNK
    slot = lax.broadcasted_iota(jnp.int32, (1, 2 * LANES), 1) // CHUNK
    jobs = []
    for j in range(SWA_TQ // CHUNK):
        qj = q_ref[j * CHUNK:(j + 1) * CHUNK, :]
        valid = (slot + (i * (SWA_TQ // CHUNK) + j - 2)) >= 0
        for kv in range(N_KV_A):
            jobs.append((qj, kd[kv][j * CHUNK:j * CHUNK + nk], vd[kv][j * CHUNK:j * CHUNK + nk], kv, valid))
    outs = _sink_attend(jobs, sink_ref, CHUNK)
    for n, out in enumerate(outs):
        j, kv = divmod(n, N_KV_A)
        for pi in range(2):
            p = 2 * kv + pi
            o_ref[j * CHUNK:(j + 1) * CHUNK, p * LANES:(p + 1) * LANES] = out[pi]


def _swa_prompt(q, k, v, l, sink_b, b, t):
    n = b * t
    nq = t // SWA_TQ
    qmap = lambda bi, i: (bi * nq + i, 0)
    pmap = lambda bi, i: (jnp.maximum(bi * (t // WINDOW) + (SWA_TQ // WINDOW) * i - 1, 0), 0)
    return pl.pallas_call(
        _swa_prompt_kernel,
        grid=(b, nq),
        in_specs=[pl.BlockSpec((SWA_TQ, A_WIDTH), qmap),
                  pl.BlockSpec((WINDOW, KV_WIDTH_A), pmap), pl.BlockSpec((SWA_TQ, KV_WIDTH_A), qmap),
                  pl.BlockSpec((WINDOW, KV_WIDTH_A), pmap), pl.BlockSpec((SWA_TQ, KV_WIDTH_A), qmap),
                  _layer_block((8, LANES), l)],
        out_specs=pl.BlockSpec((SWA_TQ, A_WIDTH), qmap),
        out_shape=jax.ShapeDtypeStruct((n, A_WIDTH), F32),
        compiler_params=_params(("parallel", "parallel")),
        name="swa_prompt",
    )(q, k, k, v, v, sink_b)


def _swa_sample_kernel(q_ref, kc_ref, kn_ref, vc_ref, vn_ref, sink_ref, o_ref, *, t, nb):
    jobs = []
    for bi in range(nb):
        rows = slice(bi * t, (bi + 1) * t)
        kd = _dup_heads(jnp.concatenate([kc_ref[bi], kn_ref[rows, :]], axis=0) * SWA_SCALE)
        vd = _dup_heads(jnp.concatenate([vc_ref[bi], vn_ref[rows, :]], axis=0))
        for kv in range(N_KV_A):
            jobs.append((q_ref[rows, :], kd[kv], vd[kv], kv, None))
    outs = _sink_attend(jobs, sink_ref, t)
    for n, out in enumerate(outs):
        bi, kv = divmod(n, N_KV_A)
        for pi in range(2):
            p = 2 * kv + pi
            o_ref[bi * t:(bi + 1) * t, p * LANES:(p + 1) * LANES] = out[pi]


def _swa_sample(q, k, v, l, kc, vc, sink_b, b, t):
    n = b * t
    cache = kc.shape[2]
    nb = 8 if b % 8 == 0 else 1
    row = lambda bi: (bi, 0)
    cmap = lambda bi: (l, bi, 0, 0)
    return pl.pallas_call(
        functools.partial(_swa_sample_kernel, t=t, nb=nb),
        grid=(b // nb,),
        in_specs=[pl.BlockSpec((nb * t, A_WIDTH), row),
                  pl.BlockSpec((None, nb, cache, KV_WIDTH_A), cmap), pl.BlockSpec((nb * t, KV_WIDTH_A), row),
                  pl.BlockSpec((None, nb, cache, KV_WIDTH_A), cmap), pl.BlockSpec((nb * t, KV_WIDTH_A), row),
                  _layer_block((8, LANES), l)],
        out_specs=pl.BlockSpec((nb * t, A_WIDTH), row),
        out_shape=jax.ShapeDtypeStruct((n, A_WIDTH), F32),
        compiler_params=_params(("parallel",)),
        name="swa_sample",
    )(q, kc, k, vc, v, sink_b)


RWKV_SUB = 256


def _interleave(gens):
    gens = list(gens)
    while gens:
        for g in list(gens):
            try:
                next(g)
            except StopIteration:
                gens.remove(g)


def _rwkv_kernel(pr_ref, s0_ref, vec_ref, rk_ref, w2_ref, a2_ref, g2_ref, hsum_ref,
                 ltri_ref, out_ref, sfin_ref,
                 s_scr, rt_s, at_s, bt_s, kt_s, bb_s, kb_s, v_s, et_s, y_s, bonus_s, gate_s,
                 *, c_len, tt, nb, sub):
    i = pl.program_id(1)
    n_i = pl.num_programs(1)
    n_rows = nb * tt

    @pl.when(i == 0)
    def _():
        zero = jnp.zeros((HEAD_DIM, HEAD_DIM), F32)
        for bi in range(nb):
            for p in range(4):
                s_scr[bi, p] = jnp.concatenate(
                    [jnp.concatenate([s0_ref[bi, 2 * p].astype(F32), zero], axis=1),
                     jnp.concatenate([zero, s0_ref[bi, 2 * p + 1].astype(F32)], axis=1)], axis=0)

    w0 = vec_ref[0:1, :]
    a0 = vec_ref[1:2, :]
    k_k = vec_ref[2:3, :]
    k_a = vec_ref[3:4, :]
    gn_g = vec_ref[4:5, :]
    gn_b = vec_ref[5:6, :]
    hsum = hsum_ref[...]
    hw = hsum.shape[0]
    halves = [slice(j * hw, (j + 1) * hw) for j in range(R_WIDTH // hw)]

    def head_sum(x):
        return _mm(x, hsum)

    n2 = 2 * c_len
    ti = lax.broadcasted_iota(jnp.int32, (c_len, n2), 0)
    lane2 = lax.broadcasted_iota(jnp.int32, (c_len, n2), 1)
    si = lane2 & (c_len - 1)
    first_c = lane2 < c_len
    strict = ti > si
    incl = ti >= si
    eye = jnp.where(ti == si, 1.0, 0.0).astype(F32)
    masks = []
    half = 1
    while half < c_len:
        blk = 2 * half
        masks.append(((ti & ~(blk - 1)) == (si & ~(blk - 1))) & ((ti & half) != 0) & ((si & half) == 0))
        half = blk
    m0 = lax.broadcasted_iota(jnp.int32, (c_len, LANES), 1) < HEAD_DIM
    rl = lax.broadcasted_iota(jnp.int32, (LANES, LANES), 0)
    cl = lax.broadcasted_iota(jnp.int32, (LANES, LANES), 1)
    same_head = (rl >= HEAD_DIM) == (cl >= HEAD_DIM)
    fused = n2 == LANES
    zeros_c = jnp.zeros((c_len, LANES), F32)

    def bd(x):
        return jnp.concatenate([jnp.where(m0, x, 0.0), jnp.where(m0, 0.0, x)], axis=0)

    def bd_t(x):
        return jnp.concatenate([jnp.where(first_c, x, 0.0), jnp.where(first_c, 0.0, x)], axis=0)

    def prologue(lo, hi):
        n = hi - lo
        bi0 = lo // tt

        def shifted(a, b):
            if hi - lo <= tt:
                return pr_ref[bi0, lo - bi0 * tt:hi - bi0 * tt, a:b]
            return pr_ref[bi0:hi // tt, :, a:b].reshape(n, b - a)

        xwa = shifted(3 * R_WIDTH, 3 * R_WIDTH + LANES)
        xg = shifted(3 * R_WIDTH + LANES, RWKV_PAD)
        th = jnp.tanh(xwa)
        sg = jax.nn.sigmoid(xg)
        ltri = ltri_ref[...]
        yield
        for cs in halves:
            r = shifted(cs.start, cs.stop)
            k = shifted(R_WIDTH + cs.start, R_WIDTH + cs.stop)
            v = shifted(2 * R_WIDTH + cs.start, 2 * R_WIDTH + cs.stop)
            yield
            z = w0[:, cs] + _mm(th, w2_ref[:, cs])
            wlog = -EXP_M05 * jax.nn.sigmoid(z)
            a = jax.nn.sigmoid(a0[:, cs] + _mm(xwa, a2_ref[:, cs]))
            gate_s[lo:hi, cs] = _mm(sg, g2_ref[:, cs])
            yield
            kk = k * k_k[:, cs]
            kk = kk * lax.rsqrt(jnp.maximum(head_sum(kk * kk), 1e-24))
            k_f = k * (1.0 + (a - 1.0) * k_a[:, cs])
            bvec = kk * a
            bonus_s[lo:hi, cs] = head_sum(r * k_f * rk_ref[:, cs]) * v
            yield
            cum = _mm_lsplit(ltri, wlog)
            tot = jnp.concatenate([jnp.broadcast_to(cum[r1 - 1:r1, :], (c_len, hw))
                                   for r1 in range(c_len, n + 1, c_len)], axis=0)
            e_in = jnp.exp(cum)
            e_inv = jnp.exp(-cum)
            e_end = jnp.exp(tot - cum)
            rt_s[lo:hi, cs] = r * e_in
            at_s[lo:hi, cs] = -kk * jnp.exp(cum - wlog)
            yield
            bt_s[lo:hi, cs] = bvec * e_inv
            kt_s[lo:hi, cs] = k_f * e_inv
            bb_s[lo:hi, cs] = bvec * e_end
            kb_s[lo:hi, cs] = k_f * e_end
            v_s[lo:hi, cs] = v
            et_s[lo:hi, cs] = jnp.exp(tot)
            yield

    s_state = [[s_scr[bi, p] for p in range(4)] for bi in range(nb)]

    def chains(lo, hi):
        chs = [(r0, p) for r0 in range(lo, hi, c_len) for p in range(4)]

        def cat(ref, ch):
            return ref[ch[0]:ch[0] + c_len, ch[1] * LANES:(ch[1] + 1) * LANES]

        a_ab, a_ak, a_rb, a_rk = {}, {}, {}, {}
        for ch in chs:
            at, rt, rb, rkt = cat(at_s, ch), cat(rt_s, ch), bd(cat(bt_s, ch)), bd(cat(kt_s, ch))
            if fused:
                amat = _mm_nt(jnp.concatenate([at, rt], axis=0), jnp.concatenate([rb, rkt], axis=0))
                q_ab, q_ak = amat[:c_len, :n2], amat[:c_len, n2:]
                q_rb, q_rk = amat[c_len:, :n2], amat[c_len:, n2:]
            else:
                q_ab, q_ak, q_rb, q_rk = _mm_nt(at, rb), _mm_nt(at, rkt), _mm_nt(rt, rb), _mm_nt(rt, rkt)
            a_ab[ch] = jnp.where(strict, q_ab, 0.0)
            a_ak[ch] = jnp.where(strict, q_ak, 0.0)
            a_rb[ch] = jnp.where(incl, q_rb, 0.0)
            a_rk[ch] = jnp.where(incl, q_rk, 0.0)
        yield
        tinv = {ch: eye + jnp.where(masks[0], a_ab[ch], 0.0) for ch in chs}
        for m in masks[1:]:
            pe = {ch: _mm(tinv[ch], bd_t(jnp.where(m, a_ab[ch], 0.0))) for ch in chs}
            yield
            tinv = {ch: tinv[ch] + _mm(pe[ch], bd_t(tinv[ch])) for ch in chs}
            yield
        zv = {ch: _mm(a_ak[ch], bd(cat(v_s, ch))) for ch in chs}
        yield
        r_hat, y_hat, m_mat, g_mat = {}, {}, {}, {}
        if fused:
            w = {ch: _mm(tinv[ch], jnp.concatenate([bd(cat(at_s, ch)), bd(zv[ch])], axis=1)) for ch in chs}
            yield
            for ch in chs:
                rhs2 = jnp.concatenate(
                    [jnp.concatenate([bd(w[ch][:, :LANES]), bd(w[ch][:, LANES:])], axis=1),
                     jnp.concatenate([jnp.zeros((n2, LANES), F32), bd(cat(v_s, ch))], axis=1)], axis=0)
                ry = _mm(jnp.concatenate([a_rb[ch], a_rk[ch]], axis=1), rhs2)
                r_hat[ch] = cat(rt_s, ch) + ry[:, :LANES]
                y_hat[ch] = ry[:, LANES:]
            yield
            for ch in chs:
                lhs_t = jnp.concatenate([w[ch], jnp.concatenate([zeros_c, cat(v_s, ch)], axis=1)], axis=0)
                mg = _mm(lhs_t.T, jnp.concatenate([cat(bb_s, ch), cat(kb_s, ch)], axis=0))
                m_mat[ch] = jnp.where(same_head, mg[:LANES], 0.0)
                g_mat[ch] = jnp.where(same_head, mg[LANES:], 0.0)
            yield
        else:
            a_hat = {ch: _mm(tinv[ch], bd(cat(at_s, ch))) for ch in chs}
            u_hat = {ch: _mm(tinv[ch], bd(zv[ch])) for ch in chs}
            yield
            for ch in chs:
                r_hat[ch] = cat(rt_s, ch) + _mm(a_rb[ch], bd(a_hat[ch]))
                y_hat[ch] = _mm(a_rb[ch], bd(u_hat[ch])) + _mm(a_rk[ch], bd(cat(v_s, ch)))
            yield
            for ch in chs:
                m_mat[ch] = jnp.where(same_head, _mm(a_hat[ch].T, cat(bb_s, ch)), 0.0)
                g_mat[ch] = jnp.where(same_head,
                                      _mm(u_hat[ch].T, cat(bb_s, ch)) + _mm(cat(v_s, ch).T, cat(kb_s, ch)), 0.0)
            yield
        for r0 in range(lo, hi, c_len):
            s_cur = s_state[r0 // tt]
            for p in range(4):
                ch = (r0, p)
                ls = slice(p * LANES, (p + 1) * LANES)
                e_last = et_s[r0:r0 + 1, ls]
                y_s[r0:r0 + c_len, ls] = y_hat[ch] + _mm_nt(r_hat[ch], s_cur[p])
                s_cur[p] = s_cur[p] * e_last + _mm(s_cur[p], m_mat[ch]) + g_mat[ch]
            yield

    def epilogue(lo, hi):
        for cs in halves:
            y = y_s[lo:hi, cs]
            mean = head_sum(y) * (1.0 / HEAD_DIM)
            yc = y - mean
            yield
            var = head_sum(yc * yc) * (1.0 / HEAD_DIM)
            yn = yc * lax.rsqrt(var + GN_EPS) * gn_g[:, cs] + gn_b[:, cs]
            res = (yn + bonus_s[lo:hi, cs]) * gate_s[lo:hi, cs]
            for r0 in range(lo, hi, tt) if hi - lo > tt else [lo]:
                r1 = min(r0 + tt, hi)
                out_ref[r0 // tt, r0 % tt:r0 % tt + (r1 - r0), cs] = res[r0 - lo:r1 - lo]
            yield

    subs = [(lo, lo + sub) for lo in range(0, n_rows, sub)]
    _interleave([prologue(*subs[0])])
    for n, sb in enumerate(subs):
        phases = [chains(*sb)]
        if n + 1 < len(subs):
            phases.append(prologue(*subs[n + 1]))
        if n > 0:
            phases.append(epilogue(*subs[n - 1]))
        _interleave(phases)
    _interleave([epilogue(*subs[-1])])

    for bi in range(nb):
        for p in range(4):
            s_scr[bi, p] = s_state[bi][p]

    @pl.when(i == n_i - 1)
    def _():
        for bi in range(nb):
            for p in range(4):
                s_pair = s_state[bi][p]
                sfin_ref[bi, 2 * p] = s_pair[:HEAD_DIM, :HEAD_DIM].astype(sfin_ref.dtype)
                sfin_ref[bi, 2 * p + 1] = s_pair[HEAD_DIM:, HEAD_DIM:].astype(sfin_ref.dtype)


def _rwkv(pr3, l, ls, s0, vecs, rk_flat, w2p, a2p, g2p, hsum, tt, c_len, nb):
    b, t, _ = pr3.shape
    n_rows = nb * tt
    sub = min(RWKV_SUB, n_rows)
    ri = jnp.arange(sub)[:, None]
    ci = jnp.arange(sub)[None, :]
    same = (ri // c_len) == (ci // c_len)
    ltri = (same & (ri >= ci)).astype(BF16)
    hw = hsum.shape[0]
    blk = lambda bi, i: (bi, i, 0)
    const = lambda bi, i: (0, 0)
    big = lambda: pltpu.VMEM((n_rows, R_WIDTH), F32)
    return pl.pallas_call(
        functools.partial(_rwkv_kernel, c_len=c_len, tt=tt, nb=nb, sub=sub),
        grid=(b // nb, t // tt),
        in_specs=[pl.BlockSpec((nb, tt, RWKV_PAD), blk),
                  pl.BlockSpec((None, nb, N_HEADS_R, HEAD_DIM, HEAD_DIM), lambda bi, i: (ls, bi, 0, 0, 0)),
                  _layer_block((6, R_WIDTH), l), _layer_block((1, R_WIDTH), l),
                  _layer_block((LANES, R_WIDTH), l), _layer_block((LANES, R_WIDTH), l),
                  _layer_block((2 * LANES, R_WIDTH), l), pl.BlockSpec((hw, hw), const),
                  pl.BlockSpec((sub, sub), const)],
        out_specs=[pl.BlockSpec((nb, tt, R_WIDTH), blk),
                   pl.BlockSpec((nb, N_HEADS_R, HEAD_DIM, HEAD_DIM), lambda bi, i: (bi, 0, 0, 0))],
        out_shape=[jax.ShapeDtypeStruct((b, t, R_WIDTH), F32),
                   jax.ShapeDtypeStruct((b, N_HEADS_R, HEAD_DIM, HEAD_DIM), s0.dtype)],
        scratch_shapes=[pltpu.VMEM((nb, 4, LANES, LANES), F32),
                        big(), big(), big(), big(), big(), big(), big(), big(), big(), big(), big()],
        compiler_params=_params(("parallel", "arbitrary")),
        name="rwkv_mix",
    )(pr3, s0, vecs, rk_flat, w2p, a2p, g2p, hsum, ltri)


POST_PARTS = 2


def _post_kernel(attn_ref, rw_ref, x_ref, g_ref, wout_ref, wq_ref, wo_ref, mk_ref, mv_ref, o_ref, *, nb, t):
    scale = MEM_HEAD_DIM ** -0.5
    cols = lambda hd: slice(hd * MEM_HEAD_DIM, (hd + 1) * MEM_HEAD_DIM)
    head_rows = lambda hd: pl.ds(hd, MEM_LEN, stride=MEM_HEADS)
    ones = jnp.ones((MEM_LEN, MEM_HEAD_DIM), F32)

    def part(lo, hi):
        rows = slice(lo, hi)
        m = _mm(attn_ref[rows, :], wout_ref[0:A_WIDTH, :]) + _mm(rw_ref[rows, :], wout_ref[A_WIDTH:, :])
        yield
        x1 = x_ref[rows, :] + _rms(m, g_ref[NORM_MIX_POST:NORM_MIX_POST + 1, :])
        hq = _rms(x1, g_ref[NORM_X_PRE:NORM_X_PRE + 1, :])
        yield
        q = _mm(hq, wq_ref[...])
        yield
        if nb == 1:
            jobs = [(0, lo, hi, hd) for hd in range(MEM_HEADS)]
        else:
            jobs = [(r0 // t, r0, r0 + t, hd) for r0 in range(lo, hi, t) for hd in range(MEM_HEADS)]
        scores = [_mm_nt(q[r0 - lo:r1 - lo, cols(hd)], mk_ref[bi, head_rows(hd), :]) * scale
                  for bi, r0, r1, hd in jobs]
        yield
        exps = [jnp.exp(s - jnp.max(s, axis=-1, keepdims=True)) for s in scores]
        yield
        outs = []
        for (bi, _, _, hd), e in zip(jobs, exps):
            o2 = _mm(e, jnp.concatenate([mv_ref[bi, head_rows(hd), :], ones], axis=1))
            outs.append(o2[:, :MEM_HEAD_DIM] / o2[:, MEM_HEAD_DIM:])
        blocks = [jnp.concatenate(outs[j:j + MEM_HEADS], axis=1) for j in range(0, len(outs), MEM_HEADS)]
        o = blocks[0] if len(blocks) == 1 else jnp.concatenate(blocks, axis=0)
        yield
        c = _mm(o, wo_ref[...])
        yield
        o_ref[rows, :] = x1 + _rms(c, g_ref[NORM_X_POST:NORM_X_POST + 1, :])

    n_rows = nb * t
    n_parts = POST_PARTS if (nb == 1 or nb % POST_PARTS == 0) and n_rows % (8 * POST_PARTS) == 0 else 1
    step = n_rows // n_parts
    gens = [part(j * step, (j + 1) * step) for j in range(n_parts)]
    for j, gen in enumerate(gens):
        for _ in range(n_parts - 1 - j):
            next(gen)
    _interleave(gens)


def _post(attn, rw, x2d, l, gains, w_out, w_mq, w_mo, lm, mk, mv, nb, t, tiles_per_batch):
    n = x2d.shape[0]
    tm = nb * t
    row = lambda i: (i, 0)
    if nb == 1:
        mmap = lambda i: (lm, i // tiles_per_batch, 0, 0)
    else:
        mmap = lambda i: (lm, i, 0, 0)
    mem_rows = MEM_LEN * MEM_HEADS
    return pl.pallas_call(
        functools.partial(_post_kernel, nb=nb, t=t),
        grid=(n // tm,),
        in_specs=[pl.BlockSpec((tm, A_WIDTH), row), pl.BlockSpec((tm, R_WIDTH), row),
                  pl.BlockSpec((tm, D_MODEL), row), _layer_block((8, D_MODEL), l),
                  _layer_block((D_MODEL, D_MODEL), 0), _layer_block((D_MODEL, MEM_WIDTH), 0),
                  _layer_block((MEM_WIDTH, D_MODEL), 0),
                  pl.BlockSpec((None, nb, mem_rows, MEM_HEAD_DIM), mmap),
                  pl.BlockSpec((None, nb, mem_rows, MEM_HEAD_DIM), mmap)],
        out_specs=pl.BlockSpec((tm, D_MODEL), row),
        out_shape=jax.ShapeDtypeStruct((n, D_MODEL), F32),
        compiler_params=_params(("parallel",)),
        name="post_mix",
    )(attn, rw, x2d, gains, w_out, w_mq, w_mo, mk, mv)


FFN_CHUNK = 256


def _ffn_kernel(x_ref, g_ref, wgu_ref, wd_ref, o_ref):
    x = x_ref[...]
    h = _rms(x, g_ref[NORM_FFN_PRE:NORM_FFN_PRE + 1, :]).astype(BF16)
    acc = None
    for j in range(D_FF // FFN_CHUNK):
        cols = slice(j * FFN_CHUNK, (j + 1) * FFN_CHUNK)
        gate = jnp.dot(h, wgu_ref[:, cols], preferred_element_type=F32)
        up = jnp.dot(h, wgu_ref[:, D_FF + j * FFN_CHUNK:D_FF + (j + 1) * FFN_CHUNK], preferred_element_type=F32)
        act = (gate * jax.nn.sigmoid(gate)) * up
        part = _mm(act, wd_ref[cols, :])
        acc = part if acc is None else acc + part
    o_ref[...] = x + _rms(acc, g_ref[NORM_FFN_POST:NORM_FFN_POST + 1, :])


def _ffn(x2d, l, gains, w_gu, w_dn, tm):
    n = x2d.shape[0]
    row = lambda i: (i, 0)
    resident = dict(pipeline_mode=pl.Buffered(1))
    return pl.pallas_call(
        _ffn_kernel,
        grid=(n // tm,),
        in_specs=[pl.BlockSpec((tm, D_MODEL), row), _layer_block((8, D_MODEL), l),
                  pl.BlockSpec((None, D_MODEL, 2 * D_FF), lambda i: (0, 0, 0), **resident),
                  pl.BlockSpec((None, D_FF, D_MODEL), lambda i: (0, 0, 0), **resident)],
        out_specs=pl.BlockSpec((tm, D_MODEL), row),
        out_shape=jax.ShapeDtypeStruct((n, D_MODEL), F32),
        compiler_params=_params(("parallel",)),
        name="ffn",
    )(x2d, gains, w_gu, w_dn)


def _memkv_kernel(x_ref, g_ref, w_ref, k_ref, v_ref):
    kv = _mm(_rms(x_ref[...], g_ref[NORM_MEM:NORM_MEM + 1, :]), w_ref[...])
    for hd in range(MEM_HEADS):
        rows = pl.ds(hd, MEM_LEN, stride=MEM_HEADS)
        k_ref[0, rows, :] = kv[:, hd * MEM_HEAD_DIM:(hd + 1) * MEM_HEAD_DIM]
        v_ref[0, rows, :] = kv[:, MEM_WIDTH + hd * MEM_HEAD_DIM:MEM_WIDTH + (hd + 1) * MEM_HEAD_DIM]


def _memkv(mem2d, l, gains, w_kv):
    n = mem2d.shape[0]
    nbatch = n // MEM_LEN
    mem_rows = MEM_LEN * MEM_HEADS
    out = jax.ShapeDtypeStruct((nbatch, mem_rows, MEM_HEAD_DIM), F32)
    ospec = pl.BlockSpec((1, mem_rows, MEM_HEAD_DIM), lambda i: (i, 0, 0))
    return pl.pallas_call(
        _memkv_kernel,
        grid=(nbatch,),
        in_specs=[pl.BlockSpec((MEM_LEN, D_MODEL), lambda i: (i, 0)), _layer_block((8, D_MODEL), l),
                  _layer_block((D_MODEL, 2 * MEM_WIDTH), l)],
        out_specs=[ospec, ospec],
        out_shape=[out, out],
        compiler_params=_params(("parallel",)),
        name="mem_kv",
    )(mem2d, gains, w_kv)


def _rope_tables(pos):
    half = HEAD_DIM // 2
    inv = ROPE_THETA ** (-jnp.arange(half, dtype=F32) / half)
    ang = pos.astype(F32)[:, None] * inv[None, :]
    cos = jnp.cos(ang)
    sin = jnp.sin(ang)
    cos_t = jnp.tile(cos, (1, LANES // half))
    sin_t = jnp.tile(jnp.concatenate([-sin, sin], axis=1), (1, LANES // HEAD_DIM))
    return cos_t, sin_t


def _stacked_params(norm_gains, w_in, attn_sink, shift_mu, rwkv_vecs, rwkv_rk, rwkv_w2, rwkv_a2, rwkv_g2,
                    w_out, w_mem_q, w_mem_kv, w_mem_o, w_gate_up, w_down):
    depth = w_in.shape[0]
    return dict(
        gains=jnp.pad(norm_gains, ((0, 0), (0, 8 - norm_gains.shape[1]), (0, 0))),
        w_in=jnp.swapaxes(w_in, 1, 2),
        w_tail=jnp.pad(jnp.swapaxes(w_in[:, :, IN_MAIN:], 1, 2),
                       ((0, 0), (0, IN_MAIN + LANES - w_in.shape[2]), (0, 0))),
        sink=jnp.broadcast_to(attn_sink[:, :, None], attn_sink.shape + (LANES,)).astype(F32),
        mu=jnp.pad(shift_mu, ((0, 0), (0, RWKV_PAD - RWKV_PROJ)))[:, None, :],
        vecs=rwkv_vecs, rk=rwkv_rk.reshape(depth, 1, R_WIDTH),
        w2=jnp.pad(rwkv_w2, ((0, 0), (0, LANES - DECAY_LORA), (0, 0))).astype(BF16),
        a2=jnp.pad(rwkv_a2, ((0, 0), (DECAY_LORA, LANES - DECAY_LORA - AAA_LORA), (0, 0))).astype(BF16),
        g2=jnp.pad(rwkv_g2, ((0, 0), (0, 2 * LANES - GATE_LORA), (0, 0))).astype(BF16),
        w_mkv=w_mem_kv,
        w_out=w_out, w_mq=w_mem_q, w_mo=w_mem_o, w_gu=w_gate_up, w_dn=w_down)


def _head_sum_matrix():
    i = jnp.arange(2 * LANES)
    return ((i[:, None] // HEAD_DIM) == (i[None, :] // HEAD_DIM)).astype(BF16)


def _layer(x2d, l, b, t, tabs, lm, mk, mv, swa_cache, ls, s0, shift0, sp, hsum, in_tm, rw_tt, post_nb, post_t,
           ffn_tm, late_w=None):
    to_cast = () if late_w else tuple(sp[name] for name in ("w_out", "w_mq", "w_mo", "w_gu", "w_dn"))
    (q, k, v, pr, shift_new), cast = _in_proj(x2d, l, sp["gains"], sp["w_in"], sp["w_tail"], tabs[0], tabs[1], sp["mu"], ls,
                                              shift0, t, in_tm, to_cast)
    late_w = late_w or tuple(cast)
    if swa_cache is None:
        attn = _swa_prompt(q, k, v, l, sp["sink"], b, t)
    else:
        attn = _swa_sample(q, k, v, l, swa_cache[0], swa_cache[1], sp["sink"], b, t)
    pr3 = pr.reshape(b, t, RWKV_PAD)
    rw, s_fin = _rwkv(pr3, l, ls, s0, sp["vecs"], sp["rk"], sp["w2"], sp["a2"], sp["g2"], hsum,
                      rw_tt[0], min(CHUNK, t), rw_tt[1])
    w_out, w_mq, w_mo, w_gu, w_dn = late_w
    x2 = _post(attn, rw.reshape(b * t, R_WIDTH), x2d, l, sp["gains"], w_out, w_mq, w_mo,
               lm, mk, mv, post_nb, post_t, t // post_t)
    x3 = _ffn(x2, l, sp["gains"], w_gu, w_dn, ffn_tm)
    return x3, k, v, s_fin, shift_new[:, :, :RWKV_PROJ], late_w


def kernel(x_prompt, mem_prompt, x_sample, cache_swa_k, cache_swa_v, cache_mem_k, cache_mem_v, state_rwkv,
           state_shift, norm_gains, w_in, attn_sink, shift_mu, rwkv_vecs, rwkv_rk, rwkv_w2, rwkv_a2, rwkv_g2,
           w_out, w_mem_q, w_mem_kv, w_mem_o, w_gate_up, w_down):
    b, t, _ = x_prompt.shape
    bd, tn, _ = x_sample.shape
    depth = w_in.shape[0]
    m_len = mem_prompt.shape[1]
    cache_len = cache_swa_k.shape[2]
    tabs_p = _rope_tables(jnp.arange(t, dtype=jnp.int32))
    cs, sn = _rope_tables(PAST_LEN + jnp.arange(tn, dtype=jnp.int32))
    tabs_s = (jnp.tile(cs, (bd, 1)), jnp.tile(sn, (bd, 1)))
    hsum = _head_sum_matrix()
    sp = _stacked_params(norm_gains, w_in, attn_sink, shift_mu, rwkv_vecs, rwkv_rk, rwkv_w2, rwkv_a2, rwkv_g2,
                         w_out, w_mem_q, w_mem_kv, w_mem_o, w_gate_up, w_down)
    s0_p = jnp.zeros((1, b, N_HEADS_R, HEAD_DIM, HEAD_DIM), F32)
    sh0_p = jnp.zeros((1, b, 1, RWKV_PAD), F32)
    sh0_s = jnp.pad(state_shift, ((0, 0), (0, 0), (0, 0), (0, RWKV_PAD - RWKV_PROJ)))
    swa_cache = (cache_swa_k.reshape(depth, bd, cache_len, KV_WIDTH_A),
                 cache_swa_v.reshape(depth, bd, cache_len, KV_WIDTH_A))
    mem_rows = m_len * MEM_HEADS
    cmk = cache_mem_k.reshape(depth, bd, mem_rows, MEM_HEAD_DIM)
    cmv = cache_mem_v.reshape(depth, bd, mem_rows, MEM_HEAD_DIM)
    xp = x_prompt.reshape(b * t, D_MODEL)
    xs = x_sample.reshape(bd * tn, D_MODEL)
    mem2d = mem_prompt.reshape(b * m_len, D_MODEL)
    in_tm_p = min(1024, t)
    rw_tt_p = (min(512, t), 1)
    rw_tt_s = (tn, 16 if bd % 16 == 0 else 1)
    post_t_p = min(1024, t)
    post_nb_s = 8 if bd % 8 == 0 else bd
    ffn_tm_p = min(512, b * t)
    keep = t - min(WINDOW, t)
    pk, pv, pmk, pmv, pS, psh = [], [], [], [], [], []
    sk, sv, sS, ssh = [], [], [], []
    for l in range(depth):
        mk_l, mv_l = _memkv(mem2d, l, sp["gains"], sp["w_mkv"])
        xp, k_l, v_l, s_l, sh_l, late_w = _layer(xp, l, b, t, tabs_p, 0, mk_l[None], mv_l[None], None, 0, s0_p, sh0_p,
                                                sp, hsum, in_tm_p, rw_tt_p, 1, post_t_p, ffn_tm_p)
        pk.append(k_l.reshape(b, t, KV_WIDTH_A)[:, keep:].reshape(b, t - keep, N_KV_A, HEAD_DIM))
        pv.append(v_l.reshape(b, t, KV_WIDTH_A)[:, keep:].reshape(b, t - keep, N_KV_A, HEAD_DIM))
        pmk.append(mk_l.reshape(b, m_len, MEM_HEADS, MEM_HEAD_DIM))
        pmv.append(mv_l.reshape(b, m_len, MEM_HEADS, MEM_HEAD_DIM))
        pS.append(s_l)
        psh.append(sh_l)

        xs, k2, v2, s2, sh2, _ = _layer(xs, l, bd, tn, tabs_s, l, cmk, cmv, swa_cache, l, state_rwkv, sh0_s, sp,
                                        hsum, bd * tn, rw_tt_s, post_nb_s, tn, bd * tn, late_w)
        sk.append(k2.reshape(bd, tn, N_KV_A, HEAD_DIM))
        sv.append(v2.reshape(bd, tn, N_KV_A, HEAD_DIM))
        sS.append(s2)
        ssh.append(sh2)
    return (xp.reshape(b, t, D_MODEL), xs.reshape(bd, tn, D_MODEL), jnp.stack(pk), jnp.stack(pv),
            jnp.stack(pmk), jnp.stack(pmv), jnp.stack(pS), jnp.stack(psh),
            jnp.stack(sk), jnp.stack(sv), jnp.stack(sS), jnp.stack(ssh))
```

```python
import functools
import math

import jax
import jax.numpy as jnp
from jax import lax
from jax.experimental import pallas as pl
from jax.experimental.pallas import tpu as pltpu

F32 = jnp.float32
BF16 = jnp.bfloat16

D_MODEL = 1024
HEAD_DIM = 64
CHUNK = 64
A_WIDTH = 512
KV_WIDTH_A = 128
N_KV_A = 2
WINDOW = 128
PAST_LEN = 4096
ROPE_THETA = 10000.0
R_WIDTH = 512
N_HEADS_R = 8
DECAY_LORA = 64
AAA_LORA = 64
GATE_LORA = 160
RWKV_PROJ = 3 * R_WIDTH + DECAY_LORA + AAA_LORA + GATE_LORA
RWKV_PAD = 1920
IN_COLS_PAD = A_WIDTH + 2 * KV_WIDTH_A + RWKV_PAD
GN_EPS = 6.4e-4
MEM_LEN = 256
MEM_HEADS = 4
MEM_HEAD_DIM = 128
MEM_WIDTH = 512
D_FF = 2816
RMS_EPS = 1e-6
NEG_INF = -1e30
NORM_MIX_PRE, NORM_MIX_POST, NORM_X_PRE, NORM_X_POST, NORM_MEM, NORM_FFN_PRE, NORM_FFN_POST = range(7)
EXP_M05 = math.exp(-0.5)

LANES = 128
IN_MAIN = IN_COLS_PAD - LANES
VMEM_LIMIT = 58 * 1024 * 1024


def _params(sem):
    return pltpu.CompilerParams(dimension_semantics=sem, vmem_limit_bytes=VMEM_LIMIT)


def _rms(x, g):
    ms = jnp.mean(x * x, axis=-1, keepdims=True)
    return x * lax.rsqrt(ms + RMS_EPS) * g


def _mm(a, b):
    return jnp.dot(a.astype(BF16), b.astype(BF16), preferred_element_type=F32)


def _mm_nt(a, b):
    return lax.dot_general(a.astype(BF16), b.astype(BF16), (((1,), (1,)), ((), ())),
                           preferred_element_type=F32)


def _split(x):
    hi = x.astype(BF16)
    lo = (x - hi.astype(F32)).astype(BF16)
    return hi, lo


def _mm_lsplit(a_exact_bf16, x):
    hi, lo = _split(x)
    return (jnp.dot(a_exact_bf16, hi, preferred_element_type=F32)
            + jnp.dot(a_exact_bf16, lo, preferred_element_type=F32))


def _in_kernel(*refs, tm, t_seq, n_cast):
    x_ref, g_ref, w_ref, wt_ref, cos_ref, sin_ref, mu_ref, sh0_ref = refs[:8]
    cast_in = refs[8:8 + n_cast]
    q_ref, k_ref, v_ref, pr_ref, shl_ref = refs[8 + n_cast:13 + n_cast]
    cast_out = refs[13 + n_cast:13 + 2 * n_cast]
    carry_scr, wb_scr, wtb_scr = refs[-3:]
    for src, dst in zip(cast_in, cast_out):
        dst[...] = src[...].astype(BF16)

    @pl.when(pl.program_id(0) == 0)
    def _():
        wb_scr[...] = w_ref[...].astype(BF16)
        wtb_scr[...] = wt_ref[...].astype(BF16)

    h = _rms(x_ref[...], g_ref[NORM_MIX_PRE:NORM_MIX_PRE + 1, :]).astype(BF16)
    nt = (((1,), (1,)), ((), ()))
    p = jnp.concatenate([lax.dot_general(h, wb_scr[...], nt, preferred_element_type=F32),
                         lax.dot_general(h, wtb_scr[...], nt, preferred_element_type=F32)], axis=1)
    cos = cos_ref[...]
    sin = sin_ref[...]
    lane = lax.broadcasted_iota(jnp.int32, cos.shape, 1)
    first_half = (lane & (HEAD_DIM // 2)) == 0

    def rope(xc):
        sw = jnp.where(first_half, pltpu.roll(xc, LANES - HEAD_DIM // 2, 1), pltpu.roll(xc, HEAD_DIM // 2, 1))
        return xc * cos + sw * sin

    for j in range(A_WIDTH // LANES):
        q_ref[:, j * LANES:(j + 1) * LANES] = rope(p[:, j * LANES:(j + 1) * LANES])
    k_ref[...] = rope(p[:, A_WIDTH:A_WIDTH + KV_WIDTH_A])
    v_ref[...] = p[:, A_WIDTH + KV_WIDTH_A:A_WIDTH + 2 * KV_WIDTH_A]
    raw = p[:, A_WIDTH + 2 * KV_WIDTH_A:]
    rowi = lax.broadcasted_iota(jnp.int32, (tm, 1), 0)
    prev = pltpu.roll(raw, 1, 0)
    if tm <= t_seq:
        first = (pl.program_id(0) % (t_seq // tm)) == 0
        prev = jnp.where(rowi == 0, jnp.where(first, sh0_ref[0], carry_scr[...]), prev)
        carry_scr[...] = raw[tm - 1:tm, :]
        shl_ref[0] = raw[tm - 1:tm, :]
    else:
        for s in range(tm // t_seq):
            prev = jnp.where(rowi == s * t_seq, sh0_ref[s], prev)
            shl_ref[s] = raw[(s + 1) * t_seq - 1:(s + 1) * t_seq, :]
    pr_ref[...] = raw + (prev - raw) * mu_ref[...]


def _layer_block(shape, l):
    nd = len(shape)
    return pl.BlockSpec((None,) + tuple(shape), lambda *_: (l,) + (0,) * nd)


def _in_proj(x2d, l, gains, w_in, w_tail, cos_t, sin_t, mu_p, ls, shift0, t_seq, tm, casts=()):
    n = x2d.shape[0]
    n_seq = n // t_seq
    steps = n // tm
    tab_blocks = cos_t.shape[0] // tm
    row = lambda i: (i, 0)
    tab = lambda i: (i % tab_blocks, 0)
    if tm <= t_seq:
        seqs, seq_of = 1, lambda i: i // (t_seq // tm)
    else:
        seqs, seq_of = tm // t_seq, lambda i: i
    cast_in, cast_out, cast_shapes = [], [], []
    for w in casts:
        rows = w.shape[1] // steps
        cast_in.append(pl.BlockSpec((None, rows, w.shape[2]), lambda i: (l, i, 0)))
        cast_out.append(pl.BlockSpec((None, rows, w.shape[2]), lambda i: (0, i, 0)))
        cast_shapes.append(jax.ShapeDtypeStruct((1,) + w.shape[1:], BF16))
    outs = pl.pallas_call(
        functools.partial(_in_kernel, tm=tm, t_seq=t_seq, n_cast=len(casts)),
        grid=(steps,),
        in_specs=[pl.BlockSpec((tm, D_MODEL), row), _layer_block((8, D_MODEL), l),
                  pl.BlockSpec((None, IN_MAIN, D_MODEL), lambda i: (l, 0, 0), pipeline_mode=pl.Buffered(1)),
                  _layer_block((LANES, D_MODEL), l),
                  pl.BlockSpec((tm, LANES), tab), pl.BlockSpec((tm, LANES), tab),
                  _layer_block((1, RWKV_PAD), l),
                  pl.BlockSpec((None, seqs, 1, RWKV_PAD), lambda i: (ls, seq_of(i), 0, 0))] + cast_in,
        out_specs=[pl.BlockSpec((tm, A_WIDTH), row), pl.BlockSpec((tm, KV_WIDTH_A), row),
                   pl.BlockSpec((tm, KV_WIDTH_A), row), pl.BlockSpec((tm, RWKV_PAD), row),
                   pl.BlockSpec((seqs, 1, RWKV_PAD), lambda i: (seq_of(i), 0, 0))] + cast_out,
        out_shape=[jax.ShapeDtypeStruct((n, A_WIDTH), F32), jax.ShapeDtypeStruct((n, KV_WIDTH_A), F32),
                   jax.ShapeDtypeStruct((n, KV_WIDTH_A), F32), jax.ShapeDtypeStruct((n, RWKV_PAD), F32),
                   jax.ShapeDtypeStruct((n_seq, 1, RWKV_PAD), F32)] + cast_shapes,
        scratch_shapes=[pltpu.VMEM((1, RWKV_PAD), F32), pltpu.VMEM((IN_MAIN, D_MODEL), BF16),
                        pltpu.VMEM((LANES, D_MODEL), BF16)],
        compiler_params=_params(("arbitrary",)),
        name="in_proj",
    )(x2d, gains, w_in, w_tail, cos_t, sin_t, mu_p, shift0, *casts)
    return outs[:5], outs[5:]


def _sink_attend(jobs, sink_ref, t):
    nk = jobs[0][1].shape[0]
    nkp = 2 * LANES
    lane = lax.broadcasted_iota(jnp.int32, (t, LANES), 1)
    m0 = lane < HEAD_DIM
    coli = lax.broadcasted_iota(jnp.int32, (1, nkp), 1)
    fills = []
    for kv in range(N_KV_A):
        blocks = []
        for g in range(4):
            sg = sink_ref[4 * kv + g:4 * kv + g + 1, :]
            blocks.append(jnp.broadcast_to(jnp.where(coli == nk, jnp.concatenate([sg, sg], axis=1), NEG_INF),
                                           (t, nkp)))
        fills.append(jnp.concatenate(blocks, axis=0))
    k_tail = jnp.zeros((nkp - nk, LANES), F32)
    tail_row = lax.broadcasted_iota(jnp.int32, (nkp - nk, nkp), 0)
    tail_col = lax.broadcasted_iota(jnp.int32, (nkp - nk, nkp), 1)
    v_tail = jnp.where((tail_row == 0) & (tail_col >= LANES), 1.0, 0.0).astype(F32)
    ones = jnp.ones((nk, LANES), F32)
    scores = []
    for q_rows, kdup, _, kv, valid in jobs:
        parts = []
        for p in (2 * kv, 2 * kv + 1):
            qp = q_rows[:, p * LANES:(p + 1) * LANES]
            parts.append(jnp.where(m0, qp, 0.0))
            parts.append(jnp.where(m0, 0.0, qp))
        lhs = jnp.concatenate(parts, axis=0)
        s = _mm_nt(lhs, jnp.concatenate([kdup, k_tail], axis=0))
        keep = coli < nk
        if valid is not None:
            keep = keep & valid
        scores.append(jnp.where(keep, s, fills[kv]))
    exps = [jnp.exp(s - jnp.max(s, axis=-1, keepdims=True)) for s in scores]
    outs = []
    for (_, _, vdup, _, _), e in zip(jobs, exps):
        v2 = jnp.concatenate([jnp.concatenate([vdup, ones], axis=1), v_tail], axis=0)
        o2 = _mm(e, v2)
        o = o2[:, :LANES] / o2[:, LANES:]
        outs.append([jnp.where(m0, o[(2 * pi) * t:(2 * pi + 1) * t], o[(2 * pi + 1) * t:(2 * pi + 2) * t])
                     for pi in range(2)])
    return outs


SWA_SCALE = HEAD_DIM ** -0.5


def _dup_heads(x):
    lane = lax.broadcasted_iota(jnp.int32, x.shape, 1)
    m0 = lane < HEAD_DIM
    xs = pltpu.roll(x, HEAD_DIM, 1)
    return [jnp.where(m0, x, xs), jnp.where(m0, xs, x)]


SWA_TQ = 512


def _swa_prompt_kernel(q_ref, kp_ref, kc_ref, vp_ref, vc_ref, sink_ref, o_ref):
    i = pl.program_id(1)
    k = jnp.concatenate([kp_ref[...], kc_ref[...]], axis=0)
    v = jnp.concatenate([vp_ref[...], vc_ref[...]], axis=0)
    kd = _dup_heads(k * SWA_SCALE)
    vd = _dup_heads(v)
    nk = 3 * CHUNK
    slot = lax.broadcasted_iota(jnp.int32, (1, 2 * LANES), 1) // CHUNK
    jobs = []
    for j in range(SWA_TQ // CHUNK):
        qj = q_ref[j * CHUNK:(j + 1) * CHUNK, :]
        valid = (slot + (i * (SWA_TQ // CHUNK) + j - 2)) >= 0
        for kv in range(N_KV_A):
            jobs.append((qj, kd[kv][j * CHUNK:j * CHUNK + nk], vd[kv][j * CHUNK:j * CHUNK + nk], kv, valid))
    outs = _sink_attend(jobs, sink_ref, CHUNK)
    for n, out in enumerate(outs):
        j, kv = divmod(n, N_KV_A)
        for pi in range(2):
            p = 2 * kv + pi
            o_ref[j * CHUNK:(j + 1) * CHUNK, p * LANES:(p + 1) * LANES] = out[pi]


def _swa_prompt(q, k, v, l, sink_b, b, t):
    n = b * t
    nq = t // SWA_TQ
    qmap = lambda bi, i: (bi * nq + i, 0)
    pmap = lambda bi, i: (jnp.maximum(bi * (t // WINDOW) + (SWA_TQ // WINDOW) * i - 1, 0), 0)
    return pl.pallas_call(
        _swa_prompt_kernel,
        grid=(b, nq),
        in_specs=[pl.BlockSpec((SWA_TQ, A_WIDTH), qmap),
                  pl.BlockSpec((WINDOW, KV_WIDTH_A), pmap), pl.BlockSpec((SWA_TQ, KV_WIDTH_A), qmap),
                  pl.BlockSpec((WINDOW, KV_WIDTH_A), pmap), pl.BlockSpec((SWA_TQ, KV_WIDTH_A), qmap),
                  _layer_block((8, LANES), l)],
        out_specs=pl.BlockSpec((SWA_TQ, A_WIDTH), qmap),
        out_shape=jax.ShapeDtypeStruct((n, A_WIDTH), F32),
        compiler_params=_params(("parallel", "parallel")),
        name="swa_prompt",
    )(q, k, k, v, v, sink_b)


def _swa_sample_kernel(q_ref, kc_ref, kn_ref, vc_ref, vn_ref, sink_ref, o_ref, *, t, nb):
    jobs = []
    for bi in range(nb):
        rows = slice(bi * t, (bi + 1) * t)
        kd = _dup_heads(jnp.concatenate([kc_ref[bi], kn_ref[rows, :]], axis=0) * SWA_SCALE)
        vd = _dup_heads(jnp.concatenate([vc_ref[bi], vn_ref[rows, :]], axis=0))
        for kv in range(N_KV_A):
            jobs.append((q_ref[rows, :], kd[kv], vd[kv], kv, None))
    outs = _sink_attend(jobs, sink_ref, t)
    for n, out in enumerate(outs):
        bi, kv = divmod(n, N_KV_A)
        for pi in range(2):
            p = 2 * kv + pi
            o_ref[bi * t:(bi + 1) * t, p * LANES:(p + 1) * LANES] = out[pi]


def _swa_sample(q, k, v, l, kc, vc, sink_b, b, t):
    n = b * t
    cache = kc.shape[2]
    nb = 8 if b % 8 == 0 else 1
    row = lambda bi: (bi, 0)
    cmap = lambda bi: (l, bi, 0, 0)
    return pl.pallas_call(
        functools.partial(_swa_sample_kernel, t=t, nb=nb),
        grid=(b // nb,),
        in_specs=[pl.BlockSpec((nb * t, A_WIDTH), row),
                  pl.BlockSpec((None, nb, cache, KV_WIDTH_A), cmap), pl.BlockSpec((nb * t, KV_WIDTH_A), row),
                  pl.BlockSpec((None, nb, cache, KV_WIDTH_A), cmap), pl.BlockSpec((nb * t, KV_WIDTH_A), row),
                  _layer_block((8, LANES), l)],
        out_specs=pl.BlockSpec((nb * t, A_WIDTH), row),
        out_shape=jax.ShapeDtypeStruct((n, A_WIDTH), F32),
        compiler_params=_params(("parallel",)),
        name="swa_sample",
    )(q, kc, k, vc, v, sink_b)


RWKV_SUB = 256


def _interleave(gens):
    gens = list(gens)
    while gens:
        for g in list(gens):
            try:
                next(g)
            except StopIteration:
                gens.remove(g)


def _rwkv_kernel(pr_ref, s0_ref, vec_ref, rk_ref, w2_ref, a2_ref, g2_ref, hsum_ref,
                 ltri_ref, out_ref, sfin_ref,
                 s_scr, rt_s, at_s, bt_s, kt_s, bb_s, kb_s, v_s, et_s, y_s, bonus_s, gate_s,
                 *, c_len, tt, nb, sub):
    i = pl.program_id(1)
    n_i = pl.num_programs(1)
    n_rows = nb * tt

    @pl.when(i == 0)
    def _():
        zero = jnp.zeros((HEAD_DIM, HEAD_DIM), F32)
        for bi in range(nb):
            for p in range(4):
                s_scr[bi, p] = jnp.concatenate(
                    [jnp.concatenate([s0_ref[bi, 2 * p].astype(F32), zero], axis=1),
                     jnp.concatenate([zero, s0_ref[bi, 2 * p + 1].astype(F32)], axis=1)], axis=0)

    w0 = vec_ref[0:1, :]
    a0 = vec_ref[1:2, :]
    k_k = vec_ref[2:3, :]
    k_a = vec_ref[3:4, :]
    gn_g = vec_ref[4:5, :]
    gn_b = vec_ref[5:6, :]
    hsum = hsum_ref[...]
    hw = hsum.shape[0]
    halves = [slice(j * hw, (j + 1) * hw) for j in range(R_WIDTH // hw)]

    def head_sum(x):
        return _mm(x, hsum)

    n2 = 2 * c_len
    ti = lax.broadcasted_iota(jnp.int32, (c_len, n2), 0)
    lane2 = lax.broadcasted_iota(jnp.int32, (c_len, n2), 1)
    si = lane2 & (c_len - 1)
    first_c = lane2 < c_len
    strict = ti > si
    incl = ti >= si
    eye = jnp.where(ti == si, 1.0, 0.0).astype(F32)
    masks = []
    half = 1
    while half < c_len:
        blk = 2 * half
        masks.append(((ti & ~(blk - 1)) == (si & ~(blk - 1))) & ((ti & half) != 0) & ((si & half) == 0))
        half = blk
    m0 = lax.broadcasted_iota(jnp.int32, (c_len, LANES), 1) < HEAD_DIM
    rl = lax.broadcasted_iota(jnp.int32, (LANES, LANES), 0)
    cl = lax.broadcasted_iota(jnp.int32, (LANES, LANES), 1)
    same_head = (rl >= HEAD_DIM) == (cl >= HEAD_DIM)
    fused = n2 == LANES
    zeros_c = jnp.zeros((c_len, LANES), F32)

    def bd(x):
        return jnp.concatenate([jnp.where(m0, x, 0.0), jnp.where(m0, 0.0, x)], axis=0)

    def bd_t(x):
        return jnp.concatenate([jnp.where(first_c, x, 0.0), jnp.where(first_c, 0.0, x)], axis=0)

    def prologue(lo, hi):
        n = hi - lo
        bi0 = lo // tt

        def shifted(a, b):
            if hi - lo <= tt:
                return pr_ref[bi0, lo - bi0 * tt:hi - bi0 * tt, a:b]
            return pr_ref[bi0:hi // tt, :, a:b].reshape(n, b - a)

        xwa = shifted(3 * R_WIDTH, 3 * R_WIDTH + LANES)
        xg = shifted(3 * R_WIDTH + LANES, RWKV_PAD)
        th = jnp.tanh(xwa)
        sg = jax.nn.sigmoid(xg)
        ltri = ltri_ref[...]
        yield
        for cs in halves:
            r = shifted(cs.start, cs.stop)
            k = shifted(R_WIDTH + cs.start, R_WIDTH + cs.stop)
            v = shifted(2 * R_WIDTH + cs.start, 2 * R_WIDTH + cs.stop)
            yield
            z = w0[:, cs] + _mm(th, w2_ref[:, cs])
            wlog = -EXP_M05 * jax.nn.sigmoid(z)
            a = jax.nn.sigmoid(a0[:, cs] + _mm(xwa, a2_ref[:, cs]))
            gate_s[lo:hi, cs] = _mm(sg, g2_ref[:, cs])
            yield
            kk = k * k_k[:, cs]
            kk = kk * lax.rsqrt(jnp.maximum(head_sum(kk * kk), 1e-24))
            k_f = k * (1.0 + (a - 1.0) * k_a[:, cs])
            bvec = kk * a
            bonus_s[lo:hi, cs] = head_sum(r * k_f * rk_ref[:, cs]) * v
            yield
            cum = _mm_lsplit(ltri, wlog)
            tot = jnp.concatenate([jnp.broadcast_to(cum[r1 - 1:r1, :], (c_len, hw))
                                   for r1 in range(c_len, n + 1, c_len)], axis=0)
            e_in = jnp.exp(cum)
            e_inv = jnp.exp(-cum)
            e_end = jnp.exp(tot - cum)
            rt_s[lo:hi, cs] = r * e_in
            at_s[lo:hi, cs] = -kk * jnp.exp(cum - wlog)
            yield
            bt_s[lo:hi, cs] = bvec * e_inv
            kt_s[lo:hi, cs] = k_f * e_inv
            bb_s[lo:hi, cs] = bvec * e_end
            kb_s[lo:hi, cs] = k_f * e_end
            v_s[lo:hi, cs] = v
            et_s[lo:hi, cs] = jnp.exp(tot)
            yield

    s_state = [[s_scr[bi, p] for p in range(4)] for bi in range(nb)]

    def chains(lo, hi):
        chs = [(r0, p) for r0 in range(lo, hi, c_len) for p in range(4)]

        def cat(ref, ch):
            return ref[ch[0]:ch[0] + c_len, ch[1] * LANES:(ch[1] + 1) * LANES]

        a_ab, a_ak, a_rb, a_rk = {}, {}, {}, {}
        for ch in chs:
            at, rt, rb, rkt = cat(at_s, ch), cat(rt_s, ch), bd(cat(bt_s, ch)), bd(cat(kt_s, ch))
            if fused:
                amat = _mm_nt(jnp.concatenate([at, rt], axis=0), jnp.concatenate([rb, rkt], axis=0))
                q_ab, q_ak = amat[:c_len, :n2], amat[:c_len, n2:]
                q_rb, q_rk = amat[c_len:, :n2], amat[c_len:, n2:]
            else:
                q_ab, q_ak, q_rb, q_rk = _mm_nt(at, rb), _mm_nt(at, rkt), _mm_nt(rt, rb), _mm_nt(rt, rkt)
            a_ab[ch] = jnp.where(strict, q_ab, 0.0)
            a_ak[ch] = jnp.where(strict, q_ak, 0.0)
            a_rb[ch] = jnp.where(incl, q_rb, 0.0)
            a_rk[ch] = jnp.where(incl, q_rk, 0.0)
        yield
        tinv = {ch: eye + jnp.where(masks[0], a_ab[ch], 0.0) for ch in chs}
        for m in masks[1:]:
            pe = {ch: _mm(tinv[ch], bd_t(jnp.where(m, a_ab[ch], 0.0))) for ch in chs}
            yield
            tinv = {ch: tinv[ch] + _mm(pe[ch], bd_t(tinv[ch])) for ch in chs}
            yield
        zv = {ch: _mm(a_ak[ch], bd(cat(v_s, ch))) for ch in chs}
        yield
        r_hat, y_hat, m_mat, g_mat = {}, {}, {}, {}
        if fused:
            w = {ch: _mm(tinv[ch], jnp.concatenate([bd(cat(at_s, ch)), bd(zv[ch])], axis=1)) for ch in chs}
            yield
            for ch in chs:
                rhs2 = jnp.concatenate(
                    [jnp.concatenate([bd(w[ch][:, :LANES]), bd(w[ch][:, LANES:])], axis=1),
                     jnp.concatenate([jnp.zeros((n2, LANES), F32), bd(cat(v_s, ch))], axis=1)], axis=0)
                ry = _mm(jnp.concatenate([a_rb[ch], a_rk[ch]], axis=1), rhs2)
                r_hat[ch] = cat(rt_s, ch) + ry[:, :LANES]
                y_hat[ch] = ry[:, LANES:]
            yield
            for ch in chs:
                lhs_t = jnp.concatenate([w[ch], jnp.concatenate([zeros_c, cat(v_s, ch)], axis=1)], axis=0)
                mg = _mm(lhs_t.T, jnp.concatenate([cat(bb_s, ch), cat(kb_s, ch)], axis=0))
                m_mat[ch] = jnp.where(same_head, mg[:LANES], 0.0)
                g_mat[ch] = jnp.where(same_head, mg[LANES:], 0.0)
            yield
        else:
            a_hat = {ch: _mm(tinv[ch], bd(cat(at_s, ch))) for ch in chs}
            u_hat = {ch: _mm(tinv[ch], bd(zv[ch])) for ch in chs}
            yield
            for ch in chs:
                r_hat[ch] = cat(rt_s, ch) + _mm(a_rb[ch], bd(a_hat[ch]))
                y_hat[ch] = _mm(a_rb[ch], bd(u_hat[ch])) + _mm(a_rk[ch], bd(cat(v_s, ch)))
            yield
            for ch in chs:
                m_mat[ch] = jnp.where(same_head, _mm(a_hat[ch].T, cat(bb_s, ch)), 0.0)
                g_mat[ch] = jnp.where(same_head,
                                      _mm(u_hat[ch].T, cat(bb_s, ch)) + _mm(cat(v_s, ch).T, cat(kb_s, ch)), 0.0)
            yield
        for r0 in range(lo, hi, c_len):
            s_cur = s_state[r0 // tt]
            for p in range(4):
                ch = (r0, p)
                ls = slice(p * LANES, (p + 1) * LANES)
                e_last = et_s[r0:r0 + 1, ls]
                y_s[r0:r0 + c_len, ls] = y_hat[ch] + _mm_nt(r_hat[ch], s_cur[p])
                s_cur[p] = s_cur[p] * e_last + _mm(s_cur[p], m_mat[ch]) + g_mat[ch]
            yield

    def epilogue(lo, hi):
        for cs in halves:
            y = y_s[lo:hi, cs]
            mean = head_sum(y) * (1.0 / HEAD_DIM)
            yc = y - mean
            yield
            var = head_sum(yc * yc) * (1.0 / HEAD_DIM)
            yn = yc * lax.rsqrt(var + GN_EPS) * gn_g[:, cs] + gn_b[:, cs]
            res = (yn + bonus_s[lo:hi, cs]) * gate_s[lo:hi, cs]
            for r0 in range(lo, hi, tt) if hi - lo > tt else [lo]:
                r1 = min(r0 + tt, hi)
                out_ref[r0 // tt, r0 % tt:r0 % tt + (r1 - r0), cs] = res[r0 - lo:r1 - lo]
            yield

    subs = [(lo, lo + sub) for lo in range(0, n_rows, sub)]
    _interleave([prologue(*subs[0])])
    for n, sb in enumerate(subs):
        phases = [chains(*sb)]
        if n + 1 < len(subs):
            phases.append(prologue(*subs[n + 1]))
        if n > 0:
            phases.append(epilogue(*subs[n - 1]))
        _interleave(phases)
    _interleave([epilogue(*subs[-1])])

    for bi in range(nb):
        for p in range(4):
            s_scr[bi, p] = s_state[bi][p]

    @pl.when(i == n_i - 1)
    def _():
        for bi in range(nb):
            for p in range(4):
                s_pair = s_state[bi][p]
                sfin_ref[bi, 2 * p] = s_pair[:HEAD_DIM, :HEAD_DIM].astype(sfin_ref.dtype)
                sfin_ref[bi, 2 * p + 1] = s_pair[HEAD_DIM:, HEAD_DIM:].astype(sfin_ref.dtype)


def _rwkv(pr3, l, ls, s0, vecs, rk_flat, w2p, a2p, g2p, hsum, tt, c_len, nb):
    b, t, _ = pr3.shape
    n_rows = nb * tt
    sub = min(RWKV_SUB, n_rows)
    ri = jnp.arange(sub)[:, None]
    ci = jnp.arange(sub)[None, :]
    same = (ri // c_len) == (ci // c_len)
    ltri = (same & (ri >= ci)).astype(BF16)
    hw = hsum.shape[0]
    blk = lambda bi, i: (bi, i, 0)
    const = lambda bi, i: (0, 0)
    big = lambda: pltpu.VMEM((n_rows, R_WIDTH), F32)
    return pl.pallas_call(
        functools.partial(_rwkv_kernel, c_len=c_len, tt=tt, nb=nb, sub=sub),
        grid=(b // nb, t // tt),
        in_specs=[pl.BlockSpec((nb, tt, RWKV_PAD), blk),
                  pl.BlockSpec((None, nb, N_HEADS_R, HEAD_DIM, HEAD_DIM), lambda bi, i: (ls, bi, 0, 0, 0)),
                  _layer_block((6, R_WIDTH), l), _layer_block((1, R_WIDTH), l),
                  _layer_block((LANES, R_WIDTH), l), _layer_block((LANES, R_WIDTH), l),
                  _layer_block((2 * LANES, R_WIDTH), l), pl.BlockSpec((hw, hw), const),
                  pl.BlockSpec((sub, sub), const)],
        out_specs=[pl.BlockSpec((nb, tt, R_WIDTH), blk),
                   pl.BlockSpec((nb, N_HEADS_R, HEAD_DIM, HEAD_DIM), lambda bi, i: (bi, 0, 0, 0))],
        out_shape=[jax.ShapeDtypeStruct((b, t, R_WIDTH), F32),
                   jax.ShapeDtypeStruct((b, N_HEADS_R, HEAD_DIM, HEAD_DIM), s0.dtype)],
        scratch_shapes=[pltpu.VMEM((nb, 4, LANES, LANES), F32),
                        big(), big(), big(), big(), big(), big(), big(), big(), big(), big(), big()],
        compiler_params=_params(("parallel", "arbitrary")),
        name="rwkv_mix",
    )(pr3, s0, vecs, rk_flat, w2p, a2p, g2p, hsum, ltri)


POST_PARTS = 2


def _post_kernel(attn_ref, rw_ref, x_ref, g_ref, wout_ref, wq_ref, wo_ref, mk_ref, mv_ref, o_ref, *, nb, t):
    scale = MEM_HEAD_DIM ** -0.5
    cols = lambda hd: slice(hd * MEM_HEAD_DIM, (hd + 1) * MEM_HEAD_DIM)
    head_rows = lambda hd: pl.ds(hd, MEM_LEN, stride=MEM_HEADS)
    ones = jnp.ones((MEM_LEN, MEM_HEAD_DIM), F32)

    def part(lo, hi):
        rows = slice(lo, hi)
        m = _mm(attn_ref[rows, :], wout_ref[0:A_WIDTH, :]) + _mm(rw_ref[rows, :], wout_ref[A_WIDTH:, :])
        yield
        x1 = x_ref[rows, :] + _rms(m, g_ref[NORM_MIX_POST:NORM_MIX_POST + 1, :])
        hq = _rms(x1, g_ref[NORM_X_PRE:NORM_X_PRE + 1, :])
        yield
        q = _mm(hq, wq_ref[...])
        yield
        if nb == 1:
            jobs = [(0, lo, hi, hd) for hd in range(MEM_HEADS)]
        else:
            jobs = [(r0 // t, r0, r0 + t, hd) for r0 in range(lo, hi, t) for hd in range(MEM_HEADS)]
        scores = [_mm_nt(q[r0 - lo:r1 - lo, cols(hd)], mk_ref[bi, head_rows(hd), :]) * scale
                  for bi, r0, r1, hd in jobs]
        yield
        exps = [jnp.exp(s - jnp.max(s, axis=-1, keepdims=True)) for s in scores]
        yield
        outs = []
        for (bi, _, _, hd), e in zip(jobs, exps):
            o2 = _mm(e, jnp.concatenate([mv_ref[bi, head_rows(hd), :], ones], axis=1))
            outs.append(o2[:, :MEM_HEAD_DIM] / o2[:, MEM_HEAD_DIM:])
        blocks = [jnp.concatenate(outs[j:j + MEM_HEADS], axis=1) for j in range(0, len(outs), MEM_HEADS)]
        o = blocks[0] if len(blocks) == 1 else jnp.concatenate(blocks, axis=0)
        yield
        c = _mm(o, wo_ref[...])
        yield
        o_ref[rows, :] = x1 + _rms(c, g_ref[NORM_X_POST:NORM_X_POST + 1, :])

    n_rows = nb * t
    n_parts = POST_PARTS if (nb == 1 or nb % POST_PARTS == 0) and n_rows % (8 * POST_PARTS) == 0 else 1
    step = n_rows // n_parts
    gens = [part(j * step, (j + 1) * step) for j in range(n_parts)]
    for j, gen in enumerate(gens):
        for _ in range(n_parts - 1 - j):
            next(gen)
    _interleave(gens)


def _post(attn, rw, x2d, l, gains, w_out, w_mq, w_mo, lm, mk, mv, nb, t, tiles_per_batch):
    n = x2d.shape[0]
    tm = nb * t
    row = lambda i: (i, 0)
    if nb == 1:
        mmap = lambda i: (lm, i // tiles_per_batch, 0, 0)
    else:
        mmap = lambda i: (lm, i, 0, 0)
    mem_rows = MEM_LEN * MEM_HEADS
    return pl.pallas_call(
        functools.partial(_post_kernel, nb=nb, t=t),
        grid=(n // tm,),
        in_specs=[pl.BlockSpec((tm, A_WIDTH), row), pl.BlockSpec((tm, R_WIDTH), row),
                  pl.BlockSpec((tm, D_MODEL), row), _layer_block((8, D_MODEL), l),
                  _layer_block((D_MODEL, D_MODEL), 0), _layer_block((D_MODEL, MEM_WIDTH), 0),
                  _layer_block((MEM_WIDTH, D_MODEL), 0),
                  pl.BlockSpec((None, nb, mem_rows, MEM_HEAD_DIM), mmap),
                  pl.BlockSpec((None, nb, mem_rows, MEM_HEAD_DIM), mmap)],
        out_specs=pl.BlockSpec((tm, D_MODEL), row),
        out_shape=jax.ShapeDtypeStruct((n, D_MODEL), F32),
        compiler_params=_params(("parallel",)),
        name="post_mix",
    )(attn, rw, x2d, gains, w_out, w_mq, w_mo, mk, mv)


FFN_CHUNK = 256


def _ffn_kernel(x_ref, g_ref, wgu_ref, wd_ref, o_ref):
    x = x_ref[...]
    h = _rms(x, g_ref[NORM_FFN_PRE:NORM_FFN_PRE + 1, :]).astype(BF16)
    acc = None
    for j in range(D_FF // FFN_CHUNK):
        cols = slice(j * FFN_CHUNK, (j + 1) * FFN_CHUNK)
        gate = jnp.dot(h, wgu_ref[:, cols], preferred_element_type=F32)
        up = jnp.dot(h, wgu_ref[:, D_FF + j * FFN_CHUNK:D_FF + (j + 1) * FFN_CHUNK], preferred_element_type=F32)
        act = (gate * jax.nn.sigmoid(gate)) * up
        part = _mm(act, wd_ref[cols, :])
        acc = part if acc is None else acc + part
    o_ref[...] = x + _rms(acc, g_ref[NORM_FFN_POST:NORM_FFN_POST + 1, :])


def _ffn(x2d, l, gains, w_gu, w_dn, tm):
    n = x2d.shape[0]
    row = lambda i: (i, 0)
    resident = dict(pipeline_mode=pl.Buffered(1))
    return pl.pallas_call(
        _ffn_kernel,
        grid=(n // tm,),
        in_specs=[pl.BlockSpec((tm, D_MODEL), row), _layer_block((8, D_MODEL), l),
                  pl.BlockSpec((None, D_MODEL, 2 * D_FF), lambda i: (0, 0, 0), **resident),
                  pl.BlockSpec((None, D_FF, D_MODEL), lambda i: (0, 0, 0), **resident)],
        out_specs=pl.BlockSpec((tm, D_MODEL), row),
        out_shape=jax.ShapeDtypeStruct((n, D_MODEL), F32),
        compiler_params=_params(("parallel",)),
        name="ffn",
    )(x2d, gains, w_gu, w_dn)


def _memkv_kernel(x_ref, g_ref, w_ref, k_ref, v_ref):
    kv = _mm(_rms(x_ref[...], g_ref[NORM_MEM:NORM_MEM + 1, :]), w_ref[...])
    for hd in range(MEM_HEADS):
        rows = pl.ds(hd, MEM_LEN, stride=MEM_HEADS)
        k_ref[0, rows, :] = kv[:, hd * MEM_HEAD_DIM:(hd + 1) * MEM_HEAD_DIM]
        v_ref[0, rows, :] = kv[:, MEM_WIDTH + hd * MEM_HEAD_DIM:MEM_WIDTH + (hd + 1) * MEM_HEAD_DIM]


def _memkv(mem2d, l, gains, w_kv):
    n = mem2d.shape[0]
    nbatch = n // MEM_LEN
    mem_rows = MEM_LEN * MEM_HEADS
    out = jax.ShapeDtypeStruct((nbatch, mem_rows, MEM_HEAD_DIM), F32)
    ospec = pl.BlockSpec((1, mem_rows, MEM_HEAD_DIM), lambda i: (i, 0, 0))
    return pl.pallas_call(
        _memkv_kernel,
        grid=(nbatch,),
        in_specs=[pl.BlockSpec((MEM_LEN, D_MODEL), lambda i: (i, 0)), _layer_block((8, D_MODEL), l),
                  _layer_block((D_MODEL, 2 * MEM_WIDTH), l)],
        out_specs=[ospec, ospec],
        out_shape=[out, out],
        compiler_params=_params(("parallel",)),
        name="mem_kv",
    )(mem2d, gains, w_kv)


def _rope_tables(pos):
    half = HEAD_DIM // 2
    lane = jnp.arange(LANES)
    inv = ROPE_THETA ** (-(lane % half).astype(F32) / half)
    sign = jnp.where((lane % HEAD_DIM) < half, -1.0, 1.0).astype(F32)
    ang = pos.astype(F32)[:, None] * inv[None, :]
    return jnp.cos(ang), jnp.sin(ang) * sign[None, :]


def _stacked_params(norm_gains, w_in, attn_sink, shift_mu, rwkv_vecs, rwkv_rk, rwkv_w2, rwkv_a2, rwkv_g2,
                    w_out, w_mem_q, w_mem_kv, w_mem_o, w_gate_up, w_down):
    depth = w_in.shape[0]
    return dict(
        gains=jnp.pad(norm_gains, ((0, 0), (0, 8 - norm_gains.shape[1]), (0, 0))),
        w_in=jnp.swapaxes(w_in, 1, 2),
        w_tail=jnp.pad(jnp.swapaxes(w_in[:, :, IN_MAIN:], 1, 2),
                       ((0, 0), (0, IN_MAIN + LANES - w_in.shape[2]), (0, 0))),
        sink=jnp.broadcast_to(attn_sink[:, :, None], attn_sink.shape + (LANES,)).astype(F32),
        mu=jnp.pad(shift_mu, ((0, 0), (0, RWKV_PAD - RWKV_PROJ)))[:, None, :],
        vecs=rwkv_vecs, rk=rwkv_rk.reshape(depth, 1, R_WIDTH),
        w2=jnp.pad(rwkv_w2, ((0, 0), (0, LANES - DECAY_LORA), (0, 0))).astype(BF16),
        a2=jnp.pad(rwkv_a2, ((0, 0), (DECAY_LORA, LANES - DECAY_LORA - AAA_LORA), (0, 0))).astype(BF16),
        g2=jnp.pad(rwkv_g2, ((0, 0), (0, 2 * LANES - GATE_LORA), (0, 0))).astype(BF16),
        w_mkv=w_mem_kv,
        w_out=w_out, w_mq=w_mem_q, w_mo=w_mem_o, w_gu=w_gate_up, w_dn=w_down)


def _head_sum_matrix():
    i = jnp.arange(2 * LANES)
    return ((i[:, None] // HEAD_DIM) == (i[None, :] // HEAD_DIM)).astype(BF16)


def _layer(x2d, l, b, t, tabs, lm, mk, mv, swa_cache, ls, s0, shift0, sp, hsum, in_tm, rw_tt, post_nb, post_t,
           ffn_tm, late_w=None):
    to_cast = () if late_w else tuple(sp[name] for name in ("w_out", "w_mq", "w_mo", "w_gu", "w_dn"))
    (q, k, v, pr, shift_new), cast = _in_proj(x2d, l, sp["gains"], sp["w_in"], sp["w_tail"], tabs[0], tabs[1], sp["mu"], ls,
                                              shift0, t, in_tm, to_cast)
    late_w = late_w or tuple(cast)
    if swa_cache is None:
        attn = _swa_prompt(q, k, v, l, sp["sink"], b, t)
    else:
        attn = _swa_sample(q, k, v, l, swa_cache[0], swa_cache[1], sp["sink"], b, t)
    pr3 = pr.reshape(b, t, RWKV_PAD)
    rw, s_fin = _rwkv(pr3, l, ls, s0, sp["vecs"], sp["rk"], sp["w2"], sp["a2"], sp["g2"], hsum,
                      rw_tt[0], min(CHUNK, t), rw_tt[1])
    w_out, w_mq, w_mo, w_gu, w_dn = late_w
    x2 = _post(attn, rw.reshape(b * t, R_WIDTH), x2d, l, sp["gains"], w_out, w_mq, w_mo,
               lm, mk, mv, post_nb, post_t, t // post_t)
    x3 = _ffn(x2, l, sp["gains"], w_gu, w_dn, ffn_tm)
    return x3, k, v, s_fin, shift_new[:, :, :RWKV_PROJ], late_w


def kernel(x_prompt, mem_prompt, x_sample, cache_swa_k, cache_swa_v, cache_mem_k, cache_mem_v, state_rwkv,
           state_shift, norm_gains, w_in, attn_sink, shift_mu, rwkv_vecs, rwkv_rk, rwkv_w2, rwkv_a2, rwkv_g2,
           w_out, w_mem_q, w_mem_kv, w_mem_o, w_gate_up, w_down):
    b, t, _ = x_prompt.shape
    bd, tn, _ = x_sample.shape
    depth = w_in.shape[0]
    m_len = mem_prompt.shape[1]
    cache_len = cache_swa_k.shape[2]
    tabs_p = _rope_tables(jnp.arange(t, dtype=jnp.int32))
    cs, sn = _rope_tables(PAST_LEN + jnp.arange(tn, dtype=jnp.int32))
    tabs_s = (jnp.tile(cs, (bd, 1)), jnp.tile(sn, (bd, 1)))
    hsum = _head_sum_matrix()
    sp = _stacked_params(norm_gains, w_in, attn_sink, shift_mu, rwkv_vecs, rwkv_rk, rwkv_w2, rwkv_a2, rwkv_g2,
                         w_out, w_mem_q, w_mem_kv, w_mem_o, w_gate_up, w_down)
    s0_p = jnp.zeros((1, b, N_HEADS_R, HEAD_DIM, HEAD_DIM), F32)
    sh0_p = jnp.zeros((1, b, 1, RWKV_PAD), F32)
    sh0_s = jnp.pad(state_shift, ((0, 0), (0, 0), (0, 0), (0, RWKV_PAD - RWKV_PROJ)))
    swa_cache = (cache_swa_k.reshape(depth, bd, cache_len, KV_WIDTH_A),
                 cache_swa_v.reshape(depth, bd, cache_len, KV_WIDTH_A))
    mem_rows = m_len * MEM_HEADS
    cmk = cache_mem_k.reshape(depth, bd, mem_rows, MEM_HEAD_DIM)
    cmv = cache_mem_v.reshape(depth, bd, mem_rows, MEM_HEAD_DIM)
    xp = x_prompt.reshape(b * t, D_MODEL)
    xs = x_sample.reshape(bd * tn, D_MODEL)
    mem2d = mem_prompt.reshape(b * m_len, D_MODEL)
    in_tm_p = min(1024, t)
    rw_tt_p = (min(512, t), 1)
    rw_tt_s = (tn, 16 if bd % 16 == 0 else 1)
    post_t_p = min(1024, t)
    post_nb_s = 8 if bd % 8 == 0 else bd
    ffn_tm_p = min(512, b * t)
    keep = t - min(WINDOW, t)
    pk, pv, pmk, pmv, pS, psh = [], [], [], [], [], []
    sk, sv, sS, ssh = [], [], [], []
    for l in range(depth):
        mk_l, mv_l = _memkv(mem2d, l, sp["gains"], sp["w_mkv"])
        xp, k_l, v_l, s_l, sh_l, late_w = _layer(xp, l, b, t, tabs_p, 0, mk_l[None], mv_l[None], None, 0, s0_p, sh0_p,
                                                sp, hsum, in_tm_p, rw_tt_p, 1, post_t_p, ffn_tm_p)
        pk.append(k_l.reshape(b, t, KV_WIDTH_A)[:, keep:].reshape(b, t - keep, N_KV_A, HEAD_DIM))
        pv.append(v_l.reshape(b, t, KV_WIDTH_A)[:, keep:].reshape(b, t - keep, N_KV_A, HEAD_DIM))
        pmk.append(mk_l.reshape(b, m_len, MEM_HEADS, MEM_HEAD_DIM))
        pmv.append(mv_l.reshape(b, m_len, MEM_HEADS, MEM_HEAD_DIM))
        pS.append(s_l)
        psh.append(sh_l)

        xs, k2, v2, s2, sh2, _ = _layer(xs, l, bd, tn, tabs_s, l, cmk, cmv, swa_cache, l, state_rwkv, sh0_s, sp,
                                        hsum, bd * tn, rw_tt_s, post_nb_s, tn, bd * tn, late_w)
        sk.append(k2.reshape(bd, tn, N_KV_A, HEAD_DIM))
        sv.append(v2.reshape(bd, tn, N_KV_A, HEAD_DIM))
        sS.append(s2)
        ssh.append(sh2)
    return (xp.reshape(b, t, D_MODEL), xs.reshape(bd, tn, D_MODEL), jnp.stack(pk), jnp.stack(pv),
            jnp.stack(pmk), jnp.stack(pmv), jnp.stack(pS), jnp.stack(psh),
            jnp.stack(sk), jnp.stack(sv), jnp.stack(sS), jnp.stack(ssh))
```

```python
import functools
import math

import jax
import jax.numpy as jnp
from jax import lax
from jax.experimental import pallas as pl
from jax.experimental.pallas import tpu as pltpu

F32 = jnp.float32
BF16 = jnp.bfloat16

D_MODEL = 1024
HEAD_DIM = 64
CHUNK = 64
A_WIDTH = 512
KV_WIDTH_A = 128
N_KV_A = 2
WINDOW = 128
PAST_LEN = 4096
ROPE_THETA = 10000.0
R_WIDTH = 512
N_HEADS_R = 8
DECAY_LORA = 64
AAA_LORA = 64
GATE_LORA = 160
RWKV_PROJ = 3 * R_WIDTH + DECAY_LORA + AAA_LORA + GATE_LORA
RWKV_PAD = 1920
IN_COLS_PAD = A_WIDTH + 2 * KV_WIDTH_A + RWKV_PAD
GN_EPS = 6.4e-4
MEM_LEN = 256
MEM_HEADS = 4
MEM_HEAD_DIM = 128
MEM_WIDTH = 512
D_FF = 2816
RMS_EPS = 1e-6
NEG_INF = -1e30
NORM_MIX_PRE, NORM_MIX_POST, NORM_X_PRE, NORM_X_POST, NORM_MEM, NORM_FFN_PRE, NORM_FFN_POST = range(7)
EXP_M05 = math.exp(-0.5)

LANES = 128
IN_MAIN = IN_COLS_PAD - LANES
VMEM_LIMIT = 58 * 1024 * 1024


def _params(sem):
    return pltpu.CompilerParams(dimension_semantics=sem, vmem_limit_bytes=VMEM_LIMIT)


def _rms(x, g):
    ms = jnp.mean(x * x, axis=-1, keepdims=True)
    return x * lax.rsqrt(ms + RMS_EPS) * g


def _mm(a, b):
    return jnp.dot(a.astype(BF16), b.astype(BF16), preferred_element_type=F32)


def _mm_nt(a, b):
    return lax.dot_general(a.astype(BF16), b.astype(BF16), (((1,), (1,)), ((), ())),
                           preferred_element_type=F32)


def _split(x):
    hi = x.astype(BF16)
    lo = (x - hi.astype(F32)).astype(BF16)
    return hi, lo


def _mm_lsplit(a_exact_bf16, x):
    hi, lo = _split(x)
    return (jnp.dot(a_exact_bf16, hi, preferred_element_type=F32)
            + jnp.dot(a_exact_bf16, lo, preferred_element_type=F32))


def _in_kernel(*refs, tm, t_seq, n_cast):
    x_ref, g_ref, w_ref, wt_ref, cos_ref, sin_ref, mu_ref, sh0_ref = refs[:8]
    cast_in = refs[8:8 + n_cast]
    q_ref, k_ref, v_ref, pr_ref, shl_ref = refs[8 + n_cast:13 + n_cast]
    cast_out = refs[13 + n_cast:13 + 2 * n_cast]
    carry_scr, wb_scr, wtb_scr = refs[-3:]
    for src, dst in zip(cast_in, cast_out):
        dst[...] = src[...].astype(BF16)

    @pl.when(pl.program_id(0) == 0)
    def _():
        wb_scr[...] = w_ref[...].astype(BF16)
        wtb_scr[...] = wt_ref[...].astype(BF16)

    h = _rms(x_ref[...], g_ref[NORM_MIX_PRE:NORM_MIX_PRE + 1, :]).astype(BF16)
    nt = (((1,), (1,)), ((), ()))
    p = jnp.concatenate([lax.dot_general(h, wb_scr[...], nt, preferred_element_type=F32),
                         lax.dot_general(h, wtb_scr[...], nt, preferred_element_type=F32)], axis=1)
    cos = cos_ref[...]
    sin = sin_ref[...]
    lane = lax.broadcasted_iota(jnp.int32, cos.shape, 1)
    first_half = (lane & (HEAD_DIM // 2)) == 0

    def rope(xc):
        sw = jnp.where(first_half, pltpu.roll(xc, LANES - HEAD_DIM // 2, 1), pltpu.roll(xc, HEAD_DIM // 2, 1))
        return xc * cos + sw * sin

    for j in range(A_WIDTH // LANES):
        q_ref[:, j * LANES:(j + 1) * LANES] = rope(p[:, j * LANES:(j + 1) * LANES])
    k_ref[...] = rope(p[:, A_WIDTH:A_WIDTH + KV_WIDTH_A])
    v_ref[...] = p[:, A_WIDTH + KV_WIDTH_A:A_WIDTH + 2 * KV_WIDTH_A]
    raw = p[:, A_WIDTH + 2 * KV_WIDTH_A:]
    rowi = lax.broadcasted_iota(jnp.int32, (tm, 1), 0)
    prev = pltpu.roll(raw, 1, 0)
    if tm <= t_seq:
        first = (pl.program_id(0) % (t_seq // tm)) == 0
        prev = jnp.where(rowi == 0, jnp.where(first, sh0_ref[0], carry_scr[...]), prev)
        carry_scr[...] = raw[tm - 1:tm, :]
        shl_ref[0] = raw[tm - 1:tm, :]
    else:
        for s in range(tm // t_seq):
            prev = jnp.where(rowi == s * t_seq, sh0_ref[s], prev)
            shl_ref[s] = raw[(s + 1) * t_seq - 1:(s + 1) * t_seq, :]
    pr_ref[...] = (raw + (prev - raw) * mu_ref[...]).astype(pr_ref.dtype)


def _layer_block(shape, l):
    nd = len(shape)
    return pl.BlockSpec((None,) + tuple(shape), lambda *_: (l,) + (0,) * nd)


def _in_proj(x2d, l, gains, w_in, w_tail, cos_t, sin_t, mu_p, ls, shift0, t_seq, tm, casts=()):
    n = x2d.shape[0]
    n_seq = n // t_seq
    steps = n // tm
    tab_blocks = cos_t.shape[0] // tm
    row = lambda i: (i, 0)
    tab = lambda i: (i % tab_blocks, 0)
    if tm <= t_seq:
        seqs, seq_of = 1, lambda i: i // (t_seq // tm)
    else:
        seqs, seq_of = tm // t_seq, lambda i: i
    cast_in, cast_out, cast_shapes = [], [], []
    for w in casts:
        rows = w.shape[1] // steps
        cast_in.append(pl.BlockSpec((None, rows, w.shape[2]), lambda i: (l, i, 0)))
        cast_out.append(pl.BlockSpec((None, rows, w.shape[2]), lambda i: (0, i, 0)))
        cast_shapes.append(jax.ShapeDtypeStruct((1,) + w.shape[1:], BF16))
    outs = pl.pallas_call(
        functools.partial(_in_kernel, tm=tm, t_seq=t_seq, n_cast=len(casts)),
        grid=(steps,),
        in_specs=[pl.BlockSpec((tm, D_MODEL), row), _layer_block((8, D_MODEL), l),
                  pl.BlockSpec((None, IN_MAIN, D_MODEL), lambda i: (l, 0, 0), pipeline_mode=pl.Buffered(1)),
                  _layer_block((LANES, D_MODEL), l),
                  pl.BlockSpec((tm, LANES), tab), pl.BlockSpec((tm, LANES), tab),
                  _layer_block((1, RWKV_PAD), l),
                  pl.BlockSpec((None, seqs, 1, RWKV_PAD), lambda i: (ls, seq_of(i), 0, 0))] + cast_in,
        out_specs=[pl.BlockSpec((tm, A_WIDTH), row), pl.BlockSpec((tm, KV_WIDTH_A), row),
                   pl.BlockSpec((tm, KV_WIDTH_A), row), pl.BlockSpec((tm, RWKV_PAD), row),
                   pl.BlockSpec((seqs, 1, RWKV_PAD), lambda i: (seq_of(i), 0, 0))] + cast_out,
        out_shape=[jax.ShapeDtypeStruct((n, A_WIDTH), F32), jax.ShapeDtypeStruct((n, KV_WIDTH_A), F32),
                   jax.ShapeDtypeStruct((n, KV_WIDTH_A), F32), jax.ShapeDtypeStruct((n, RWKV_PAD), BF16),
                   jax.ShapeDtypeStruct((n_seq, 1, RWKV_PAD), F32)] + cast_shapes,
        scratch_shapes=[pltpu.VMEM((1, RWKV_PAD), F32), pltpu.VMEM((IN_MAIN, D_MODEL), BF16),
                        pltpu.VMEM((LANES, D_MODEL), BF16)],
        compiler_params=_params(("arbitrary",)),
        name="in_proj",
    )(x2d, gains, w_in, w_tail, cos_t, sin_t, mu_p, shift0, *casts)
    return outs[:5], outs[5:]


def _sink_attend(jobs, sink_ref, t):
    nk = jobs[0][1].shape[0]
    nkp = 2 * LANES
    lane = lax.broadcasted_iota(jnp.int32, (t, LANES), 1)
    m0 = lane < HEAD_DIM
    coli = lax.broadcasted_iota(jnp.int32, (1, nkp), 1)
    fills = []
    for kv in range(N_KV_A):
        blocks = []
        for g in range(4):
            sg = sink_ref[4 * kv + g:4 * kv + g + 1, :]
            blocks.append(jnp.broadcast_to(jnp.where(coli == nk, jnp.concatenate([sg, sg], axis=1), NEG_INF),
                                           (t, nkp)))
        fills.append(jnp.concatenate(blocks, axis=0))
    k_tail = jnp.zeros((nkp - nk, LANES), F32)
    tail_row = lax.broadcasted_iota(jnp.int32, (nkp - nk, nkp), 0)
    tail_col = lax.broadcasted_iota(jnp.int32, (nkp - nk, nkp), 1)
    v_tail = jnp.where((tail_row == 0) & (tail_col >= LANES), 1.0, 0.0).astype(F32)
    ones = jnp.ones((nk, LANES), F32)
    scores = []
    for q_rows, kdup, _, kv, valid in jobs:
        parts = []
        for p in (2 * kv, 2 * kv + 1):
            qp = q_rows[:, p * LANES:(p + 1) * LANES]
            parts.append(jnp.where(m0, qp, 0.0))
            parts.append(jnp.where(m0, 0.0, qp))
        lhs = jnp.concatenate(parts, axis=0)
        s = _mm_nt(lhs, jnp.concatenate([kdup, k_tail], axis=0))
        keep = coli < nk
        if valid is not None:
            keep = keep & valid
        scores.append(jnp.where(keep, s, fills[kv]))
    exps = [jnp.exp(s - jnp.max(s, axis=-1, keepdims=True)) for s in scores]
    outs = []
    for (_, _, vdup, _, _), e in zip(jobs, exps):
        v2 = jnp.concatenate([jnp.concatenate([vdup, ones], axis=1), v_tail], axis=0)
        o2 = _mm(e, v2)
        o = o2[:, :LANES] / o2[:, LANES:]
        outs.append([jnp.where(m0, o[(2 * pi) * t:(2 * pi + 1) * t], o[(2 * pi + 1) * t:(2 * pi + 2) * t])
                     for pi in range(2)])
    return outs


SWA_SCALE = HEAD_DIM ** -0.5


def _dup_heads(x):
    lane = lax.broadcasted_iota(jnp.int32, x.shape, 1)
    m0 = lane < HEAD_DIM
    xs = pltpu.roll(x, HEAD_DIM, 1)
    return [jnp.where(m0, x, xs), jnp.where(m0, xs, x)]


SWA_TQ = 512


def _swa_prompt_kernel(q_ref, kp_ref, kc_ref, vp_ref, vc_ref, sink_ref, o_ref):
    i = pl.program_id(1)
    k = jnp.concatenate([kp_ref[...], kc_ref[...]], axis=0)
    v = jnp.concatenate([vp_ref[...], vc_ref[...]], axis=0)
    kd = _dup_heads(k * SWA_SCALE)
    vd = _dup_heads(v)
    nk = 3 * CHUNK
    slot = lax.broadcasted_iota(jnp.int32, (1, 2 * LANES), 1) // CHUNK
    jobs = []
    for j in range(SWA_TQ // CHUNK):
        qj = q_ref[j * CHUNK:(j + 1) * CHUNK, :]
        valid = (slot + (i * (SWA_TQ // CHUNK) + j - 2)) >= 0
        for kv in range(N_KV_A):
            jobs.append((qj, kd[kv][j * CHUNK:j * CHUNK + nk], vd[kv][j * CHUNK:j * CHUNK + nk], kv, valid))
    outs = _sink_attend(jobs, sink_ref, CHUNK)
    for n, out in enumerate(outs):
        j, kv = divmod(n, N_KV_A)
        for pi in range(2):
            p = 2 * kv + pi
            o_ref[j * CHUNK:(j + 1) * CHUNK, p * LANES:(p + 1) * LANES] = out[pi]


def _swa_prompt(q, k, v, l, sink_b, b, t):
    n = b * t
    nq = t // SWA_TQ
    qmap = lambda bi, i: (bi * nq + i, 0)
    pmap = lambda bi, i: (jnp.maximum(bi * (t // WINDOW) + (SWA_TQ // WINDOW) * i - 1, 0), 0)
    return pl.pallas_call(
        _swa_prompt_kernel,
        grid=(b, nq),
        in_specs=[pl.BlockSpec((SWA_TQ, A_WIDTH), qmap),
                  pl.BlockSpec((WINDOW, KV_WIDTH_A), pmap), pl.BlockSpec((SWA_TQ, KV_WIDTH_A), qmap),
                  pl.BlockSpec((WINDOW, KV_WIDTH_A), pmap), pl.BlockSpec((SWA_TQ, KV_WIDTH_A), qmap),
                  _layer_block((8, LANES), l)],
        out_specs=pl.BlockSpec((SWA_TQ, A_WIDTH), qmap),
        out_shape=jax.ShapeDtypeStruct((n, A_WIDTH), F32),
        compiler_params=_params(("parallel", "parallel")),
        name="swa_prompt",
    )(q, k, k, v, v, sink_b)


def _swa_sample_kernel(q_ref, kc_ref, kn_ref, vc_ref, vn_ref, sink_ref, o_ref, *, t, nb):
    jobs = []
    for bi in range(nb):
        rows = slice(bi * t, (bi + 1) * t)
        kd = _dup_heads(jnp.concatenate([kc_ref[bi], kn_ref[rows, :]], axis=0) * SWA_SCALE)
        vd = _dup_heads(jnp.concatenate([vc_ref[bi], vn_ref[rows, :]], axis=0))
        for kv in range(N_KV_A):
            jobs.append((q_ref[rows, :], kd[kv], vd[kv], kv, None))
    outs = _sink_attend(jobs, sink_ref, t)
    for n, out in enumerate(outs):
        bi, kv = divmod(n, N_KV_A)
        for pi in range(2):
            p = 2 * kv + pi
            o_ref[bi * t:(bi + 1) * t, p * LANES:(p + 1) * LANES] = out[pi]


def _swa_sample(q, k, v, l, kc, vc, sink_b, b, t):
    n = b * t
    cache = kc.shape[2]
    nb = 8 if b % 8 == 0 else 1
    row = lambda bi: (bi, 0)
    cmap = lambda bi: (l, bi, 0, 0)
    return pl.pallas_call(
        functools.partial(_swa_sample_kernel, t=t, nb=nb),
        grid=(b // nb,),
        in_specs=[pl.BlockSpec((nb * t, A_WIDTH), row),
                  pl.BlockSpec((None, nb, cache, KV_WIDTH_A), cmap), pl.BlockSpec((nb * t, KV_WIDTH_A), row),
                  pl.BlockSpec((None, nb, cache, KV_WIDTH_A), cmap), pl.BlockSpec((nb * t, KV_WIDTH_A), row),
                  _layer_block((8, LANES), l)],
        out_specs=pl.BlockSpec((nb * t, A_WIDTH), row),
        out_shape=jax.ShapeDtypeStruct((n, A_WIDTH), F32),
        compiler_params=_params(("parallel",)),
        name="swa_sample",
    )(q, kc, k, vc, v, sink_b)


RWKV_SUB = 256


def _interleave(gens):
    gens = list(gens)
    while gens:
        for g in list(gens):
            try:
                next(g)
            except StopIteration:
                gens.remove(g)


def _rwkv_kernel(pr_ref, s0_ref, vec_ref, rk_ref, w2_ref, a2_ref, g2_ref, hsum_ref,
                 ltri_ref, out_ref, sfin_ref,
                 s_scr, rt_s, at_s, bt_s, kt_s, bb_s, kb_s, v_s, et_s, y_s, bonus_s, gate_s,
                 *, c_len, tt, nb, sub):
    i = pl.program_id(1)
    n_i = pl.num_programs(1)
    n_rows = nb * tt

    @pl.when(i == 0)
    def _():
        zero = jnp.zeros((HEAD_DIM, HEAD_DIM), F32)
        for bi in range(nb):
            for p in range(4):
                s_scr[bi, p] = jnp.concatenate(
                    [jnp.concatenate([s0_ref[bi, 2 * p].astype(F32), zero], axis=1),
                     jnp.concatenate([zero, s0_ref[bi, 2 * p + 1].astype(F32)], axis=1)], axis=0)

    w0 = vec_ref[0:1, :]
    a0 = vec_ref[1:2, :]
    k_k = vec_ref[2:3, :]
    k_a = vec_ref[3:4, :]
    gn_g = vec_ref[4:5, :]
    gn_b = vec_ref[5:6, :]
    hsum = hsum_ref[...]
    hw = hsum.shape[0]
    halves = [slice(j * hw, (j + 1) * hw) for j in range(R_WIDTH // hw)]

    def head_sum(x):
        return _mm(x, hsum)

    n2 = 2 * c_len
    ti = lax.broadcasted_iota(jnp.int32, (c_len, n2), 0)
    lane2 = lax.broadcasted_iota(jnp.int32, (c_len, n2), 1)
    si = lane2 & (c_len - 1)
    first_c = lane2 < c_len
    strict = ti > si
    incl = ti >= si
    eye = jnp.where(ti == si, 1.0, 0.0).astype(F32)
    masks = []
    half = 1
    while half < c_len:
        blk = 2 * half
        masks.append(((ti & ~(blk - 1)) == (si & ~(blk - 1))) & ((ti & half) != 0) & ((si & half) == 0))
        half = blk
    m0 = lax.broadcasted_iota(jnp.int32, (c_len, LANES), 1) < HEAD_DIM
    rl = lax.broadcasted_iota(jnp.int32, (LANES, LANES), 0)
    cl = lax.broadcasted_iota(jnp.int32, (LANES, LANES), 1)
    same_head = (rl >= HEAD_DIM) == (cl >= HEAD_DIM)
    fused = n2 == LANES
    zeros_c = jnp.zeros((c_len, LANES), F32)

    def bd(x):
        return jnp.concatenate([jnp.where(m0, x, 0.0), jnp.where(m0, 0.0, x)], axis=0)

    def bd_t(x):
        return jnp.concatenate([jnp.where(first_c, x, 0.0), jnp.where(first_c, 0.0, x)], axis=0)

    def prologue(lo, hi):
        n = hi - lo
        bi0 = lo // tt

        def shifted(a, b):
            if hi - lo <= tt:
                return pr_ref[bi0, lo - bi0 * tt:hi - bi0 * tt, a:b].astype(F32)
            return pr_ref[bi0:hi // tt, :, a:b].astype(F32).reshape(n, b - a)

        xwa = shifted(3 * R_WIDTH, 3 * R_WIDTH + LANES)
        xg = shifted(3 * R_WIDTH + LANES, RWKV_PAD)
        th = jnp.tanh(xwa)
        sg = jax.nn.sigmoid(xg)
        ltri = ltri_ref[...]
        yield
        for cs in halves:
            r = shifted(cs.start, cs.stop)
            k = shifted(R_WIDTH + cs.start, R_WIDTH + cs.stop)
            v = shifted(2 * R_WIDTH + cs.start, 2 * R_WIDTH + cs.stop)
            yield
            z = w0[:, cs] + _mm(th, w2_ref[:, cs])
            wlog = -EXP_M05 * jax.nn.sigmoid(z)
            a = jax.nn.sigmoid(a0[:, cs] + _mm(xwa, a2_ref[:, cs]))
            gate_s[lo:hi, cs] = _mm(sg, g2_ref[:, cs])
            yield
            kk = k * k_k[:, cs]
            kk = kk * lax.rsqrt(jnp.maximum(head_sum(kk * kk), 1e-24))
            k_f = k * (1.0 + (a - 1.0) * k_a[:, cs])
            bvec = kk * a
            bonus_s[lo:hi, cs] = head_sum(r * k_f * rk_ref[:, cs]) * v
            yield
            cum = _mm_lsplit(ltri, wlog)
            tot = jnp.concatenate([jnp.broadcast_to(cum[r1 - 1:r1, :], (c_len, hw))
                                   for r1 in range(c_len, n + 1, c_len)], axis=0)
            e_in = jnp.exp(cum)
            e_inv = jnp.exp(-cum)
            e_end = jnp.exp(tot - cum)
            rt_s[lo:hi, cs] = r * e_in
            at_s[lo:hi, cs] = -kk * jnp.exp(cum - wlog)
            yield
            bt_s[lo:hi, cs] = bvec * e_inv
            kt_s[lo:hi, cs] = k_f * e_inv
            bb_s[lo:hi, cs] = bvec * e_end
            kb_s[lo:hi, cs] = k_f * e_end
            v_s[lo:hi, cs] = v
            et_s[lo:hi, cs] = jnp.exp(tot)
            yield

    s_state = [[s_scr[bi, p] for p in range(4)] for bi in range(nb)]

    def chains(lo, hi):
        chs = [(r0, p) for r0 in range(lo, hi, c_len) for p in range(4)]

        def cat(ref, ch):
            return ref[ch[0]:ch[0] + c_len, ch[1] * LANES:(ch[1] + 1) * LANES]

        a_ab, a_ak, a_rb, a_rk = {}, {}, {}, {}
        for ch in chs:
            at, rt, rb, rkt = cat(at_s, ch), cat(rt_s, ch), bd(cat(bt_s, ch)), bd(cat(kt_s, ch))
            if fused:
                amat = _mm_nt(jnp.concatenate([at, rt], axis=0), jnp.concatenate([rb, rkt], axis=0))
                q_ab, q_ak = amat[:c_len, :n2], amat[:c_len, n2:]
                q_rb, q_rk = amat[c_len:, :n2], amat[c_len:, n2:]
            else:
                q_ab, q_ak, q_rb, q_rk = _mm_nt(at, rb), _mm_nt(at, rkt), _mm_nt(rt, rb), _mm_nt(rt, rkt)
            a_ab[ch] = jnp.where(strict, q_ab, 0.0)
            a_ak[ch] = jnp.where(strict, q_ak, 0.0)
            a_rb[ch] = jnp.where(incl, q_rb, 0.0)
            a_rk[ch] = jnp.where(incl, q_rk, 0.0)
        yield
        tinv = {ch: eye + jnp.where(masks[0], a_ab[ch], 0.0) for ch in chs}
        for m in masks[1:]:
            pe = {ch: _mm(tinv[ch], bd_t(jnp.where(m, a_ab[ch], 0.0))) for ch in chs}
            yield
            tinv = {ch: tinv[ch] + _mm(pe[ch], bd_t(tinv[ch])) for ch in chs}
            yield
        zv = {ch: _mm(a_ak[ch], bd(cat(v_s, ch))) for ch in chs}
        yield
        r_hat, y_hat, m_mat, g_mat = {}, {}, {}, {}
        if fused:
            w = {ch: _mm(tinv[ch], jnp.concatenate([bd(cat(at_s, ch)), bd(zv[ch])], axis=1)) for ch in chs}
            yield
            for ch in chs:
                rhs2 = jnp.concatenate(
                    [jnp.concatenate([bd(w[ch][:, :LANES]), bd(w[ch][:, LANES:])], axis=1),
                     jnp.concatenate([jnp.zeros((n2, LANES), F32), bd(cat(v_s, ch))], axis=1)], axis=0)
                ry = _mm(jnp.concatenate([a_rb[ch], a_rk[ch]], axis=1), rhs2)
                r_hat[ch] = cat(rt_s, ch) + ry[:, :LANES]
                y_hat[ch] = ry[:, LANES:]
            yield
            for ch in chs:
                lhs_t = jnp.concatenate([w[ch], jnp.concatenate([zeros_c, cat(v_s, ch)], axis=1)], axis=0)
                mg = _mm(lhs_t.T, jnp.concatenate([cat(bb_s, ch), cat(kb_s, ch)], axis=0))
                m_mat[ch] = jnp.where(same_head, mg[:LANES], 0.0)
                g_mat[ch] = jnp.where(same_head, mg[LANES:], 0.0)
            yield
        else:
            a_hat = {ch: _mm(tinv[ch], bd(cat(at_s, ch))) for ch in chs}
            u_hat = {ch: _mm(tinv[ch], bd(zv[ch])) for ch in chs}
            yield
            for ch in chs:
                r_hat[ch] = cat(rt_s, ch) + _mm(a_rb[ch], bd(a_hat[ch]))
                y_hat[ch] = _mm(a_rb[ch], bd(u_hat[ch])) + _mm(a_rk[ch], bd(cat(v_s, ch)))
            yield
            for ch in chs:
                m_mat[ch] = jnp.where(same_head, _mm(a_hat[ch].T, cat(bb_s, ch)), 0.0)
                g_mat[ch] = jnp.where(same_head,
                                      _mm(u_hat[ch].T, cat(bb_s, ch)) + _mm(cat(v_s, ch).T, cat(kb_s, ch)), 0.0)
            yield
        for r0 in range(lo, hi, c_len):
            s_cur = s_state[r0 // tt]
            for p in range(4):
                ch = (r0, p)
                ls = slice(p * LANES, (p + 1) * LANES)
                e_last = et_s[r0:r0 + 1, ls]
                y_s[r0:r0 + c_len, ls] = y_hat[ch] + _mm_nt(r_hat[ch], s_cur[p])
                s_cur[p] = s_cur[p] * e_last + _mm(s_cur[p], m_mat[ch]) + g_mat[ch]
            yield

    def epilogue(lo, hi):
        for cs in halves:
            y = y_s[lo:hi, cs]
            mean = head_sum(y) * (1.0 / HEAD_DIM)
            yc = y - mean
            yield
            var = head_sum(yc * yc) * (1.0 / HEAD_DIM)
            yn = yc * lax.rsqrt(var + GN_EPS) * gn_g[:, cs] + gn_b[:, cs]
            res = (yn + bonus_s[lo:hi, cs]) * gate_s[lo:hi, cs]
            for r0 in range(lo, hi, tt) if hi - lo > tt else [lo]:
                r1 = min(r0 + tt, hi)
                out_ref[r0 // tt, r0 % tt:r0 % tt + (r1 - r0), cs] = res[r0 - lo:r1 - lo]
            yield

    subs = [(lo, lo + sub) for lo in range(0, n_rows, sub)]
    _interleave([prologue(*subs[0])])
    for n, sb in enumerate(subs):
        phases = [chains(*sb)]
        if n + 1 < len(subs):
            phases.append(prologue(*subs[n + 1]))
        if n > 0:
            phases.append(epilogue(*subs[n - 1]))
        _interleave(phases)
    _interleave([epilogue(*subs[-1])])

    for bi in range(nb):
        for p in range(4):
            s_scr[bi, p] = s_state[bi][p]

    @pl.when(i == n_i - 1)
    def _():
        for bi in range(nb):
            for p in range(4):
                s_pair = s_state[bi][p]
                sfin_ref[bi, 2 * p] = s_pair[:HEAD_DIM, :HEAD_DIM].astype(sfin_ref.dtype)
                sfin_ref[bi, 2 * p + 1] = s_pair[HEAD_DIM:, HEAD_DIM:].astype(sfin_ref.dtype)


def _rwkv(pr3, l, ls, s0, vecs, rk_flat, w2p, a2p, g2p, hsum, tt, c_len, nb):
    b, t, _ = pr3.shape
    n_rows = nb * tt
    sub = min(RWKV_SUB, n_rows)
    ri = jnp.arange(sub)[:, None]
    ci = jnp.arange(sub)[None, :]
    same = (ri // c_len) == (ci // c_len)
    ltri = (same & (ri >= ci)).astype(BF16)
    hw = hsum.shape[0]
    blk = lambda bi, i: (bi, i, 0)
    const = lambda bi, i: (0, 0)
    big = lambda: pltpu.VMEM((n_rows, R_WIDTH), F32)
    return pl.pallas_call(
        functools.partial(_rwkv_kernel, c_len=c_len, tt=tt, nb=nb, sub=sub),
        grid=(b // nb, t // tt),
        in_specs=[pl.BlockSpec((nb, tt, RWKV_PAD), blk),
                  pl.BlockSpec((None, nb, N_HEADS_R, HEAD_DIM, HEAD_DIM), lambda bi, i: (ls, bi, 0, 0, 0)),
                  _layer_block((6, R_WIDTH), l), _layer_block((1, R_WIDTH), l),
                  _layer_block((LANES, R_WIDTH), l), _layer_block((LANES, R_WIDTH), l),
                  _layer_block((2 * LANES, R_WIDTH), l), pl.BlockSpec((hw, hw), const),
                  pl.BlockSpec((sub, sub), const)],
        out_specs=[pl.BlockSpec((nb, tt, R_WIDTH), blk),
                   pl.BlockSpec((nb, N_HEADS_R, HEAD_DIM, HEAD_DIM), lambda bi, i: (bi, 0, 0, 0))],
        out_shape=[jax.ShapeDtypeStruct((b, t, R_WIDTH), F32),
                   jax.ShapeDtypeStruct((b, N_HEADS_R, HEAD_DIM, HEAD_DIM), s0.dtype)],
        scratch_shapes=[pltpu.VMEM((nb, 4, LANES, LANES), F32),
                        big(), big(), big(), big(), big(), big(), big(), big(), big(), big(), big()],
        compiler_params=_params(("parallel", "arbitrary")),
        name="rwkv_mix",
    )(pr3, s0, vecs, rk_flat, w2p, a2p, g2p, hsum, ltri)


POST_PARTS = 2


def _post_kernel(attn_ref, rw_ref, x_ref, g_ref, wout_ref, wq_ref, wo_ref, mk_ref, mv_ref, o_ref, *, nb, t):
    scale = MEM_HEAD_DIM ** -0.5
    cols = lambda hd: slice(hd * MEM_HEAD_DIM, (hd + 1) * MEM_HEAD_DIM)
    head_rows = lambda hd: pl.ds(hd, MEM_LEN, stride=MEM_HEADS)
    ones = jnp.ones((MEM_LEN, MEM_HEAD_DIM), F32)

    def part(lo, hi):
        rows = slice(lo, hi)
        m = _mm(attn_ref[rows, :], wout_ref[0:A_WIDTH, :]) + _mm(rw_ref[rows, :], wout_ref[A_WIDTH:, :])
        yield
        x1 = x_ref[rows, :] + _rms(m, g_ref[NORM_MIX_POST:NORM_MIX_POST + 1, :])
        hq = _rms(x1, g_ref[NORM_X_PRE:NORM_X_PRE + 1, :])
        yield
        q = _mm(hq, wq_ref[...])
        yield
        if nb == 1:
            jobs = [(0, lo, hi, hd) for hd in range(MEM_HEADS)]
        else:
            jobs = [(r0 // t, r0, r0 + t, hd) for r0 in range(lo, hi, t) for hd in range(MEM_HEADS)]
        scores = [_mm_nt(q[r0 - lo:r1 - lo, cols(hd)], mk_ref[bi, head_rows(hd), :]) * scale
                  for bi, r0, r1, hd in jobs]
        yield
        exps = [jnp.exp(s - jnp.max(s, axis=-1, keepdims=True)) for s in scores]
        yield
        outs = []
        for (bi, _, _, hd), e in zip(jobs, exps):
            o2 = _mm(e, jnp.concatenate([mv_ref[bi, head_rows(hd), :], ones], axis=1))
            outs.append(o2[:, :MEM_HEAD_DIM] / o2[:, MEM_HEAD_DIM:])
        blocks = [jnp.concatenate(outs[j:j + MEM_HEADS], axis=1) for j in range(0, len(outs), MEM_HEADS)]
        o = blocks[0] if len(blocks) == 1 else jnp.concatenate(blocks, axis=0)
        yield
        c = _mm(o, wo_ref[...])
        yield
        o_ref[rows, :] = x1 + _rms(c, g_ref[NORM_X_POST:NORM_X_POST + 1, :])

    n_rows = nb * t
    n_parts = POST_PARTS if (nb == 1 or nb % POST_PARTS == 0) and n_rows % (8 * POST_PARTS) == 0 else 1
    step = n_rows // n_parts
    gens = [part(j * step, (j + 1) * step) for j in range(n_parts)]
    for j, gen in enumerate(gens):
        for _ in range(n_parts - 1 - j):
            next(gen)
    _interleave(gens)


def _post(attn, rw, x2d, l, gains, w_out, w_mq, w_mo, lm, mk, mv, nb, t, tiles_per_batch):
    n = x2d.shape[0]
    tm = nb * t
    row = lambda i: (i, 0)
    if nb == 1:
        mmap = lambda i: (lm, i // tiles_per_batch, 0, 0)
    else:
        mmap = lambda i: (lm, i, 0, 0)
    mem_rows = MEM_LEN * MEM_HEADS
    return pl.pallas_call(
        functools.partial(_post_kernel, nb=nb, t=t),
        grid=(n // tm,),
        in_specs=[pl.BlockSpec((tm, A_WIDTH), row), pl.BlockSpec((tm, R_WIDTH), row),
                  pl.BlockSpec((tm, D_MODEL), row), _layer_block((8, D_MODEL), l),
                  _layer_block((D_MODEL, D_MODEL), 0), _layer_block((D_MODEL, MEM_WIDTH), 0),
                  _layer_block((MEM_WIDTH, D_MODEL), 0),
                  pl.BlockSpec((None, nb, mem_rows, MEM_HEAD_DIM), mmap),
                  pl.BlockSpec((None, nb, mem_rows, MEM_HEAD_DIM), mmap)],
        out_specs=pl.BlockSpec((tm, D_MODEL), row),
        out_shape=jax.ShapeDtypeStruct((n, D_MODEL), F32),
        compiler_params=_params(("parallel",)),
        name="post_mix",
    )(attn, rw, x2d, gains, w_out, w_mq, w_mo, mk, mv)


FFN_CHUNK = 256


def _ffn_kernel(x_ref, g_ref, wgu_ref, wd_ref, o_ref):
    x = x_ref[...]
    h = _rms(x, g_ref[NORM_FFN_PRE:NORM_FFN_PRE + 1, :]).astype(BF16)
    acc = None
    for j in range(D_FF // FFN_CHUNK):
        cols = slice(j * FFN_CHUNK, (j + 1) * FFN_CHUNK)
        gate = jnp.dot(h, wgu_ref[:, cols], preferred_element_type=F32)
        up = jnp.dot(h, wgu_ref[:, D_FF + j * FFN_CHUNK:D_FF + (j + 1) * FFN_CHUNK], preferred_element_type=F32)
        act = (gate * jax.nn.sigmoid(gate)) * up
        part = _mm(act, wd_ref[cols, :])
        acc = part if acc is None else acc + part
    o_ref[...] = x + _rms(acc, g_ref[NORM_FFN_POST:NORM_FFN_POST + 1, :])


def _ffn(x2d, l, gains, w_gu, w_dn, tm):
    n = x2d.shape[0]
    row = lambda i: (i, 0)
    resident = dict(pipeline_mode=pl.Buffered(1))
    return pl.pallas_call(
        _ffn_kernel,
        grid=(n // tm,),
        in_specs=[pl.BlockSpec((tm, D_MODEL), row), _layer_block((8, D_MODEL), l),
                  pl.BlockSpec((None, D_MODEL, 2 * D_FF), lambda i: (0, 0, 0), **resident),
                  pl.BlockSpec((None, D_FF, D_MODEL), lambda i: (0, 0, 0), **resident)],
        out_specs=pl.BlockSpec((tm, D_MODEL), row),
        out_shape=jax.ShapeDtypeStruct((n, D_MODEL), F32),
        compiler_params=_params(("parallel",)),
        name="ffn",
    )(x2d, gains, w_gu, w_dn)


def _memkv_kernel(x_ref, g_ref, w_ref, k_ref, v_ref):
    kv = _mm(_rms(x_ref[...], g_ref[NORM_MEM:NORM_MEM + 1, :]), w_ref[...])
    for hd in range(MEM_HEADS):
        rows = pl.ds(hd, MEM_LEN, stride=MEM_HEADS)
        k_ref[0, rows, :] = kv[:, hd * MEM_HEAD_DIM:(hd + 1) * MEM_HEAD_DIM]
        v_ref[0, rows, :] = kv[:, MEM_WIDTH + hd * MEM_HEAD_DIM:MEM_WIDTH + (hd + 1) * MEM_HEAD_DIM]


def _memkv(mem2d, l, gains, w_kv):
    n = mem2d.shape[0]
    nbatch = n // MEM_LEN
    mem_rows = MEM_LEN * MEM_HEADS
    out = jax.ShapeDtypeStruct((nbatch, mem_rows, MEM_HEAD_DIM), F32)
    ospec = pl.BlockSpec((1, mem_rows, MEM_HEAD_DIM), lambda i: (i, 0, 0))
    return pl.pallas_call(
        _memkv_kernel,
        grid=(nbatch,),
        in_specs=[pl.BlockSpec((MEM_LEN, D_MODEL), lambda i: (i, 0)), _layer_block((8, D_MODEL), l),
                  _layer_block((D_MODEL, 2 * MEM_WIDTH), l)],
        out_specs=[ospec, ospec],
        out_shape=[out, out],
        compiler_params=_params(("parallel",)),
        name="mem_kv",
    )(mem2d, gains, w_kv)


def _rope_tables(pos):
    half = HEAD_DIM // 2
    inv = ROPE_THETA ** (-jnp.arange(half, dtype=F32) / half)
    ang = pos.astype(F32)[:, None] * inv[None, :]
    cos = jnp.cos(ang)
    sin = jnp.sin(ang)
    cos_t = jnp.tile(cos, (1, LANES // half))
    sin_t = jnp.tile(jnp.concatenate([-sin, sin], axis=1), (1, LANES // HEAD_DIM))
    return cos_t, sin_t


def _stacked_params(norm_gains, w_in, attn_sink, shift_mu, rwkv_vecs, rwkv_rk, rwkv_w2, rwkv_a2, rwkv_g2,
                    w_out, w_mem_q, w_mem_kv, w_mem_o, w_gate_up, w_down):
    depth = w_in.shape[0]
    return dict(
        gains=jnp.pad(norm_gains, ((0, 0), (0, 8 - norm_gains.shape[1]), (0, 0))),
        w_in=jnp.swapaxes(w_in, 1, 2),
        w_tail=jnp.pad(jnp.swapaxes(w_in[:, :, IN_MAIN:], 1, 2),
                       ((0, 0), (0, IN_MAIN + LANES - w_in.shape[2]), (0, 0))),
        sink=jnp.broadcast_to(attn_sink[:, :, None], attn_sink.shape + (LANES,)).astype(F32),
        mu=jnp.pad(shift_mu, ((0, 0), (0, RWKV_PAD - RWKV_PROJ)))[:, None, :],
        vecs=rwkv_vecs, rk=rwkv_rk.reshape(depth, 1, R_WIDTH),
        w2=jnp.pad(rwkv_w2, ((0, 0), (0, LANES - DECAY_LORA), (0, 0))).astype(BF16),
        a2=jnp.pad(rwkv_a2, ((0, 0), (DECAY_LORA, LANES - DECAY_LORA - AAA_LORA), (0, 0))).astype(BF16),
        g2=jnp.pad(rwkv_g2, ((0, 0), (0, 2 * LANES - GATE_LORA), (0, 0))).astype(BF16),
        w_mkv=w_mem_kv,
        w_out=w_out, w_mq=w_mem_q, w_mo=w_mem_o, w_gu=w_gate_up, w_dn=w_down)


def _head_sum_matrix():
    i = jnp.arange(2 * LANES)
    return ((i[:, None] // HEAD_DIM) == (i[None, :] // HEAD_DIM)).astype(BF16)


def _layer(x2d, l, b, t, tabs, lm, mk, mv, swa_cache, ls, s0, shift0, sp, hsum, in_tm, rw_tt, post_nb, post_t,
           ffn_tm, late_w=None):
    to_cast = () if late_w else tuple(sp[name] for name in ("w_out", "w_mq", "w_mo", "w_gu", "w_dn"))
    (q, k, v, pr, shift_new), cast = _in_proj(x2d, l, sp["gains"], sp["w_in"], sp["w_tail"], tabs[0], tabs[1], sp["mu"], ls,
                                              shift0, t, in_tm, to_cast)
    late_w = late_w or tuple(cast)
    if swa_cache is None:
        attn = _swa_prompt(q, k, v, l, sp["sink"], b, t)
    else:
        attn = _swa_sample(q, k, v, l, swa_cache[0], swa_cache[1], sp["sink"], b, t)
    pr3 = pr.reshape(b, t, RWKV_PAD)
    rw, s_fin = _rwkv(pr3, l, ls, s0, sp["vecs"], sp["rk"], sp["w2"], sp["a2"], sp["g2"], hsum,
                      rw_tt[0], min(CHUNK, t), rw_tt[1])
    w_out, w_mq, w_mo, w_gu, w_dn = late_w
    x2 = _post(attn, rw.reshape(b * t, R_WIDTH), x2d, l, sp["gains"], w_out, w_mq, w_mo,
               lm, mk, mv, post_nb, post_t, t // post_t)
    x3 = _ffn(x2, l, sp["gains"], w_gu, w_dn, ffn_tm)
    return x3, k, v, s_fin, shift_new[:, :, :RWKV_PROJ], late_w


def kernel(x_prompt, mem_prompt, x_sample, cache_swa_k, cache_swa_v, cache_mem_k, cache_mem_v, state_rwkv,
           state_shift, norm_gains, w_in, attn_sink, shift_mu, rwkv_vecs, rwkv_rk, rwkv_w2, rwkv_a2, rwkv_g2,
           w_out, w_mem_q, w_mem_kv, w_mem_o, w_gate_up, w_down):
    b, t, _ = x_prompt.shape
    bd, tn, _ = x_sample.shape
    depth = w_in.shape[0]
    m_len = mem_prompt.shape[1]
    cache_len = cache_swa_k.shape[2]
    tabs_p = _rope_tables(jnp.arange(t, dtype=jnp.int32))
    cs, sn = _rope_tables(PAST_LEN + jnp.arange(tn, dtype=jnp.int32))
    tabs_s = (jnp.tile(cs, (bd, 1)), jnp.tile(sn, (bd, 1)))
    hsum = _head_sum_matrix()
    sp = _stacked_params(norm_gains, w_in, attn_sink, shift_mu, rwkv_vecs, rwkv_rk, rwkv_w2, rwkv_a2, rwkv_g2,
                         w_out, w_mem_q, w_mem_kv, w_mem_o, w_gate_up, w_down)
    s0_p = jnp.zeros((1, b, N_HEADS_R, HEAD_DIM, HEAD_DIM), F32)
    sh0_p = jnp.zeros((1, b, 1, RWKV_PAD), F32)
    sh0_s = jnp.pad(state_shift, ((0, 0), (0, 0), (0, 0), (0, RWKV_PAD - RWKV_PROJ)))
    swa_cache = (cache_swa_k.reshape(depth, bd, cache_len, KV_WIDTH_A),
                 cache_swa_v.reshape(depth, bd, cache_len, KV_WIDTH_A))
    mem_rows = m_len * MEM_HEADS
    cmk = cache_mem_k.reshape(depth, bd, mem_rows, MEM_HEAD_DIM)
    cmv = cache_mem_v.reshape(depth, bd, mem_rows, MEM_HEAD_DIM)
    xp = x_prompt.reshape(b * t, D_MODEL)
    xs = x_sample.reshape(bd * tn, D_MODEL)
    mem2d = mem_prompt.reshape(b * m_len, D_MODEL)
    in_tm_p = min(1024, t)
    rw_tt_p = (min(512, t), 1)
    rw_tt_s = (tn, 16 if bd % 16 == 0 else 1)
    post_t_p = min(1024, t)
    post_nb_s = 8 if bd % 8 == 0 else bd
    ffn_tm_p = min(512, b * t)
    keep = t - min(WINDOW, t)
    pk, pv, pmk, pmv, pS, psh = [], [], [], [], [], []
    sk, sv, sS, ssh = [], [], [], []
    for l in range(depth):
        mk_l, mv_l = _memkv(mem2d, l, sp["gains"], sp["w_mkv"])
        xp, k_l, v_l, s_l, sh_l, late_w = _layer(xp, l, b, t, tabs_p, 0, mk_l[None], mv_l[None], None, 0, s0_p, sh0_p,
                                                sp, hsum, in_tm_p, rw_tt_p, 1, post_t_p, ffn_tm_p)
        pk.append(k_l.reshape(b, t, KV_WIDTH_A)[:, keep:].reshape(b, t - keep, N_KV_A, HEAD_DIM))
        pv.append(v_l.reshape(b, t, KV_WIDTH_A)[:, keep:].reshape(b, t - keep, N_KV_A, HEAD_DIM))
        pmk.append(mk_l.reshape(b, m_len, MEM_HEADS, MEM_HEAD_DIM))
        pmv.append(mv_l.reshape(b, m_len, MEM_HEADS, MEM_HEAD_DIM))
        pS.append(s_l)
        psh.append(sh_l)

        xs, k2, v2, s2, sh2, _ = _layer(xs, l, bd, tn, tabs_s, l, cmk, cmv, swa_cache, l, state_rwkv, sh0_s, sp,
                                        hsum, bd * tn, rw_tt_s, post_nb_s, tn, bd * tn, late_w)
        sk.append(k2.reshape(bd, tn, N_KV_A, HEAD_DIM))
        sv.append(v2.reshape(bd, tn, N_KV_A, HEAD_DIM))
        sS.append(s2)
        ssh.append(sh2)
    return (xp.reshape(b, t, D_MODEL), xs.reshape(bd, tn, D_MODEL), jnp.stack(pk), jnp.stack(pv),
            jnp.stack(pmk), jnp.stack(pmv), jnp.stack(pS), jnp.stack(psh),
            jnp.stack(sk), jnp.stack(sv), jnp.stack(sS), jnp.stack(ssh))
```
